```python
import jax, jax.numpy as jnp
from jax import lax
import numpy as np

D_MODEL = 1024
BATCH = 32
SEQ = 2048
DEPTH = 1

D_RNN = 1280
N_RNN_BLOCKS = 10
RNN_BLOCK = D_RNN // N_RNN_BLOCKS
RNN_CONV_WIDTH = 4
LRU_C = 8.0
N_HEADS = 8
HEAD_DIM = 128
D_ATTN = N_HEADS * HEAD_DIM
Q_BLOCK = 128
D_FF = 3 * D_MODEL
FFN_CONV_WIDTH = 3
RMS_EPS = 1e-6
SPLIT_SIZES = (D_RNN, D_RNN, D_ATTN, D_ATTN, D_ATTN, N_HEADS, 2 * D_MODEL)
D_IN = sum(SPLIT_SIZES)
N_MOD = 6

kernel_name = "hybrid_rglru_fox_convffn_adaln"


def rmsnorm(x, g):
    xf = x.astype(jnp.float32)
    y = xf * lax.rsqrt(jnp.mean(xf * xf, axis=-1, keepdims=True) + RMS_EPS)
    return (y * g.astype(jnp.float32)).astype(x.dtype)


def modulate(h, shift, scale):
    return h * (1 + scale[:, None, :]) + shift[:, None, :]


def causal_dwconv(x, w, b):
    K = w.shape[0]
    S = x.shape[1]
    xp = jnp.pad(x, ((0, 0), (K - 1, 0), (0, 0)))
    y = b
    for k in range(K):
        y = y + xp[:, k:k + S, :] * w[k]
    return y


def block_diag_linear(x, w, b):
    B, S, _ = x.shape
    xb = x.reshape(B, S, N_RNN_BLOCKS, RNN_BLOCK)
    return jnp.einsum('bsnc,ncd->bsnd', xb, w).reshape(B, S, D_RNN) + b


def rg_lru(x, w_a, b_a, w_i, b_i, lam):
    r = jax.nn.sigmoid(block_diag_linear(x, w_a, b_a)).astype(jnp.float32)
    i = jax.nn.sigmoid(block_diag_linear(x, w_i, b_i))
    log_a = -LRU_C * r * jax.nn.softplus(-lam.astype(jnp.float32))
    a = jnp.exp(log_a)
    mult = jnp.sqrt(-jnp.expm1(2.0 * log_a))
    u = mult * (i * x).astype(jnp.float32)

    def combine(left, right):
        a_l, b_l = left
        a_r, b_r = right
        return a_l * a_r, a_r * b_l + b_r

    _, h = lax.associative_scan(combine, (a, u), axis=1)
    return h.astype(x.dtype)


def forgetting_attention(q, k, v, log_f):
    B, S, H, Dh = q.shape
    F = jnp.cumsum(log_f, axis=1).transpose(0, 2, 1)
    qh = (q * (Dh ** -0.5)).transpose(0, 2, 1, 3)
    kh = k.transpose(0, 2, 1, 3)
    vh = v.transpose(0, 2, 1, 3)
    outs = []
    for blk in range(S // Q_BLOCK):
        q0 = blk * Q_BLOCK
        q1 = q0 + Q_BLOCK
        s = jnp.einsum('bhqd,bhkd->bhqk', qh[:, :, q0:q1], kh[:, :, :q1],
                       preferred_element_type=jnp.float32)
        s = s + F[:, :, q0:q1, None] - F[:, :, None, :q1]
        mask = (q0 + jnp.arange(Q_BLOCK))[:, None] >= jnp.arange(q1)[None, :]
        s = jnp.where(mask, s, -jnp.inf)
        p = jax.nn.softmax(s, axis=-1)
        outs.append(jnp.einsum('bhqk,bhkd->bhqd', p.astype(vh.dtype), vh[:, :, :q1]))
    o = jnp.concatenate(outs, axis=2)
    return o.transpose(0, 2, 1, 3).reshape(B, S, H * Dh)


def _fwd_setup_inputs(seed: int = 0) -> dict:
    key = jax.random.key(seed)
    ks = iter(jax.random.split(key, 40))
    nrm = lambda shape, s: jax.random.normal(next(ks), shape, jnp.float32) * s
    L = DEPTH
    a0 = jax.random.uniform(next(ks), (L, D_RNN), jnp.float32, 0.9, 0.999)
    return {
        "x": nrm((BATCH, SEQ, D_MODEL), 1.0),
        "c": nrm((BATCH, D_MODEL), 1.0),
        "w_ada": nrm((L, D_MODEL, N_MOD * D_MODEL), D_MODEL ** -0.5),
        "b_ada": nrm((L, N_MOD * D_MODEL), 0.01),
        "g_norm1": 1.0 + nrm((L, D_MODEL), 0.02),
        "w_in": nrm((L, D_MODEL, D_IN), D_MODEL ** -0.5),
        "w_rnn_conv": nrm((L, RNN_CONV_WIDTH, D_RNN), RNN_CONV_WIDTH ** -0.5),
        "b_rnn_conv": nrm((L, D_RNN), 0.01),
        "w_lru_a": nrm((L, N_RNN_BLOCKS, RNN_BLOCK, RNN_BLOCK), RNN_BLOCK ** -0.5),
        "b_lru_a": nrm((L, D_RNN), 0.01),
        "w_lru_i": nrm((L, N_RNN_BLOCKS, RNN_BLOCK, RNN_BLOCK), RNN_BLOCK ** -0.5),
        "b_lru_i": nrm((L, D_RNN), 0.01),
        "lru_lambda": jnp.log(a0) - jnp.log1p(-a0),
        "b_fgate": 3.0 + nrm((L, N_HEADS), 0.1),
        "w_proj_rnn": nrm((L, D_RNN, D_MODEL), D_RNN ** -0.5),
        "w_proj_attn": nrm((L, D_ATTN, D_MODEL), D_ATTN ** -0.5),
        "w_out": nrm((L, D_MODEL, D_MODEL), D_MODEL ** -0.5),
        "g_norm2": 1.0 + nrm((L, D_MODEL), 0.02),
        "w_ffn_up": nrm((L, D_MODEL, 2 * D_FF), D_MODEL ** -0.5),
        "w_ffn_conv": nrm((L, FFN_CONV_WIDTH, D_FF), FFN_CONV_WIDTH ** -0.5),
        "b_ffn_conv": nrm((L, D_FF), 0.01),
        "w_ffn_down": nrm((L, D_FF, D_MODEL), D_FF ** -0.5),
        "w_ada_final": nrm((D_MODEL, 2 * D_MODEL), D_MODEL ** -0.5),
        "b_ada_final": nrm((2 * D_MODEL,), 0.01),
        "g_final": 1.0 + nrm((D_MODEL,), 0.02),
    }


def _fwd_reference(x, c, w_ada, b_ada, g_norm1, w_in, w_rnn_conv, b_rnn_conv, w_lru_a, b_lru_a,
              w_lru_i, b_lru_i, lru_lambda, b_fgate, w_proj_rnn, w_proj_attn, w_out, g_norm2,
              w_ffn_up, w_ffn_conv, b_ffn_conv, w_ffn_down, w_ada_final, b_ada_final, g_final):
    B, S, _ = x.shape
    c_act = jax.nn.silu(c)
    split_idx = [int(v) for v in np.cumsum(SPLIT_SIZES)[:-1]]
    for l in range(DEPTH):
        mod = c_act @ w_ada[l] + b_ada[l]
        shift1, scale1, gate1, shift2, scale2, gate2 = jnp.split(mod, N_MOD, axis=-1)

        h = modulate(rmsnorm(x, g_norm1[l]), shift1, scale1)
        z = h @ w_in[l]
        xr, gr, q, k, v, fl, mg = jnp.split(z, split_idx, axis=-1)

        xr = causal_dwconv(xr, w_rnn_conv[l], b_rnn_conv[l])
        hr = rg_lru(xr, w_lru_a[l], b_lru_a[l], w_lru_i[l], b_lru_i[l], lru_lambda[l])
        y_rnn = jax.nn.gelu(gr, approximate=True) * hr

        log_f = jax.nn.log_sigmoid(fl.astype(jnp.float32) + b_fgate[l].astype(jnp.float32))
        y_attn = forgetting_attention(q.reshape(B, S, N_HEADS, HEAD_DIM),
                                      k.reshape(B, S, N_HEADS, HEAD_DIM),
                                      v.reshape(B, S, N_HEADS, HEAD_DIM), log_f)

        g_r, g_a = jnp.split(jax.nn.sigmoid(mg), 2, axis=-1)
        merged = g_r * (y_rnn @ w_proj_rnn[l]) + g_a * (y_attn @ w_proj_attn[l])
        x = x + gate1[:, None, :] * (merged @ w_out[l])

        h = modulate(rmsnorm(x, g_norm2[l]), shift2, scale2)
        gf, uf = jnp.split(h @ w_ffn_up[l], 2, axis=-1)
        gf = causal_dwconv(gf, w_ffn_conv[l], b_ffn_conv[l])
        y_ffn = (jax.nn.gelu(gf, approximate=True) * uf) @ w_ffn_down[l]
        x = x + gate2[:, None, :] * y_ffn

    shift_f, scale_f = jnp.split(c_act @ w_ada_final + b_ada_final, 2, axis=-1)
    return modulate(rmsnorm(x, g_final), shift_f, scale_f)


import jax as _jax
import jax.numpy as _jnp

TWIN_FORMAT = 'train_step'
FWD_PARAMS = ['x', 'c', 'w_ada', 'b_ada', 'g_norm1', 'w_in', 'w_rnn_conv', 'b_rnn_conv', 'w_lru_a', 'b_lru_a', 'w_lru_i', 'b_lru_i', 'lru_lambda', 'b_fgate', 'w_proj_rnn', 'w_proj_attn', 'w_out', 'g_norm2', 'w_ffn_up', 'w_ffn_conv', 'b_ffn_conv', 'w_ffn_down', 'w_ada_final', 'b_ada_final', 'g_final']
TWIN_WEIGHTS = ['w_ada', 'b_ada', 'g_norm1', 'w_in', 'w_rnn_conv', 'b_rnn_conv', 'w_lru_a', 'b_lru_a', 'w_lru_i', 'b_lru_i', 'lru_lambda', 'b_fgate', 'w_proj_rnn', 'w_proj_attn', 'w_out', 'g_norm2', 'w_ffn_up', 'w_ffn_conv', 'b_ffn_conv', 'w_ffn_down', 'w_ada_final', 'b_ada_final', 'g_final']
TWIN_DIFF_INPUT = 'x'
TWIN_INPUTS = ['x', 'c', 'w_ada', 'b_ada', 'g_norm1', 'w_in', 'w_rnn_conv', 'b_rnn_conv', 'w_lru_a', 'b_lru_a', 'w_lru_i', 'b_lru_i', 'lru_lambda', 'b_fgate', 'w_proj_rnn', 'w_proj_attn', 'w_out', 'g_norm2', 'w_ffn_up', 'w_ffn_conv', 'b_ffn_conv', 'w_ffn_down', 'w_ada_final', 'b_ada_final', 'g_final', 'loss_target', 'm_w_ada', 'm_b_ada', 'm_g_norm1', 'm_w_in', 'm_w_rnn_conv', 'm_b_rnn_conv', 'm_w_lru_a', 'm_b_lru_a', 'm_w_lru_i', 'm_b_lru_i', 'm_lru_lambda', 'm_b_fgate', 'm_w_proj_rnn', 'm_w_proj_attn', 'm_w_out', 'm_g_norm2', 'm_w_ffn_up', 'm_w_ffn_conv', 'm_b_ffn_conv', 'm_w_ffn_down', 'm_w_ada_final', 'm_b_ada_final', 'm_g_final', 'v_w_ada', 'v_b_ada', 'v_g_norm1', 'v_w_in', 'v_w_rnn_conv', 'v_b_rnn_conv', 'v_w_lru_a', 'v_b_lru_a', 'v_w_lru_i', 'v_b_lru_i', 'v_lru_lambda', 'v_b_fgate', 'v_w_proj_rnn', 'v_w_proj_attn', 'v_w_out', 'v_g_norm2', 'v_w_ffn_up', 'v_w_ffn_conv', 'v_b_ffn_conv', 'v_w_ffn_down', 'v_w_ada_final', 'v_b_ada_final', 'v_g_final']
TWIN_OUTPUTS = ['loss', 'grad_x', 'grad_w_ada', 'grad_b_ada', 'grad_g_norm1', 'grad_w_in', 'grad_w_rnn_conv', 'grad_b_rnn_conv', 'grad_w_lru_a', 'grad_b_lru_a', 'grad_w_lru_i', 'grad_b_lru_i', 'grad_lru_lambda', 'grad_b_fgate', 'grad_w_proj_rnn', 'grad_w_proj_attn', 'grad_w_out', 'grad_g_norm2', 'grad_w_ffn_up', 'grad_w_ffn_conv', 'grad_b_ffn_conv', 'grad_w_ffn_down', 'grad_w_ada_final', 'grad_b_ada_final', 'grad_g_final', 'delta_w_ada', 'delta_b_ada', 'delta_g_norm1', 'delta_w_in', 'delta_w_rnn_conv', 'delta_b_rnn_conv', 'delta_w_lru_a', 'delta_b_lru_a', 'delta_w_lru_i', 'delta_b_lru_i', 'delta_lru_lambda', 'delta_b_fgate', 'delta_w_proj_rnn', 'delta_w_proj_attn', 'delta_w_out', 'delta_g_norm2', 'delta_w_ffn_up', 'delta_w_ffn_conv', 'delta_b_ffn_conv', 'delta_w_ffn_down', 'delta_w_ada_final', 'delta_b_ada_final', 'delta_g_final', 'new_m_w_ada', 'new_m_b_ada', 'new_m_g_norm1', 'new_m_w_in', 'new_m_w_rnn_conv', 'new_m_b_rnn_conv', 'new_m_w_lru_a', 'new_m_b_lru_a', 'new_m_w_lru_i', 'new_m_b_lru_i', 'new_m_lru_lambda', 'new_m_b_fgate', 'new_m_w_proj_rnn', 'new_m_w_proj_attn', 'new_m_w_out', 'new_m_g_norm2', 'new_m_w_ffn_up', 'new_m_w_ffn_conv', 'new_m_b_ffn_conv', 'new_m_w_ffn_down', 'new_m_w_ada_final', 'new_m_b_ada_final', 'new_m_g_final', 'new_v_w_ada', 'new_v_b_ada', 'new_v_g_norm1', 'new_v_w_in', 'new_v_w_rnn_conv', 'new_v_b_rnn_conv', 'new_v_w_lru_a', 'new_v_b_lru_a', 'new_v_w_lru_i', 'new_v_b_lru_i', 'new_v_lru_lambda', 'new_v_b_fgate', 'new_v_w_proj_rnn', 'new_v_w_proj_attn', 'new_v_w_out', 'new_v_g_norm2', 'new_v_w_ffn_up', 'new_v_w_ffn_conv', 'new_v_b_ffn_conv', 'new_v_w_ffn_down', 'new_v_w_ada_final', 'new_v_b_ada_final', 'new_v_g_final']
TWIN_LEAF_KINDS = {'loss': 'loss', 'grad_x': 'grad_x', 'grad_w_ada': 'grad_w', 'grad_b_ada': 'grad_w', 'grad_g_norm1': 'grad_w', 'grad_w_in': 'grad_w', 'grad_w_rnn_conv': 'grad_w', 'grad_b_rnn_conv': 'grad_w', 'grad_w_lru_a': 'grad_w', 'grad_b_lru_a': 'grad_w', 'grad_w_lru_i': 'grad_w', 'grad_b_lru_i': 'grad_w', 'grad_lru_lambda': 'grad_w', 'grad_b_fgate': 'grad_w', 'grad_w_proj_rnn': 'grad_w', 'grad_w_proj_attn': 'grad_w', 'grad_w_out': 'grad_w', 'grad_g_norm2': 'grad_w', 'grad_w_ffn_up': 'grad_w', 'grad_w_ffn_conv': 'grad_w', 'grad_b_ffn_conv': 'grad_w', 'grad_w_ffn_down': 'grad_w', 'grad_w_ada_final': 'grad_w', 'grad_b_ada_final': 'grad_w', 'grad_g_final': 'grad_w', 'delta_w_ada': 'delta_w', 'delta_b_ada': 'delta_w', 'delta_g_norm1': 'delta_w', 'delta_w_in': 'delta_w', 'delta_w_rnn_conv': 'delta_w', 'delta_b_rnn_conv': 'delta_w', 'delta_w_lru_a': 'delta_w', 'delta_b_lru_a': 'delta_w', 'delta_w_lru_i': 'delta_w', 'delta_b_lru_i': 'delta_w', 'delta_lru_lambda': 'delta_w', 'delta_b_fgate': 'delta_w', 'delta_w_proj_rnn': 'delta_w', 'delta_w_proj_attn': 'delta_w', 'delta_w_out': 'delta_w', 'delta_g_norm2': 'delta_w', 'delta_w_ffn_up': 'delta_w', 'delta_w_ffn_conv': 'delta_w', 'delta_b_ffn_conv': 'delta_w', 'delta_w_ffn_down': 'delta_w', 'delta_w_ada_final': 'delta_w', 'delta_b_ada_final': 'delta_w', 'delta_g_final': 'delta_w', 'new_m_w_ada': 'new_m', 'new_m_b_ada': 'new_m', 'new_m_g_norm1': 'new_m', 'new_m_w_in': 'new_m', 'new_m_w_rnn_conv': 'new_m', 'new_m_b_rnn_conv': 'new_m', 'new_m_w_lru_a': 'new_m', 'new_m_b_lru_a': 'new_m', 'new_m_w_lru_i': 'new_m', 'new_m_b_lru_i': 'new_m', 'new_m_lru_lambda': 'new_m', 'new_m_b_fgate': 'new_m', 'new_m_w_proj_rnn': 'new_m', 'new_m_w_proj_attn': 'new_m', 'new_m_w_out': 'new_m', 'new_m_g_norm2': 'new_m', 'new_m_w_ffn_up': 'new_m', 'new_m_w_ffn_conv': 'new_m', 'new_m_b_ffn_conv': 'new_m', 'new_m_w_ffn_down': 'new_m', 'new_m_w_ada_final': 'new_m', 'new_m_b_ada_final': 'new_m', 'new_m_g_final': 'new_m', 'new_v_w_ada': 'new_v', 'new_v_b_ada': 'new_v', 'new_v_g_norm1': 'new_v', 'new_v_w_in': 'new_v', 'new_v_w_rnn_conv': 'new_v', 'new_v_b_rnn_conv': 'new_v', 'new_v_w_lru_a': 'new_v', 'new_v_b_lru_a': 'new_v', 'new_v_w_lru_i': 'new_v', 'new_v_b_lru_i': 'new_v', 'new_v_lru_lambda': 'new_v', 'new_v_b_fgate': 'new_v', 'new_v_w_proj_rnn': 'new_v', 'new_v_w_proj_attn': 'new_v', 'new_v_w_out': 'new_v', 'new_v_g_norm2': 'new_v', 'new_v_w_ffn_up': 'new_v', 'new_v_w_ffn_conv': 'new_v', 'new_v_b_ffn_conv': 'new_v', 'new_v_w_ffn_down': 'new_v', 'new_v_w_ada_final': 'new_v', 'new_v_b_ada_final': 'new_v', 'new_v_g_final': 'new_v'}


def _forward(args):
    return _fwd_reference(*[args[k] for k in FWD_PARAMS])


def _output_shape():
    out = _jax.eval_shape(lambda: _forward(_fwd_setup_inputs(0)))
    return out.shape, out.dtype

N_MICROBATCH = 1
ADAM_LR = 0.001
ADAM_B1 = 0.9
ADAM_B2 = 0.999
ADAM_EPS = 1e-08
ADAM_WD = 0.01
ADAM_STEP = 10
PER_EXAMPLE_BATCH_AXIS = {'x': 0, 'c': 0, 'loss_target': 0}
SHARED_INPUTS = []
_WEIGHT_DTYPES = {'w_ada': _jnp.float32, 'b_ada': _jnp.float32, 'g_norm1': _jnp.float32, 'w_in': _jnp.float32, 'w_rnn_conv': _jnp.float32, 'b_rnn_conv': _jnp.float32, 'w_lru_a': _jnp.float32, 'b_lru_a': _jnp.float32, 'w_lru_i': _jnp.float32, 'b_lru_i': _jnp.float32, 'lru_lambda': _jnp.float32, 'b_fgate': _jnp.float32, 'w_proj_rnn': _jnp.float32, 'w_proj_attn': _jnp.float32, 'w_out': _jnp.float32, 'g_norm2': _jnp.float32, 'w_ffn_up': _jnp.float32, 'w_ffn_conv': _jnp.float32, 'b_ffn_conv': _jnp.float32, 'w_ffn_down': _jnp.float32, 'w_ada_final': _jnp.float32, 'b_ada_final': _jnp.float32, 'g_final': _jnp.float32}
MOMENT_SCALE = {'w_ada': 9.033893e+00, 'b_ada': 1.547247e+01, 'g_norm1': 7.554071e-01, 'w_in': 3.717643e+00, 'w_rnn_conv': 6.078752e+00, 'b_rnn_conv': 1.027433e+01, 'w_lru_a': 7.327864e-01, 'b_lru_a': 8.930368e-01, 'w_lru_i': 1.639779e+00, 'b_lru_i': 2.032161e+00, 'lru_lambda': 2.299758e+00, 'b_fgate': 8.736790e-01, 'w_proj_rnn': 6.262369e+00, 'w_proj_attn': 5.584265e+00, 'w_out': 8.407691e+00, 'g_norm2': 2.548593e+00, 'w_ffn_up': 2.687340e+00, 'w_ffn_conv': 2.636007e+00, 'b_ffn_conv': 3.077024e+00, 'w_ffn_down': 5.494376e+00, 'w_ada_final': 2.311618e+01, 'b_ada_final': 5.454143e+01, 'g_final': 1.122503e+02}


def _to_microbatches(a, axis):
    t = _jnp.moveaxis(a, axis, 0)
    t = t.reshape((N_MICROBATCH, t.shape[0] // N_MICROBATCH) + t.shape[1:])
    return _jnp.moveaxis(t, 1, axis + 1)


def setup_inputs(seed: int = 0) -> dict:
    inp = _fwd_setup_inputs(seed)
    key = _jax.random.fold_in(_jax.random.key(seed), 7919)
    shape, _ = _output_shape()
    out = dict(inp)
    out["loss_target"] = _jax.random.normal(_jax.random.fold_in(key, 0), shape, _jnp.float32)
    for i, name in enumerate(TWIN_WEIGHTS):
        w = inp[name].astype(_jnp.float32)
        if MOMENT_SCALE is None:
            s = _jnp.sqrt(_jnp.mean(_jnp.square(w)) + 1e-30)
        else:
            s = MOMENT_SCALE[name]
        km, kv = _jax.random.split(_jax.random.fold_in(key, i + 1))
        out[name] = w
        out["m_" + name] = s * _jax.random.normal(km, w.shape, _jnp.float32)
        out["v_" + name] = (s * s) * _jax.random.uniform(kv, w.shape, _jnp.float32, 0.5, 1.5)
    if N_MICROBATCH > 1:
        for name, axis in PER_EXAMPLE_BATCH_AXIS.items():
            out[name] = _to_microbatches(out[name], axis)
    return {'x': out['x'], 'c': out['c'], 'w_ada': out['w_ada'], 'b_ada': out['b_ada'], 'g_norm1': out['g_norm1'], 'w_in': out['w_in'], 'w_rnn_conv': out['w_rnn_conv'], 'b_rnn_conv': out['b_rnn_conv'], 'w_lru_a': out['w_lru_a'], 'b_lru_a': out['b_lru_a'], 'w_lru_i': out['w_lru_i'], 'b_lru_i': out['b_lru_i'], 'lru_lambda': out['lru_lambda'], 'b_fgate': out['b_fgate'], 'w_proj_rnn': out['w_proj_rnn'], 'w_proj_attn': out['w_proj_attn'], 'w_out': out['w_out'], 'g_norm2': out['g_norm2'], 'w_ffn_up': out['w_ffn_up'], 'w_ffn_conv': out['w_ffn_conv'], 'b_ffn_conv': out['b_ffn_conv'], 'w_ffn_down': out['w_ffn_down'], 'w_ada_final': out['w_ada_final'], 'b_ada_final': out['b_ada_final'], 'g_final': out['g_final'], 'loss_target': out['loss_target'], 'm_w_ada': out['m_w_ada'], 'm_b_ada': out['m_b_ada'], 'm_g_norm1': out['m_g_norm1'], 'm_w_in': out['m_w_in'], 'm_w_rnn_conv': out['m_w_rnn_conv'], 'm_b_rnn_conv': out['m_b_rnn_conv'], 'm_w_lru_a': out['m_w_lru_a'], 'm_b_lru_a': out['m_b_lru_a'], 'm_w_lru_i': out['m_w_lru_i'], 'm_b_lru_i': out['m_b_lru_i'], 'm_lru_lambda': out['m_lru_lambda'], 'm_b_fgate': out['m_b_fgate'], 'm_w_proj_rnn': out['m_w_proj_rnn'], 'm_w_proj_attn': out['m_w_proj_attn'], 'm_w_out': out['m_w_out'], 'm_g_norm2': out['m_g_norm2'], 'm_w_ffn_up': out['m_w_ffn_up'], 'm_w_ffn_conv': out['m_w_ffn_conv'], 'm_b_ffn_conv': out['m_b_ffn_conv'], 'm_w_ffn_down': out['m_w_ffn_down'], 'm_w_ada_final': out['m_w_ada_final'], 'm_b_ada_final': out['m_b_ada_final'], 'm_g_final': out['m_g_final'], 'v_w_ada': out['v_w_ada'], 'v_b_ada': out['v_b_ada'], 'v_g_norm1': out['v_g_norm1'], 'v_w_in': out['v_w_in'], 'v_w_rnn_conv': out['v_w_rnn_conv'], 'v_b_rnn_conv': out['v_b_rnn_conv'], 'v_w_lru_a': out['v_w_lru_a'], 'v_b_lru_a': out['v_b_lru_a'], 'v_w_lru_i': out['v_w_lru_i'], 'v_b_lru_i': out['v_b_lru_i'], 'v_lru_lambda': out['v_lru_lambda'], 'v_b_fgate': out['v_b_fgate'], 'v_w_proj_rnn': out['v_w_proj_rnn'], 'v_w_proj_attn': out['v_w_proj_attn'], 'v_w_out': out['v_w_out'], 'v_g_norm2': out['v_g_norm2'], 'v_w_ffn_up': out['v_w_ffn_up'], 'v_w_ffn_conv': out['v_w_ffn_conv'], 'v_b_ffn_conv': out['v_b_ffn_conv'], 'v_w_ffn_down': out['v_w_ffn_down'], 'v_w_ada_final': out['v_w_ada_final'], 'v_b_ada_final': out['v_b_ada_final'], 'v_g_final': out['v_g_final']}


def _loss(weights, diff, rest, loss_target):
    with _jax.named_scope("forward"):
        args = {**rest, TWIN_DIFF_INPUT: diff, **{k: w.astype(_WEIGHT_DTYPES[k]) for k, w in weights.items()}}
        y = _forward(args)
    with _jax.named_scope("loss_head"):
        err = _jnp.square(y.astype(_jnp.float32) - loss_target)
        return 0.5 * _jnp.sum(_jnp.mean(err, axis=-1)) if err.ndim else 0.5 * err


def _adamw(w, g, m, v):
    m = ADAM_B1 * m + (1.0 - ADAM_B1) * g
    v = ADAM_B2 * v + (1.0 - ADAM_B2) * _jnp.square(g)
    m_hat = m / (1.0 - ADAM_B1 ** ADAM_STEP)
    v_hat = v / (1.0 - ADAM_B2 ** ADAM_STEP)
    delta = -ADAM_LR * (m_hat / (_jnp.sqrt(v_hat) + ADAM_EPS) + ADAM_WD * w)
    return delta, m, v


def reference(x, c, w_ada, b_ada, g_norm1, w_in, w_rnn_conv, b_rnn_conv, w_lru_a, b_lru_a, w_lru_i, b_lru_i, lru_lambda, b_fgate, w_proj_rnn, w_proj_attn, w_out, g_norm2, w_ffn_up, w_ffn_conv, b_ffn_conv, w_ffn_down, w_ada_final, b_ada_final, g_final, loss_target, m_w_ada, m_b_ada, m_g_norm1, m_w_in, m_w_rnn_conv, m_b_rnn_conv, m_w_lru_a, m_b_lru_a, m_w_lru_i, m_b_lru_i, m_lru_lambda, m_b_fgate, m_w_proj_rnn, m_w_proj_attn, m_w_out, m_g_norm2, m_w_ffn_up, m_w_ffn_conv, m_b_ffn_conv, m_w_ffn_down, m_w_ada_final, m_b_ada_final, m_g_final, v_w_ada, v_b_ada, v_g_norm1, v_w_in, v_w_rnn_conv, v_b_rnn_conv, v_w_lru_a, v_b_lru_a, v_w_lru_i, v_b_lru_i, v_lru_lambda, v_b_fgate, v_w_proj_rnn, v_w_proj_attn, v_w_out, v_g_norm2, v_w_ffn_up, v_w_ffn_conv, v_b_ffn_conv, v_w_ffn_down, v_w_ada_final, v_b_ada_final, v_g_final):
    given = dict(x=x, c=c, w_ada=w_ada, b_ada=b_ada, g_norm1=g_norm1, w_in=w_in, w_rnn_conv=w_rnn_conv, b_rnn_conv=b_rnn_conv, w_lru_a=w_lru_a, b_lru_a=b_lru_a, w_lru_i=w_lru_i, b_lru_i=b_lru_i, lru_lambda=lru_lambda, b_fgate=b_fgate, w_proj_rnn=w_proj_rnn, w_proj_attn=w_proj_attn, w_out=w_out, g_norm2=g_norm2, w_ffn_up=w_ffn_up, w_ffn_conv=w_ffn_conv, b_ffn_conv=b_ffn_conv, w_ffn_down=w_ffn_down, w_ada_final=w_ada_final, b_ada_final=b_ada_final, g_final=g_final, loss_target=loss_target, m_w_ada=m_w_ada, m_b_ada=m_b_ada, m_g_norm1=m_g_norm1, m_w_in=m_w_in, m_w_rnn_conv=m_w_rnn_conv, m_b_rnn_conv=m_b_rnn_conv, m_w_lru_a=m_w_lru_a, m_b_lru_a=m_b_lru_a, m_w_lru_i=m_w_lru_i, m_b_lru_i=m_b_lru_i, m_lru_lambda=m_lru_lambda, m_b_fgate=m_b_fgate, m_w_proj_rnn=m_w_proj_rnn, m_w_proj_attn=m_w_proj_attn, m_w_out=m_w_out, m_g_norm2=m_g_norm2, m_w_ffn_up=m_w_ffn_up, m_w_ffn_conv=m_w_ffn_conv, m_b_ffn_conv=m_b_ffn_conv, m_w_ffn_down=m_w_ffn_down, m_w_ada_final=m_w_ada_final, m_b_ada_final=m_b_ada_final, m_g_final=m_g_final, v_w_ada=v_w_ada, v_b_ada=v_b_ada, v_g_norm1=v_g_norm1, v_w_in=v_w_in, v_w_rnn_conv=v_w_rnn_conv, v_b_rnn_conv=v_b_rnn_conv, v_w_lru_a=v_w_lru_a, v_b_lru_a=v_b_lru_a, v_w_lru_i=v_w_lru_i, v_b_lru_i=v_b_lru_i, v_lru_lambda=v_lru_lambda, v_b_fgate=v_b_fgate, v_w_proj_rnn=v_w_proj_rnn, v_w_proj_attn=v_w_proj_attn, v_w_out=v_w_out, v_g_norm2=v_g_norm2, v_w_ffn_up=v_w_ffn_up, v_w_ffn_conv=v_w_ffn_conv, v_b_ffn_conv=v_b_ffn_conv, v_w_ffn_down=v_w_ffn_down, v_w_ada_final=v_w_ada_final, v_b_ada_final=v_b_ada_final, v_g_final=v_g_final)
    weights = {n: given[n] for n in TWIN_WEIGHTS}
    shared = {n: given[n] for n in SHARED_INPUTS}
    per_example = {n: given[n] for n in ['x', 'c']}
    grad_fn = _jax.value_and_grad(_loss, argnums=(0, 1))

    def one_microbatch(ex, loss_target):
        ex = dict(ex)
        diff = ex.pop(TWIN_DIFF_INPUT)
        return grad_fn(weights, diff, {**shared, **ex}, loss_target)

    if N_MICROBATCH == 1:
        loss, (grad_w, grad_x) = one_microbatch(per_example, given["loss_target"])
    else:
        def body(carry, xs):
            loss_sum, grad_sum = carry
            l_k, (gw_k, gx_k) = one_microbatch(xs[0], xs[1])
            with _jax.named_scope("update"):
                return (loss_sum + l_k, _jax.tree.map(_jnp.add, grad_sum, gw_k)), gx_k

        init = (_jnp.zeros((), _jnp.float32), _jax.tree.map(_jnp.zeros_like, weights))
        (loss, grad_w), grad_x = _jax.lax.scan(body, init, (per_example, given["loss_target"]))
    with _jax.named_scope("update"):
        delta_w, new_m, new_v = {}, {}, {}
        for n in TWIN_WEIGHTS:
            delta_w[n], new_m[n], new_v[n] = _adamw(weights[n], grad_w[n], given["m_" + n], given["v_" + n])
    return (loss, grad_x, *[grad_w[n] for n in TWIN_WEIGHTS], *[delta_w[n] for n in TWIN_WEIGHTS],
            *[new_m[n] for n in TWIN_WEIGHTS], *[new_v[n] for n in TWIN_WEIGHTS])
```

```python
import jax
import jax.numpy as jnp
from jax import lax
from jax.experimental import pallas as pl
from jax.experimental.pallas import tpu as pltpu

F32, BF16 = jnp.float32, jnp.bfloat16
D_MODEL, D_RNN, N_BLK, RNN_BLK = 1024, 1280, 10, 128
N_HEADS, HEAD_DIM, D_FF = 8, 128, 3072
N_DEV = 8
NZ = 8192
QKV0, XG0, FL0, MG0 = 0, 3072, 5632, 6144
RMS_EPS = 1e-6
LRU_C = 8.0
ADAM_LR, ADAM_B1, ADAM_B2, ADAM_EPS, ADAM_WD, ADAM_STEP = 0.001, 0.9, 0.999, 1e-08, 0.01, 10
VMEM_LIMIT = 48 << 20
SCAN_CHUNK = 256
ATTN_TQ = 256
CUM_BLK = 512
MESH = pl.DeviceIdType.MESH
ANY = pl.BlockSpec(memory_space=pl.ANY)


def _call(body, **kw):
    return pl.pallas_call(body, **kw)


def _params():
    return pltpu.CompilerParams(vmem_limit_bytes=VMEM_LIMIT)


def _sigmoid(x):
    return 1.0 / (1.0 + jnp.exp(-x))


def _log1p(e):
    u = 1.0 + e
    d = u - 1.0
    return jnp.where(d == 0.0, e, jnp.log(u) * (e / jnp.where(d == 0.0, 1.0, d)))


def _softplus(x):
    return jnp.maximum(x, 0.0) + _log1p(jnp.exp(-jnp.abs(x)))


_GELU_C = 0.7978845608028654


def _gelu(x):
    t = jnp.tanh(_GELU_C * (x + 0.044715 * x * x * x))
    return 0.5 * x * (1.0 + t)


def _gelu_and_grad(x):
    t = jnp.tanh(_GELU_C * (x + 0.044715 * x * x * x))
    g = 0.5 * x * (1.0 + t)
    dg = 0.5 * (1.0 + t) + 0.5 * x * (1.0 - t * t) * _GELU_C * (1.0 + 3.0 * 0.044715 * x * x)
    return g, dg


def _acc(ref, idx, val, first):
    @pl.when(first)
    def _():
        ref[idx] = val

    @pl.when(jnp.logical_not(first))
    def _():
        ref[idx] = ref[idx] + val


def _split3(x):
    hi = x.astype(BF16)
    r1 = x - hi.astype(F32)
    mid = r1.astype(BF16)
    lo = (r1 - mid.astype(F32)).astype(BF16)
    return hi, mid, lo


def _mm(a, b, *, name, ta=False, tb=False, out_dtype=F32, tm, tn, tk, bias=None, silu_a=False):
    m, k = (a.shape[1], a.shape[0]) if ta else a.shape
    n = b.shape[0] if tb else b.shape[1]
    assert m % tm == 0 and n % tn == 0 and k % tk == 0, (name, m, n, k)
    nk = k // tk
    dn = (((0 if ta else 1,), (1 if tb else 0,)), ((), ()))
    use_acc = nk > 1 and out_dtype != F32

    def body(*refs):
        a_ref, b_ref = refs[0], refs[1]
        bias_ref = refs[2] if bias is not None else None
        o_ref = refs[3 if bias is not None else 2]
        av = a_ref[...]
        if silu_a:
            av = av * _sigmoid(av)
        p = lax.dot_general(av.astype(BF16), b_ref[...].astype(BF16), dn, preferred_element_type=F32)
        if bias is not None:
            p = p + bias_ref[...]
        if nk == 1:
            o_ref[...] = p.astype(out_dtype)
            return
        kk = pl.program_id(2)
        acc = refs[-1] if use_acc else o_ref

        @pl.when(kk == 0)
        def _():
            acc[...] = p

        @pl.when(kk > 0)
        def _():
            acc[...] = acc[...] + p

        if use_acc:
            @pl.when(kk == nk - 1)
            def _():
                o_ref[...] = acc[...].astype(out_dtype)

    a_spec = pl.BlockSpec((tk, tm), lambda i, j, q: (q, i)) if ta else pl.BlockSpec((tm, tk), lambda i, j, q: (i, q))
    b_spec = pl.BlockSpec((tn, tk), lambda i, j, q: (j, q)) if tb else pl.BlockSpec((tk, tn), lambda i, j, q: (q, j))
    in_specs, args = [a_spec, b_spec], [a, b]
    if bias is not None:
        in_specs.append(pl.BlockSpec((1, tn), lambda i, j, q: (0, j)))
        args.append(bias)
    return _call(
        body, name=name, grid=(m // tm, n // tn, nk), in_specs=in_specs,
        out_specs=pl.BlockSpec((tm, tn), lambda i, j, q: (i, j)),
        out_shape=jax.ShapeDtypeStruct((m, n), out_dtype),
        scratch_shapes=[pltpu.VMEM((tm, tn), F32)] if use_acc else [],
        compiler_params=_params(),
    )(*args)


def _exchange(arrs, bcast, name):
    n = len(arrs)

    def body(*refs):
        ins, outs = refs[:n], refs[n:2 * n]
        send_sems, recv_sems, loc_sems = refs[2 * n:]
        x, y, c = lax.axis_index("x"), lax.axis_index("y"), lax.axis_index("c")
        me = 4 * x + 2 * y + c
        local = []
        for a in range(n):
            src = ins[a] if bcast[a] else ins[a].at[me]
            cp = pltpu.make_async_copy(src, outs[a].at[me], loc_sems.at[a])
            cp.start()
            local.append(cp)
        remote = []
        for k in range(1, N_DEV):
            px = 1 - x if k & 4 else x
            py = 1 - y if k & 2 else y
            pc = 1 - c if k & 1 else c
            peer = 4 * px + 2 * py + pc
            for a in range(n):
                src = ins[a] if bcast[a] else ins[a].at[peer]
                cp = pltpu.make_async_remote_copy(
                    src_ref=src, dst_ref=outs[a].at[me],
                    send_sem=send_sems.at[a * (N_DEV - 1) + k - 1], recv_sem=recv_sems.at[a * (N_DEV - 1) + k - 1],
                    device_id=(px, py, pc), device_id_type=MESH)
                cp.start()
                remote.append(cp)
        for cp in remote:
            cp.wait()
        for cp in local:
            cp.wait()

    out_shape = [jax.ShapeDtypeStruct((N_DEV,) + (a.shape if bc else a.shape[1:]), a.dtype) for a, bc in zip(arrs, bcast)]
    return _call(
        body, name=name, in_specs=[ANY] * n, out_specs=[ANY] * n, out_shape=out_shape,
        scratch_shapes=[pltpu.SemaphoreType.DMA((n * (N_DEV - 1),)), pltpu.SemaphoreType.DMA((n * (N_DEV - 1),)),
                        pltpu.SemaphoreType.DMA((n,))],
        compiler_params=pltpu.CompilerParams(has_side_effects=True),
    )(*arrs)


def _tok_spec(ts):
    return pl.BlockSpec((1, ts, D_MODEL), lambda b, s: (b, s, 0))


def _seq_spec():
    return pl.BlockSpec((1, 1, D_MODEL), lambda b, s: (b, 0, 0))


def _vec_spec():
    return pl.BlockSpec((1, D_MODEL), lambda b, s: (0, 0))


def _norm_fwd(x, g, shift, scale, *, name, yprev=None, gate=None, ts=512):
    bsz, s, _ = x.shape
    res = yprev is not None

    def body(*refs):
        if res:
            x_ref, g_ref, sh_ref, sc_ref, y_ref, gt_ref, h_ref, xn_ref = refs
            xv = x_ref[0] + gt_ref[0] * y_ref[0]
            xn_ref[0] = xv
        else:
            x_ref, g_ref, sh_ref, sc_ref, h_ref = refs
            xv = x_ref[0]
        rstd = lax.rsqrt(jnp.mean(xv * xv, axis=-1, keepdims=True) + RMS_EPS)
        nv = xv * rstd * g_ref[...]
        h_ref[0] = (nv * (1.0 + sc_ref[0]) + sh_ref[0]).astype(BF16)

    in_specs = [_tok_spec(ts), _vec_spec(), _seq_spec(), _seq_spec()]
    args = [x, g, shift, scale]
    out_specs = [_tok_spec(ts)]
    out_shape = [jax.ShapeDtypeStruct(x.shape, BF16)]
    if res:
        in_specs += [_tok_spec(ts), _seq_spec()]
        args += [yprev, gate]
        out_specs.append(_tok_spec(ts))
        out_shape.append(jax.ShapeDtypeStruct(x.shape, F32))
    return _call(body, name=name, grid=(bsz, s // ts), in_specs=in_specs, out_specs=out_specs,
                 out_shape=out_shape, compiler_params=_params())(*args)


def _final(x1, yf, gate2, g, shift, scale, target, *, ts=512):
    bsz, s, _ = x1.shape
    ns = s // ts

    def body(x_ref, y_ref, gt_ref, g_ref, sh_ref, sc_ref, t_ref,
             lp_ref, dx_ref, dyf_ref, dsh_ref, dsc_ref, dgt_ref, dg_ref):
        b, si = pl.program_id(0), pl.program_id(1)
        yv, gt, gv = y_ref[0], gt_ref[0], g_ref[...]
        xv = x_ref[0] + gt * yv
        rstd = lax.rsqrt(jnp.mean(xv * xv, axis=-1, keepdims=True) + RMS_EPS)
        xhat = xv * rstd
        nv = xhat * gv
        sc1 = 1.0 + sc_ref[0]
        err = nv * sc1 + sh_ref[0] - t_ref[0]
        lp_ref[...] = jnp.zeros((1, 1, 8, 128), F32) + jnp.sum(err * err)
        dy = err * (1.0 / D_MODEL)
        dn = dy * sc1
        dxh = dn * gv
        dx = rstd * (dxh - xhat * jnp.mean(dxh * xhat, axis=-1, keepdims=True))
        dx_ref[0] = dx
        dyf_ref[0] = (dx * gt).astype(BF16)
        _acc(dsh_ref, 0, jnp.sum(dy, axis=0, keepdims=True), si == 0)
        _acc(dsc_ref, 0, jnp.sum(dy * nv, axis=0, keepdims=True), si == 0)
        _acc(dgt_ref, 0, jnp.sum(dx * yv, axis=0, keepdims=True), si == 0)
        _acc(dg_ref, slice(None), jnp.sum(dn * xhat, axis=0, keepdims=True), jnp.logical_and(b == 0, si == 0))

    seq_shape = jax.ShapeDtypeStruct((bsz, 1, D_MODEL), F32)
    return _call(
        body, name="final_loss", grid=(bsz, ns),
        in_specs=[_tok_spec(ts), _tok_spec(ts), _seq_spec(), _vec_spec(), _seq_spec(), _seq_spec(), _tok_spec(ts)],
        out_specs=[pl.BlockSpec((1, 1, 8, 128), lambda b, s: (b, s, 0, 0)), _tok_spec(ts), _tok_spec(ts),
                   _seq_spec(), _seq_spec(), _seq_spec(), _vec_spec()],
        out_shape=[jax.ShapeDtypeStruct((bsz, ns, 8, 128), F32), jax.ShapeDtypeStruct(x1.shape, F32),
                   jax.ShapeDtypeStruct(x1.shape, BF16), seq_shape, seq_shape, seq_shape,
                   jax.ShapeDtypeStruct((1, D_MODEL), F32)],
        compiler_params=_params(),
    )(x1, yf, gate2, g, shift, scale, target)


def _norm_bwd(x, dh, g, scale, dres, *, name, yprev=None, gate=None, ts=512):
    bsz, s, _ = x.shape
    res = yprev is not None

    def body(*refs):
        if res:
            (x_ref, dh_ref, g_ref, sc_ref, dr_ref, y_ref, gt_ref,
             dx_ref, dg_ref, dsh_ref, dsc_ref, dy_ref, dgt_ref) = refs
        else:
            x_ref, dh_ref, g_ref, sc_ref, dr_ref, dx_ref, dg_ref, dsh_ref, dsc_ref = refs
        b, si = pl.program_id(0), pl.program_id(1)
        xv, dhv, gv = x_ref[0], dh_ref[0], g_ref[...]
        rstd = lax.rsqrt(jnp.mean(xv * xv, axis=-1, keepdims=True) + RMS_EPS)
        xhat = xv * rstd
        dn = dhv * (1.0 + sc_ref[0])
        dxh = dn * gv
        dx = dr_ref[0] + rstd * (dxh - xhat * jnp.mean(dxh * xhat, axis=-1, keepdims=True))
        dx_ref[0] = dx
        _acc(dsh_ref, 0, jnp.sum(dhv, axis=0, keepdims=True), si == 0)
        _acc(dsc_ref, 0, jnp.sum(dhv * (xhat * gv), axis=0, keepdims=True), si == 0)
        _acc(dg_ref, slice(None), jnp.sum(dn * xhat, axis=0, keepdims=True), jnp.logical_and(b == 0, si == 0))
        if res:
            dy_ref[0] = (dx * gt_ref[0]).astype(BF16)
            _acc(dgt_ref, 0, jnp.sum(dx * y_ref[0], axis=0, keepdims=True), si == 0)

    seq_shape = jax.ShapeDtypeStruct((bsz, 1, D_MODEL), F32)
    in_specs = [_tok_spec(ts), _tok_spec(ts), _vec_spec(), _seq_spec(), _tok_spec(ts)]
    args = [x, dh, g, scale, dres]
    out_specs = [_tok_spec(ts), _vec_spec(), _seq_spec(), _seq_spec()]
    out_shape = [jax.ShapeDtypeStruct(x.shape, F32), jax.ShapeDtypeStruct((1, D_MODEL), F32), seq_shape, seq_shape]
    if res:
        in_specs += [_tok_spec(ts), _seq_spec()]
        args += [yprev, gate]
        out_specs += [_tok_spec(ts), _seq_spec()]
        out_shape += [jax.ShapeDtypeStruct(x.shape, BF16), seq_shape]
    return _call(body, name=name, grid=(bsz, s // ts), in_specs=in_specs, out_specs=out_specs,
                 out_shape=out_shape, compiler_params=_params())(*args)


def _lru_gates(xr, wa, ba, wi, bi, sp):
    xb = xr.astype(BF16)
    r = _sigmoid(jnp.dot(xb, wa, preferred_element_type=F32) + ba)
    i = _sigmoid(jnp.dot(xb, wi, preferred_element_type=F32) + bi)
    la = -LRU_C * r * sp
    a = jnp.exp(la)
    mult = jnp.sqrt(-jnp.tanh(la) * (a * a + 1.0))
    return r, i, a, mult


def _rnn_fwd(z, wc, bc, wa, ba, wi, bi, lam):
    bsz, s, _ = z.shape
    ch = min(SCAN_CHUNK, s)
    nch = s // ch

    def body(z_ref, wc_ref, bc_ref, wa_ref, ba_ref, wi_ref, bi_ref, lam_ref, h_ref, y_ref):
        wcv, bcv = wc_ref[...], bc_ref[...]
        wav, wiv = wa_ref[0].astype(BF16), wi_ref[0].astype(BF16)
        bav, biv = ba_ref[...], bi_ref[...]
        sp = _softplus(-lam_ref[...])
        row = lax.broadcasted_iota(jnp.int32, (ch, RNN_BLK), 0)

        def chunk(ci, carry):
            prev8, hc = carry
            st = pl.multiple_of(ci * ch, ch)
            blk = z_ref[0, pl.ds(st, ch), :]
            xp, gr = blk[:, :RNN_BLK], blk[:, RNN_BLK:]
            xe = jnp.concatenate([prev8, xp], axis=0)
            xr = bcv + wcv[3:4] * xp
            for j in range(1, 4):
                xr = xr + wcv[3 - j:4 - j] * pltpu.roll(xe, j, 0)[8:]
            _, i, a, mult = _lru_gates(xr, wav, bav, wiv, biv, sp)
            aa, uu = a, mult * (i * xr)
            k = 1
            while k < ch:
                m = row >= k
                uu = jnp.where(m, aa * pltpu.roll(uu, k, 0) + uu, uu)
                aa = jnp.where(m, aa * pltpu.roll(aa, k, 0), aa)
                k *= 2
            hh = uu + aa * hc
            h_ref[0, pl.ds(st, ch), :] = hh
            y_ref[0, pl.ds(st, ch), :] = (_gelu(gr) * hh).astype(BF16)
            return xp[ch - 8:], hh[ch - 1:]

        lax.fori_loop(0, nch, chunk, (jnp.zeros((8, RNN_BLK), F32), jnp.zeros((1, RNN_BLK), F32)))

    vec = pl.BlockSpec((1, RNN_BLK), lambda b, n: (0, n))
    mat = pl.BlockSpec((1, RNN_BLK, RNN_BLK), lambda b, n: (n, 0, 0))
    out = pl.BlockSpec((1, s, RNN_BLK), lambda b, n: (b, 0, n))
    return _call(
        body, name="rnn_fwd", grid=(bsz, N_BLK),
        in_specs=[pl.BlockSpec((1, s, 2 * RNN_BLK), lambda b, n: (b, 0, XG0 // (2 * RNN_BLK) + n)),
                  pl.BlockSpec((4, RNN_BLK), lambda b, n: (0, n)), vec, mat, vec, mat, vec, vec],
        out_specs=[out, out],
        out_shape=[jax.ShapeDtypeStruct((bsz, s, D_RNN), F32), jax.ShapeDtypeStruct((bsz, s, D_RNN), BF16)],
        compiler_params=_params(),
    )(z, wc, bc, wa, ba, wi, bi, lam)


def _rnn_bwd(z, h, dy, dz, wc, bc, wa, ba, wi, bi, lam):
    bsz, s, _ = z.shape
    ch = min(SCAN_CHUNK, s)
    nch = s // ch

    def body(z_ref, h_ref, dy_ref, dz_in, wc_ref, bc_ref, wa_ref, ba_ref, wi_ref, bi_ref, lam_ref,
             dz_ref, dwa_ref, dwi_ref, dsm_ref):
        del dz_in
        b = pl.program_id(1)
        wcv, bcv = wc_ref[...], bc_ref[...]
        wav, wiv = wa_ref[0].astype(BF16), wi_ref[0].astype(BF16)
        bav, biv = ba_ref[...], bi_ref[...]
        lamv = lam_ref[...]
        sp = _softplus(-lamv)
        row = lax.broadcasted_iota(jnp.int32, (ch, RNN_BLK), 0)

        @pl.when(b == 0)
        def _():
            dwa_ref[...] = jnp.zeros_like(dwa_ref)
            dwi_ref[...] = jnp.zeros_like(dwi_ref)
            dsm_ref[...] = jnp.zeros_like(dsm_ref)

        def chunk(k, carry):
            a_next, g_next, dxr_next8 = carry
            ci = nch - 1 - k
            st = pl.multiple_of(ci * ch, ch)
            pst = pl.multiple_of(jnp.maximum(st - 8, 0), 8)
            has_prev = (ci > 0).astype(F32)
            blk = z_ref[0, pl.ds(st, ch), :]
            xp, gr = blk[:, :RNN_BLK], blk[:, RNN_BLK:]
            xprev8 = z_ref[0, pl.ds(pst, 8), :][:, :RNN_BLK] * has_prev
            hprev8 = h_ref[0, pl.ds(pst, 8), :] * has_prev
            xe = jnp.concatenate([xprev8, xp], axis=0)
            xs = [xp] + [pltpu.roll(xe, j, 0)[8:] for j in range(1, 4)]
            xr = bcv
            for kk in range(4):
                xr = xr + wcv[kk:kk + 1] * xs[3 - kk]
            r, i, a, mult = _lru_gates(xr, wav, bav, wiv, biv, sp)
            hh = h_ref[0, pl.ds(st, ch), :]
            hm1 = pltpu.roll(jnp.concatenate([hprev8, hh], axis=0), 1, 0)[8:]
            dyv = dy_ref[0, pl.ds(st, ch), :]
            ge, dge = _gelu_and_grad(gr)
            dgr = dyv * hh * dge
            aa = jnp.where(row == ch - 1, a_next, pltpu.roll(a, ch - 1, 0))
            gg = dyv * ge
            kk = 1
            while kk < ch:
                m = row < ch - kk
                gg = jnp.where(m, aa * pltpu.roll(gg, ch - kk, 0) + gg, gg)
                aa = jnp.where(m, aa * pltpu.roll(aa, ch - kk, 0), aa)
                kk *= 2
            gg = gg + aa * g_next
            da = gg * hm1
            ix = i * xr
            dmult = gg * ix
            di = gg * mult * xr
            dxr = gg * mult * i
            dla = da * a - dmult * (a * a) / mult
            dr = dla * (-LRU_C * sp)
            dsp = jnp.sum(dla * (-LRU_C * r), axis=0, keepdims=True)
            dpa = dr * r * (1.0 - r)
            dpi = di * i * (1.0 - i)
            dpab, dpib, xrb = dpa.astype(BF16), dpi.astype(BF16), xr.astype(BF16)
            nt = (((1,), (1,)), ((), ()))
            tn = (((0,), (0,)), ((), ()))
            dxr = dxr + lax.dot_general(dpab, wav, nt, preferred_element_type=F32)
            dxr = dxr + lax.dot_general(dpib, wiv, nt, preferred_element_type=F32)
            dwa_ref[0] = dwa_ref[0] + lax.dot_general(xrb, dpab, tn, preferred_element_type=F32)
            dwi_ref[0] = dwi_ref[0] + lax.dot_general(xrb, dpib, tn, preferred_element_type=F32)
            rows = [jnp.sum(dxr * xs[3 - q], axis=0, keepdims=True) for q in range(4)]
            rows += [jnp.sum(dxr, axis=0, keepdims=True), jnp.sum(dpa, axis=0, keepdims=True),
                     jnp.sum(dpi, axis=0, keepdims=True), dsp * (-_sigmoid(-lamv))]
            dsm_ref[...] = dsm_ref[...] + jnp.concatenate(rows, axis=0)
            dxe = jnp.concatenate([dxr, dxr_next8], axis=0)
            dxp = wcv[3:4] * dxr
            for j in range(1, 4):
                dxp = dxp + wcv[3 - j:4 - j] * pltpu.roll(dxe, ch + 8 - j, 0)[:ch]
            dz_ref[0, pl.ds(st, ch), :] = jnp.concatenate([dxp, dgr], axis=1).astype(BF16)
            return a[0:1], gg[0:1], dxr[0:8]

        lax.fori_loop(0, nch, chunk, (jnp.zeros((1, RNN_BLK), F32), jnp.zeros((1, RNN_BLK), F32),
                                      jnp.zeros((8, RNN_BLK), F32)))

    vec = pl.BlockSpec((1, RNN_BLK), lambda n, b: (0, n))
    mat = pl.BlockSpec((1, RNN_BLK, RNN_BLK), lambda n, b: (n, 0, 0))
    tok = pl.BlockSpec((1, s, RNN_BLK), lambda n, b: (b, 0, n))
    zblk = pl.BlockSpec((1, s, 2 * RNN_BLK), lambda n, b: (b, 0, XG0 // (2 * RNN_BLK) + n))
    return _call(
        body, name="rnn_bwd", grid=(N_BLK, bsz),
        in_specs=[zblk, tok, tok, ANY, pl.BlockSpec((4, RNN_BLK), lambda n, b: (0, n)), vec, mat, vec, mat, vec, vec],
        out_specs=[zblk, mat, mat, pl.BlockSpec((8, RNN_BLK), lambda n, b: (0, n))],
        out_shape=[jax.ShapeDtypeStruct(dz.shape, dz.dtype), jax.ShapeDtypeStruct((N_BLK, RNN_BLK, RNN_BLK), F32),
                   jax.ShapeDtypeStruct((N_BLK, RNN_BLK, RNN_BLK), F32), jax.ShapeDtypeStruct((8, D_RNN), F32)],
        input_output_aliases={3: 0},
        compiler_params=_params(),
    )(z, h, dy, dz, wc, bc, wa, ba, wi, bi, lam)


def _tri(n, upper):
    r = lax.broadcasted_iota(jnp.int32, (n, n), 0)
    c = lax.broadcasted_iota(jnp.int32, (n, n), 1)
    return (r <= c if upper else r >= c).astype(BF16)


def _exact_dot(x, t):
    hi, mid, lo = _split3(x)
    return (jnp.dot(hi, t, preferred_element_type=F32) + jnp.dot(mid, t, preferred_element_type=F32)
            + jnp.dot(lo, t, preferred_element_type=F32))


def _fgate_fwd(flt, bfg):
    bsz, nh, s = flt.shape
    blk = min(CUM_BLK, s)

    def body(fl_ref, b_ref, f_ref):
        tri = _tri(blk, True)
        carry = jnp.zeros((nh, 1), F32)
        for q in range(s // blk):
            xq = fl_ref[0, :, q * blk:(q + 1) * blk] + b_ref[...]
            fq = _exact_dot(-_softplus(-xq), tri) + carry
            f_ref[0, :, q * blk:(q + 1) * blk] = fq
            carry = fq[:, blk - 1:blk]

    spec = pl.BlockSpec((1, nh, s), lambda b: (b, 0, 0))
    return _call(body, name="fgate_fwd", grid=(bsz,), in_specs=[spec, pl.BlockSpec((nh, 1), lambda b: (0, 0))],
                 out_specs=spec, out_shape=jax.ShapeDtypeStruct(flt.shape, F32), compiler_params=_params())(flt, bfg)


def _fgate_bwd(dft, flt, bfg):
    bsz, nh, s = flt.shape
    blk = min(CUM_BLK, s)

    def body(df_ref, fl_ref, b_ref, dfl_ref, db_ref):
        b = pl.program_id(0)
        tri = _tri(blk, False)
        carry = jnp.zeros((nh, 1), F32)
        tot = jnp.zeros((nh, 1), F32)
        for q in reversed(range(s // blk)):
            sl = slice(q * blk, (q + 1) * blk)
            dlq = _exact_dot(df_ref[0, :, sl], tri) + carry
            carry = dlq[:, 0:1]
            dflq = dlq * _sigmoid(-(fl_ref[0, :, sl] + b_ref[...]))
            dfl_ref[0, :, sl] = dflq
            tot = tot + jnp.sum(dflq, axis=1, keepdims=True)
        _acc(db_ref, slice(None), tot, b == 0)

    spec = pl.BlockSpec((1, nh, s), lambda b: (b, 0, 0))
    small = pl.BlockSpec((nh, 1), lambda b: (0, 0))
    return _call(body, name="fgate_bwd", grid=(bsz,), in_specs=[spec, spec, small], out_specs=[spec, small],
                 out_shape=[jax.ShapeDtypeStruct(flt.shape, F32), jax.ShapeDtypeStruct((nh, 1), F32)],
                 compiler_params=_params())(dft, flt, bfg)


def _attn_scores(q_b, k_ref, fcol_ref, frow_ref, q0, tq, ln):
    kk = k_ref[0, :ln, HEAD_DIM:2 * HEAD_DIM].astype(BF16)
    sc = lax.dot_general(q_b, kk, (((1,), (1,)), ((), ())), preferred_element_type=F32)
    sc = sc + fcol_ref[0, 0, q0:q0 + tq, :] - frow_ref[0, 0, :, :ln]
    qi = q0 + lax.broadcasted_iota(jnp.int32, (tq, ln), 0)
    ki = lax.broadcasted_iota(jnp.int32, (tq, ln), 1)
    return sc, qi >= ki


def _attn_fwd(z, fcol, frow):
    bsz, s, _ = z.shape
    tq = min(ATTN_TQ, s)
    scale = HEAD_DIM ** -0.5

    def body(z_ref, fcol_ref, frow_ref, o_ref, lse_ref):
        for qb in range(s // tq):
            q0, ln = qb * tq, (qb + 1) * tq
            q_b = (z_ref[0, q0:ln, :HEAD_DIM] * scale).astype(BF16)
            sc, mask = _attn_scores(q_b, z_ref, fcol_ref, frow_ref, q0, tq, ln)
            sc = jnp.where(mask, sc, -jnp.inf)
            mx = jnp.max(sc, axis=-1, keepdims=True)
            p = jnp.exp(sc - mx)
            l = jnp.sum(p, axis=-1, keepdims=True)
            vv = z_ref[0, :ln, 2 * HEAD_DIM:].astype(BF16)
            o = jnp.dot(p.astype(BF16), vv, preferred_element_type=F32) / l
            o_ref[0, q0:ln, :] = o.astype(BF16)
            lse_ref[0, 0, q0:ln, :] = mx + jnp.log(l)

    return _call(
        body, name="attn_fwd", grid=(bsz, N_HEADS),
        in_specs=[pl.BlockSpec((1, s, 3 * HEAD_DIM), lambda b, h: (b, 0, h)),
                  pl.BlockSpec((1, 1, s, 1), lambda b, h: (b, h, 0, 0)),
                  pl.BlockSpec((1, 1, 1, s), lambda b, h: (b, h, 0, 0))],
        out_specs=[pl.BlockSpec((1, s, HEAD_DIM), lambda b, h: (b, 0, h)),
                   pl.BlockSpec((1, 1, s, 1), lambda b, h: (b, h, 0, 0))],
        out_shape=[jax.ShapeDtypeStruct((bsz, s, N_HEADS * HEAD_DIM), BF16),
                   jax.ShapeDtypeStruct((bsz, N_HEADS, s, 1), F32)],
        compiler_params=_params(),
    )(z, fcol, frow)


def _attn_bwd(z, do, lse, fcol, frow, dz):
    bsz, s, _ = z.shape
    tq = min(ATTN_TQ, s)
    scale = HEAD_DIM ** -0.5
    nt = (((1,), (1,)), ((), ()))
    tn = (((0,), (0,)), ((), ()))

    def body(z_ref, do_ref, lse_ref, fcol_ref, frow_ref, dz_in, dz_ref, df_ref, dk_acc, dv_acc):
        del dz_in
        dk_acc[...] = jnp.zeros_like(dk_acc)
        dv_acc[...] = jnp.zeros_like(dv_acc)
        df_ref[...] = jnp.zeros_like(df_ref)
        for qb in range(s // tq):
            q0, ln = qb * tq, (qb + 1) * tq
            q_b = (z_ref[0, q0:ln, :HEAD_DIM] * scale).astype(BF16)
            sc, mask = _attn_scores(q_b, z_ref, fcol_ref, frow_ref, q0, tq, ln)
            p = jnp.where(mask, jnp.exp(sc - lse_ref[0, 0, q0:ln, :]), 0.0)
            do_b = do_ref[0, q0:ln, :]
            pb = p.astype(BF16)
            dv_acc[:ln, :] = dv_acc[:ln, :] + lax.dot_general(pb, do_b, tn, preferred_element_type=F32)
            vv = z_ref[0, :ln, 2 * HEAD_DIM:].astype(BF16)
            dp = lax.dot_general(do_b, vv, nt, preferred_element_type=F32)
            ds = p * (dp - jnp.sum(p * dp, axis=-1, keepdims=True))
            dsb = ds.astype(BF16)
            kk = z_ref[0, :ln, HEAD_DIM:2 * HEAD_DIM].astype(BF16)
            dq = jnp.dot(dsb, kk, preferred_element_type=F32) * scale
            dz_ref[0, q0:ln, :HEAD_DIM] = dq.astype(BF16)
            dk_acc[:ln, :] = dk_acc[:ln, :] + lax.dot_general(dsb, q_b, tn, preferred_element_type=F32)
            df_ref[0, 0, :, :ln] = df_ref[0, 0, :, :ln] - jnp.sum(ds, axis=0, keepdims=True)
        dz_ref[0, :, HEAD_DIM:2 * HEAD_DIM] = dk_acc[...].astype(BF16)
        dz_ref[0, :, 2 * HEAD_DIM:] = dv_acc[...].astype(BF16)

    zblk = pl.BlockSpec((1, s, 3 * HEAD_DIM), lambda b, h: (b, 0, h))
    hblk = pl.BlockSpec((1, s, HEAD_DIM), lambda b, h: (b, 0, h))
    col = pl.BlockSpec((1, 1, s, 1), lambda b, h: (b, h, 0, 0))
    rowspec = pl.BlockSpec((1, 1, 1, s), lambda b, h: (b, h, 0, 0))
    return _call(
        body, name="attn_bwd", grid=(bsz, N_HEADS),
        in_specs=[zblk, hblk, col, col, rowspec, ANY],
        out_specs=[zblk, rowspec],
        out_shape=[jax.ShapeDtypeStruct(dz.shape, dz.dtype), jax.ShapeDtypeStruct((bsz, N_HEADS, 1, s), F32)],
        scratch_shapes=[pltpu.VMEM((s, HEAD_DIM), F32), pltpu.VMEM((s, HEAD_DIM), F32)],
        input_output_aliases={5: 0},
        compiler_params=_params(),
    )(z, do, lse, fcol, frow, dz)


def _merge_fwd(z2, pr, pa, *, tr=512):
    t = z2.shape[0]

    def body(mg_ref, pr_ref, pa_ref, o_ref):
        gr_ = _sigmoid(mg_ref[:, :D_MODEL])
        ga = _sigmoid(mg_ref[:, D_MODEL:])
        o_ref[...] = (gr_ * pr_ref[...] + ga * pa_ref[...]).astype(BF16)

    tok = pl.BlockSpec((tr, D_MODEL), lambda i: (i, 0))
    return _call(body, name="merge_fwd", grid=(t // tr,),
                 in_specs=[pl.BlockSpec((tr, 2 * D_MODEL), lambda i: (i, MG0 // (2 * D_MODEL))), tok, tok],
                 out_specs=tok, out_shape=jax.ShapeDtypeStruct((t, D_MODEL), BF16),
                 compiler_params=_params())(z2, pr, pa)


def _merge_bwd(dm, z2, pr, pa, *, tr=512):
    t = z2.shape[0]

    def body(dm_ref, mg_ref, pr_ref, pa_ref, dz_ref, dpr_ref, dpa_ref):
        dmv = dm_ref[...]
        gr_ = _sigmoid(mg_ref[:, :D_MODEL])
        ga = _sigmoid(mg_ref[:, D_MODEL:])
        dpr_ref[...] = (dmv * gr_).astype(BF16)
        dpa_ref[...] = (dmv * ga).astype(BF16)
        dz_ref[:, :D_MODEL] = (dmv * pr_ref[...] * gr_ * (1.0 - gr_)).astype(BF16)
        dz_ref[:, D_MODEL:] = (dmv * pa_ref[...] * ga * (1.0 - ga)).astype(BF16)

    tok = pl.BlockSpec((tr, D_MODEL), lambda i: (i, 0))
    mg = pl.BlockSpec((tr, 2 * D_MODEL), lambda i: (i, MG0 // (2 * D_MODEL)))
    return _call(body, name="merge_bwd", grid=(t // tr,), in_specs=[tok, mg, tok, tok],
                 out_specs=[mg, tok, tok],
                 out_shape=[jax.ShapeDtypeStruct(z2.shape, BF16), jax.ShapeDtypeStruct((t, D_MODEL), BF16),
                            jax.ShapeDtypeStruct((t, D_MODEL), BF16)],
                 compiler_params=_params())(dm, z2, pr, pa)


def _fl_write(dfl_blk, dz):
    bsz, s, w = dfl_blk.shape

    def body(src_ref, dz_in, dz_ref):
        del dz_in
        dz_ref[...] = src_ref[...]

    return _call(body, name="fl_write", grid=(bsz,),
                 in_specs=[pl.BlockSpec((1, s, w), lambda b: (b, 0, 0)), ANY],
                 out_specs=pl.BlockSpec((1, s, w), lambda b: (b, 0, FL0 // w)),
                 out_shape=jax.ShapeDtypeStruct(dz.shape, dz.dtype), input_output_aliases={1: 0},
                 compiler_params=_params())(dfl_blk, dz)


def _ffn_fwd(zf, wc, bc):
    bsz, s, _ = zf.shape
    cw = 128

    def body(zf_ref, wc_ref, bc_ref, o_ref):
        gfp, uf = zf_ref[0, :, :cw], zf_ref[0, :, cw:]
        wcv = wc_ref[...]
        row = lax.broadcasted_iota(jnp.int32, (s, cw), 0)
        gf = bc_ref[...] + wcv[2:3] * gfp
        for j in (1, 2):
            gf = gf + wcv[2 - j:3 - j] * jnp.where(row >= j, pltpu.roll(gfp, j, 0), 0.0)
        o_ref[0] = (_gelu(gf) * uf).astype(BF16)

    return _call(body, name="ffn_fwd", grid=(bsz, D_FF // cw),
                 in_specs=[pl.BlockSpec((1, s, 2 * cw), lambda b, j: (b, 0, j)),
                           pl.BlockSpec((3, cw), lambda b, j: (0, j)), pl.BlockSpec((1, cw), lambda b, j: (0, j))],
                 out_specs=pl.BlockSpec((1, s, cw), lambda b, j: (b, 0, j)),
                 out_shape=jax.ShapeDtypeStruct((bsz, s, D_FF), BF16), compiler_params=_params())(zf, wc, bc)


def _ffn_bwd(dact, zf, wc, bc):
    bsz, s, _ = zf.shape
    cw = 128

    def body(da_ref, zf_ref, wc_ref, bc_ref, dzf_ref, dsm_ref):
        b = pl.program_id(1)
        gfp, uf = zf_ref[0, :, :cw], zf_ref[0, :, cw:]
        wcv = wc_ref[...]
        row = lax.broadcasted_iota(jnp.int32, (s, cw), 0)
        sh = [gfp] + [jnp.where(row >= j, pltpu.roll(gfp, j, 0), 0.0) for j in (1, 2)]
        gf = bc_ref[...]
        for kk in range(3):
            gf = gf + wcv[kk:kk + 1] * sh[2 - kk]
        ge, dge = _gelu_and_grad(gf)
        dav = da_ref[0]
        dgf = dav * uf * dge
        dgfp = wcv[2:3] * dgf
        for j in (1, 2):
            dgfp = dgfp + wcv[2 - j:3 - j] * jnp.where(row < s - j, pltpu.roll(dgf, s - j, 0), 0.0)
        dzf_ref[0] = jnp.concatenate([dgfp, dav * ge], axis=1).astype(BF16)
        rows = [jnp.sum(dgf * sh[2 - kk], axis=0, keepdims=True) for kk in range(3)]
        rows += [jnp.sum(dgf, axis=0, keepdims=True), jnp.zeros((4, cw), F32)]
        _acc(dsm_ref, slice(None), jnp.concatenate(rows, axis=0), b == 0)

    return _call(body, name="ffn_bwd", grid=(D_FF // cw, bsz),
                 in_specs=[pl.BlockSpec((1, s, cw), lambda j, b: (b, 0, j)),
                           pl.BlockSpec((1, s, 2 * cw), lambda j, b: (b, 0, j)),
                           pl.BlockSpec((3, cw), lambda j, b: (0, j)), pl.BlockSpec((1, cw), lambda j, b: (0, j))],
                 out_specs=[pl.BlockSpec((1, s, 2 * cw), lambda j, b: (b, 0, j)),
                            pl.BlockSpec((8, cw), lambda j, b: (0, j))],
                 out_shape=[jax.ShapeDtypeStruct(zf.shape, BF16), jax.ShapeDtypeStruct((8, D_FF), F32)],
                 compiler_params=_params())(dact, zf, wc, bc)


def _adamw(w, m, v, parts, *, name):
    r, c = w.shape
    p = parts.shape[0]
    tr = r if r <= 512 else 256
    assert r % tr == 0 and parts.shape[1:] == (r, c), (name, w.shape, parts.shape)
    c1 = 1.0 / (1.0 - ADAM_B1 ** ADAM_STEP)
    c2 = 1.0 / (1.0 - ADAM_B2 ** ADAM_STEP)

    def body(w_ref, m_ref, v_ref, p_ref, g_ref, d_ref, mo_ref, vo_ref):
        g = p_ref[0]
        for q in range(1, p):
            g = g + p_ref[q]
        mn = ADAM_B1 * m_ref[...] + (1.0 - ADAM_B1) * g
        vn = ADAM_B2 * v_ref[...] + (1.0 - ADAM_B2) * (g * g)
        g_ref[...] = g
        mo_ref[...] = mn
        vo_ref[...] = vn
        d_ref[...] = -ADAM_LR * ((mn * c1) / (jnp.sqrt(vn * c2) + ADAM_EPS) + ADAM_WD * w_ref[...])

    spec = pl.BlockSpec((tr, c), lambda i: (i, 0))
    shp = jax.ShapeDtypeStruct((r, c), F32)
    return _call(body, name=name, grid=(r // tr,),
                 in_specs=[spec, spec, spec, pl.BlockSpec((p, tr, c), lambda i: (0, i, 0))],
                 out_specs=[spec] * 4, out_shape=[shp] * 4, compiler_params=_params())(w, m, v, parts)


def _interleave(w, groups, width):
    rows = w.shape[0]
    return w.reshape(rows, groups, -1, width).transpose(0, 2, 1, 3).reshape(rows, -1)


def _deinterleave(w, groups, width):
    rows = w.shape[0]
    return w.reshape(rows, -1, groups, width).transpose(0, 2, 1, 3).reshape(rows, -1)


def _to_z_layout(w):
    xg = _interleave(w[:, :2 * D_RNN], 2, RNN_BLK)
    qkv = _interleave(w[:, 2 * D_RNN:2 * D_RNN + 3 * D_MODEL], 3, HEAD_DIM)
    fl0 = 2 * D_RNN + 3 * D_MODEL
    fl = w[:, fl0:fl0 + N_HEADS]
    mg = w[:, fl0 + N_HEADS:]
    pad = jnp.zeros((w.shape[0], MG0 - FL0 - N_HEADS), w.dtype)
    return jnp.concatenate([qkv, xg, fl, pad, mg], axis=1)


def _from_z_layout(w):
    qkv = _deinterleave(w[:, QKV0:XG0], 3, HEAD_DIM)
    xg = _deinterleave(w[:, XG0:FL0], 2, RNN_BLK)
    return jnp.concatenate([xg, qkv, w[:, FL0:FL0 + N_HEADS], w[:, MG0:]], axis=1)


def _col_gather(g):
    return g.transpose(1, 0, 2).reshape(g.shape[1], -1)


def _col_scatter(w):
    return w.reshape(w.shape[0], N_DEV, -1).transpose(1, 0, 2)


def kernel(x, c, w_ada, b_ada, g_norm1, w_in, w_rnn_conv, b_rnn_conv, w_lru_a, b_lru_a, w_lru_i, b_lru_i, lru_lambda, b_fgate, w_proj_rnn, w_proj_attn, w_out, g_norm2, w_ffn_up, w_ffn_conv, b_ffn_conv, w_ffn_down, w_ada_final, b_ada_final, g_final, loss_target, m_w_ada, m_b_ada, m_g_norm1, m_w_in, m_w_rnn_conv, m_b_rnn_conv, m_w_lru_a, m_b_lru_a, m_w_lru_i, m_b_lru_i, m_lru_lambda, m_b_fgate, m_w_proj_rnn, m_w_proj_attn, m_w_out, m_g_norm2, m_w_ffn_up, m_w_ffn_conv, m_b_ffn_conv, m_w_ffn_down, m_w_ada_final, m_b_ada_final, m_g_final, v_w_ada, v_b_ada, v_g_norm1, v_w_in, v_w_rnn_conv, v_b_rnn_conv, v_w_lru_a, v_b_lru_a, v_w_lru_i, v_b_lru_i, v_lru_lambda, v_b_fgate, v_w_proj_rnn, v_w_proj_attn, v_w_out, v_g_norm2, v_w_ffn_up, v_w_ffn_conv, v_b_ffn_conv, v_w_ffn_down, v_w_ada_final, v_b_ada_final, v_g_final):
    args = dict(locals())
    bsz, s, _ = x.shape
    t = bsz * s
    nb = N_DEV * bsz
    me = 4 * lax.axis_index("x") + 2 * lax.axis_index("y") + lax.axis_index("c")
    tm = min(1024, t)

    ex_a = _exchange(
        [c, w_in[0].astype(BF16), w_proj_rnn[0].astype(BF16), w_proj_attn[0].astype(BF16), w_out[0].astype(BF16),
         w_ffn_up[0].astype(BF16), w_ffn_down[0].astype(BF16), w_rnn_conv[0], w_ffn_conv[0]],
        [True] * 9, "gather_weights")
    c_all = ex_a[0].reshape(nb, D_MODEL)
    win_z = _to_z_layout(_col_gather(ex_a[1]))
    wpr = ex_a[2].reshape(D_RNN, D_MODEL)
    wpa = ex_a[3].reshape(D_MODEL, D_MODEL)
    wout = ex_a[4].reshape(D_MODEL, D_MODEL)
    wup_z = _interleave(_col_gather(ex_a[5]), 2, 128)
    wdown = ex_a[6].reshape(D_FF, D_MODEL)
    wrc = _col_gather(ex_a[7])
    wfc = _col_gather(ex_a[8])

    nmod = w_ada.shape[2]
    nmodf = w_ada_final.shape[1]
    mod_cols = _mm(c_all, w_ada[0], name="mod_cols", tm=nb, tn=nmod, tk=D_MODEL, silu_a=True,
                   bias=lax.dynamic_slice(b_ada, (0, me * nmod), (1, nmod)))
    modf_cols = _mm(c_all, w_ada_final, name="modf_cols", tm=nb, tn=nmodf, tk=D_MODEL, silu_a=True,
                    bias=lax.dynamic_slice(b_ada_final.reshape(1, -1), (0, me * nmodf), (1, nmodf)))
    ex_b = _exchange([mod_cols, modf_cols], [True, True], "gather_mod")
    mod = lax.dynamic_slice(ex_b[0], (0, me * bsz, 0), (N_DEV, bsz, nmod)).transpose(1, 0, 2).reshape(bsz, 6, 1, D_MODEL)
    modf = lax.dynamic_slice(ex_b[1], (0, me * bsz, 0), (N_DEV, bsz, nmodf)).transpose(1, 0, 2).reshape(bsz, 2, 1, D_MODEL)
    shift1, scale1, gate1, shift2, scale2, gate2 = [mod[:, i] for i in range(6)]
    shift_f, scale_f = modf[:, 0], modf[:, 1]

    h1 = _norm_fwd(x, g_norm1, shift1, scale1, name="norm1_fwd")[0]
    z = _mm(h1.reshape(t, D_MODEL), win_z, name="mm_in", tm=tm, tn=1024, tk=D_MODEL).reshape(bsz, s, NZ)
    h_rnn, y_rnn = _rnn_fwd(z, wrc, b_rnn_conv, w_lru_a[0], b_lru_a, w_lru_i[0], b_lru_i, lru_lambda)
    flt = z[:, :, FL0:FL0 + N_HEADS].transpose(0, 2, 1)
    bfg = b_fgate.reshape(N_HEADS, 1)
    fcum = _fgate_fwd(flt, bfg)
    fcol = fcum.reshape(bsz, N_HEADS, s, 1)
    frow = fcum.reshape(bsz, N_HEADS, 1, s)
    o_attn, lse = _attn_fwd(z, fcol, frow)
    pr = _mm(y_rnn.reshape(t, D_RNN), wpr, name="mm_proj_rnn", tm=tm, tn=D_MODEL, tk=D_RNN)
    pa = _mm(o_attn.reshape(t, D_MODEL), wpa, name="mm_proj_attn", tm=tm, tn=D_MODEL, tk=D_MODEL)
    z2 = z.reshape(t, NZ)
    merged = _merge_fwd(z2, pr, pa)
    mo = _mm(merged, wout, name="mm_out", tm=tm, tn=D_MODEL, tk=D_MODEL).reshape(bsz, s, D_MODEL)
    h2, x1 = _norm_fwd(x, g_norm2, shift2, scale2, name="norm2_fwd", yprev=mo, gate=gate1)
    zf = _mm(h2.reshape(t, D_MODEL), wup_z, name="mm_up", tm=tm, tn=1024, tk=D_MODEL).reshape(bsz, s, 2 * D_FF)
    act = _ffn_fwd(zf, wfc, b_ffn_conv)
    yf = _mm(act.reshape(t, D_FF), wdown, name="mm_down", tm=tm, tn=D_MODEL, tk=1024).reshape(bsz, s, D_MODEL)
    lp, dx2, dyf, dshift_f, dscale_f, dgate2, dg_final = _final(x1, yf, gate2, g_final.reshape(1, -1), shift_f, scale_f, loss_target)
    loss = lax.psum(jnp.sum(lp[:, :, 0, 0]) * (0.5 / D_MODEL), ("x", "y", "c"))

    dyf2 = dyf.reshape(t, D_MODEL)
    act2 = act.reshape(t, D_FF)
    dact = _mm(dyf2, wdown, name="mm_down_dx", tb=True, tm=tm, tn=1024, tk=D_MODEL).reshape(bsz, s, D_FF)
    g_wdown = _mm(act2, dyf2, name="mm_down_dw", ta=True, tm=1024, tn=D_MODEL, tk=tm)
    dzf, dsm_ffn = _ffn_bwd(dact, zf, wfc, b_ffn_conv)
    dzf2 = dzf.reshape(t, 2 * D_FF)
    dh2 = _mm(dzf2, wup_z, name="mm_up_dx", tb=True, tm=tm, tn=D_MODEL, tk=1024).reshape(bsz, s, D_MODEL)
    g_wup_z = _mm(h2.reshape(t, D_MODEL), dzf2, name="mm_up_dw", ta=True, tm=D_MODEL, tn=1024, tk=tm)
    dx1, dg_norm2, dshift2, dscale2, dmo, dgate1 = _norm_bwd(x1, dh2, g_norm2, scale2, dx2, name="norm2_bwd", yprev=mo, gate=gate1)
    dmo2 = dmo.reshape(t, D_MODEL)
    dmerged = _mm(dmo2, wout, name="mm_out_dx", tb=True, tm=tm, tn=D_MODEL, tk=D_MODEL)
    g_wout = _mm(merged, dmo2, name="mm_out_dw", ta=True, tm=D_MODEL, tn=D_MODEL, tk=tm)
    dz2, dpr, dpa = _merge_bwd(dmerged, z2, pr, pa)
    dy_rnn = _mm(dpr, wpr, name="mm_proj_rnn_dx", tb=True, tm=tm, tn=D_RNN, tk=D_MODEL).reshape(bsz, s, D_RNN)
    g_wpr = _mm(y_rnn.reshape(t, D_RNN), dpr, name="mm_proj_rnn_dw", ta=True, tm=D_RNN, tn=D_MODEL, tk=tm)
    do = _mm(dpa, wpa, name="mm_proj_attn_dx", tb=True, out_dtype=BF16, tm=tm, tn=D_MODEL, tk=D_MODEL).reshape(bsz, s, D_MODEL)
    g_wpa = _mm(o_attn.reshape(t, D_MODEL), dpa, name="mm_proj_attn_dw", ta=True, tm=D_MODEL, tn=D_MODEL, tk=tm)
    dz, dfrow = _attn_bwd(z, do, lse, fcol, frow, dz2.reshape(bsz, s, NZ))
    dflt, db_fgate = _fgate_bwd(dfrow.reshape(bsz, N_HEADS, s), flt, bfg)
    dfl_blk = jnp.pad(dflt.transpose(0, 2, 1).astype(BF16), ((0, 0), (0, 0), (0, MG0 - FL0 - N_HEADS)))
    dz = _fl_write(dfl_blk, dz)
    dz, g_wa, g_wi, dsm_rnn = _rnn_bwd(z, h_rnn, dy_rnn, dz, wrc, b_rnn_conv, w_lru_a[0], b_lru_a, w_lru_i[0], b_lru_i, lru_lambda)
    dz2 = dz.reshape(t, NZ)
    dh1 = _mm(dz2, win_z, name="mm_in_dx", tb=True, tm=tm, tn=D_MODEL, tk=1024).reshape(bsz, s, D_MODEL)
    g_win_z = _mm(h1.reshape(t, D_MODEL), dz2, name="mm_in_dw", ta=True, tm=D_MODEL, tn=1024, tk=tm)
    grad_x, dg_norm1, dshift1, dscale1 = _norm_bwd(x, dh1, g_norm1, scale1, dx1, name="norm1_bwd")

    dmod = jnp.concatenate([dshift1, dscale1, dgate1, dshift2, dscale2, dgate2], axis=1).reshape(bsz, 6 * D_MODEL)
    dmodf = jnp.concatenate([dshift_f, dscale_f], axis=1).reshape(bsz, 2 * D_MODEL)
    small = [dg_norm1, dg_norm2, dg_final, dsm_rnn, g_wa, g_wi, db_fgate, dsm_ffn]
    small_sizes = [a.size for a in small]
    n_small = sum(small_sizes)
    n_pad = -n_small % 1024
    packed = jnp.concatenate([a.reshape(-1) for a in small] + [jnp.zeros((n_pad,), F32)]).reshape(-1, 128)
    ex_c = _exchange(
        [dmod, dmodf, packed, _col_scatter(_from_z_layout(g_win_z)), g_wpr.reshape(N_DEV, -1, D_MODEL),
         g_wpa.reshape(N_DEV, -1, D_MODEL), g_wout.reshape(N_DEV, -1, D_MODEL),
         _col_scatter(_deinterleave(g_wup_z, 2, 128)), g_wdown.reshape(N_DEV, -1, D_MODEL)],
        [True, True, True, False, False, False, False, False, False], "exchange_grads")
    dmod_all = ex_c[0].reshape(nb, 6 * D_MODEL)
    dmodf_all = ex_c[1].reshape(nb, 2 * D_MODEL)

    res = {}

    def upd(name, parts, shape2):
        w, m, v = args[name], args["m_" + name], args["v_" + name]
        outs = _adamw(w.reshape(shape2), m.reshape(shape2), v.reshape(shape2), parts, name="adamw_" + name)
        res[name] = [o.reshape(w.shape) for o in outs]

    g_wada = _mm(c_all, lax.dynamic_slice(dmod_all, (0, me * nmod), (nb, nmod)), name="mm_ada_dw", ta=True,
                 silu_a=True, tm=D_MODEL, tn=nmod, tk=nb)
    g_wadaf = _mm(c_all, lax.dynamic_slice(dmodf_all, (0, me * nmodf), (nb, nmodf)), name="mm_adaf_dw", ta=True,
                  silu_a=True, tm=D_MODEL, tn=nmodf, tk=nb)
    upd("w_ada", g_wada[None], (D_MODEL, nmod))
    upd("w_ada_final", g_wadaf[None], (D_MODEL, nmodf))
    upd("b_ada", dmod_all.reshape(nb, 1, -1), (1, 6 * D_MODEL))
    upd("b_ada_final", dmodf_all.reshape(nb, 1, -1), (1, 2 * D_MODEL))
    upd("w_in", ex_c[3], w_in.shape[1:])
    upd("w_proj_rnn", ex_c[4], w_proj_rnn.shape[1:])
    upd("w_proj_attn", ex_c[5], w_proj_attn.shape[1:])
    upd("w_out", ex_c[6], w_out.shape[1:])
    upd("w_ffn_up", ex_c[7], w_ffn_up.shape[1:])
    upd("w_ffn_down", ex_c[8], w_ffn_down.shape[1:])

    flat = ex_c[2].reshape(N_DEV, -1)
    parts, off = [], 0
    for a, n in zip(small, small_sizes):
        parts.append(flat[:, off:off + n].reshape((N_DEV,) + a.shape))
        off += n
    p_g1, p_g2, p_gf, p_rnn, p_wa, p_wi, p_bfg, p_ffn = parts
    upd("g_norm1", p_g1, (1, D_MODEL))
    upd("g_norm2", p_g2, (1, D_MODEL))
    upd("g_final", p_gf, (1, D_MODEL))
    nrc = w_rnn_conv.shape[2]
    upd("w_rnn_conv", lax.dynamic_slice(p_rnn, (0, 0, me * nrc), (N_DEV, 4, nrc)), (4, nrc))
    upd("b_rnn_conv", p_rnn[:, 4:5], (1, D_RNN))
    upd("b_lru_a", p_rnn[:, 5:6], (1, D_RNN))
    upd("b_lru_i", p_rnn[:, 6:7], (1, D_RNN))
    upd("lru_lambda", p_rnn[:, 7:8], (1, D_RNN))
    upd("w_lru_a", p_wa.reshape(N_DEV, D_RNN, RNN_BLK), (D_RNN, RNN_BLK))
    upd("w_lru_i", p_wi.reshape(N_DEV, D_RNN, RNN_BLK), (D_RNN, RNN_BLK))
    upd("b_fgate", p_bfg.reshape(N_DEV, 1, N_HEADS), (1, N_HEADS))
    nfc = w_ffn_conv.shape[2]
    upd("w_ffn_conv", lax.dynamic_slice(p_ffn, (0, 0, me * nfc), (N_DEV, 3, nfc)), (3, nfc))
    upd("b_ffn_conv", p_ffn[:, 3:4], (1, D_FF))

    names = ['w_ada', 'b_ada', 'g_norm1', 'w_in', 'w_rnn_conv', 'b_rnn_conv', 'w_lru_a', 'b_lru_a', 'w_lru_i',
             'b_lru_i', 'lru_lambda', 'b_fgate', 'w_proj_rnn', 'w_proj_attn', 'w_out', 'g_norm2', 'w_ffn_up',
             'w_ffn_conv', 'b_ffn_conv', 'w_ffn_down', 'w_ada_final', 'b_ada_final', 'g_final']
    return (loss, grad_x, *[res[n][0] for n in names], *[res[n][1] for n in names],
            *[res[n][2] for n in names], *[res[n][3] for n in names])
```

```python
import jax
import jax.numpy as jnp
from jax import lax
from jax.experimental import pallas as pl
from jax.experimental.pallas import tpu as pltpu

F32, BF16 = jnp.float32, jnp.bfloat16
D_MODEL, D_RNN, N_BLK, RNN_BLK = 1024, 1280, 10, 128
N_HEADS, HEAD_DIM, D_FF = 8, 128, 3072
N_DEV = 8
NZ = 8192
QKV0, XG0, FL0, MG0 = 0, 3072, 5632, 6144
RMS_EPS = 1e-6
LRU_C = 8.0
ADAM_LR, ADAM_B1, ADAM_B2, ADAM_EPS, ADAM_WD, ADAM_STEP = 0.001, 0.9, 0.999, 1e-08, 0.01, 10
VMEM_LIMIT = 48 << 20
SCAN_CHUNK = 256
ATTN_TQ = 256
CUM_BLK = 512
MESH = pl.DeviceIdType.MESH
ANY = pl.BlockSpec(memory_space=pl.ANY)


def _call(body, **kw):
    return pl.pallas_call(body, **kw)


def _params():
    return pltpu.CompilerParams(vmem_limit_bytes=VMEM_LIMIT)


def _direct_copies(ins, outs, sems, bcast):
    send_sems, recv_sems, loc_sems = sems
    x, y, c = lax.axis_index("x"), lax.axis_index("y"), lax.axis_index("c")
    me = 4 * x + 2 * y + c
    local, remote = [], []
    for a in range(len(ins)):
        src = ins[a] if bcast[a] else ins[a].at[me]
        local.append(pltpu.make_async_copy(src, outs[a].at[me], loc_sems.at[a]))
    for k in range(1, N_DEV):
        px = 1 - x if k & 4 else x
        py = 1 - y if k & 2 else y
        pc = 1 - c if k & 1 else c
        peer = 4 * px + 2 * py + pc
        for a in range(len(ins)):
            src = ins[a] if bcast[a] else ins[a].at[peer]
            remote.append(pltpu.make_async_remote_copy(
                src_ref=src, dst_ref=outs[a].at[me],
                send_sem=send_sems.at[a * (N_DEV - 1) + k - 1], recv_sem=recv_sems.at[a * (N_DEV - 1) + k - 1],
                device_id=(px, py, pc), device_id_type=MESH))
    return local, remote


def _with_exchange(body, args, arrs, bcast, *, name, grid, in_specs, out_specs, out_shape, scratch_shapes=(), **kw):
    n_in, n_out, n_sc, nc = len(args), len(out_shape), len(scratch_shapes), len(arrs)

    def wrapped(*refs):
        core_in, cin = refs[:n_in], refs[n_in:n_in + nc]
        core_out = refs[n_in + nc:n_in + nc + n_out]
        cout = refs[n_in + nc + n_out:n_in + 2 * nc + n_out]
        rest = refs[n_in + 2 * nc + n_out:]
        core_sc, sems = rest[:n_sc], rest[n_sc:]

        def start():
            local, remote = _direct_copies(cin, cout, sems, bcast)
            for cp in local + remote:
                cp.start()

        def wait():
            local, remote = _direct_copies(cin, cout, sems, bcast)
            for cp in remote + local:
                cp.wait()

        if not grid:
            start()
            body(*core_in, *core_out, *core_sc)
            wait()
            return
        first, last = None, None
        for d, g in enumerate(grid):
            i = pl.program_id(d)
            first = (i == 0) if first is None else jnp.logical_and(first, i == 0)
            last = (i == g - 1) if last is None else jnp.logical_and(last, i == g - 1)
        pl.when(first)(start)
        body(*core_in, *core_out, *core_sc)
        pl.when(last)(wait)

    ex_shape = [jax.ShapeDtypeStruct((N_DEV,) + (a.shape if bc else a.shape[1:]), a.dtype) for a, bc in zip(arrs, bcast)]
    sems = [pltpu.SemaphoreType.DMA((nc * (N_DEV - 1),)), pltpu.SemaphoreType.DMA((nc * (N_DEV - 1),)),
            pltpu.SemaphoreType.DMA((nc,))]
    res = _call(wrapped, name=name, grid=grid, in_specs=list(in_specs) + [ANY] * nc,
                out_specs=list(out_specs) + [ANY] * nc, out_shape=list(out_shape) + ex_shape,
                scratch_shapes=list(scratch_shapes) + sems, **kw)(*args, *arrs)
    return list(res[:n_out]), list(res[n_out:])


def _run(body, args, exchange, **kw):
    if exchange is None:
        return list(_call(body, **kw)(*args))
    outs, got = _with_exchange(body, args, *exchange, **kw)
    return outs + [got]


def _exchange(arrs, bcast, name):
    return _with_exchange(lambda: None, [], arrs, bcast, name=name, grid=(), in_specs=[], out_specs=[], out_shape=[],
                          compiler_params=pltpu.CompilerParams(has_side_effects=True))[1]


def _gather_two_level(arrs, name):
    n = len(arrs)

    def body(*refs):
        ins, outs = refs[:n], refs[n:2 * n]
        send_sems, recv_sems, loc_sems = refs[2 * n:]
        x, y, c = lax.axis_index("x"), lax.axis_index("y"), lax.axis_index("c")
        me, sibling = (x, y, c), (x, y, 1 - c)
        chips = [(1 - x, y), (x, 1 - y), (1 - x, 1 - y)]

        def slot(a, p):
            return outs[a].at[4 * p[0] + 2 * p[1] + p[2]]

        def copy(a, k, block, to, src=None):
            return pltpu.make_async_remote_copy(
                src_ref=slot(a, block) if src is None else src, dst_ref=slot(a, block),
                send_sem=send_sems.at[a * 7 + k], recv_sem=recv_sems.at[a * 7 + k],
                device_id=to, device_id_type=MESH)

        mine = [pltpu.make_async_copy(ins[a], slot(a, me), loc_sems.at[a]) for a in range(n)]
        first = []
        for a in range(n):
            first.append(copy(a, 0, me, sibling, src=ins[a]))
            first += [copy(a, 1 + j, me, (*chip, c), src=ins[a]) for j, chip in enumerate(chips)]
        for cp in mine + first:
            cp.start()
        passed = []
        for j, chip in enumerate(chips):
            for a in range(n):
                copy(a, 1 + j, (*chip, c), me).wait_recv()
                fwd = copy(a, 4 + j, (*chip, c), sibling)
                fwd.start()
                passed.append(fwd)
        for a in range(n):
            copy(a, 0, sibling, me).wait_recv()
        for j, chip in enumerate(chips):
            for a in range(n):
                copy(a, 4 + j, (*chip, 1 - c), me).wait_recv()
        for cp in first + passed:
            cp.wait_send()
        for cp in mine:
            cp.wait()

    return _call(
        body, name=name, in_specs=[ANY] * n, out_specs=[ANY] * n,
        out_shape=[jax.ShapeDtypeStruct((N_DEV,) + a.shape, a.dtype) for a in arrs],
        scratch_shapes=[pltpu.SemaphoreType.DMA((n * 7,)), pltpu.SemaphoreType.DMA((n * 7,)),
                        pltpu.SemaphoreType.DMA((n,))],
        compiler_params=pltpu.CompilerParams(has_side_effects=True),
    )(*arrs)


def _sigmoid(x):
    return 1.0 / (1.0 + jnp.exp(-x))


def _log1p(e):
    u = 1.0 + e
    d = u - 1.0
    return jnp.where(d == 0.0, e, jnp.log(u) * (e / jnp.where(d == 0.0, 1.0, d)))


def _softplus(x):
    return jnp.maximum(x, 0.0) + _log1p(jnp.exp(-jnp.abs(x)))


_GELU_C = 0.7978845608028654


def _gelu(x):
    t = jnp.tanh(_GELU_C * (x + 0.044715 * x * x * x))
    return 0.5 * x * (1.0 + t)


def _gelu_and_grad(x):
    t = jnp.tanh(_GELU_C * (x + 0.044715 * x * x * x))
    g = 0.5 * x * (1.0 + t)
    dg = 0.5 * (1.0 + t) + 0.5 * x * (1.0 - t * t) * _GELU_C * (1.0 + 3.0 * 0.044715 * x * x)
    return g, dg


def _acc(ref, idx, val, first):
    @pl.when(first)
    def _():
        ref[idx] = val

    @pl.when(jnp.logical_not(first))
    def _():
        ref[idx] = ref[idx] + val


def _split3(x):
    hi = x.astype(BF16)
    r1 = x - hi.astype(F32)
    mid = r1.astype(BF16)
    lo = (r1 - mid.astype(F32)).astype(BF16)
    return hi, mid, lo


def _mm(a, b, *, name, ta=False, tb=False, out_dtype=F32, tm, tn, tk, bias=None, silu_a=False, exchange=None):
    m, k = (a.shape[1], a.shape[0]) if ta else a.shape
    n = b.shape[0] if tb else b.shape[1]
    assert m % tm == 0 and n % tn == 0 and k % tk == 0, (name, m, n, k)
    nk = k // tk
    dn = (((0 if ta else 1,), (1 if tb else 0,)), ((), ()))
    use_acc = nk > 1 and out_dtype != F32

    def body(*refs):
        a_ref, b_ref = refs[0], refs[1]
        bias_ref = refs[2] if bias is not None else None
        o_ref = refs[3 if bias is not None else 2]
        av = a_ref[...]
        if silu_a:
            av = av * _sigmoid(av)
        p = lax.dot_general(av.astype(BF16), b_ref[...].astype(BF16), dn, preferred_element_type=F32)
        if bias is not None:
            p = p + bias_ref[...]
        if nk == 1:
            o_ref[...] = p.astype(out_dtype)
            return
        kk = pl.program_id(2)
        acc = refs[-1] if use_acc else o_ref

        @pl.when(kk == 0)
        def _():
            acc[...] = p

        @pl.when(kk > 0)
        def _():
            acc[...] = acc[...] + p

        if use_acc:
            @pl.when(kk == nk - 1)
            def _():
                o_ref[...] = acc[...].astype(out_dtype)

    a_spec = pl.BlockSpec((tk, tm), lambda i, j, q: (q, i)) if ta else pl.BlockSpec((tm, tk), lambda i, j, q: (i, q))
    b_spec = pl.BlockSpec((tn, tk), lambda i, j, q: (j, q)) if tb else pl.BlockSpec((tk, tn), lambda i, j, q: (q, j))
    in_specs, args = [a_spec, b_spec], [a, b]
    if bias is not None:
        in_specs.append(pl.BlockSpec((1, tn), lambda i, j, q: (0, j)))
        args.append(bias)
    kw = dict(name=name, grid=(m // tm, n // tn, nk), in_specs=in_specs,
              out_specs=[pl.BlockSpec((tm, tn), lambda i, j, q: (i, j))],
              out_shape=[jax.ShapeDtypeStruct((m, n), out_dtype)],
              scratch_shapes=[pltpu.VMEM((tm, tn), F32)] if use_acc else [],
              compiler_params=_params())
    if exchange is None:
        return _call(body, **kw)(*args)[0]
    outs, got = _with_exchange(body, args, *exchange, **kw)
    return outs[0], got


def _tok_spec(ts):
    return pl.BlockSpec((1, ts, D_MODEL), lambda b, s: (b, s, 0))


def _seq_spec():
    return pl.BlockSpec((1, 1, D_MODEL), lambda b, s: (b, 0, 0))


def _vec_spec():
    return pl.BlockSpec((1, D_MODEL), lambda b, s: (0, 0))


def _norm_fwd(x, g, shift, scale, *, name, yprev=None, gate=None, ts=512):
    bsz, s, _ = x.shape
    res = yprev is not None

    def body(*refs):
        if res:
            x_ref, g_ref, sh_ref, sc_ref, y_ref, gt_ref, h_ref, xn_ref = refs
            xv = x_ref[0] + gt_ref[0] * y_ref[0]
            xn_ref[0] = xv
        else:
            x_ref, g_ref, sh_ref, sc_ref, h_ref = refs
            xv = x_ref[0]
        rstd = lax.rsqrt(jnp.mean(xv * xv, axis=-1, keepdims=True) + RMS_EPS)
        nv = xv * rstd * g_ref[...]
        h_ref[0] = (nv * (1.0 + sc_ref[0]) + sh_ref[0]).astype(BF16)

    in_specs = [_tok_spec(ts), _vec_spec(), _seq_spec(), _seq_spec()]
    args = [x, g, shift, scale]
    out_specs = [_tok_spec(ts)]
    out_shape = [jax.ShapeDtypeStruct(x.shape, BF16)]
    if res:
        in_specs += [_tok_spec(ts), _seq_spec()]
        args += [yprev, gate]
        out_specs.append(_tok_spec(ts))
        out_shape.append(jax.ShapeDtypeStruct(x.shape, F32))
    return _call(body, name=name, grid=(bsz, s // ts), in_specs=in_specs, out_specs=out_specs,
                 out_shape=out_shape, compiler_params=_params())(*args)


def _final(x1, yf, gate2, g, shift, scale, target, *, ts=512):
    bsz, s, _ = x1.shape
    ns = s // ts

    def body(x_ref, y_ref, gt_ref, g_ref, sh_ref, sc_ref, t_ref,
             lp_ref, dx_ref, dyf_ref, dsh_ref, dsc_ref, dgt_ref, dg_ref):
        b, si = pl.program_id(0), pl.program_id(1)
        yv, gt, gv = y_ref[0], gt_ref[0], g_ref[...]
        xv = x_ref[0] + gt * yv
        rstd = lax.rsqrt(jnp.mean(xv * xv, axis=-1, keepdims=True) + RMS_EPS)
        xhat = xv * rstd
        nv = xhat * gv
        sc1 = 1.0 + sc_ref[0]
        err = nv * sc1 + sh_ref[0] - t_ref[0]
        lp_ref[...] = jnp.zeros((1, 1, 8, 128), F32) + jnp.sum(err * err)
        dy = err * (1.0 / D_MODEL)
        dn = dy * sc1
        dxh = dn * gv
        dx = rstd * (dxh - xhat * jnp.mean(dxh * xhat, axis=-1, keepdims=True))
        dx_ref[0] = dx
        dyf_ref[0] = (dx * gt).astype(BF16)
        _acc(dsh_ref, 0, jnp.sum(dy, axis=0, keepdims=True), si == 0)
        _acc(dsc_ref, 0, jnp.sum(dy * nv, axis=0, keepdims=True), si == 0)
        _acc(dgt_ref, 0, jnp.sum(dx * yv, axis=0, keepdims=True), si == 0)
        _acc(dg_ref, slice(None), jnp.sum(dn * xhat, axis=0, keepdims=True), jnp.logical_and(b == 0, si == 0))

    seq_shape = jax.ShapeDtypeStruct((bsz, 1, D_MODEL), F32)
    return _call(
        body, name="final_loss", grid=(bsz, ns),
        in_specs=[_tok_spec(ts), _tok_spec(ts), _seq_spec(), _vec_spec(), _seq_spec(), _seq_spec(), _tok_spec(ts)],
        out_specs=[pl.BlockSpec((1, 1, 8, 128), lambda b, s: (b, s, 0, 0)), _tok_spec(ts), _tok_spec(ts),
                   _seq_spec(), _seq_spec(), _seq_spec(), _vec_spec()],
        out_shape=[jax.ShapeDtypeStruct((bsz, ns, 8, 128), F32), jax.ShapeDtypeStruct(x1.shape, F32),
                   jax.ShapeDtypeStruct(x1.shape, BF16), seq_shape, seq_shape, seq_shape,
                   jax.ShapeDtypeStruct((1, D_MODEL), F32)],
        compiler_params=_params(),
    )(x1, yf, gate2, g, shift, scale, target)


def _norm_bwd(x, dh, g, scale, dres, *, name, yprev=None, gate=None, ts=512):
    bsz, s, _ = x.shape
    res = yprev is not None

    def body(*refs):
        if res:
            (x_ref, dh_ref, g_ref, sc_ref, dr_ref, y_ref, gt_ref,
             dx_ref, dg_ref, dsh_ref, dsc_ref, dy_ref, dgt_ref) = refs
        else:
            x_ref, dh_ref, g_ref, sc_ref, dr_ref, dx_ref, dg_ref, dsh_ref, dsc_ref = refs
        b, si = pl.program_id(0), pl.program_id(1)
        xv, dhv, gv = x_ref[0], dh_ref[0], g_ref[...]
        rstd = lax.rsqrt(jnp.mean(xv * xv, axis=-1, keepdims=True) + RMS_EPS)
        xhat = xv * rstd
        dn = dhv * (1.0 + sc_ref[0])
        dxh = dn * gv
        dx = dr_ref[0] + rstd * (dxh - xhat * jnp.mean(dxh * xhat, axis=-1, keepdims=True))
        dx_ref[0] = dx
        _acc(dsh_ref, 0, jnp.sum(dhv, axis=0, keepdims=True), si == 0)
        _acc(dsc_ref, 0, jnp.sum(dhv * (xhat * gv), axis=0, keepdims=True), si == 0)
        _acc(dg_ref, slice(None), jnp.sum(dn * xhat, axis=0, keepdims=True), jnp.logical_and(b == 0, si == 0))
        if res:
            dy_ref[0] = (dx * gt_ref[0]).astype(BF16)
            _acc(dgt_ref, 0, jnp.sum(dx * y_ref[0], axis=0, keepdims=True), si == 0)

    seq_shape = jax.ShapeDtypeStruct((bsz, 1, D_MODEL), F32)
    in_specs = [_tok_spec(ts), _tok_spec(ts), _vec_spec(), _seq_spec(), _tok_spec(ts)]
    args = [x, dh, g, scale, dres]
    out_specs = [_tok_spec(ts), _vec_spec(), _seq_spec(), _seq_spec()]
    out_shape = [jax.ShapeDtypeStruct(x.shape, F32), jax.ShapeDtypeStruct((1, D_MODEL), F32), seq_shape, seq_shape]
    if res:
        in_specs += [_tok_spec(ts), _seq_spec()]
        args += [yprev, gate]
        out_specs += [_tok_spec(ts), _seq_spec()]
        out_shape += [jax.ShapeDtypeStruct(x.shape, BF16), seq_shape]
    return _call(body, name=name, grid=(bsz, s // ts), in_specs=in_specs, out_specs=out_specs,
                 out_shape=out_shape, compiler_params=_params())(*args)


def _lru_gates(xr, wa, ba, wi, bi, sp):
    xb = xr.astype(BF16)
    r = _sigmoid(jnp.dot(xb, wa, preferred_element_type=F32) + ba)
    i = _sigmoid(jnp.dot(xb, wi, preferred_element_type=F32) + bi)
    la = -LRU_C * r * sp
    a = jnp.exp(la)
    mult = jnp.sqrt(-jnp.tanh(la) * (a * a + 1.0))
    return r, i, a, mult


def _rnn_fwd(z, wc, bc, wa, ba, wi, bi, lam):
    bsz, s, _ = z.shape
    ch = min(SCAN_CHUNK, s)
    nch = s // ch

    def body(z_ref, wc_ref, bc_ref, wa_ref, ba_ref, wi_ref, bi_ref, lam_ref, h_ref, y_ref):
        wcv, bcv = wc_ref[...], bc_ref[...]
        wav, wiv = wa_ref[0].astype(BF16), wi_ref[0].astype(BF16)
        bav, biv = ba_ref[...], bi_ref[...]
        sp = _softplus(-lam_ref[...])
        row = lax.broadcasted_iota(jnp.int32, (ch, RNN_BLK), 0)

        def chunk(ci, carry):
            prev8, hc = carry
            st = pl.multiple_of(ci * ch, ch)
            blk = z_ref[0, pl.ds(st, ch), :]
            xp, gr = blk[:, :RNN_BLK], blk[:, RNN_BLK:]
            xe = jnp.concatenate([prev8, xp], axis=0)
            xr = bcv + wcv[3:4] * xp
            for j in range(1, 4):
                xr = xr + wcv[3 - j:4 - j] * pltpu.roll(xe, j, 0)[8:]
            _, i, a, mult = _lru_gates(xr, wav, bav, wiv, biv, sp)
            aa, uu = a, mult * (i * xr)
            k = 1
            while k < ch:
                m = row >= k
                uu = jnp.where(m, aa * pltpu.roll(uu, k, 0) + uu, uu)
                aa = jnp.where(m, aa * pltpu.roll(aa, k, 0), aa)
                k *= 2
            hh = uu + aa * hc
            h_ref[0, pl.ds(st, ch), :] = hh
            y_ref[0, pl.ds(st, ch), :] = (_gelu(gr) * hh).astype(BF16)
            return xp[ch - 8:], hh[ch - 1:]

        lax.fori_loop(0, nch, chunk, (jnp.zeros((8, RNN_BLK), F32), jnp.zeros((1, RNN_BLK), F32)))

    vec = pl.BlockSpec((1, RNN_BLK), lambda b, n: (0, n))
    mat = pl.BlockSpec((1, RNN_BLK, RNN_BLK), lambda b, n: (n, 0, 0))
    out = pl.BlockSpec((1, s, RNN_BLK), lambda b, n: (b, 0, n))
    return _call(
        body, name="rnn_fwd", grid=(bsz, N_BLK),
        in_specs=[pl.BlockSpec((1, s, 2 * RNN_BLK), lambda b, n: (b, 0, XG0 // (2 * RNN_BLK) + n)),
                  pl.BlockSpec((4, RNN_BLK), lambda b, n: (0, n)), vec, mat, vec, mat, vec, vec],
        out_specs=[out, out],
        out_shape=[jax.ShapeDtypeStruct((bsz, s, D_RNN), F32), jax.ShapeDtypeStruct((bsz, s, D_RNN), BF16)],
        compiler_params=_params(),
    )(z, wc, bc, wa, ba, wi, bi, lam)


def _rnn_bwd(z, h, dy, dz, wc, bc, wa, ba, wi, bi, lam, exchange=None):
    bsz, s, _ = z.shape
    ch = min(SCAN_CHUNK, s)
    nch = s // ch

    def body(z_ref, h_ref, dy_ref, dz_in, wc_ref, bc_ref, wa_ref, ba_ref, wi_ref, bi_ref, lam_ref,
             dz_ref, dwa_ref, dwi_ref, dsm_ref):
        del dz_in
        b = pl.program_id(1)
        wcv, bcv = wc_ref[...], bc_ref[...]
        wav, wiv = wa_ref[0].astype(BF16), wi_ref[0].astype(BF16)
        bav, biv = ba_ref[...], bi_ref[...]
        lamv = lam_ref[...]
        sp = _softplus(-lamv)
        row = lax.broadcasted_iota(jnp.int32, (ch, RNN_BLK), 0)

        @pl.when(b == 0)
        def _():
            dwa_ref[...] = jnp.zeros_like(dwa_ref)
            dwi_ref[...] = jnp.zeros_like(dwi_ref)
            dsm_ref[...] = jnp.zeros_like(dsm_ref)

        def chunk(k, carry):
            a_next, g_next, dxr_next8 = carry
            ci = nch - 1 - k
            st = pl.multiple_of(ci * ch, ch)
            pst = pl.multiple_of(jnp.maximum(st - 8, 0), 8)
            has_prev = (ci > 0).astype(F32)
            blk = z_ref[0, pl.ds(st, ch), :]
            xp, gr = blk[:, :RNN_BLK], blk[:, RNN_BLK:]
            xprev8 = z_ref[0, pl.ds(pst, 8), :][:, :RNN_BLK] * has_prev
            hprev8 = h_ref[0, pl.ds(pst, 8), :] * has_prev
            xe = jnp.concatenate([xprev8, xp], axis=0)
            xs = [xp] + [pltpu.roll(xe, j, 0)[8:] for j in range(1, 4)]
            xr = bcv
            for kk in range(4):
                xr = xr + wcv[kk:kk + 1] * xs[3 - kk]
            r, i, a, mult = _lru_gates(xr, wav, bav, wiv, biv, sp)
            hh = h_ref[0, pl.ds(st, ch), :]
            hm1 = pltpu.roll(jnp.concatenate([hprev8, hh], axis=0), 1, 0)[8:]
            dyv = dy_ref[0, pl.ds(st, ch), :]
            ge, dge = _gelu_and_grad(gr)
            dgr = dyv * hh * dge
            aa = jnp.where(row == ch - 1, a_next, pltpu.roll(a, ch - 1, 0))
            gg = dyv * ge
            kk = 1
            while kk < ch:
                m = row < ch - kk
                gg = jnp.where(m, aa * pltpu.roll(gg, ch - kk, 0) + gg, gg)
                aa = jnp.where(m, aa * pltpu.roll(aa, ch - kk, 0), aa)
                kk *= 2
            gg = gg + aa * g_next
            da = gg * hm1
            ix = i * xr
            dmult = gg * ix
            di = gg * mult * xr
            dxr = gg * mult * i
            dla = da * a - dmult * (a * a) / mult
            dr = dla * (-LRU_C * sp)
            dsp = jnp.sum(dla * (-LRU_C * r), axis=0, keepdims=True)
            dpa = dr * r * (1.0 - r)
            dpi = di * i * (1.0 - i)
            dpab, dpib, xrb = dpa.astype(BF16), dpi.astype(BF16), xr.astype(BF16)
            nt = (((1,), (1,)), ((), ()))
            tn = (((0,), (0,)), ((), ()))
            dxr = dxr + lax.dot_general(dpab, wav, nt, preferred_element_type=F32)
            dxr = dxr + lax.dot_general(dpib, wiv, nt, preferred_element_type=F32)
            dwa_ref[0] = dwa_ref[0] + lax.dot_general(xrb, dpab, tn, preferred_element_type=F32)
            dwi_ref[0] = dwi_ref[0] + lax.dot_general(xrb, dpib, tn, preferred_element_type=F32)
            rows = [jnp.sum(dxr * xs[3 - q], axis=0, keepdims=True) for q in range(4)]
            rows += [jnp.sum(dxr, axis=0, keepdims=True), jnp.sum(dpa, axis=0, keepdims=True),
                     jnp.sum(dpi, axis=0, keepdims=True), dsp * (-_sigmoid(-lamv))]
            dsm_ref[...] = dsm_ref[...] + jnp.concatenate(rows, axis=0)
            dxe = jnp.concatenate([dxr, dxr_next8], axis=0)
            dxp = wcv[3:4] * dxr
            for j in range(1, 4):
                dxp = dxp + wcv[3 - j:4 - j] * pltpu.roll(dxe, ch + 8 - j, 0)[:ch]
            dz_ref[0, pl.ds(st, ch), :] = jnp.concatenate([dxp, dgr], axis=1).astype(BF16)
            return a[0:1], gg[0:1], dxr[0:8]

        lax.fori_loop(0, nch, chunk, (jnp.zeros((1, RNN_BLK), F32), jnp.zeros((1, RNN_BLK), F32),
                                      jnp.zeros((8, RNN_BLK), F32)))

    vec = pl.BlockSpec((1, RNN_BLK), lambda n, b: (0, n))
    mat = pl.BlockSpec((1, RNN_BLK, RNN_BLK), lambda n, b: (n, 0, 0))
    tok = pl.BlockSpec((1, s, RNN_BLK), lambda n, b: (b, 0, n))
    zblk = pl.BlockSpec((1, s, 2 * RNN_BLK), lambda n, b: (b, 0, XG0 // (2 * RNN_BLK) + n))
    return _run(
        body, [z, h, dy, dz, wc, bc, wa, ba, wi, bi, lam], exchange, name="rnn_bwd", grid=(N_BLK, bsz),
        in_specs=[zblk, tok, tok, ANY, pl.BlockSpec((4, RNN_BLK), lambda n, b: (0, n)), vec, mat, vec, mat, vec, vec],
        out_specs=[zblk, mat, mat, pl.BlockSpec((8, RNN_BLK), lambda n, b: (0, n))],
        out_shape=[jax.ShapeDtypeStruct(dz.shape, dz.dtype), jax.ShapeDtypeStruct((N_BLK, RNN_BLK, RNN_BLK), F32),
                   jax.ShapeDtypeStruct((N_BLK, RNN_BLK, RNN_BLK), F32), jax.ShapeDtypeStruct((8, D_RNN), F32)],
        input_output_aliases={3: 0},
        compiler_params=_params())


def _tri(n, upper):
    r = lax.broadcasted_iota(jnp.int32, (n, n), 0)
    c = lax.broadcasted_iota(jnp.int32, (n, n), 1)
    return (r <= c if upper else r >= c).astype(BF16)


def _exact_dot(x, t):
    hi, mid, lo = _split3(x)
    return (jnp.dot(hi, t, preferred_element_type=F32) + jnp.dot(mid, t, preferred_element_type=F32)
            + jnp.dot(lo, t, preferred_element_type=F32))


def _fgate_fwd(flt, bfg):
    bsz, nh, s = flt.shape
    blk = min(CUM_BLK, s)

    def body(fl_ref, b_ref, f_ref):
        tri = _tri(blk, True)
        carry = jnp.zeros((nh, 1), F32)
        for q in range(s // blk):
            xq = fl_ref[0, :, q * blk:(q + 1) * blk] + b_ref[...]
            fq = _exact_dot(-_softplus(-xq), tri) + carry
            f_ref[0, :, q * blk:(q + 1) * blk] = fq
            carry = fq[:, blk - 1:blk]

    spec = pl.BlockSpec((1, nh, s), lambda b: (b, 0, 0))
    return _call(body, name="fgate_fwd", grid=(bsz,), in_specs=[spec, pl.BlockSpec((nh, 1), lambda b: (0, 0))],
                 out_specs=spec, out_shape=jax.ShapeDtypeStruct(flt.shape, F32), compiler_params=_params())(flt, bfg)


def _fgate_bwd(dft, flt, bfg):
    bsz, nh, s = flt.shape
    blk = min(CUM_BLK, s)

    def body(df_ref, fl_ref, b_ref, dfl_ref, db_ref):
        b = pl.program_id(0)
        tri = _tri(blk, False)
        carry = jnp.zeros((nh, 1), F32)
        tot = jnp.zeros((nh, 1), F32)
        for q in reversed(range(s // blk)):
            sl = slice(q * blk, (q + 1) * blk)
            dlq = _exact_dot(df_ref[0, :, sl], tri) + carry
            carry = dlq[:, 0:1]
            dflq = dlq * _sigmoid(-(fl_ref[0, :, sl] + b_ref[...]))
            dfl_ref[0, :, sl] = dflq
            tot = tot + jnp.sum(dflq, axis=1, keepdims=True)
        _acc(db_ref, slice(None), tot, b == 0)

    spec = pl.BlockSpec((1, nh, s), lambda b: (b, 0, 0))
    small = pl.BlockSpec((nh, 1), lambda b: (0, 0))
    return _call(body, name="fgate_bwd", grid=(bsz,), in_specs=[spec, spec, small], out_specs=[spec, small],
                 out_shape=[jax.ShapeDtypeStruct(flt.shape, F32), jax.ShapeDtypeStruct((nh, 1), F32)],
                 compiler_params=_params())(dft, flt, bfg)


def _attn_scores(q_b, k_ref, fcol_ref, frow_ref, q0, tq, ln):
    kk = k_ref[0, :ln, HEAD_DIM:2 * HEAD_DIM].astype(BF16)
    sc = lax.dot_general(q_b, kk, (((1,), (1,)), ((), ())), preferred_element_type=F32)
    sc = sc + fcol_ref[0, 0, q0:q0 + tq, :] - frow_ref[0, 0, :, :ln]
    qi = q0 + lax.broadcasted_iota(jnp.int32, (tq, ln), 0)
    ki = lax.broadcasted_iota(jnp.int32, (tq, ln), 1)
    return sc, qi >= ki


def _attn_fwd(z, fcol, frow, exchange=None):
    bsz, s, _ = z.shape
    tq = min(ATTN_TQ, s)
    scale = HEAD_DIM ** -0.5

    def body(z_ref, fcol_ref, frow_ref, o_ref, lse_ref):
        for qb in range(s // tq):
            q0, ln = qb * tq, (qb + 1) * tq
            q_b = (z_ref[0, q0:ln, :HEAD_DIM] * scale).astype(BF16)
            sc, mask = _attn_scores(q_b, z_ref, fcol_ref, frow_ref, q0, tq, ln)
            sc = jnp.where(mask, sc, -jnp.inf)
            mx = jnp.max(sc, axis=-1, keepdims=True)
            p = jnp.exp(sc - mx)
            l = jnp.sum(p, axis=-1, keepdims=True)
            vv = z_ref[0, :ln, 2 * HEAD_DIM:].astype(BF16)
            o = jnp.dot(p.astype(BF16), vv, preferred_element_type=F32) / l
            o_ref[0, q0:ln, :] = o.astype(BF16)
            lse_ref[0, 0, q0:ln, :] = mx + jnp.log(l)

    return _run(
        body, [z, fcol, frow], exchange, name="attn_fwd", grid=(bsz, N_HEADS),
        in_specs=[pl.BlockSpec((1, s, 3 * HEAD_DIM), lambda b, h: (b, 0, h)),
                  pl.BlockSpec((1, 1, s, 1), lambda b, h: (b, h, 0, 0)),
                  pl.BlockSpec((1, 1, 1, s), lambda b, h: (b, h, 0, 0))],
        out_specs=[pl.BlockSpec((1, s, HEAD_DIM), lambda b, h: (b, 0, h)),
                   pl.BlockSpec((1, 1, s, 1), lambda b, h: (b, h, 0, 0))],
        out_shape=[jax.ShapeDtypeStruct((bsz, s, N_HEADS * HEAD_DIM), BF16),
                   jax.ShapeDtypeStruct((bsz, N_HEADS, s, 1), F32)],
        compiler_params=_params())


def _attn_bwd(z, do, lse, fcol, frow, dz, exchange=None):
    bsz, s, _ = z.shape
    tq = min(ATTN_TQ, s)
    scale = HEAD_DIM ** -0.5
    nt = (((1,), (1,)), ((), ()))
    tn = (((0,), (0,)), ((), ()))

    def body(z_ref, do_ref, lse_ref, fcol_ref, frow_ref, dz_in, dz_ref, df_ref, dk_acc, dv_acc):
        del dz_in
        dk_acc[...] = jnp.zeros_like(dk_acc)
        dv_acc[...] = jnp.zeros_like(dv_acc)
        df_ref[...] = jnp.zeros_like(df_ref)
        for qb in range(s // tq):
            q0, ln = qb * tq, (qb + 1) * tq
            q_b = (z_ref[0, q0:ln, :HEAD_DIM] * scale).astype(BF16)
            sc, mask = _attn_scores(q_b, z_ref, fcol_ref, frow_ref, q0, tq, ln)
            p = jnp.where(mask, jnp.exp(sc - lse_ref[0, 0, q0:ln, :]), 0.0)
            do_b = do_ref[0, q0:ln, :]
            pb = p.astype(BF16)
            dv_acc[:ln, :] = dv_acc[:ln, :] + lax.dot_general(pb, do_b, tn, preferred_element_type=F32)
            vv = z_ref[0, :ln, 2 * HEAD_DIM:].astype(BF16)
            dp = lax.dot_general(do_b, vv, nt, preferred_element_type=F32)
            ds = p * (dp - jnp.sum(p * dp, axis=-1, keepdims=True))
            dsb = ds.astype(BF16)
            kk = z_ref[0, :ln, HEAD_DIM:2 * HEAD_DIM].astype(BF16)
            dq = jnp.dot(dsb, kk, preferred_element_type=F32) * scale
            dz_ref[0, q0:ln, :HEAD_DIM] = dq.astype(BF16)
            dk_acc[:ln, :] = dk_acc[:ln, :] + lax.dot_general(dsb, q_b, tn, preferred_element_type=F32)
            df_ref[0, 0, :, :ln] = df_ref[0, 0, :, :ln] - jnp.sum(ds, axis=0, keepdims=True)
        dz_ref[0, :, HEAD_DIM:2 * HEAD_DIM] = dk_acc[...].astype(BF16)
        dz_ref[0, :, 2 * HEAD_DIM:] = dv_acc[...].astype(BF16)

    zblk = pl.BlockSpec((1, s, 3 * HEAD_DIM), lambda b, h: (b, 0, h))
    hblk = pl.BlockSpec((1, s, HEAD_DIM), lambda b, h: (b, 0, h))
    col = pl.BlockSpec((1, 1, s, 1), lambda b, h: (b, h, 0, 0))
    rowspec = pl.BlockSpec((1, 1, 1, s), lambda b, h: (b, h, 0, 0))
    return _run(
        body, [z, do, lse, fcol, frow, dz], exchange, name="attn_bwd", grid=(bsz, N_HEADS),
        in_specs=[zblk, hblk, col, col, rowspec, ANY],
        out_specs=[zblk, rowspec],
        out_shape=[jax.ShapeDtypeStruct(dz.shape, dz.dtype), jax.ShapeDtypeStruct((bsz, N_HEADS, 1, s), F32)],
        scratch_shapes=[pltpu.VMEM((s, HEAD_DIM), F32), pltpu.VMEM((s, HEAD_DIM), F32)],
        input_output_aliases={5: 0},
        compiler_params=_params())


def _merge_fwd(z2, pr, pa, *, tr=512):
    t = z2.shape[0]

    def body(mg_ref, pr_ref, pa_ref, o_ref):
        gr_ = _sigmoid(mg_ref[:, :D_MODEL])
        ga = _sigmoid(mg_ref[:, D_MODEL:])
        o_ref[...] = (gr_ * pr_ref[...] + ga * pa_ref[...]).astype(BF16)

    tok = pl.BlockSpec((tr, D_MODEL), lambda i: (i, 0))
    return _call(body, name="merge_fwd", grid=(t // tr,),
                 in_specs=[pl.BlockSpec((tr, 2 * D_MODEL), lambda i: (i, MG0 // (2 * D_MODEL))), tok, tok],
                 out_specs=tok, out_shape=jax.ShapeDtypeStruct((t, D_MODEL), BF16),
                 compiler_params=_params())(z2, pr, pa)


def _merge_bwd(dm, z2, pr, pa, *, tr=512):
    t = z2.shape[0]

    def body(dm_ref, mg_ref, pr_ref, pa_ref, dz_ref, dpr_ref, dpa_ref):
        dmv = dm_ref[...]
        gr_ = _sigmoid(mg_ref[:, :D_MODEL])
        ga = _sigmoid(mg_ref[:, D_MODEL:])
        dpr_ref[...] = (dmv * gr_).astype(BF16)
        dpa_ref[...] = (dmv * ga).astype(BF16)
        dz_ref[:, :D_MODEL] = (dmv * pr_ref[...] * gr_ * (1.0 - gr_)).astype(BF16)
        dz_ref[:, D_MODEL:] = (dmv * pa_ref[...] * ga * (1.0 - ga)).astype(BF16)

    tok = pl.BlockSpec((tr, D_MODEL), lambda i: (i, 0))
    mg = pl.BlockSpec((tr, 2 * D_MODEL), lambda i: (i, MG0 // (2 * D_MODEL)))
    return _call(body, name="merge_bwd", grid=(t // tr,), in_specs=[tok, mg, tok, tok],
                 out_specs=[mg, tok, tok],
                 out_shape=[jax.ShapeDtypeStruct(z2.shape, BF16), jax.ShapeDtypeStruct((t, D_MODEL), BF16),
                            jax.ShapeDtypeStruct((t, D_MODEL), BF16)],
                 compiler_params=_params())(dm, z2, pr, pa)


def _fl_write(dfl_blk, dz):
    bsz, s, w = dfl_blk.shape

    def body(src_ref, dz_in, dz_ref):
        del dz_in
        dz_ref[...] = src_ref[...]

    return _call(body, name="fl_write", grid=(bsz,),
                 in_specs=[pl.BlockSpec((1, s, w), lambda b: (b, 0, 0)), ANY],
                 out_specs=pl.BlockSpec((1, s, w), lambda b: (b, 0, FL0 // w)),
                 out_shape=jax.ShapeDtypeStruct(dz.shape, dz.dtype), input_output_aliases={1: 0},
                 compiler_params=_params())(dfl_blk, dz)


def _ffn_fwd(zf, wc, bc):
    bsz, s, _ = zf.shape
    cw = 128

    def body(zf_ref, wc_ref, bc_ref, o_ref):
        gfp, uf = zf_ref[0, :, :cw], zf_ref[0, :, cw:]
        wcv = wc_ref[...]
        row = lax.broadcasted_iota(jnp.int32, (s, cw), 0)
        gf = bc_ref[...] + wcv[2:3] * gfp
        for j in (1, 2):
            gf = gf + wcv[2 - j:3 - j] * jnp.where(row >= j, pltpu.roll(gfp, j, 0), 0.0)
        o_ref[0] = (_gelu(gf) * uf).astype(BF16)

    return _call(body, name="ffn_fwd", grid=(bsz, D_FF // cw),
                 in_specs=[pl.BlockSpec((1, s, 2 * cw), lambda b, j: (b, 0, j)),
                           pl.BlockSpec((3, cw), lambda b, j: (0, j)), pl.BlockSpec((1, cw), lambda b, j: (0, j))],
                 out_specs=pl.BlockSpec((1, s, cw), lambda b, j: (b, 0, j)),
                 out_shape=jax.ShapeDtypeStruct((bsz, s, D_FF), BF16), compiler_params=_params())(zf, wc, bc)


def _ffn_bwd(dact, zf, wc, bc, exchange=None):
    bsz, s, _ = zf.shape
    cw = 128

    def body(da_ref, zf_ref, wc_ref, bc_ref, dzf_ref, dsm_ref):
        b = pl.program_id(1)
        gfp, uf = zf_ref[0, :, :cw], zf_ref[0, :, cw:]
        wcv = wc_ref[...]
        row = lax.broadcasted_iota(jnp.int32, (s, cw), 0)
        sh = [gfp] + [jnp.where(row >= j, pltpu.roll(gfp, j, 0), 0.0) for j in (1, 2)]
        gf = bc_ref[...]
        for kk in range(3):
            gf = gf + wcv[kk:kk + 1] * sh[2 - kk]
        ge, dge = _gelu_and_grad(gf)
        dav = da_ref[0]
        dgf = dav * uf * dge
        dgfp = wcv[2:3] * dgf
        for j in (1, 2):
            dgfp = dgfp + wcv[2 - j:3 - j] * jnp.where(row < s - j, pltpu.roll(dgf, s - j, 0), 0.0)
        dzf_ref[0] = jnp.concatenate([dgfp, dav * ge], axis=1).astype(BF16)
        rows = [jnp.sum(dgf * sh[2 - kk], axis=0, keepdims=True) for kk in range(3)]
        rows += [jnp.sum(dgf, axis=0, keepdims=True), jnp.zeros((4, cw), F32)]
        _acc(dsm_ref, slice(None), jnp.concatenate(rows, axis=0), b == 0)

    return _run(body, [dact, zf, wc, bc], exchange, name="ffn_bwd", grid=(D_FF // cw, bsz),
                in_specs=[pl.BlockSpec((1, s, cw), lambda j, b: (b, 0, j)),
                          pl.BlockSpec((1, s, 2 * cw), lambda j, b: (b, 0, j)),
                          pl.BlockSpec((3, cw), lambda j, b: (0, j)), pl.BlockSpec((1, cw), lambda j, b: (0, j))],
                out_specs=[pl.BlockSpec((1, s, 2 * cw), lambda j, b: (b, 0, j)),
                           pl.BlockSpec((8, cw), lambda j, b: (0, j))],
                out_shape=[jax.ShapeDtypeStruct(zf.shape, BF16), jax.ShapeDtypeStruct((8, D_FF), F32)],
                compiler_params=_params())


def _adamw(w, m, v, parts, *, name):
    r, c = w.shape
    p = parts.shape[0]
    tr = r if r <= 512 else 256
    assert r % tr == 0 and parts.shape[1:] == (r, c), (name, w.shape, parts.shape)
    c1 = 1.0 / (1.0 - ADAM_B1 ** ADAM_STEP)
    c2 = 1.0 / (1.0 - ADAM_B2 ** ADAM_STEP)

    def body(w_ref, m_ref, v_ref, p_ref, g_ref, d_ref, mo_ref, vo_ref):
        g = p_ref[0].astype(F32)
        for q in range(1, p):
            g = g + p_ref[q].astype(F32)
        mn = ADAM_B1 * m_ref[...] + (1.0 - ADAM_B1) * g
        vn = ADAM_B2 * v_ref[...] + (1.0 - ADAM_B2) * (g * g)
        g_ref[...] = g
        mo_ref[...] = mn
        vo_ref[...] = vn
        d_ref[...] = -ADAM_LR * ((mn * c1) / (jnp.sqrt(vn * c2) + ADAM_EPS) + ADAM_WD * w_ref[...])

    spec = pl.BlockSpec((tr, c), lambda i: (i, 0))
    shp = jax.ShapeDtypeStruct((r, c), F32)
    return _call(body, name=name, grid=(r // tr,),
                 in_specs=[spec, spec, spec, pl.BlockSpec((p, tr, c), lambda i: (0, i, 0))],
                 out_specs=[spec] * 4, out_shape=[shp] * 4, compiler_params=_params())(w, m, v, parts)


def _interleave(w, groups, width):
    n = w.shape[1] // (groups * width)
    return jnp.concatenate([w[:, (g * n + j) * width:(g * n + j + 1) * width]
                            for j in range(n) for g in range(groups)], axis=1)


def _deinterleave(w, groups, width):
    n = w.shape[1] // (groups * width)
    return jnp.concatenate([w[:, (j * groups + g) * width:(j * groups + g + 1) * width]
                            for g in range(groups) for j in range(n)], axis=1)


def _up_from_shards(ex):
    per = ex.shape[2] // 128
    blocks = []
    for j in range(D_FF // 128):
        for r in range(2):
            i, lb = divmod(r * (D_FF // 128) + j, per)
            blocks.append(ex[i, :, lb * 128:(lb + 1) * 128])
    return jnp.concatenate(blocks, axis=1)


def _up_to_shards(g):
    per = 2 * D_FF // 128 // N_DEV
    shards = []
    for i in range(N_DEV):
        blocks = []
        for ob in range(per * i, per * (i + 1)):
            r, j = divmod(ob, D_FF // 128)
            blocks.append(g[:, (2 * j + r) * 128:(2 * j + r + 1) * 128])
        shards.append(jnp.concatenate(blocks, axis=1))
    return jnp.stack(shards)


def _to_z_layout(w):
    xg = _interleave(w[:, :2 * D_RNN], 2, RNN_BLK)
    qkv = _interleave(w[:, 2 * D_RNN:2 * D_RNN + 3 * D_MODEL], 3, HEAD_DIM)
    fl0 = 2 * D_RNN + 3 * D_MODEL
    fl = w[:, fl0:fl0 + N_HEADS]
    mg = w[:, fl0 + N_HEADS:]
    pad = jnp.zeros((w.shape[0], MG0 - FL0 - N_HEADS), w.dtype)
    return jnp.concatenate([qkv, xg, fl, pad, mg], axis=1)


def _from_z_layout(w):
    qkv = _deinterleave(w[:, QKV0:XG0], 3, HEAD_DIM)
    xg = _deinterleave(w[:, XG0:FL0], 2, RNN_BLK)
    return jnp.concatenate([xg, qkv, w[:, FL0:FL0 + N_HEADS], w[:, MG0:]], axis=1)


def _col_gather(g):
    return g.transpose(1, 0, 2).reshape(g.shape[1], -1)


def _col_scatter(w):
    return w.reshape(w.shape[0], N_DEV, -1).transpose(1, 0, 2)


def kernel(x, c, w_ada, b_ada, g_norm1, w_in, w_rnn_conv, b_rnn_conv, w_lru_a, b_lru_a, w_lru_i, b_lru_i, lru_lambda, b_fgate, w_proj_rnn, w_proj_attn, w_out, g_norm2, w_ffn_up, w_ffn_conv, b_ffn_conv, w_ffn_down, w_ada_final, b_ada_final, g_final, loss_target, m_w_ada, m_b_ada, m_g_norm1, m_w_in, m_w_rnn_conv, m_b_rnn_conv, m_w_lru_a, m_b_lru_a, m_w_lru_i, m_b_lru_i, m_lru_lambda, m_b_fgate, m_w_proj_rnn, m_w_proj_attn, m_w_out, m_g_norm2, m_w_ffn_up, m_w_ffn_conv, m_b_ffn_conv, m_w_ffn_down, m_w_ada_final, m_b_ada_final, m_g_final, v_w_ada, v_b_ada, v_g_norm1, v_w_in, v_w_rnn_conv, v_b_rnn_conv, v_w_lru_a, v_b_lru_a, v_w_lru_i, v_b_lru_i, v_lru_lambda, v_b_fgate, v_w_proj_rnn, v_w_proj_attn, v_w_out, v_g_norm2, v_w_ffn_up, v_w_ffn_conv, v_b_ffn_conv, v_w_ffn_down, v_w_ada_final, v_b_ada_final, v_g_final):
    args = dict(locals())
    bsz, s, _ = x.shape
    t = bsz * s
    nb = N_DEV * bsz
    me = 4 * lax.axis_index("x") + 2 * lax.axis_index("y") + lax.axis_index("c")
    tm = min(1024, t)

    ex_a = _gather_two_level([c, w_in[0].astype(BF16), w_rnn_conv[0], w_ffn_conv[0]], "gather_first")
    c_all = ex_a[0].reshape(nb, D_MODEL)
    win_z = _to_z_layout(_col_gather(ex_a[1]))
    wrc = _col_gather(ex_a[2])
    wfc = _col_gather(ex_a[3])

    nmod = w_ada.shape[2]
    nmodf = w_ada_final.shape[1]
    mod_cols = _mm(c_all, w_ada[0], name="mod_cols", tm=nb, tn=nmod, tk=D_MODEL, silu_a=True,
                   bias=lax.dynamic_slice(b_ada, (0, me * nmod), (1, nmod)))
    modf_cols = _mm(c_all, w_ada_final, name="modf_cols", tm=nb, tn=nmodf, tk=D_MODEL, silu_a=True,
                    bias=lax.dynamic_slice(b_ada_final.reshape(1, -1), (0, me * nmodf), (1, nmodf)))
    ex_b = _exchange([mod_cols, modf_cols], [True, True], "gather_mod")
    mod = lax.dynamic_slice(ex_b[0], (0, me * bsz, 0), (N_DEV, bsz, nmod)).transpose(1, 0, 2).reshape(bsz, 6, 1, D_MODEL)
    modf = lax.dynamic_slice(ex_b[1], (0, me * bsz, 0), (N_DEV, bsz, nmodf)).transpose(1, 0, 2).reshape(bsz, 2, 1, D_MODEL)
    shift1, scale1, gate1, shift2, scale2, gate2 = [mod[:, i] for i in range(6)]
    shift_f, scale_f = modf[:, 0], modf[:, 1]

    h1 = _norm_fwd(x, g_norm1, shift1, scale1, name="norm1_fwd")[0]
    z, got = _mm(h1.reshape(t, D_MODEL), win_z, name="mm_in", tm=tm, tn=1024, tk=D_MODEL,
                 exchange=([w_ffn_up[0].astype(BF16)], [True]))
    z = z.reshape(bsz, s, NZ)
    wup_z = _up_from_shards(got[0])
    h_rnn, y_rnn = _rnn_fwd(z, wrc, b_rnn_conv, w_lru_a[0], b_lru_a, w_lru_i[0], b_lru_i, lru_lambda)
    flt = z[:, :, FL0:FL0 + N_HEADS].transpose(0, 2, 1)
    bfg = b_fgate.reshape(N_HEADS, 1)
    fcum = _fgate_fwd(flt, bfg)
    fcol = fcum.reshape(bsz, N_HEADS, s, 1)
    frow = fcum.reshape(bsz, N_HEADS, 1, s)
    o_attn, lse, got = _attn_fwd(
        z, fcol, frow, exchange=([w_proj_rnn[0].astype(BF16), w_proj_attn[0].astype(BF16), w_out[0].astype(BF16),
                                  w_ffn_down[0].astype(BF16)], [True] * 4))
    wpr = got[0].reshape(D_RNN, D_MODEL)
    wpa = got[1].reshape(D_MODEL, D_MODEL)
    wout = got[2].reshape(D_MODEL, D_MODEL)
    wdown = got[3].reshape(D_FF, D_MODEL)
    pr = _mm(y_rnn.reshape(t, D_RNN), wpr, name="mm_proj_rnn", tm=tm, tn=D_MODEL, tk=D_RNN)
    pa = _mm(o_attn.reshape(t, D_MODEL), wpa, name="mm_proj_attn", tm=tm, tn=D_MODEL, tk=D_MODEL)
    z2 = z.reshape(t, NZ)
    merged = _merge_fwd(z2, pr, pa)
    mo = _mm(merged, wout, name="mm_out", tm=tm, tn=D_MODEL, tk=D_MODEL).reshape(bsz, s, D_MODEL)
    h2, x1 = _norm_fwd(x, g_norm2, shift2, scale2, name="norm2_fwd", yprev=mo, gate=gate1)
    zf = _mm(h2.reshape(t, D_MODEL), wup_z, name="mm_up", tm=tm, tn=1024, tk=D_MODEL).reshape(bsz, s, 2 * D_FF)
    act = _ffn_fwd(zf, wfc, b_ffn_conv)
    yf = _mm(act.reshape(t, D_FF), wdown, name="mm_down", tm=tm, tn=D_MODEL, tk=1024).reshape(bsz, s, D_MODEL)
    lp, dx2, dyf, dshift_f, dscale_f, dgate2, dg_final = _final(x1, yf, gate2, g_final.reshape(1, -1), shift_f, scale_f, loss_target)
    loss = lax.psum(jnp.sum(lp[:, :, 0, 0]) * (0.5 / D_MODEL), ("x", "y", "c"))

    dyf2 = dyf.reshape(t, D_MODEL)
    act2 = act.reshape(t, D_FF)
    dact = _mm(dyf2, wdown, name="mm_down_dx", tb=True, tm=tm, tn=1024, tk=D_MODEL).reshape(bsz, s, D_FF)
    g_wdown = _mm(act2, dyf2, name="mm_down_dw", ta=True, out_dtype=BF16, tm=1024, tn=D_MODEL, tk=tm)
    dzf, dsm_ffn, got = _ffn_bwd(dact, zf, wfc, b_ffn_conv, exchange=([g_wdown.reshape(N_DEV, -1, D_MODEL)], [False]))
    p_wdown = got[0]
    dzf2 = dzf.reshape(t, 2 * D_FF)
    dh2 = _mm(dzf2, wup_z, name="mm_up_dx", tb=True, tm=tm, tn=D_MODEL, tk=1024).reshape(bsz, s, D_MODEL)
    g_wup_z = _mm(h2.reshape(t, D_MODEL), dzf2, name="mm_up_dw", ta=True, out_dtype=BF16, tm=D_MODEL, tn=1024, tk=tm)
    dx1, dg_norm2, dshift2, dscale2, dmo, dgate1 = _norm_bwd(x1, dh2, g_norm2, scale2, dx2, name="norm2_bwd", yprev=mo, gate=gate1)
    dmo2 = dmo.reshape(t, D_MODEL)
    dmerged = _mm(dmo2, wout, name="mm_out_dx", tb=True, tm=tm, tn=D_MODEL, tk=D_MODEL)
    g_wout = _mm(merged, dmo2, name="mm_out_dw", ta=True, out_dtype=BF16, tm=D_MODEL, tn=D_MODEL, tk=tm)
    dz2, dpr, dpa = _merge_bwd(dmerged, z2, pr, pa)
    dy_rnn = _mm(dpr, wpr, name="mm_proj_rnn_dx", tb=True, tm=tm, tn=D_RNN, tk=D_MODEL).reshape(bsz, s, D_RNN)
    g_wpr = _mm(y_rnn.reshape(t, D_RNN), dpr, name="mm_proj_rnn_dw", ta=True, out_dtype=BF16, tm=D_RNN, tn=D_MODEL, tk=tm)
    do = _mm(dpa, wpa, name="mm_proj_attn_dx", tb=True, out_dtype=BF16, tm=tm, tn=D_MODEL, tk=D_MODEL).reshape(bsz, s, D_MODEL)
    g_wpa = _mm(o_attn.reshape(t, D_MODEL), dpa, name="mm_proj_attn_dw", ta=True, out_dtype=BF16, tm=D_MODEL, tn=D_MODEL, tk=tm)
    dz, dfrow, got = _attn_bwd(z, do, lse, fcol, frow, dz2.reshape(bsz, s, NZ),
                               exchange=([_up_to_shards(g_wup_z), g_wout.reshape(N_DEV, -1, D_MODEL)], [False, False]))
    p_wup, p_wout = got
    dflt, db_fgate = _fgate_bwd(dfrow.reshape(bsz, N_HEADS, s), flt, bfg)
    dfl_blk = jnp.pad(dflt.transpose(0, 2, 1).astype(BF16), ((0, 0), (0, 0), (0, MG0 - FL0 - N_HEADS)))
    dz = _fl_write(dfl_blk, dz)
    dz, g_wa, g_wi, dsm_rnn, got = _rnn_bwd(
        z, h_rnn, dy_rnn, dz, wrc, b_rnn_conv, w_lru_a[0], b_lru_a, w_lru_i[0], b_lru_i, lru_lambda,
        exchange=([g_wpr.reshape(N_DEV, -1, D_MODEL), g_wpa.reshape(N_DEV, -1, D_MODEL)], [False, False]))
    p_wpr, p_wpa = got
    dz2 = dz.reshape(t, NZ)
    g_win_z = _mm(h1.reshape(t, D_MODEL), dz2, name="mm_in_dw", ta=True, out_dtype=BF16, tm=D_MODEL, tn=1024, tk=tm)
    dh1, got = _mm(dz2, win_z, name="mm_in_dx", tb=True, tm=tm, tn=D_MODEL, tk=1024,
                   exchange=([_col_scatter(_from_z_layout(g_win_z))], [False]))
    p_win = got[0]
    grad_x, dg_norm1, dshift1, dscale1 = _norm_bwd(x, dh1.reshape(bsz, s, D_MODEL), g_norm1, scale1, dx1, name="norm1_bwd")

    dmod = jnp.concatenate([dshift1, dscale1, dgate1, dshift2, dscale2, dgate2], axis=1).reshape(bsz, 6 * D_MODEL)
    dmodf = jnp.concatenate([dshift_f, dscale_f], axis=1).reshape(bsz, 2 * D_MODEL)
    small = [dg_norm1, dg_norm2, dg_final, dsm_rnn, g_wa, g_wi, db_fgate, dsm_ffn]
    small_sizes = [a.size for a in small]
    n_small = sum(small_sizes)
    n_pad = -n_small % 1024
    packed = jnp.concatenate([a.reshape(-1) for a in small] + [jnp.zeros((n_pad,), F32)]).reshape(-1, 128)
    ex_c = _exchange([dmod, dmodf, packed], [True, True, True], "gather_small_grads")
    dmod_all = ex_c[0].reshape(nb, 6 * D_MODEL)
    dmodf_all = ex_c[1].reshape(nb, 2 * D_MODEL)

    res = {}

    def upd(name, parts, shape2):
        w, m, v = args[name], args["m_" + name], args["v_" + name]
        outs = _adamw(w.reshape(shape2), m.reshape(shape2), v.reshape(shape2), parts, name="adamw_" + name)
        res[name] = [o.reshape(w.shape) for o in outs]

    g_wada = _mm(c_all, lax.dynamic_slice(dmod_all, (0, me * nmod), (nb, nmod)), name="mm_ada_dw", ta=True,
                 silu_a=True, tm=D_MODEL, tn=nmod, tk=nb)
    g_wadaf = _mm(c_all, lax.dynamic_slice(dmodf_all, (0, me * nmodf), (nb, nmodf)), name="mm_adaf_dw", ta=True,
                  silu_a=True, tm=D_MODEL, tn=nmodf, tk=nb)
    upd("w_ada", g_wada[None], (D_MODEL, nmod))
    upd("w_ada_final", g_wadaf[None], (D_MODEL, nmodf))
    upd("b_ada", dmod_all.reshape(nb, 1, -1), (1, 6 * D_MODEL))
    upd("b_ada_final", dmodf_all.reshape(nb, 1, -1), (1, 2 * D_MODEL))
    upd("w_in", p_win, w_in.shape[1:])
    upd("w_proj_rnn", p_wpr, w_proj_rnn.shape[1:])
    upd("w_proj_attn", p_wpa, w_proj_attn.shape[1:])
    upd("w_out", p_wout, w_out.shape[1:])
    upd("w_ffn_up", p_wup, w_ffn_up.shape[1:])
    upd("w_ffn_down", p_wdown, w_ffn_down.shape[1:])

    flat = ex_c[2].reshape(N_DEV, -1)
    parts, off = [], 0
    for a, n in zip(small, small_sizes):
        parts.append(flat[:, off:off + n].reshape((N_DEV,) + a.shape))
        off += n
    p_g1, p_g2, p_gf, p_rnn, p_wa, p_wi, p_bfg, p_ffn = parts
    upd("g_norm1", p_g1, (1, D_MODEL))
    upd("g_norm2", p_g2, (1, D_MODEL))
    upd("g_final", p_gf, (1, D_MODEL))
    nrc = w_rnn_conv.shape[2]
    upd("w_rnn_conv", lax.dynamic_slice(p_rnn, (0, 0, me * nrc), (N_DEV, 4, nrc)), (4, nrc))
    upd("b_rnn_conv", p_rnn[:, 4:5], (1, D_RNN))
    upd("b_lru_a", p_rnn[:, 5:6], (1, D_RNN))
    upd("b_lru_i", p_rnn[:, 6:7], (1, D_RNN))
    upd("lru_lambda", p_rnn[:, 7:8], (1, D_RNN))
    upd("w_lru_a", p_wa.reshape(N_DEV, D_RNN, RNN_BLK), (D_RNN, RNN_BLK))
    upd("w_lru_i", p_wi.reshape(N_DEV, D_RNN, RNN_BLK), (D_RNN, RNN_BLK))
    upd("b_fgate", p_bfg.reshape(N_DEV, 1, N_HEADS), (1, N_HEADS))
    nfc = w_ffn_conv.shape[2]
    upd("w_ffn_conv", lax.dynamic_slice(p_ffn, (0, 0, me * nfc), (N_DEV, 3, nfc)), (3, nfc))
    upd("b_ffn_conv", p_ffn[:, 3:4], (1, D_FF))

    names = ['w_ada', 'b_ada', 'g_norm1', 'w_in', 'w_rnn_conv', 'b_rnn_conv', 'w_lru_a', 'b_lru_a', 'w_lru_i',
             'b_lru_i', 'lru_lambda', 'b_fgate', 'w_proj_rnn', 'w_proj_attn', 'w_out', 'g_norm2', 'w_ffn_up',
             'w_ffn_conv', 'b_ffn_conv', 'w_ffn_down', 'w_ada_final', 'b_ada_final', 'g_final']
    return (loss, grad_x, *[res[n][0] for n in names], *[res[n][1] for n in names],
            *[res[n][2] for n in names], *[res[n][3] for n in names])
```

```python
import jax
import jax.numpy as jnp
from jax import lax
from jax.experimental import pallas as pl
from jax.experimental.pallas import tpu as pltpu

F32, BF16 = jnp.float32, jnp.bfloat16
D_MODEL, D_RNN, N_BLK, RNN_BLK = 1024, 1280, 10, 128
N_HEADS, HEAD_DIM, D_FF = 8, 128, 3072
N_DEV = 8
NZ = 8192
QKV0, XG0, FL0, MG0 = 0, 3072, 5632, 6144
NZR = NZ - XG0
XG0R, FL0R, MG0R = 0, FL0 - XG0, MG0 - XG0
RMS_EPS = 1e-6
LRU_C = 8.0
ADAM_LR, ADAM_B1, ADAM_B2, ADAM_EPS, ADAM_WD, ADAM_STEP = 0.001, 0.9, 0.999, 1e-08, 0.01, 10
VMEM_LIMIT = 48 << 20
SCAN_CHUNK = 256
ATTN_TQ = 256
CUM_BLK = 512
MESH = pl.DeviceIdType.MESH
ANY = pl.BlockSpec(memory_space=pl.ANY)


def _call(body, **kw):
    return pl.pallas_call(body, **kw)


def _params():
    return pltpu.CompilerParams(vmem_limit_bytes=VMEM_LIMIT)


def _direct_copies(ins, outs, sems, bcast):
    send_sems, recv_sems, loc_sems = sems
    x, y, c = lax.axis_index("x"), lax.axis_index("y"), lax.axis_index("c")
    me = 4 * x + 2 * y + c
    local, remote = [], []
    for a in range(len(ins)):
        src = ins[a] if bcast[a] else ins[a].at[me]
        local.append(pltpu.make_async_copy(src, outs[a].at[me], loc_sems.at[a]))
    for k in range(1, N_DEV):
        px = 1 - x if k & 4 else x
        py = 1 - y if k & 2 else y
        pc = 1 - c if k & 1 else c
        peer = 4 * px + 2 * py + pc
        for a in range(len(ins)):
            src = ins[a] if bcast[a] else ins[a].at[peer]
            remote.append(pltpu.make_async_remote_copy(
                src_ref=src, dst_ref=outs[a].at[me],
                send_sem=send_sems.at[a * (N_DEV - 1) + k - 1], recv_sem=recv_sems.at[a * (N_DEV - 1) + k - 1],
                device_id=(px, py, pc), device_id_type=MESH))
    return local, remote


def _with_exchange(body, args, arrs, bcast, *, name, grid, in_specs, out_specs, out_shape, scratch_shapes=(), **kw):
    n_in, n_out, n_sc, nc = len(args), len(out_shape), len(scratch_shapes), len(arrs)

    def wrapped(*refs):
        core_in, cin = refs[:n_in], refs[n_in:n_in + nc]
        core_out = refs[n_in + nc:n_in + nc + n_out]
        cout = refs[n_in + nc + n_out:n_in + 2 * nc + n_out]
        rest = refs[n_in + 2 * nc + n_out:]
        core_sc, sems = rest[:n_sc], rest[n_sc:]

        def start():
            local, remote = _direct_copies(cin, cout, sems, bcast)
            for cp in local + remote:
                cp.start()

        def wait():
            local, remote = _direct_copies(cin, cout, sems, bcast)
            for cp in remote + local:
                cp.wait()

        if not grid:
            start()
            body(*core_in, *core_out, *core_sc)
            wait()
            return
        first, last = None, None
        for d, g in enumerate(grid):
            i = pl.program_id(d)
            first = (i == 0) if first is None else jnp.logical_and(first, i == 0)
            last = (i == g - 1) if last is None else jnp.logical_and(last, i == g - 1)
        pl.when(first)(start)
        body(*core_in, *core_out, *core_sc)
        pl.when(last)(wait)

    ex_shape = [jax.ShapeDtypeStruct((N_DEV,) + (a.shape if bc else a.shape[1:]), a.dtype) for a, bc in zip(arrs, bcast)]
    sems = [pltpu.SemaphoreType.DMA((nc * (N_DEV - 1),)), pltpu.SemaphoreType.DMA((nc * (N_DEV - 1),)),
            pltpu.SemaphoreType.DMA((nc,))]
    res = _call(wrapped, name=name, grid=grid, in_specs=list(in_specs) + [ANY] * nc,
                out_specs=list(out_specs) + [ANY] * nc, out_shape=list(out_shape) + ex_shape,
                scratch_shapes=list(scratch_shapes) + sems, **kw)(*args, *arrs)
    return list(res[:n_out]), list(res[n_out:])


def _run(body, args, exchange, **kw):
    if exchange is None:
        return list(_call(body, **kw)(*args))
    outs, got = _with_exchange(body, args, *exchange, **kw)
    return outs + [got]


def _exchange(arrs, bcast, name):
    return _with_exchange(lambda: None, [], arrs, bcast, name=name, grid=(), in_specs=[], out_specs=[], out_shape=[],
                          compiler_params=pltpu.CompilerParams(has_side_effects=True))[1]


def _gather_two_level(arrs, name):
    n = len(arrs)

    def body(*refs):
        ins, outs = refs[:n], refs[n:2 * n]
        send_sems, recv_sems, loc_sems = refs[2 * n:]
        x, y, c = lax.axis_index("x"), lax.axis_index("y"), lax.axis_index("c")
        me, sibling = (x, y, c), (x, y, 1 - c)
        chips = [(1 - x, y), (x, 1 - y), (1 - x, 1 - y)]

        def slot(a, p):
            return outs[a].at[4 * p[0] + 2 * p[1] + p[2]]

        def copy(a, k, block, to, src=None):
            return pltpu.make_async_remote_copy(
                src_ref=slot(a, block) if src is None else src, dst_ref=slot(a, block),
                send_sem=send_sems.at[a * 7 + k], recv_sem=recv_sems.at[a * 7 + k],
                device_id=to, device_id_type=MESH)

        mine = [pltpu.make_async_copy(ins[a], slot(a, me), loc_sems.at[a]) for a in range(n)]
        first = []
        for a in range(n):
            first.append(copy(a, 0, me, sibling, src=ins[a]))
            first += [copy(a, 1 + j, me, (*chip, c), src=ins[a]) for j, chip in enumerate(chips)]
        for cp in mine + first:
            cp.start()
        passed = []
        for j, chip in enumerate(chips):
            for a in range(n):
                copy(a, 1 + j, (*chip, c), me).wait_recv()
                fwd = copy(a, 4 + j, (*chip, c), sibling)
                fwd.start()
                passed.append(fwd)
        for a in range(n):
            copy(a, 0, sibling, me).wait_recv()
        for j, chip in enumerate(chips):
            for a in range(n):
                copy(a, 4 + j, (*chip, 1 - c), me).wait_recv()
        for cp in first + passed:
            cp.wait_send()
        for cp in mine:
            cp.wait()

    return _call(
        body, name=name, in_specs=[ANY] * n, out_specs=[ANY] * n,
        out_shape=[jax.ShapeDtypeStruct((N_DEV,) + a.shape, a.dtype) for a in arrs],
        scratch_shapes=[pltpu.SemaphoreType.DMA((n * 7,)), pltpu.SemaphoreType.DMA((n * 7,)),
                        pltpu.SemaphoreType.DMA((n,))],
        compiler_params=pltpu.CompilerParams(has_side_effects=True),
    )(*arrs)


def _sigmoid(x):
    return 1.0 / (1.0 + jnp.exp(-x))


def _log1p(e):
    u = 1.0 + e
    d = u - 1.0
    return jnp.where(d == 0.0, e, jnp.log(u) * (e / jnp.where(d == 0.0, 1.0, d)))


def _softplus(x):
    return jnp.maximum(x, 0.0) + _log1p(jnp.exp(-jnp.abs(x)))


_GELU_C = 0.7978845608028654


def _gelu(x):
    t = jnp.tanh(_GELU_C * (x + 0.044715 * x * x * x))
    return 0.5 * x * (1.0 + t)


def _gelu_and_grad(x):
    t = jnp.tanh(_GELU_C * (x + 0.044715 * x * x * x))
    g = 0.5 * x * (1.0 + t)
    dg = 0.5 * (1.0 + t) + 0.5 * x * (1.0 - t * t) * _GELU_C * (1.0 + 3.0 * 0.044715 * x * x)
    return g, dg


def _acc(ref, idx, val, first):
    @pl.when(first)
    def _():
        ref[idx] = val

    @pl.when(jnp.logical_not(first))
    def _():
        ref[idx] = ref[idx] + val


def _split3(x):
    hi = x.astype(BF16)
    r1 = x - hi.astype(F32)
    mid = r1.astype(BF16)
    lo = (r1 - mid.astype(F32)).astype(BF16)
    return hi, mid, lo


def _mm(a, b, *, name, ta=False, tb=False, out_dtype=F32, tm, tn, tk, bias=None, silu_a=False, exchange=None,
        b_cols=None):
    m, k = (a.shape[1], a.shape[0]) if ta else a.shape
    col0, n = b_cols if b_cols is not None else (0, b.shape[0] if tb else b.shape[1])
    assert m % tm == 0 and n % tn == 0 and k % tk == 0 and col0 % tn == 0, (name, m, n, k)
    jo = col0 // tn
    nk = k // tk
    dn = (((0 if ta else 1,), (1 if tb else 0,)), ((), ()))
    use_acc = nk > 1 and out_dtype != F32

    def body(*refs):
        a_ref, b_ref = refs[0], refs[1]
        bias_ref = refs[2] if bias is not None else None
        o_ref = refs[3 if bias is not None else 2]
        av = a_ref[...]
        if silu_a:
            av = av * _sigmoid(av)
        p = lax.dot_general(av.astype(BF16), b_ref[...].astype(BF16), dn, preferred_element_type=F32)
        if bias is not None:
            p = p + bias_ref[...]
        if nk == 1:
            o_ref[...] = p.astype(out_dtype)
            return
        kk = pl.program_id(2)
        acc = refs[-1] if use_acc else o_ref

        @pl.when(kk == 0)
        def _():
            acc[...] = p

        @pl.when(kk > 0)
        def _():
            acc[...] = acc[...] + p

        if use_acc:
            @pl.when(kk == nk - 1)
            def _():
                o_ref[...] = acc[...].astype(out_dtype)

    a_spec = pl.BlockSpec((tk, tm), lambda i, j, q: (q, i)) if ta else pl.BlockSpec((tm, tk), lambda i, j, q: (i, q))
    b_spec = (pl.BlockSpec((tn, tk), lambda i, j, q: (j + jo, q)) if tb
              else pl.BlockSpec((tk, tn), lambda i, j, q: (q, j + jo)))
    in_specs, args = [a_spec, b_spec], [a, b]
    if bias is not None:
        in_specs.append(pl.BlockSpec((1, tn), lambda i, j, q: (0, j)))
        args.append(bias)
    kw = dict(name=name, grid=(m // tm, n // tn, nk), in_specs=in_specs,
              out_specs=[pl.BlockSpec((tm, tn), lambda i, j, q: (i, j))],
              out_shape=[jax.ShapeDtypeStruct((m, n), out_dtype)],
              scratch_shapes=[pltpu.VMEM((tm, tn), F32)] if use_acc else [],
              compiler_params=_params())
    if exchange is None:
        return _call(body, **kw)(*args)[0]
    outs, got = _with_exchange(body, args, *exchange, **kw)
    return outs[0], got


def _tok_spec(ts):
    return pl.BlockSpec((1, ts, D_MODEL), lambda b, s: (b, s, 0))


def _seq_spec():
    return pl.BlockSpec((1, 1, D_MODEL), lambda b, s: (b, 0, 0))


def _vec_spec():
    return pl.BlockSpec((1, D_MODEL), lambda b, s: (0, 0))


def _norm_fwd(x, g, shift, scale, *, name, yprev=None, gate=None, ts=512):
    bsz, s, _ = x.shape
    res = yprev is not None

    def body(*refs):
        if res:
            x_ref, g_ref, sh_ref, sc_ref, y_ref, gt_ref, h_ref, xn_ref = refs
            xv = x_ref[0] + gt_ref[0] * y_ref[0]
            xn_ref[0] = xv
        else:
            x_ref, g_ref, sh_ref, sc_ref, h_ref = refs
            xv = x_ref[0]
        rstd = lax.rsqrt(jnp.mean(xv * xv, axis=-1, keepdims=True) + RMS_EPS)
        nv = xv * rstd * g_ref[...]
        h_ref[0] = (nv * (1.0 + sc_ref[0]) + sh_ref[0]).astype(BF16)

    in_specs = [_tok_spec(ts), _vec_spec(), _seq_spec(), _seq_spec()]
    args = [x, g, shift, scale]
    out_specs = [_tok_spec(ts)]
    out_shape = [jax.ShapeDtypeStruct(x.shape, BF16)]
    if res:
        in_specs += [_tok_spec(ts), _seq_spec()]
        args += [yprev, gate]
        out_specs.append(_tok_spec(ts))
        out_shape.append(jax.ShapeDtypeStruct(x.shape, F32))
    return _call(body, name=name, grid=(bsz, s // ts), in_specs=in_specs, out_specs=out_specs,
                 out_shape=out_shape, compiler_params=_params())(*args)


def _final(x1, yf, gate2, g, shift, scale, target, *, ts=512):
    bsz, s, _ = x1.shape
    ns = s // ts

    def body(x_ref, y_ref, gt_ref, g_ref, sh_ref, sc_ref, t_ref,
             lp_ref, dx_ref, dyf_ref, dsh_ref, dsc_ref, dgt_ref, dg_ref):
        b, si = pl.program_id(0), pl.program_id(1)
        yv, gt, gv = y_ref[0], gt_ref[0], g_ref[...]
        xv = x_ref[0] + gt * yv
        rstd = lax.rsqrt(jnp.mean(xv * xv, axis=-1, keepdims=True) + RMS_EPS)
        xhat = xv * rstd
        nv = xhat * gv
        sc1 = 1.0 + sc_ref[0]
        err = nv * sc1 + sh_ref[0] - t_ref[0]
        lp_ref[...] = jnp.zeros((1, 1, 8, 128), F32) + jnp.sum(err * err)
        dy = err * (1.0 / D_MODEL)
        dn = dy * sc1
        dxh = dn * gv
        dx = rstd * (dxh - xhat * jnp.mean(dxh * xhat, axis=-1, keepdims=True))
        dx_ref[0] = dx
        dyf_ref[0] = (dx * gt).astype(BF16)
        _acc(dsh_ref, 0, jnp.sum(dy, axis=0, keepdims=True), si == 0)
        _acc(dsc_ref, 0, jnp.sum(dy * nv, axis=0, keepdims=True), si == 0)
        _acc(dgt_ref, 0, jnp.sum(dx * yv, axis=0, keepdims=True), si == 0)
        _acc(dg_ref, slice(None), jnp.sum(dn * xhat, axis=0, keepdims=True), jnp.logical_and(b == 0, si == 0))

    seq_shape = jax.ShapeDtypeStruct((bsz, 1, D_MODEL), F32)
    return _call(
        body, name="final_loss", grid=(bsz, ns),
        in_specs=[_tok_spec(ts), _tok_spec(ts), _seq_spec(), _vec_spec(), _seq_spec(), _seq_spec(), _tok_spec(ts)],
        out_specs=[pl.BlockSpec((1, 1, 8, 128), lambda b, s: (b, s, 0, 0)), _tok_spec(ts), _tok_spec(ts),
                   _seq_spec(), _seq_spec(), _seq_spec(), _vec_spec()],
        out_shape=[jax.ShapeDtypeStruct((bsz, ns, 8, 128), F32), jax.ShapeDtypeStruct(x1.shape, F32),
                   jax.ShapeDtypeStruct(x1.shape, BF16), seq_shape, seq_shape, seq_shape,
                   jax.ShapeDtypeStruct((1, D_MODEL), F32)],
        compiler_params=_params(),
    )(x1, yf, gate2, g, shift, scale, target)


def _norm_bwd(x, dh, g, scale, dres, *, name, yprev=None, gate=None, ts=512):
    bsz, s, _ = x.shape
    res = yprev is not None

    def body(*refs):
        if res:
            (x_ref, dh_ref, g_ref, sc_ref, dr_ref, y_ref, gt_ref,
             dx_ref, dg_ref, dsh_ref, dsc_ref, dy_ref, dgt_ref) = refs
        else:
            x_ref, dh_ref, g_ref, sc_ref, dr_ref, dx_ref, dg_ref, dsh_ref, dsc_ref = refs
        b, si = pl.program_id(0), pl.program_id(1)
        xv, dhv, gv = x_ref[0], dh_ref[0], g_ref[...]
        rstd = lax.rsqrt(jnp.mean(xv * xv, axis=-1, keepdims=True) + RMS_EPS)
        xhat = xv * rstd
        dn = dhv * (1.0 + sc_ref[0])
        dxh = dn * gv
        dx = dr_ref[0] + rstd * (dxh - xhat * jnp.mean(dxh * xhat, axis=-1, keepdims=True))
        dx_ref[0] = dx
        _acc(dsh_ref, 0, jnp.sum(dhv, axis=0, keepdims=True), si == 0)
        _acc(dsc_ref, 0, jnp.sum(dhv * (xhat * gv), axis=0, keepdims=True), si == 0)
        _acc(dg_ref, slice(None), jnp.sum(dn * xhat, axis=0, keepdims=True), jnp.logical_and(b == 0, si == 0))
        if res:
            dy_ref[0] = (dx * gt_ref[0]).astype(BF16)
            _acc(dgt_ref, 0, jnp.sum(dx * y_ref[0], axis=0, keepdims=True), si == 0)

    seq_shape = jax.ShapeDtypeStruct((bsz, 1, D_MODEL), F32)
    in_specs = [_tok_spec(ts), _tok_spec(ts), _vec_spec(), _seq_spec(), _tok_spec(ts)]
    args = [x, dh, g, scale, dres]
    out_specs = [_tok_spec(ts), _vec_spec(), _seq_spec(), _seq_spec()]
    out_shape = [jax.ShapeDtypeStruct(x.shape, F32), jax.ShapeDtypeStruct((1, D_MODEL), F32), seq_shape, seq_shape]
    if res:
        in_specs += [_tok_spec(ts), _seq_spec()]
        args += [yprev, gate]
        out_specs += [_tok_spec(ts), _seq_spec()]
        out_shape += [jax.ShapeDtypeStruct(x.shape, BF16), seq_shape]
    return _call(body, name=name, grid=(bsz, s // ts), in_specs=in_specs, out_specs=out_specs,
                 out_shape=out_shape, compiler_params=_params())(*args)


def _lru_gates(xr, wa, ba, wi, bi, sp):
    xb = xr.astype(BF16)
    r = _sigmoid(jnp.dot(xb, wa, preferred_element_type=F32) + ba)
    i = _sigmoid(jnp.dot(xb, wi, preferred_element_type=F32) + bi)
    la = -LRU_C * r * sp
    a = jnp.exp(la)
    mult = jnp.sqrt(-jnp.tanh(la) * (a * a + 1.0))
    return r, i, a, mult


def _rnn_fwd(z, wc, bc, wa, ba, wi, bi, lam, exchange=None):
    bsz, s, _ = z.shape
    ch = min(SCAN_CHUNK, s)
    nch = s // ch

    def body(z_ref, wc_ref, bc_ref, wa_ref, ba_ref, wi_ref, bi_ref, lam_ref, h_ref, y_ref):
        wcv, bcv = wc_ref[...], bc_ref[...]
        wav, wiv = wa_ref[0].astype(BF16), wi_ref[0].astype(BF16)
        bav, biv = ba_ref[...], bi_ref[...]
        sp = _softplus(-lam_ref[...])
        row = lax.broadcasted_iota(jnp.int32, (ch, RNN_BLK), 0)

        def chunk(ci, carry):
            prev8, hc = carry
            st = pl.multiple_of(ci * ch, ch)
            blk = z_ref[0, pl.ds(st, ch), :]
            xp, gr = blk[:, :RNN_BLK], blk[:, RNN_BLK:]
            xe = jnp.concatenate([prev8, xp], axis=0)
            xr = bcv + wcv[3:4] * xp
            for j in range(1, 4):
                xr = xr + wcv[3 - j:4 - j] * pltpu.roll(xe, j, 0)[8:]
            _, i, a, mult = _lru_gates(xr, wav, bav, wiv, biv, sp)
            aa, uu = a, mult * (i * xr)
            k = 1
            while k < ch:
                m = row >= k
                uu = jnp.where(m, aa * pltpu.roll(uu, k, 0) + uu, uu)
                aa = jnp.where(m, aa * pltpu.roll(aa, k, 0), aa)
                k *= 2
            hh = uu + aa * hc
            h_ref[0, pl.ds(st, ch), :] = hh
            y_ref[0, pl.ds(st, ch), :] = (_gelu(gr) * hh).astype(BF16)
            return xp[ch - 8:], hh[ch - 1:]

        lax.fori_loop(0, nch, chunk, (jnp.zeros((8, RNN_BLK), F32), jnp.zeros((1, RNN_BLK), F32)))

    vec = pl.BlockSpec((1, RNN_BLK), lambda b, n: (0, n))
    mat = pl.BlockSpec((1, RNN_BLK, RNN_BLK), lambda b, n: (n, 0, 0))
    out = pl.BlockSpec((1, s, RNN_BLK), lambda b, n: (b, 0, n))
    return _run(
        body, [z, wc, bc, wa, ba, wi, bi, lam], exchange, name="rnn_fwd", grid=(bsz, N_BLK),
        in_specs=[pl.BlockSpec((1, s, 2 * RNN_BLK), lambda b, n: (b, 0, XG0R // (2 * RNN_BLK) + n)),
                  pl.BlockSpec((4, RNN_BLK), lambda b, n: (0, n)), vec, mat, vec, mat, vec, vec],
        out_specs=[out, out],
        out_shape=[jax.ShapeDtypeStruct((bsz, s, D_RNN), F32), jax.ShapeDtypeStruct((bsz, s, D_RNN), BF16)],
        compiler_params=_params())


def _rnn_bwd(z, h, dy, dz, wc, bc, wa, ba, wi, bi, lam, exchange=None):
    bsz, s, _ = z.shape
    ch = min(SCAN_CHUNK, s)
    nch = s // ch

    def body(z_ref, h_ref, dy_ref, dz_in, wc_ref, bc_ref, wa_ref, ba_ref, wi_ref, bi_ref, lam_ref,
             dz_ref, dwa_ref, dwi_ref, dsm_ref):
        del dz_in
        b = pl.program_id(1)
        wcv, bcv = wc_ref[...], bc_ref[...]
        wav, wiv = wa_ref[0].astype(BF16), wi_ref[0].astype(BF16)
        bav, biv = ba_ref[...], bi_ref[...]
        lamv = lam_ref[...]
        sp = _softplus(-lamv)
        row = lax.broadcasted_iota(jnp.int32, (ch, RNN_BLK), 0)

        @pl.when(b == 0)
        def _():
            dwa_ref[...] = jnp.zeros_like(dwa_ref)
            dwi_ref[...] = jnp.zeros_like(dwi_ref)
            dsm_ref[...] = jnp.zeros_like(dsm_ref)

        def chunk(k, carry):
            a_next, g_next, dxr_next8 = carry
            ci = nch - 1 - k
            st = pl.multiple_of(ci * ch, ch)
            pst = pl.multiple_of(jnp.maximum(st - 8, 0), 8)
            has_prev = (ci > 0).astype(F32)
            blk = z_ref[0, pl.ds(st, ch), :]
            xp, gr = blk[:, :RNN_BLK], blk[:, RNN_BLK:]
            xprev8 = z_ref[0, pl.ds(pst, 8), :][:, :RNN_BLK] * has_prev
            hprev8 = h_ref[0, pl.ds(pst, 8), :] * has_prev
            xe = jnp.concatenate([xprev8, xp], axis=0)
            xs = [xp] + [pltpu.roll(xe, j, 0)[8:] for j in range(1, 4)]
            xr = bcv
            for kk in range(4):
                xr = xr + wcv[kk:kk + 1] * xs[3 - kk]
            r, i, a, mult = _lru_gates(xr, wav, bav, wiv, biv, sp)
            hh = h_ref[0, pl.ds(st, ch), :]
            hm1 = pltpu.roll(jnp.concatenate([hprev8, hh], axis=0), 1, 0)[8:]
            dyv = dy_ref[0, pl.ds(st, ch), :]
            ge, dge = _gelu_and_grad(gr)
            dgr = dyv * hh * dge
            aa = jnp.where(row == ch - 1, a_next, pltpu.roll(a, ch - 1, 0))
            gg = dyv * ge
            kk = 1
            while kk < ch:
                m = row < ch - kk
                gg = jnp.where(m, aa * pltpu.roll(gg, ch - kk, 0) + gg, gg)
                aa = jnp.where(m, aa * pltpu.roll(aa, ch - kk, 0), aa)
                kk *= 2
            gg = gg + aa * g_next
            da = gg * hm1
            ix = i * xr
            dmult = gg * ix
            di = gg * mult * xr
            dxr = gg * mult * i
            dla = da * a - dmult * (a * a) / mult
            dr = dla * (-LRU_C * sp)
            dsp = jnp.sum(dla * (-LRU_C * r), axis=0, keepdims=True)
            dpa = dr * r * (1.0 - r)
            dpi = di * i * (1.0 - i)
            dpab, dpib, xrb = dpa.astype(BF16), dpi.astype(BF16), xr.astype(BF16)
            nt = (((1,), (1,)), ((), ()))
            tn = (((0,), (0,)), ((), ()))
            dxr = dxr + lax.dot_general(dpab, wav, nt, preferred_element_type=F32)
            dxr = dxr + lax.dot_general(dpib, wiv, nt, preferred_element_type=F32)
            dwa_ref[0] = dwa_ref[0] + lax.dot_general(xrb, dpab, tn, preferred_element_type=F32)
            dwi_ref[0] = dwi_ref[0] + lax.dot_general(xrb, dpib, tn, preferred_element_type=F32)
            rows = [jnp.sum(dxr * xs[3 - q], axis=0, keepdims=True) for q in range(4)]
            rows += [jnp.sum(dxr, axis=0, keepdims=True), jnp.sum(dpa, axis=0, keepdims=True),
                     jnp.sum(dpi, axis=0, keepdims=True), dsp * (-_sigmoid(-lamv))]
            dsm_ref[...] = dsm_ref[...] + jnp.concatenate(rows, axis=0)
            dxe = jnp.concatenate([dxr, dxr_next8], axis=0)
            dxp = wcv[3:4] * dxr
            for j in range(1, 4):
                dxp = dxp + wcv[3 - j:4 - j] * pltpu.roll(dxe, ch + 8 - j, 0)[:ch]
            dz_ref[0, pl.ds(st, ch), :] = jnp.concatenate([dxp, dgr], axis=1).astype(BF16)
            return a[0:1], gg[0:1], dxr[0:8]

        lax.fori_loop(0, nch, chunk, (jnp.zeros((1, RNN_BLK), F32), jnp.zeros((1, RNN_BLK), F32),
                                      jnp.zeros((8, RNN_BLK), F32)))

    vec = pl.BlockSpec((1, RNN_BLK), lambda n, b: (0, n))
    mat = pl.BlockSpec((1, RNN_BLK, RNN_BLK), lambda n, b: (n, 0, 0))
    tok = pl.BlockSpec((1, s, RNN_BLK), lambda n, b: (b, 0, n))
    zblk = pl.BlockSpec((1, s, 2 * RNN_BLK), lambda n, b: (b, 0, XG0R // (2 * RNN_BLK) + n))
    dzblk = pl.BlockSpec((1, s, 2 * RNN_BLK), lambda n, b: (b, 0, XG0 // (2 * RNN_BLK) + n))
    return _run(
        body, [z, h, dy, dz, wc, bc, wa, ba, wi, bi, lam], exchange, name="rnn_bwd", grid=(N_BLK, bsz),
        in_specs=[zblk, tok, tok, ANY, pl.BlockSpec((4, RNN_BLK), lambda n, b: (0, n)), vec, mat, vec, mat, vec, vec],
        out_specs=[dzblk, mat, mat, pl.BlockSpec((8, RNN_BLK), lambda n, b: (0, n))],
        out_shape=[jax.ShapeDtypeStruct(dz.shape, dz.dtype), jax.ShapeDtypeStruct((N_BLK, RNN_BLK, RNN_BLK), F32),
                   jax.ShapeDtypeStruct((N_BLK, RNN_BLK, RNN_BLK), F32), jax.ShapeDtypeStruct((8, D_RNN), F32)],
        input_output_aliases={3: 0},
        compiler_params=_params())


def _tri(n, upper):
    r = lax.broadcasted_iota(jnp.int32, (n, n), 0)
    c = lax.broadcasted_iota(jnp.int32, (n, n), 1)
    return (r <= c if upper else r >= c).astype(BF16)


def _exact_dot(x, t):
    hi, mid, lo = _split3(x)
    return (jnp.dot(hi, t, preferred_element_type=F32) + jnp.dot(mid, t, preferred_element_type=F32)
            + jnp.dot(lo, t, preferred_element_type=F32))


def _fgate_fwd(flt, bfg):
    bsz, nh, s = flt.shape
    blk = min(CUM_BLK, s)

    def body(fl_ref, b_ref, f_ref):
        tri = _tri(blk, True)
        carry = jnp.zeros((nh, 1), F32)
        for q in range(s // blk):
            xq = fl_ref[0, :, q * blk:(q + 1) * blk] + b_ref[...]
            fq = _exact_dot(-_softplus(-xq), tri) + carry
            f_ref[0, :, q * blk:(q + 1) * blk] = fq
            carry = fq[:, blk - 1:blk]

    spec = pl.BlockSpec((1, nh, s), lambda b: (b, 0, 0))
    return _call(body, name="fgate_fwd", grid=(bsz,), in_specs=[spec, pl.BlockSpec((nh, 1), lambda b: (0, 0))],
                 out_specs=spec, out_shape=jax.ShapeDtypeStruct(flt.shape, F32), compiler_params=_params())(flt, bfg)


def _fgate_bwd(dft, flt, bfg):
    bsz, nh, s = flt.shape
    blk = min(CUM_BLK, s)

    def body(df_ref, fl_ref, b_ref, dfl_ref, db_ref):
        b = pl.program_id(0)
        tri = _tri(blk, False)
        carry = jnp.zeros((nh, 1), F32)
        tot = jnp.zeros((nh, 1), F32)
        for q in reversed(range(s // blk)):
            sl = slice(q * blk, (q + 1) * blk)
            dlq = _exact_dot(df_ref[0, :, sl], tri) + carry
            carry = dlq[:, 0:1]
            dflq = dlq * _sigmoid(-(fl_ref[0, :, sl] + b_ref[...]))
            dfl_ref[0, :, sl] = dflq
            tot = tot + jnp.sum(dflq, axis=1, keepdims=True)
        _acc(db_ref, slice(None), tot, b == 0)

    spec = pl.BlockSpec((1, nh, s), lambda b: (b, 0, 0))
    small = pl.BlockSpec((nh, 1), lambda b: (0, 0))
    return _call(body, name="fgate_bwd", grid=(bsz,), in_specs=[spec, spec, small], out_specs=[spec, small],
                 out_shape=[jax.ShapeDtypeStruct(flt.shape, F32), jax.ShapeDtypeStruct((nh, 1), F32)],
                 compiler_params=_params())(dft, flt, bfg)


def _attn_scores(q_b, k_ref, fcol_ref, frow_ref, q0, tq, ln):
    kk = k_ref[0, :ln, HEAD_DIM:2 * HEAD_DIM]
    sc = lax.dot_general(q_b, kk, (((1,), (1,)), ((), ())), preferred_element_type=F32)
    sc = sc + fcol_ref[0, 0, q0:q0 + tq, :] - frow_ref[0, 0, :, :ln]
    qi = q0 + lax.broadcasted_iota(jnp.int32, (tq, ln), 0)
    ki = lax.broadcasted_iota(jnp.int32, (tq, ln), 1)
    return sc, qi >= ki


def _attn_fwd(z, fcol, frow, exchange=None):
    bsz, s, _ = z.shape
    tq = min(ATTN_TQ, s)
    scale = HEAD_DIM ** -0.5

    def body(z_ref, fcol_ref, frow_ref, o_ref, lse_ref):
        for qb in range(s // tq):
            q0, ln = qb * tq, (qb + 1) * tq
            q_b = (z_ref[0, q0:ln, :HEAD_DIM].astype(F32) * scale).astype(BF16)
            sc, mask = _attn_scores(q_b, z_ref, fcol_ref, frow_ref, q0, tq, ln)
            sc = jnp.where(mask, sc, -jnp.inf)
            mx = jnp.max(sc, axis=-1, keepdims=True)
            p = jnp.exp(sc - mx)
            l = jnp.sum(p, axis=-1, keepdims=True)
            vv = z_ref[0, :ln, 2 * HEAD_DIM:]
            o = jnp.dot(p.astype(BF16), vv, preferred_element_type=F32) / l
            o_ref[0, q0:ln, :] = o.astype(BF16)
            lse_ref[0, 0, q0:ln, :] = mx + jnp.log(l)

    return _run(
        body, [z, fcol, frow], exchange, name="attn_fwd", grid=(bsz, N_HEADS),
        in_specs=[pl.BlockSpec((1, s, 3 * HEAD_DIM), lambda b, h: (b, 0, h)),
                  pl.BlockSpec((1, 1, s, 1), lambda b, h: (b, h, 0, 0)),
                  pl.BlockSpec((1, 1, 1, s), lambda b, h: (b, h, 0, 0))],
        out_specs=[pl.BlockSpec((1, s, HEAD_DIM), lambda b, h: (b, 0, h)),
                   pl.BlockSpec((1, 1, s, 1), lambda b, h: (b, h, 0, 0))],
        out_shape=[jax.ShapeDtypeStruct((bsz, s, N_HEADS * HEAD_DIM), BF16),
                   jax.ShapeDtypeStruct((bsz, N_HEADS, s, 1), F32)],
        compiler_params=_params())


def _attn_bwd(z, do, lse, fcol, frow, dz, exchange=None):
    bsz, s, _ = z.shape
    tq = min(ATTN_TQ, s)
    scale = HEAD_DIM ** -0.5
    nt = (((1,), (1,)), ((), ()))
    tn = (((0,), (0,)), ((), ()))

    def body(z_ref, do_ref, lse_ref, fcol_ref, frow_ref, dz_in, dz_ref, df_ref, dk_acc, dv_acc):
        del dz_in
        dk_acc[...] = jnp.zeros_like(dk_acc)
        dv_acc[...] = jnp.zeros_like(dv_acc)
        df_ref[...] = jnp.zeros_like(df_ref)
        for qb in range(s // tq):
            q0, ln = qb * tq, (qb + 1) * tq
            q_b = (z_ref[0, q0:ln, :HEAD_DIM].astype(F32) * scale).astype(BF16)
            sc, mask = _attn_scores(q_b, z_ref, fcol_ref, frow_ref, q0, tq, ln)
            p = jnp.where(mask, jnp.exp(sc - lse_ref[0, 0, q0:ln, :]), 0.0)
            do_b = do_ref[0, q0:ln, :]
            pb = p.astype(BF16)
            dv_acc[:ln, :] = dv_acc[:ln, :] + lax.dot_general(pb, do_b, tn, preferred_element_type=F32)
            vv = z_ref[0, :ln, 2 * HEAD_DIM:]
            dp = lax.dot_general(do_b, vv, nt, preferred_element_type=F32)
            ds = p * (dp - jnp.sum(p * dp, axis=-1, keepdims=True))
            dsb = ds.astype(BF16)
            kk = z_ref[0, :ln, HEAD_DIM:2 * HEAD_DIM]
            dq =jnp.dot(dsb, kk, preferred_element_type=F32) * scale
            dz_ref[0, q0:ln, :HEAD_DIM] = dq.astype(BF16)
            dk_acc[:ln, :] = dk_acc[:ln, :] + lax.dot_general(dsb, q_b, tn, preferred_element_type=F32)
            df_ref[0, 0, :, :ln] = df_ref[0, 0, :, :ln] - jnp.sum(ds, axis=0, keepdims=True)
        dz_ref[0, :, HEAD_DIM:2 * HEAD_DIM] = dk_acc[...].astype(BF16)
        dz_ref[0, :, 2 * HEAD_DIM:] = dv_acc[...].astype(BF16)

    zblk = pl.BlockSpec((1, s, 3 * HEAD_DIM), lambda b, h: (b, 0, h))
    hblk = pl.BlockSpec((1, s, HEAD_DIM), lambda b, h: (b, 0, h))
    col = pl.BlockSpec((1, 1, s, 1), lambda b, h: (b, h, 0, 0))
    rowspec = pl.BlockSpec((1, 1, 1, s), lambda b, h: (b, h, 0, 0))
    return _run(
        body, [z, do, lse, fcol, frow, dz], exchange, name="attn_bwd", grid=(bsz, N_HEADS),
        in_specs=[zblk, hblk, col, col, rowspec, ANY],
        out_specs=[zblk, rowspec],
        out_shape=[jax.ShapeDtypeStruct(dz.shape, dz.dtype), jax.ShapeDtypeStruct((bsz, N_HEADS, 1, s), F32)],
        scratch_shapes=[pltpu.VMEM((s, HEAD_DIM), F32), pltpu.VMEM((s, HEAD_DIM), F32)],
        input_output_aliases={5: 0},
        compiler_params=_params())


def _merge_fwd(z2, pr, pa, *, tr=512):
    t = z2.shape[0]

    def body(mgr_ref, mga_ref, pr_ref, pa_ref, o_ref):
        gr_ = _sigmoid(mgr_ref[...])
        ga = _sigmoid(mga_ref[...])
        o_ref[...] = (gr_ * pr_ref[...] + ga * pa_ref[...]).astype(BF16)

    tok = pl.BlockSpec((tr, D_MODEL), lambda i: (i, 0))
    return _call(body, name="merge_fwd", grid=(t // tr,),
                 in_specs=[pl.BlockSpec((tr, D_MODEL), lambda i: (i, MG0R // D_MODEL)),
                           pl.BlockSpec((tr, D_MODEL), lambda i: (i, MG0R // D_MODEL + 1)), tok, tok],
                 out_specs=tok, out_shape=jax.ShapeDtypeStruct((t, D_MODEL), BF16),
                 compiler_params=_params())(z2, z2, pr, pa)


def _merge_bwd(dm, z2, pr, pa, *, tr=512):
    t = z2.shape[0]

    def body(dm_ref, mgr_ref, mga_ref, pr_ref, pa_ref, dz_ref, dpr_ref, dpa_ref):
        dmv = dm_ref[...]
        gr_ = _sigmoid(mgr_ref[...])
        ga = _sigmoid(mga_ref[...])
        dpr_ref[...] = (dmv * gr_).astype(BF16)
        dpa_ref[...] = (dmv * ga).astype(BF16)
        dz_ref[:, :D_MODEL] = (dmv * pr_ref[...] * gr_ * (1.0 - gr_)).astype(BF16)
        dz_ref[:, D_MODEL:] = (dmv * pa_ref[...] * ga * (1.0 - ga)).astype(BF16)

    tok = pl.BlockSpec((tr, D_MODEL), lambda i: (i, 0))
    mg = pl.BlockSpec((tr, 2 * D_MODEL), lambda i: (i, MG0 // (2 * D_MODEL)))
    return _call(body, name="merge_bwd", grid=(t // tr,),
                 in_specs=[tok, pl.BlockSpec((tr, D_MODEL), lambda i: (i, MG0R // D_MODEL)),
                           pl.BlockSpec((tr, D_MODEL), lambda i: (i, MG0R // D_MODEL + 1)), tok, tok],
                 out_specs=[mg, tok, tok],
                 out_shape=[jax.ShapeDtypeStruct((t, NZ), BF16), jax.ShapeDtypeStruct((t, D_MODEL), BF16),
                            jax.ShapeDtypeStruct((t, D_MODEL), BF16)],
                 compiler_params=_params())(dm, z2, z2, pr, pa)


def _fl_write(dfl_blk, dz):
    bsz, s, w = dfl_blk.shape

    def body(src_ref, dz_in, dz_ref):
        del dz_in
        dz_ref[...] = src_ref[...]

    return _call(body, name="fl_write", grid=(bsz,),
                 in_specs=[pl.BlockSpec((1, s, w), lambda b: (b, 0, 0)), ANY],
                 out_specs=pl.BlockSpec((1, s, w), lambda b: (b, 0, FL0 // w)),
                 out_shape=jax.ShapeDtypeStruct(dz.shape, dz.dtype), input_output_aliases={1: 0},
                 compiler_params=_params())(dfl_blk, dz)


def _ffn_fwd(zf, wc, bc):
    bsz, s, _ = zf.shape
    cw = 128

    def body(zf_ref, wc_ref, bc_ref, o_ref):
        gfp, uf = zf_ref[0, :, :cw].astype(F32), zf_ref[0, :, cw:].astype(F32)
        wcv = wc_ref[...]
        row = lax.broadcasted_iota(jnp.int32, (s, cw), 0)
        gf = bc_ref[...] + wcv[2:3] * gfp
        for j in (1, 2):
            gf = gf + wcv[2 - j:3 - j] * jnp.where(row >= j, pltpu.roll(gfp, j, 0), 0.0)
        o_ref[0] = (_gelu(gf) * uf).astype(BF16)

    return _call(body, name="ffn_fwd", grid=(bsz, D_FF // cw),
                 in_specs=[pl.BlockSpec((1, s, 2 * cw), lambda b, j: (b, 0, j)),
                           pl.BlockSpec((3, cw), lambda b, j: (0, j)), pl.BlockSpec((1, cw), lambda b, j: (0, j))],
                 out_specs=pl.BlockSpec((1, s, cw), lambda b, j: (b, 0, j)),
                 out_shape=jax.ShapeDtypeStruct((bsz, s, D_FF), BF16), compiler_params=_params())(zf, wc, bc)


def _ffn_bwd(dact, zf, wc, bc, exchange=None):
    bsz, s, _ = zf.shape
    cw = 128

    def body(da_ref, zf_ref, wc_ref, bc_ref, dzf_ref, dsm_ref):
        b = pl.program_id(1)
        gfp, uf = zf_ref[0, :, :cw].astype(F32), zf_ref[0, :, cw:].astype(F32)
        wcv = wc_ref[...]
        row = lax.broadcasted_iota(jnp.int32, (s, cw), 0)
        sh = [gfp] + [jnp.where(row >= j, pltpu.roll(gfp, j, 0), 0.0) for j in (1, 2)]
        gf = bc_ref[...]
        for kk in range(3):
            gf = gf + wcv[kk:kk + 1] * sh[2 - kk]
        ge, dge = _gelu_and_grad(gf)
        dav = da_ref[0].astype(F32)
        dgf = dav * uf * dge
        dgfp = wcv[2:3] * dgf
        for j in (1, 2):
            dgfp = dgfp + wcv[2 - j:3 - j] * jnp.where(row < s - j, pltpu.roll(dgf, s - j, 0), 0.0)
        dzf_ref[0] = jnp.concatenate([dgfp, dav * ge], axis=1).astype(BF16)
        rows = [jnp.sum(dgf * sh[2 - kk], axis=0, keepdims=True) for kk in range(3)]
        rows += [jnp.sum(dgf, axis=0, keepdims=True), jnp.zeros((4, cw), F32)]
        _acc(dsm_ref, slice(None), jnp.concatenate(rows, axis=0), b == 0)

    return _run(body, [dact, zf, wc, bc], exchange, name="ffn_bwd", grid=(D_FF // cw, bsz),
                in_specs=[pl.BlockSpec((1, s, cw), lambda j, b: (b, 0, j)),
                          pl.BlockSpec((1, s, 2 * cw), lambda j, b: (b, 0, j)),
                          pl.BlockSpec((3, cw), lambda j, b: (0, j)), pl.BlockSpec((1, cw), lambda j, b: (0, j))],
                out_specs=[pl.BlockSpec((1, s, 2 * cw), lambda j, b: (b, 0, j)),
                           pl.BlockSpec((8, cw), lambda j, b: (0, j))],
                out_shape=[jax.ShapeDtypeStruct(zf.shape, BF16), jax.ShapeDtypeStruct((8, D_FF), F32)],
                compiler_params=_params())


def _adamw(w, m, v, parts, *, name):
    r, c = w.shape
    p = parts.shape[0]
    tr = r if r <= 512 else 256
    assert r % tr == 0 and parts.shape[1:] == (r, c), (name, w.shape, parts.shape)
    c1 = 1.0 / (1.0 - ADAM_B1 ** ADAM_STEP)
    c2 = 1.0 / (1.0 - ADAM_B2 ** ADAM_STEP)

    def body(w_ref, m_ref, v_ref, p_ref, g_ref, d_ref, mo_ref, vo_ref):
        g = p_ref[0].astype(F32)
        for q in range(1, p):
            g = g + p_ref[q].astype(F32)
        mn = ADAM_B1 * m_ref[...] + (1.0 - ADAM_B1) * g
        vn = ADAM_B2 * v_ref[...] + (1.0 - ADAM_B2) * (g * g)
        g_ref[...] = g
        mo_ref[...] = mn
        vo_ref[...] = vn
        d_ref[...] = -ADAM_LR * ((mn * c1) / (jnp.sqrt(vn * c2) + ADAM_EPS) + ADAM_WD * w_ref[...])

    spec = pl.BlockSpec((tr, c), lambda i: (i, 0))
    shp = jax.ShapeDtypeStruct((r, c), F32)
    return _call(body, name=name, grid=(r // tr,),
                 in_specs=[spec, spec, spec, pl.BlockSpec((p, tr, c), lambda i: (0, i, 0))],
                 out_specs=[spec] * 4, out_shape=[shp] * 4, compiler_params=_params())(w, m, v, parts)


def _interleave(w, groups, width):
    n = w.shape[1] // (groups * width)
    return jnp.concatenate([w[:, (g * n + j) * width:(g * n + j + 1) * width]
                            for j in range(n) for g in range(groups)], axis=1)


def _deinterleave(w, groups, width):
    n = w.shape[1] // (groups * width)
    return jnp.concatenate([w[:, (j * groups + g) * width:(j * groups + g + 1) * width]
                            for g in range(groups) for j in range(n)], axis=1)


def _up_from_shards(ex):
    per = ex.shape[2] // 128
    blocks = []
    for j in range(D_FF // 128):
        for r in range(2):
            i, lb = divmod(r * (D_FF // 128) + j, per)
            blocks.append(ex[i, :, lb * 128:(lb + 1) * 128])
    return jnp.concatenate(blocks, axis=1)


def _up_to_shards(g):
    per = 2 * D_FF // 128 // N_DEV
    shards = []
    for i in range(N_DEV):
        blocks = []
        for ob in range(per * i, per * (i + 1)):
            r, j = divmod(ob, D_FF // 128)
            blocks.append(g[:, (2 * j + r) * 128:(2 * j + r + 1) * 128])
        shards.append(jnp.concatenate(blocks, axis=1))
    return jnp.stack(shards)


def _to_z_layout(w):
    xg = _interleave(w[:, :2 * D_RNN], 2, RNN_BLK)
    qkv = _interleave(w[:, 2 * D_RNN:2 * D_RNN + 3 * D_MODEL], 3, HEAD_DIM)
    fl0 = 2 * D_RNN + 3 * D_MODEL
    fl = w[:, fl0:fl0 + N_HEADS]
    mg = w[:, fl0 + N_HEADS:]
    pad = jnp.zeros((w.shape[0], MG0 - FL0 - N_HEADS), w.dtype)
    return jnp.concatenate([qkv, xg, fl, pad, mg], axis=1)


def _from_z_layout(w):
    qkv = _deinterleave(w[:, QKV0:XG0], 3, HEAD_DIM)
    xg = _deinterleave(w[:, XG0:FL0], 2, RNN_BLK)
    return jnp.concatenate([xg, qkv, w[:, FL0:FL0 + N_HEADS], w[:, MG0:]], axis=1)


def _col_gather(g):
    return g.transpose(1, 0, 2).reshape(g.shape[1], -1)


def _col_scatter(w):
    return w.reshape(w.shape[0], N_DEV, -1).transpose(1, 0, 2)


def kernel(x, c, w_ada, b_ada, g_norm1, w_in, w_rnn_conv, b_rnn_conv, w_lru_a, b_lru_a, w_lru_i, b_lru_i, lru_lambda, b_fgate, w_proj_rnn, w_proj_attn, w_out, g_norm2, w_ffn_up, w_ffn_conv, b_ffn_conv, w_ffn_down, w_ada_final, b_ada_final, g_final, loss_target, m_w_ada, m_b_ada, m_g_norm1, m_w_in, m_w_rnn_conv, m_b_rnn_conv, m_w_lru_a, m_b_lru_a, m_w_lru_i, m_b_lru_i, m_lru_lambda, m_b_fgate, m_w_proj_rnn, m_w_proj_attn, m_w_out, m_g_norm2, m_w_ffn_up, m_w_ffn_conv, m_b_ffn_conv, m_w_ffn_down, m_w_ada_final, m_b_ada_final, m_g_final, v_w_ada, v_b_ada, v_g_norm1, v_w_in, v_w_rnn_conv, v_b_rnn_conv, v_w_lru_a, v_b_lru_a, v_w_lru_i, v_b_lru_i, v_lru_lambda, v_b_fgate, v_w_proj_rnn, v_w_proj_attn, v_w_out, v_g_norm2, v_w_ffn_up, v_w_ffn_conv, v_b_ffn_conv, v_w_ffn_down, v_w_ada_final, v_b_ada_final, v_g_final):
    args = dict(locals())
    bsz, s, _ = x.shape
    t = bsz * s
    nb = N_DEV * bsz
    me = 4 * lax.axis_index("x") + 2 * lax.axis_index("y") + lax.axis_index("c")
    tm = min(1024, t)

    ex_a = _gather_two_level([c, w_in[0].astype(BF16), w_rnn_conv[0], w_ffn_conv[0]], "gather_first")
    c_all = ex_a[0].reshape(nb, D_MODEL)
    win_z = _to_z_layout(_col_gather(ex_a[1]))
    wrc = _col_gather(ex_a[2])
    wfc = _col_gather(ex_a[3])

    nmod = w_ada.shape[2]
    nmodf = w_ada_final.shape[1]
    mod_cols = _mm(c_all, w_ada[0], name="mod_cols", tm=nb, tn=nmod, tk=D_MODEL, silu_a=True,
                   bias=lax.dynamic_slice(b_ada, (0, me * nmod), (1, nmod)))
    modf_cols = _mm(c_all, w_ada_final, name="modf_cols", tm=nb, tn=nmodf, tk=D_MODEL, silu_a=True,
                    bias=lax.dynamic_slice(b_ada_final.reshape(1, -1), (0, me * nmodf), (1, nmodf)))
    ex_b = _exchange([mod_cols, modf_cols], [True, True], "gather_mod")
    mod = lax.dynamic_slice(ex_b[0], (0, me * bsz, 0), (N_DEV, bsz, nmod)).transpose(1, 0, 2).reshape(bsz, 6, 1, D_MODEL)
    modf = lax.dynamic_slice(ex_b[1], (0, me * bsz, 0), (N_DEV, bsz, nmodf)).transpose(1, 0, 2).reshape(bsz, 2, 1, D_MODEL)
    shift1, scale1, gate1, shift2, scale2, gate2 = [mod[:, i] for i in range(6)]
    shift_f, scale_f = modf[:, 0], modf[:, 1]

    h1 = _norm_fwd(x, g_norm1, shift1, scale1, name="norm1_fwd")[0]
    h1f = h1.reshape(t, D_MODEL)
    zq = _mm(h1f, win_z, name="mm_in_qkv", out_dtype=BF16, tm=tm, tn=1024, tk=D_MODEL,
             b_cols=(QKV0, XG0)).reshape(bsz, s, XG0)
    zr = _mm(h1f, win_z, name="mm_in_rest", tm=tm, tn=1024, tk=D_MODEL, b_cols=(XG0, NZR)).reshape(bsz, s, NZR)
    h_rnn, y_rnn, got = _rnn_fwd(zr, wrc, b_rnn_conv, w_lru_a[0], b_lru_a, w_lru_i[0], b_lru_i, lru_lambda,
                                 exchange=([w_ffn_up[0].astype(BF16)], [True]))
    wup_z = _up_from_shards(got[0])
    flt = zr[:, :, FL0R:FL0R + N_HEADS].transpose(0, 2, 1)
    bfg = b_fgate.reshape(N_HEADS, 1)
    fcum = _fgate_fwd(flt, bfg)
    fcol = fcum.reshape(bsz, N_HEADS, s, 1)
    frow = fcum.reshape(bsz, N_HEADS, 1, s)
    o_attn, lse, got = _attn_fwd(
        zq, fcol, frow, exchange=([w_proj_rnn[0].astype(BF16), w_proj_attn[0].astype(BF16), w_out[0].astype(BF16),
                                  w_ffn_down[0].astype(BF16)], [True] * 4))
    wpr = got[0].reshape(D_RNN, D_MODEL)
    wpa = got[1].reshape(D_MODEL, D_MODEL)
    wout = got[2].reshape(D_MODEL, D_MODEL)
    wdown = got[3].reshape(D_FF, D_MODEL)
    pr = _mm(y_rnn.reshape(t, D_RNN), wpr, name="mm_proj_rnn", tm=tm, tn=D_MODEL, tk=D_RNN)
    pa = _mm(o_attn.reshape(t, D_MODEL), wpa, name="mm_proj_attn", tm=tm, tn=D_MODEL, tk=D_MODEL)
    z2 = zr.reshape(t, NZR)
    merged = _merge_fwd(z2, pr, pa)
    mo = _mm(merged, wout, name="mm_out", tm=tm, tn=D_MODEL, tk=D_MODEL).reshape(bsz, s, D_MODEL)
    h2, x1 = _norm_fwd(x, g_norm2, shift2, scale2, name="norm2_fwd", yprev=mo, gate=gate1)
    zf = _mm(h2.reshape(t, D_MODEL), wup_z, name="mm_up", out_dtype=BF16, tm=tm, tn=1024,
             tk=D_MODEL).reshape(bsz, s, 2 * D_FF)
    act = _ffn_fwd(zf, wfc, b_ffn_conv)
    yf = _mm(act.reshape(t, D_FF), wdown, name="mm_down", tm=tm, tn=D_MODEL, tk=1024).reshape(bsz, s, D_MODEL)
    lp, dx2, dyf, dshift_f, dscale_f, dgate2, dg_final = _final(x1, yf, gate2, g_final.reshape(1, -1), shift_f, scale_f, loss_target)
    loss = lax.psum(jnp.sum(lp[:, :, 0, 0]) * (0.5 / D_MODEL), ("x", "y", "c"))

    dyf2 = dyf.reshape(t, D_MODEL)
    act2 = act.reshape(t, D_FF)
    dact = _mm(dyf2, wdown, name="mm_down_dx", tb=True, out_dtype=BF16, tm=tm, tn=1024, tk=D_MODEL).reshape(bsz, s, D_FF)
    g_wdown = _mm(act2, dyf2, name="mm_down_dw", ta=True, out_dtype=BF16, tm=1024, tn=D_MODEL, tk=tm)
    dzf, dsm_ffn, got = _ffn_bwd(dact, zf, wfc, b_ffn_conv, exchange=([g_wdown.reshape(N_DEV, -1, D_MODEL)], [False]))
    p_wdown = got[0]
    dzf2 = dzf.reshape(t, 2 * D_FF)
    dh2 = _mm(dzf2, wup_z, name="mm_up_dx", tb=True, tm=tm, tn=D_MODEL, tk=1024).reshape(bsz, s, D_MODEL)
    g_wup_z = _mm(h2.reshape(t, D_MODEL), dzf2, name="mm_up_dw", ta=True, out_dtype=BF16, tm=D_MODEL, tn=1024, tk=tm)
    dx1, dg_norm2, dshift2, dscale2, dmo, dgate1 = _norm_bwd(x1, dh2, g_norm2, scale2, dx2, name="norm2_bwd", yprev=mo, gate=gate1)
    dmo2 = dmo.reshape(t, D_MODEL)
    dmerged = _mm(dmo2, wout, name="mm_out_dx", tb=True, tm=tm, tn=D_MODEL, tk=D_MODEL)
    g_wout = _mm(merged, dmo2, name="mm_out_dw", ta=True, out_dtype=BF16, tm=D_MODEL, tn=D_MODEL, tk=tm)
    dz2, dpr, dpa = _merge_bwd(dmerged, z2, pr, pa)
    dy_rnn = _mm(dpr, wpr, name="mm_proj_rnn_dx", tb=True, tm=tm, tn=D_RNN, tk=D_MODEL).reshape(bsz, s, D_RNN)
    g_wpr = _mm(y_rnn.reshape(t, D_RNN), dpr, name="mm_proj_rnn_dw", ta=True, out_dtype=BF16, tm=D_RNN, tn=D_MODEL, tk=tm)
    do = _mm(dpa, wpa, name="mm_proj_attn_dx", tb=True, out_dtype=BF16, tm=tm, tn=D_MODEL, tk=D_MODEL).reshape(bsz, s, D_MODEL)
    g_wpa = _mm(o_attn.reshape(t, D_MODEL), dpa, name="mm_proj_attn_dw", ta=True, out_dtype=BF16, tm=D_MODEL, tn=D_MODEL, tk=tm)
    dz, dfrow, got = _attn_bwd(zq, do, lse, fcol, frow, dz2.reshape(bsz, s, NZ),
                               exchange=([_up_to_shards(g_wup_z), g_wout.reshape(N_DEV, -1, D_MODEL)], [False, False]))
    p_wup, p_wout = got
    dflt, db_fgate = _fgate_bwd(dfrow.reshape(bsz, N_HEADS, s), flt, bfg)
    dfl_blk = jnp.pad(dflt.transpose(0, 2, 1).astype(BF16), ((0, 0), (0, 0), (0, MG0 - FL0 - N_HEADS)))
    dz = _fl_write(dfl_blk, dz)
    dz, g_wa, g_wi, dsm_rnn, got = _rnn_bwd(
        zr, h_rnn, dy_rnn, dz, wrc, b_rnn_conv, w_lru_a[0], b_lru_a, w_lru_i[0], b_lru_i, lru_lambda,
        exchange=([g_wpr.reshape(N_DEV, -1, D_MODEL), g_wpa.reshape(N_DEV, -1, D_MODEL)], [False, False]))
    p_wpr, p_wpa = got
    dz2 = dz.reshape(t, NZ)
    small = [dg_norm2, dg_final, dsm_rnn, g_wa, g_wi, db_fgate, dsm_ffn]
    small_sizes = [a.size for a in small]
    n_pad = -sum(small_sizes) % 1024
    packed = jnp.concatenate([a.reshape(-1) for a in small] + [jnp.zeros((n_pad,), F32)]).reshape(-1, 128)
    g_win_z, got = _mm(h1f, dz2, name="mm_in_dw", ta=True, out_dtype=BF16, tm=D_MODEL, tn=1024, tk=tm,
                       exchange=([packed], [True]))
    flat = got[0].reshape(N_DEV, -1)
    dh1, got = _mm(dz2, win_z, name="mm_in_dx", tb=True, tm=tm, tn=D_MODEL, tk=1024,
                   exchange=([_col_scatter(_from_z_layout(g_win_z))], [False]))
    p_win = got[0]
    grad_x, dg_norm1, dshift1, dscale1 = _norm_bwd(x, dh1.reshape(bsz, s, D_MODEL), g_norm1, scale1, dx1, name="norm1_bwd")

    dmod = jnp.concatenate([dshift1, dscale1, dgate1, dshift2, dscale2, dgate2], axis=1).reshape(bsz, 6 * D_MODEL)
    dmodf = jnp.concatenate([dshift_f, dscale_f], axis=1).reshape(bsz, 2 * D_MODEL)
    ex_c = _exchange([dmod, dmodf, dg_norm1], [True, True, True], "gather_last_grads")
    dmod_all = ex_c[0].reshape(nb, 6 * D_MODEL)
    dmodf_all = ex_c[1].reshape(nb, 2 * D_MODEL)
    p_g1 = ex_c[2]

    res = {}

    def upd(name, parts, shape2):
        w, m, v = args[name], args["m_" + name], args["v_" + name]
        outs = _adamw(w.reshape(shape2), m.reshape(shape2), v.reshape(shape2), parts, name="adamw_" + name)
        res[name] = [o.reshape(w.shape) for o in outs]

    g_wada = _mm(c_all, lax.dynamic_slice(dmod_all, (0, me * nmod), (nb, nmod)), name="mm_ada_dw", ta=True,
                 silu_a=True, tm=D_MODEL, tn=nmod, tk=nb)
    g_wadaf = _mm(c_all, lax.dynamic_slice(dmodf_all, (0, me * nmodf), (nb, nmodf)), name="mm_adaf_dw", ta=True,
                  silu_a=True, tm=D_MODEL, tn=nmodf, tk=nb)
    upd("w_ada", g_wada[None], (D_MODEL, nmod))
    upd("w_ada_final", g_wadaf[None], (D_MODEL, nmodf))
    upd("b_ada", dmod_all.reshape(nb, 1, -1), (1, 6 * D_MODEL))
    upd("b_ada_final", dmodf_all.reshape(nb, 1, -1), (1, 2 * D_MODEL))
    upd("w_in", p_win, w_in.shape[1:])
    upd("w_proj_rnn", p_wpr, w_proj_rnn.shape[1:])
    upd("w_proj_attn", p_wpa, w_proj_attn.shape[1:])
    upd("w_out", p_wout, w_out.shape[1:])
    upd("w_ffn_up", p_wup, w_ffn_up.shape[1:])
    upd("w_ffn_down", p_wdown, w_ffn_down.shape[1:])

    parts, off = [], 0
    for a, n in zip(small, small_sizes):
        parts.append(flat[:, off:off + n].reshape((N_DEV,) + a.shape))
        off += n
    p_g2, p_gf, p_rnn, p_wa, p_wi, p_bfg, p_ffn = parts
    upd("g_norm1", p_g1, (1, D_MODEL))
    upd("g_norm2", p_g2, (1, D_MODEL))
    upd("g_final", p_gf, (1, D_MODEL))
    nrc = w_rnn_conv.shape[2]
    upd("w_rnn_conv", lax.dynamic_slice(p_rnn, (0, 0, me * nrc), (N_DEV, 4, nrc)), (4, nrc))
    upd("b_rnn_conv", p_rnn[:, 4:5], (1, D_RNN))
    upd("b_lru_a", p_rnn[:, 5:6], (1, D_RNN))
    upd("b_lru_i", p_rnn[:, 6:7], (1, D_RNN))
    upd("lru_lambda", p_rnn[:, 7:8], (1, D_RNN))
    upd("w_lru_a", p_wa.reshape(N_DEV, D_RNN, RNN_BLK), (D_RNN, RNN_BLK))
    upd("w_lru_i", p_wi.reshape(N_DEV, D_RNN, RNN_BLK), (D_RNN, RNN_BLK))
    upd("b_fgate", p_bfg.reshape(N_DEV, 1, N_HEADS), (1, N_HEADS))
    nfc = w_ffn_conv.shape[2]
    upd("w_ffn_conv", lax.dynamic_slice(p_ffn, (0, 0, me * nfc), (N_DEV, 3, nfc)), (3, nfc))
    upd("b_ffn_conv", p_ffn[:, 3:4], (1, D_FF))

    names = ['w_ada', 'b_ada', 'g_norm1', 'w_in', 'w_rnn_conv', 'b_rnn_conv', 'w_lru_a', 'b_lru_a', 'w_lru_i',
             'b_lru_i', 'lru_lambda', 'b_fgate', 'w_proj_rnn', 'w_proj_attn', 'w_out', 'g_norm2', 'w_ffn_up',
             'w_ffn_conv', 'b_ffn_conv', 'w_ffn_down', 'w_ada_final', 'b_ada_final', 'g_final']
    return (loss, grad_x, *[res[n][0] for n in names], *[res[n][1] for n in names],
            *[res[n][2] for n in names], *[res[n][3] for n in names])
```

```python
import jax
import jax.numpy as jnp
from jax import lax
from jax.experimental import pallas as pl
from jax.experimental.pallas import tpu as pltpu

F32, BF16 = jnp.float32, jnp.bfloat16
D_MODEL, D_RNN, N_BLK, RNN_BLK = 1024, 1280, 10, 128
N_HEADS, HEAD_DIM, D_FF = 8, 128, 3072
N_DEV = 8
NZ = 8192
QKV0, XG0, FL0, MG0 = 0, 3072, 5632, 6144
NZR = NZ - XG0
XG0R, FL0R, MG0R = 0, FL0 - XG0, MG0 - XG0
RMS_EPS = 1e-6
LRU_C = 8.0
ADAM_LR, ADAM_B1, ADAM_B2, ADAM_EPS, ADAM_WD, ADAM_STEP = 0.001, 0.9, 0.999, 1e-08, 0.01, 10
VMEM_LIMIT = 48 << 20
SCAN_CHUNK = 256
ATTN_TQ = 256
CUM_BLK = 512
MESH = pl.DeviceIdType.MESH
ANY = pl.BlockSpec(memory_space=pl.ANY)


def _call(body, **kw):
    return pl.pallas_call(body, **kw)


def _params():
    return pltpu.CompilerParams(vmem_limit_bytes=VMEM_LIMIT)


ZF_PIECES = 2 * D_FF // 128 // N_DEV


def _zf_cols(owner, p):
    ob = owner * ZF_PIECES + p
    blk = 2 * (ob % (D_FF // 128)) + ob // (D_FF // 128)
    return pl.ds(pl.multiple_of(blk * 128, 128), 128)


def _n_pieces(mode):
    return ZF_PIECES if mode in ("gather_zf", "scatter_zf") else 1


def _exchanged_shape(a, mode):
    if mode is True:
        return (N_DEV,) + a.shape
    if mode is False:
        return a.shape
    if mode == "gather_zf":
        return (a.shape[0], N_DEV * a.shape[1])
    return (N_DEV, a.shape[0], a.shape[1] // N_DEV)


def _direct_copies(ins, outs, sems, modes):
    send_sems, recv_sems, loc_sems = sems
    x, y, c = lax.axis_index("x"), lax.axis_index("y"), lax.axis_index("c")
    me = 4 * x + 2 * y + c

    def ends(a, to, p):
        mode = modes[a]
        if mode is True:
            return ins[a], outs[a].at[me]
        if mode is False:
            return ins[a].at[to], outs[a].at[me]
        mine = slice(p * 128, (p + 1) * 128)
        if mode == "gather_zf":
            return ins[a].at[:, mine], outs[a].at[:, _zf_cols(me, p)]
        return ins[a].at[:, _zf_cols(to, p)], outs[a].at[me, :, mine]

    local, remote = [], []
    nloc = 0
    for a in range(len(ins)):
        for p in range(_n_pieces(modes[a])):
            src, dst = ends(a, me, p)
            local.append(pltpu.make_async_copy(src, dst, loc_sems.at[nloc]))
            nloc += 1
    for k in range(1, N_DEV):
        px = 1 - x if k & 4 else x
        py = 1 - y if k & 2 else y
        pc = 1 - c if k & 1 else c
        peer = 4 * px + 2 * py + pc
        nrem = (k - 1) * nloc
        for a in range(len(ins)):
            for p in range(_n_pieces(modes[a])):
                src, dst = ends(a, peer, p)
                remote.append(pltpu.make_async_remote_copy(
                    src_ref=src, dst_ref=dst, send_sem=send_sems.at[nrem], recv_sem=recv_sems.at[nrem],
                    device_id=(px, py, pc), device_id_type=MESH))
                nrem += 1
    return local, remote


def _with_exchange(body, args, arrs, bcast, *, name, grid, in_specs, out_specs, out_shape, scratch_shapes=(), **kw):
    n_in, n_out, n_sc, nc = len(args), len(out_shape), len(scratch_shapes), len(arrs)

    def wrapped(*refs):
        core_in, cin = refs[:n_in], refs[n_in:n_in + nc]
        core_out = refs[n_in + nc:n_in + nc + n_out]
        cout = refs[n_in + nc + n_out:n_in + 2 * nc + n_out]
        rest = refs[n_in + 2 * nc + n_out:]
        core_sc, sems = rest[:n_sc], rest[n_sc:]

        def start():
            local, remote = _direct_copies(cin, cout, sems, bcast)
            for cp in local + remote:
                cp.start()

        def wait():
            local, remote = _direct_copies(cin, cout, sems, bcast)
            for cp in remote + local:
                cp.wait()

        if not grid:
            start()
            body(*core_in, *core_out, *core_sc)
            wait()
            return
        first, last = None, None
        for d, g in enumerate(grid):
            i = pl.program_id(d)
            first = (i == 0) if first is None else jnp.logical_and(first, i == 0)
            last = (i == g - 1) if last is None else jnp.logical_and(last, i == g - 1)
        pl.when(first)(start)
        body(*core_in, *core_out, *core_sc)
        pl.when(last)(wait)

    ex_shape = [jax.ShapeDtypeStruct(_exchanged_shape(a, mode), a.dtype) for a, mode in zip(arrs, bcast)]
    ncopies = sum(_n_pieces(mode) for mode in bcast)
    sems = [pltpu.SemaphoreType.DMA((ncopies * (N_DEV - 1),)), pltpu.SemaphoreType.DMA((ncopies * (N_DEV - 1),)),
            pltpu.SemaphoreType.DMA((ncopies,))]
    res = _call(wrapped, name=name, grid=grid, in_specs=list(in_specs) + [ANY] * nc,
                out_specs=list(out_specs) + [ANY] * nc, out_shape=list(out_shape) + ex_shape,
                scratch_shapes=list(scratch_shapes) + sems, **kw)(*args, *arrs)
    return list(res[:n_out]), list(res[n_out:])


def _run(body, args, exchange, **kw):
    if exchange is None:
        return list(_call(body, **kw)(*args))
    outs, got = _with_exchange(body, args, *exchange, **kw)
    return outs + [got]


def _exchange(arrs, bcast, name):
    return _with_exchange(lambda: None, [], arrs, bcast, name=name, grid=(), in_specs=[], out_specs=[], out_shape=[],
                          compiler_params=pltpu.CompilerParams(has_side_effects=True))[1]


def _gather_two_level(arrs, name):
    n = len(arrs)

    def body(*refs):
        ins, outs = refs[:n], refs[n:2 * n]
        send_sems, recv_sems, loc_sems = refs[2 * n:]
        x, y, c = lax.axis_index("x"), lax.axis_index("y"), lax.axis_index("c")
        me, sibling = (x, y, c), (x, y, 1 - c)
        chips = [(1 - x, y), (x, 1 - y), (1 - x, 1 - y)]

        def slot(a, p):
            return outs[a].at[4 * p[0] + 2 * p[1] + p[2]]

        def copy(a, k, block, to, src=None):
            return pltpu.make_async_remote_copy(
                src_ref=slot(a, block) if src is None else src, dst_ref=slot(a, block),
                send_sem=send_sems.at[a * 7 + k], recv_sem=recv_sems.at[a * 7 + k],
                device_id=to, device_id_type=MESH)

        mine = [pltpu.make_async_copy(ins[a], slot(a, me), loc_sems.at[a]) for a in range(n)]
        first = []
        for a in range(n):
            first.append(copy(a, 0, me, sibling, src=ins[a]))
            first += [copy(a, 1 + j, me, (*chip, c), src=ins[a]) for j, chip in enumerate(chips)]
        for cp in mine + first:
            cp.start()
        passed = []
        for j, chip in enumerate(chips):
            for a in range(n):
                copy(a, 1 + j, (*chip, c), me).wait_recv()
                fwd = copy(a, 4 + j, (*chip, c), sibling)
                fwd.start()
                passed.append(fwd)
        for a in range(n):
            copy(a, 0, sibling, me).wait_recv()
        for j, chip in enumerate(chips):
            for a in range(n):
                copy(a, 4 + j, (*chip, 1 - c), me).wait_recv()
        for cp in first + passed:
            cp.wait_send()
        for cp in mine:
            cp.wait()

    return _call(
        body, name=name, in_specs=[ANY] * n, out_specs=[ANY] * n,
        out_shape=[jax.ShapeDtypeStruct((N_DEV,) + a.shape, a.dtype) for a in arrs],
        scratch_shapes=[pltpu.SemaphoreType.DMA((n * 7,)), pltpu.SemaphoreType.DMA((n * 7,)),
                        pltpu.SemaphoreType.DMA((n,))],
        compiler_params=pltpu.CompilerParams(has_side_effects=True),
    )(*arrs)


def _sigmoid(x):
    return 1.0 / (1.0 + jnp.exp(-x))


def _log1p(e):
    u = 1.0 + e
    d = u - 1.0
    return jnp.where(d == 0.0, e, jnp.log(u) * (e / jnp.where(d == 0.0, 1.0, d)))


def _softplus(x):
    return jnp.maximum(x, 0.0) + _log1p(jnp.exp(-jnp.abs(x)))


_GELU_C = 0.7978845608028654


def _gelu(x):
    t = jnp.tanh(_GELU_C * (x + 0.044715 * x * x * x))
    return 0.5 * x * (1.0 + t)


def _gelu_and_grad(x):
    t = jnp.tanh(_GELU_C * (x + 0.044715 * x * x * x))
    g = 0.5 * x * (1.0 + t)
    dg = 0.5 * (1.0 + t) + 0.5 * x * (1.0 - t * t) * _GELU_C * (1.0 + 3.0 * 0.044715 * x * x)
    return g, dg


def _acc(ref, idx, val, first):
    @pl.when(first)
    def _():
        ref[idx] = val

    @pl.when(jnp.logical_not(first))
    def _():
        ref[idx] = ref[idx] + val


def _split3(x):
    hi = x.astype(BF16)
    r1 = x - hi.astype(F32)
    mid = r1.astype(BF16)
    lo = (r1 - mid.astype(F32)).astype(BF16)
    return hi, mid, lo


def _mm(a, b, *, name, ta=False, tb=False, out_dtype=F32, tm, tn, tk, bias=None, silu_a=False, exchange=None,
        b_cols=None):
    m, k = (a.shape[1], a.shape[0]) if ta else a.shape
    col0, n = b_cols if b_cols is not None else (0, b.shape[0] if tb else b.shape[1])
    assert m % tm == 0 and n % tn == 0 and k % tk == 0 and col0 % tn == 0, (name, m, n, k)
    jo = col0 // tn
    nk = k // tk
    dn = (((0 if ta else 1,), (1 if tb else 0,)), ((), ()))
    use_acc = nk > 1 and out_dtype != F32

    def body(*refs):
        a_ref, b_ref = refs[0], refs[1]
        bias_ref = refs[2] if bias is not None else None
        o_ref = refs[3 if bias is not None else 2]
        av = a_ref[...]
        if silu_a:
            av = av * _sigmoid(av)
        p = lax.dot_general(av.astype(BF16), b_ref[...].astype(BF16), dn, preferred_element_type=F32)
        if bias is not None:
            p = p + bias_ref[...]
        if nk == 1:
            o_ref[...] = p.astype(out_dtype)
            return
        kk = pl.program_id(2)
        acc = refs[-1] if use_acc else o_ref

        @pl.when(kk == 0)
        def _():
            acc[...] = p

        @pl.when(kk > 0)
        def _():
            acc[...] = acc[...] + p

        if use_acc:
            @pl.when(kk == nk - 1)
            def _():
                o_ref[...] = acc[...].astype(out_dtype)

    a_spec = pl.BlockSpec((tk, tm), lambda i, j, q: (q, i)) if ta else pl.BlockSpec((tm, tk), lambda i, j, q: (i, q))
    b_spec = (pl.BlockSpec((tn, tk), lambda i, j, q: (j + jo, q)) if tb
              else pl.BlockSpec((tk, tn), lambda i, j, q: (q, j + jo)))
    in_specs, args = [a_spec, b_spec], [a, b]
    if bias is not None:
        in_specs.append(pl.BlockSpec((1, tn), lambda i, j, q: (0, j)))
        args.append(bias)
    kw = dict(name=name, grid=(m // tm, n // tn, nk), in_specs=in_specs,
              out_specs=[pl.BlockSpec((tm, tn), lambda i, j, q: (i, j))],
              out_shape=[jax.ShapeDtypeStruct((m, n), out_dtype)],
              scratch_shapes=[pltpu.VMEM((tm, tn), F32)] if use_acc else [],
              compiler_params=_params())
    if exchange is None:
        return _call(body, **kw)(*args)[0]
    outs, got = _with_exchange(body, args, *exchange, **kw)
    return outs[0], got


def _tok_spec(ts):
    return pl.BlockSpec((1, ts, D_MODEL), lambda b, s: (b, s, 0))


def _seq_spec():
    return pl.BlockSpec((1, 1, D_MODEL), lambda b, s: (b, 0, 0))


def _vec_spec():
    return pl.BlockSpec((1, D_MODEL), lambda b, s: (0, 0))


def _norm_fwd(x, g, shift, scale, *, name, yprev=None, gate=None, ts=512):
    bsz, s, _ = x.shape
    res = yprev is not None

    def body(*refs):
        if res:
            x_ref, g_ref, sh_ref, sc_ref, y_ref, gt_ref, h_ref, xn_ref = refs
            xv = x_ref[0] + gt_ref[0] * y_ref[0]
            xn_ref[0] = xv
        else:
            x_ref, g_ref, sh_ref, sc_ref, h_ref = refs
            xv = x_ref[0]
        rstd = lax.rsqrt(jnp.mean(xv * xv, axis=-1, keepdims=True) + RMS_EPS)
        nv = xv * rstd * g_ref[...]
        h_ref[0] = (nv * (1.0 + sc_ref[0]) + sh_ref[0]).astype(BF16)

    in_specs = [_tok_spec(ts), _vec_spec(), _seq_spec(), _seq_spec()]
    args = [x, g, shift, scale]
    out_specs = [_tok_spec(ts)]
    out_shape = [jax.ShapeDtypeStruct(x.shape, BF16)]
    if res:
        in_specs += [_tok_spec(ts), _seq_spec()]
        args += [yprev, gate]
        out_specs.append(_tok_spec(ts))
        out_shape.append(jax.ShapeDtypeStruct(x.shape, F32))
    return _call(body, name=name, grid=(bsz, s // ts), in_specs=in_specs, out_specs=out_specs,
                 out_shape=out_shape, compiler_params=_params())(*args)


def _final(x1, yf, gate2, g, shift, scale, target, *, ts=512):
    bsz, s, _ = x1.shape
    ns = s // ts

    def body(x_ref, y_ref, gt_ref, g_ref, sh_ref, sc_ref, t_ref,
             lp_ref, dx_ref, dyf_ref, dsh_ref, dsc_ref, dgt_ref, dg_ref):
        b, si = pl.program_id(0), pl.program_id(1)
        yv, gt, gv = y_ref[0], gt_ref[0], g_ref[...]
        xv = x_ref[0] + gt * yv
        rstd = lax.rsqrt(jnp.mean(xv * xv, axis=-1, keepdims=True) + RMS_EPS)
        xhat = xv * rstd
        nv = xhat * gv
        sc1 = 1.0 + sc_ref[0]
        err = nv * sc1 + sh_ref[0] - t_ref[0]
        lp_ref[...] = jnp.zeros((1, 1, 8, 128), F32) + jnp.sum(err * err)
        dy = err * (1.0 / D_MODEL)
        dn = dy * sc1
        dxh = dn * gv
        dx = rstd * (dxh - xhat * jnp.mean(dxh * xhat, axis=-1, keepdims=True))
        dx_ref[0] = dx
        dyf_ref[0] = (dx * gt).astype(BF16)
        _acc(dsh_ref, 0, jnp.sum(dy, axis=0, keepdims=True), si == 0)
        _acc(dsc_ref, 0, jnp.sum(dy * nv, axis=0, keepdims=True), si == 0)
        _acc(dgt_ref, 0, jnp.sum(dx * yv, axis=0, keepdims=True), si == 0)
        _acc(dg_ref, slice(None), jnp.sum(dn * xhat, axis=0, keepdims=True), jnp.logical_and(b == 0, si == 0))

    seq_shape = jax.ShapeDtypeStruct((bsz, 1, D_MODEL), F32)
    return _call(
        body, name="final_loss", grid=(bsz, ns),
        in_specs=[_tok_spec(ts), _tok_spec(ts), _seq_spec(), _vec_spec(), _seq_spec(), _seq_spec(), _tok_spec(ts)],
        out_specs=[pl.BlockSpec((1, 1, 8, 128), lambda b, s: (b, s, 0, 0)), _tok_spec(ts), _tok_spec(ts),
                   _seq_spec(), _seq_spec(), _seq_spec(), _vec_spec()],
        out_shape=[jax.ShapeDtypeStruct((bsz, ns, 8, 128), F32), jax.ShapeDtypeStruct(x1.shape, F32),
                   jax.ShapeDtypeStruct(x1.shape, BF16), seq_shape, seq_shape, seq_shape,
                   jax.ShapeDtypeStruct((1, D_MODEL), F32)],
        compiler_params=_params(),
    )(x1, yf, gate2, g, shift, scale, target)


def _norm_bwd(x, dh, g, scale, dres, *, name, yprev=None, gate=None, ts=512):
    bsz, s, _ = x.shape
    res = yprev is not None

    def body(*refs):
        if res:
            (x_ref, dh_ref, g_ref, sc_ref, dr_ref, y_ref, gt_ref,
             dx_ref, dg_ref, dsh_ref, dsc_ref, dy_ref, dgt_ref) = refs
        else:
            x_ref, dh_ref, g_ref, sc_ref, dr_ref, dx_ref, dg_ref, dsh_ref, dsc_ref = refs
        b, si = pl.program_id(0), pl.program_id(1)
        xv, dhv, gv = x_ref[0], dh_ref[0], g_ref[...]
        rstd = lax.rsqrt(jnp.mean(xv * xv, axis=-1, keepdims=True) + RMS_EPS)
        xhat = xv * rstd
        dn = dhv * (1.0 + sc_ref[0])
        dxh = dn * gv
        dx = dr_ref[0] + rstd * (dxh - xhat * jnp.mean(dxh * xhat, axis=-1, keepdims=True))
        dx_ref[0] = dx
        _acc(dsh_ref, 0, jnp.sum(dhv, axis=0, keepdims=True), si == 0)
        _acc(dsc_ref, 0, jnp.sum(dhv * (xhat * gv), axis=0, keepdims=True), si == 0)
        _acc(dg_ref, slice(None), jnp.sum(dn * xhat, axis=0, keepdims=True), jnp.logical_and(b == 0, si == 0))
        if res:
            dy_ref[0] = (dx * gt_ref[0]).astype(BF16)
            _acc(dgt_ref, 0, jnp.sum(dx * y_ref[0], axis=0, keepdims=True), si == 0)

    seq_shape = jax.ShapeDtypeStruct((bsz, 1, D_MODEL), F32)
    in_specs = [_tok_spec(ts), _tok_spec(ts), _vec_spec(), _seq_spec(), _tok_spec(ts)]
    args = [x, dh, g, scale, dres]
    out_specs = [_tok_spec(ts), _vec_spec(), _seq_spec(), _seq_spec()]
    out_shape = [jax.ShapeDtypeStruct(x.shape, F32), jax.ShapeDtypeStruct((1, D_MODEL), F32), seq_shape, seq_shape]
    if res:
        in_specs += [_tok_spec(ts), _seq_spec()]
        args += [yprev, gate]
        out_specs += [_tok_spec(ts), _seq_spec()]
        out_shape += [jax.ShapeDtypeStruct(x.shape, BF16), seq_shape]
    return _call(body, name=name, grid=(bsz, s // ts), in_specs=in_specs, out_specs=out_specs,
                 out_shape=out_shape, compiler_params=_params())(*args)


def _lru_gates(xr, wa, ba, wi, bi, sp):
    xb = xr.astype(BF16)
    r = _sigmoid(jnp.dot(xb, wa, preferred_element_type=F32) + ba)
    i = _sigmoid(jnp.dot(xb, wi, preferred_element_type=F32) + bi)
    la = -LRU_C * r * sp
    a = jnp.exp(la)
    mult = jnp.sqrt(-jnp.tanh(la) * (a * a + 1.0))
    return r, i, a, mult


def _scan_forward(a, u, h_in, row):
    n = a.shape[0]
    r8 = row & 7
    for k in (1, 2, 4):
        m = r8 >= k
        u = jnp.where(m, a * pltpu.roll(u, k, 0) + u, u)
        a = jnp.where(m, a * pltpu.roll(a, k, 0), a)
    carry = jnp.broadcast_to(h_in, (8, a.shape[1]))
    out = []
    for j in range(n // 8):
        hj = u[8 * j:8 * j + 8] + a[8 * j:8 * j + 8] * carry
        out.append(hj)
        carry = jnp.broadcast_to(hj[7:8], hj.shape)
    return jnp.concatenate(out, axis=0)


def _scan_backward(a_up, g, g_in, row):
    n = g.shape[0]
    r8 = row & 7
    for k in (1, 2, 4):
        m = r8 < 8 - k
        g = jnp.where(m, a_up * pltpu.roll(g, n - k, 0) + g, g)
        a_up = jnp.where(m, a_up * pltpu.roll(a_up, n - k, 0), a_up)
    carry = jnp.broadcast_to(g_in, (8, g.shape[1]))
    out = [None] * (n // 8)
    for j in reversed(range(n // 8)):
        gj = g[8 * j:8 * j + 8] + a_up[8 * j:8 * j + 8] * carry
        out[j] = gj
        carry = jnp.broadcast_to(gj[0:1], gj.shape)
    return jnp.concatenate(out, axis=0)


def _rnn_fwd(z, wc, bc, wa, ba, wi, bi, lam, exchange=None):
    bsz, s, _ = z.shape
    ch = min(SCAN_CHUNK, s)
    nch = s // ch

    def body(z_ref, wc_ref, bc_ref, wa_ref, ba_ref, wi_ref, bi_ref, lam_ref, h_ref, y_ref):
        wcv, bcv = wc_ref[...], bc_ref[...]
        wav, wiv = wa_ref[0].astype(BF16), wi_ref[0].astype(BF16)
        bav, biv = ba_ref[...], bi_ref[...]
        sp = _softplus(-lam_ref[...])
        row = lax.broadcasted_iota(jnp.int32, (ch, RNN_BLK), 0)

        def chunk(ci, carry):
            prev8, hc = carry
            st = pl.multiple_of(ci * ch, ch)
            blk = z_ref[0, pl.ds(st, ch), :]
            xp, gr = blk[:, :RNN_BLK], blk[:, RNN_BLK:]
            xe = jnp.concatenate([prev8, xp], axis=0)
            xr = bcv + wcv[3:4] * xp
            for j in range(1, 4):
                xr = xr + wcv[3 - j:4 - j] * pltpu.roll(xe, j, 0)[8:]
            _, i, a, mult = _lru_gates(xr, wav, bav, wiv, biv, sp)
            hh = _scan_forward(a, mult * (i * xr), hc, row)
            h_ref[0, pl.ds(st, ch), :] = hh
            y_ref[0, pl.ds(st, ch), :] = (_gelu(gr) * hh).astype(BF16)
            return xp[ch - 8:], hh[ch - 1:]

        lax.fori_loop(0, nch, chunk, (jnp.zeros((8, RNN_BLK), F32), jnp.zeros((1, RNN_BLK), F32)))

    vec = pl.BlockSpec((1, RNN_BLK), lambda b, n: (0, n))
    mat = pl.BlockSpec((1, RNN_BLK, RNN_BLK), lambda b, n: (n, 0, 0))
    out = pl.BlockSpec((1, s, RNN_BLK), lambda b, n: (b, 0, n))
    return _run(
        body, [z, wc, bc, wa, ba, wi, bi, lam], exchange, name="rnn_fwd", grid=(bsz, N_BLK),
        in_specs=[pl.BlockSpec((1, s, 2 * RNN_BLK), lambda b, n: (b, 0, XG0R // (2 * RNN_BLK) + n)),
                  pl.BlockSpec((4, RNN_BLK), lambda b, n: (0, n)), vec, mat, vec, mat, vec, vec],
        out_specs=[out, out],
        out_shape=[jax.ShapeDtypeStruct((bsz, s, D_RNN), F32), jax.ShapeDtypeStruct((bsz, s, D_RNN), BF16)],
        compiler_params=_params())


def _rnn_bwd(z, h, dy, dz, wc, bc, wa, ba, wi, bi, lam, exchange=None):
    bsz, s, _ = z.shape
    ch = min(SCAN_CHUNK, s)
    nch = s // ch

    def body(z_ref, h_ref, dy_ref, dz_in, wc_ref, bc_ref, wa_ref, ba_ref, wi_ref, bi_ref, lam_ref,
             dz_ref, dwa_ref, dwi_ref, dsm_ref):
        del dz_in
        b = pl.program_id(1)
        wcv, bcv = wc_ref[...], bc_ref[...]
        wav, wiv = wa_ref[0].astype(BF16), wi_ref[0].astype(BF16)
        bav, biv = ba_ref[...], bi_ref[...]
        lamv = lam_ref[...]
        sp = _softplus(-lamv)
        row = lax.broadcasted_iota(jnp.int32, (ch, RNN_BLK), 0)

        @pl.when(b == 0)
        def _():
            dwa_ref[...] = jnp.zeros_like(dwa_ref)
            dwi_ref[...] = jnp.zeros_like(dwi_ref)
            dsm_ref[...] = jnp.zeros_like(dsm_ref)

        def chunk(k, carry):
            a_next, g_next, dxr_next8 = carry
            ci = nch - 1 - k
            st = pl.multiple_of(ci * ch, ch)
            pst = pl.multiple_of(jnp.maximum(st - 8, 0), 8)
            has_prev = (ci > 0).astype(F32)
            blk = z_ref[0, pl.ds(st, ch), :]
            xp, gr = blk[:, :RNN_BLK], blk[:, RNN_BLK:]
            xprev8 = z_ref[0, pl.ds(pst, 8), :][:, :RNN_BLK] * has_prev
            hprev8 = h_ref[0, pl.ds(pst, 8), :] * has_prev
            xe = jnp.concatenate([xprev8, xp], axis=0)
            xs = [xp] + [pltpu.roll(xe, j, 0)[8:] for j in range(1, 4)]
            xr = bcv
            for kk in range(4):
                xr = xr + wcv[kk:kk + 1] * xs[3 - kk]
            r, i, a, mult = _lru_gates(xr, wav, bav, wiv, biv, sp)
            hh = h_ref[0, pl.ds(st, ch), :]
            hm1 = pltpu.roll(jnp.concatenate([hprev8, hh], axis=0), 1, 0)[8:]
            dyv = dy_ref[0, pl.ds(st, ch), :]
            ge, dge = _gelu_and_grad(gr)
            dgr = dyv * hh * dge
            gg = _scan_backward(jnp.where(row == ch - 1, a_next, pltpu.roll(a, ch - 1, 0)), dyv * ge, g_next, row)
            da = gg * hm1
            ix = i * xr
            dmult = gg * ix
            di = gg * mult * xr
            dxr = gg * mult * i
            dla = da * a - dmult * (a * a) / mult
            dr = dla * (-LRU_C * sp)
            dsp = jnp.sum(dla * (-LRU_C * r), axis=0, keepdims=True)
            dpa = dr * r * (1.0 - r)
            dpi = di * i * (1.0 - i)
            dpab, dpib, xrb = dpa.astype(BF16), dpi.astype(BF16), xr.astype(BF16)
            nt = (((1,), (1,)), ((), ()))
            tn = (((0,), (0,)), ((), ()))
            dxr = dxr + lax.dot_general(dpab, wav, nt, preferred_element_type=F32)
            dxr = dxr + lax.dot_general(dpib, wiv, nt, preferred_element_type=F32)
            dwa_ref[0] = dwa_ref[0] + lax.dot_general(xrb, dpab, tn, preferred_element_type=F32)
            dwi_ref[0] = dwi_ref[0] + lax.dot_general(xrb, dpib, tn, preferred_element_type=F32)
            rows = [jnp.sum(dxr * xs[3 - q], axis=0, keepdims=True) for q in range(4)]
            rows += [jnp.sum(dxr, axis=0, keepdims=True), jnp.sum(dpa, axis=0, keepdims=True),
                     jnp.sum(dpi, axis=0, keepdims=True), dsp * (-_sigmoid(-lamv))]
            dsm_ref[...] = dsm_ref[...] + jnp.concatenate(rows, axis=0)
            dxe = jnp.concatenate([dxr, dxr_next8], axis=0)
            dxp = wcv[3:4] * dxr
            for j in range(1, 4):
                dxp = dxp + wcv[3 - j:4 - j] * pltpu.roll(dxe, ch + 8 - j, 0)[:ch]
            dz_ref[0, pl.ds(st, ch), :] = jnp.concatenate([dxp, dgr], axis=1).astype(BF16)
            return a[0:1], gg[0:1], dxr[0:8]

        lax.fori_loop(0, nch, chunk, (jnp.zeros((1, RNN_BLK), F32), jnp.zeros((1, RNN_BLK), F32),
                                      jnp.zeros((8, RNN_BLK), F32)))

    vec = pl.BlockSpec((1, RNN_BLK), lambda n, b: (0, n))
    mat = pl.BlockSpec((1, RNN_BLK, RNN_BLK), lambda n, b: (n, 0, 0))
    tok = pl.BlockSpec((1, s, RNN_BLK), lambda n, b: (b, 0, n))
    zblk = pl.BlockSpec((1, s, 2 * RNN_BLK), lambda n, b: (b, 0, XG0R // (2 * RNN_BLK) + n))
    dzblk = pl.BlockSpec((1, s, 2 * RNN_BLK), lambda n, b: (b, 0, XG0 // (2 * RNN_BLK) + n))
    return _run(
        body, [z, h, dy, dz, wc, bc, wa, ba, wi, bi, lam], exchange, name="rnn_bwd", grid=(N_BLK, bsz),
        in_specs=[zblk, tok, tok, ANY, pl.BlockSpec((4, RNN_BLK), lambda n, b: (0, n)), vec, mat, vec, mat, vec, vec],
        out_specs=[dzblk, mat, mat, pl.BlockSpec((8, RNN_BLK), lambda n, b: (0, n))],
        out_shape=[jax.ShapeDtypeStruct(dz.shape, dz.dtype), jax.ShapeDtypeStruct((N_BLK, RNN_BLK, RNN_BLK), F32),
                   jax.ShapeDtypeStruct((N_BLK, RNN_BLK, RNN_BLK), F32), jax.ShapeDtypeStruct((8, D_RNN), F32)],
        input_output_aliases={3: 0},
        compiler_params=_params())


def _tri(n, upper):
    r = lax.broadcasted_iota(jnp.int32, (n, n), 0)
    c = lax.broadcasted_iota(jnp.int32, (n, n), 1)
    return (r <= c if upper else r >= c).astype(BF16)


def _exact_dot(x, t):
    hi, mid, lo = _split3(x)
    return (jnp.dot(hi, t, preferred_element_type=F32) + jnp.dot(mid, t, preferred_element_type=F32)
            + jnp.dot(lo, t, preferred_element_type=F32))


def _fgate_fwd(flt, bfg):
    bsz, nh, s = flt.shape
    blk = min(CUM_BLK, s)

    def body(fl_ref, b_ref, f_ref):
        tri = _tri(blk, True)
        carry = jnp.zeros((nh, 1), F32)
        for q in range(s // blk):
            xq = fl_ref[0, :, q * blk:(q + 1) * blk] + b_ref[...]
            fq = _exact_dot(-_softplus(-xq), tri) + carry
            f_ref[0, :, q * blk:(q + 1) * blk] = fq
            carry = fq[:, blk - 1:blk]

    spec = pl.BlockSpec((1, nh, s), lambda b: (b, 0, 0))
    return _call(body, name="fgate_fwd", grid=(bsz,), in_specs=[spec, pl.BlockSpec((nh, 1), lambda b: (0, 0))],
                 out_specs=spec, out_shape=jax.ShapeDtypeStruct(flt.shape, F32), compiler_params=_params())(flt, bfg)


def _fgate_bwd(dft, flt, bfg):
    bsz, nh, s = flt.shape
    blk = min(CUM_BLK, s)

    def body(df_ref, fl_ref, b_ref, dfl_ref, db_ref):
        b = pl.program_id(0)
        tri = _tri(blk, False)
        carry = jnp.zeros((nh, 1), F32)
        tot = jnp.zeros((nh, 1), F32)
        for q in reversed(range(s // blk)):
            sl = slice(q * blk, (q + 1) * blk)
            dlq = _exact_dot(df_ref[0, :, sl], tri) + carry
            carry = dlq[:, 0:1]
            dflq = dlq * _sigmoid(-(fl_ref[0, :, sl] + b_ref[...]))
            dfl_ref[0, :, sl] = dflq
            tot = tot + jnp.sum(dflq, axis=1, keepdims=True)
        _acc(db_ref, slice(None), tot, b == 0)

    spec = pl.BlockSpec((1, nh, s), lambda b: (b, 0, 0))
    small = pl.BlockSpec((nh, 1), lambda b: (0, 0))
    return _call(body, name="fgate_bwd", grid=(bsz,), in_specs=[spec, spec, small], out_specs=[spec, small],
                 out_shape=[jax.ShapeDtypeStruct(flt.shape, F32), jax.ShapeDtypeStruct((nh, 1), F32)],
                 compiler_params=_params())(dft, flt, bfg)


def _attn_scores(q_b, k_ref, fcol_ref, frow_ref, q0, tq, ln):
    kk = k_ref[0, :ln, HEAD_DIM:2 * HEAD_DIM]
    sc = lax.dot_general(q_b, kk, (((1,), (1,)), ((), ())), preferred_element_type=F32)
    sc = sc + fcol_ref[0, 0, q0:q0 + tq, :] - frow_ref[0, 0, :, :ln]
    qi = q0 + lax.broadcasted_iota(jnp.int32, (tq, ln), 0)
    ki = lax.broadcasted_iota(jnp.int32, (tq, ln), 1)
    return sc, qi >= ki


def _attn_fwd(z, fcol, frow, exchange=None):
    bsz, s, _ = z.shape
    tq = min(ATTN_TQ, s)
    scale = HEAD_DIM ** -0.5

    def body(z_ref, fcol_ref, frow_ref, o_ref, lse_ref):
        for qb in range(s // tq):
            q0, ln = qb * tq, (qb + 1) * tq
            q_b = (z_ref[0, q0:ln, :HEAD_DIM].astype(F32) * scale).astype(BF16)
            sc, mask = _attn_scores(q_b, z_ref, fcol_ref, frow_ref, q0, tq, ln)
            sc = jnp.where(mask, sc, -jnp.inf)
            mx = jnp.max(sc, axis=-1, keepdims=True)
            p = jnp.exp(sc - mx)
            l = jnp.sum(p, axis=-1, keepdims=True)
            vv = z_ref[0, :ln, 2 * HEAD_DIM:]
            o = jnp.dot(p.astype(BF16), vv, preferred_element_type=F32) / l
            o_ref[0, q0:ln, :] = o.astype(BF16)
            lse_ref[0, 0, q0:ln, :] = mx + jnp.log(l)

    return _run(
        body, [z, fcol, frow], exchange, name="attn_fwd", grid=(bsz, N_HEADS),
        in_specs=[pl.BlockSpec((1, s, 3 * HEAD_DIM), lambda b, h: (b, 0, h)),
                  pl.BlockSpec((1, 1, s, 1), lambda b, h: (b, h, 0, 0)),
                  pl.BlockSpec((1, 1, 1, s), lambda b, h: (b, h, 0, 0))],
        out_specs=[pl.BlockSpec((1, s, HEAD_DIM), lambda b, h: (b, 0, h)),
                   pl.BlockSpec((1, 1, s, 1), lambda b, h: (b, h, 0, 0))],
        out_shape=[jax.ShapeDtypeStruct((bsz, s, N_HEADS * HEAD_DIM), BF16),
                   jax.ShapeDtypeStruct((bsz, N_HEADS, s, 1), F32)],
        compiler_params=_params())


def _attn_bwd(z, do, lse, fcol, frow, dz, exchange=None):
    bsz, s, _ = z.shape
    tq = min(ATTN_TQ, s)
    scale = HEAD_DIM ** -0.5
    nt = (((1,), (1,)), ((), ()))
    tn = (((0,), (0,)), ((), ()))

    def body(z_ref, do_ref, lse_ref, fcol_ref, frow_ref, dz_in, dz_ref, df_ref, dk_acc, dv_acc):
        del dz_in
        dk_acc[...] = jnp.zeros_like(dk_acc)
        dv_acc[...] = jnp.zeros_like(dv_acc)
        df_ref[...] = jnp.zeros_like(df_ref)
        for qb in range(s // tq):
            q0, ln = qb * tq, (qb + 1) * tq
            q_b = (z_ref[0, q0:ln, :HEAD_DIM].astype(F32) * scale).astype(BF16)
            sc, mask = _attn_scores(q_b, z_ref, fcol_ref, frow_ref, q0, tq, ln)
            p = jnp.where(mask, jnp.exp(sc - lse_ref[0, 0, q0:ln, :]), 0.0)
            do_b = do_ref[0, q0:ln, :]
            pb = p.astype(BF16)
            dv_acc[:ln, :] = dv_acc[:ln, :] + lax.dot_general(pb, do_b, tn, preferred_element_type=F32)
            vv = z_ref[0, :ln, 2 * HEAD_DIM:]
            dp = lax.dot_general(do_b, vv, nt, preferred_element_type=F32)
            ds = p * (dp - jnp.sum(p * dp, axis=-1, keepdims=True))
            dsb = ds.astype(BF16)
            kk = z_ref[0, :ln, HEAD_DIM:2 * HEAD_DIM]
            dq =jnp.dot(dsb, kk, preferred_element_type=F32) * scale
            dz_ref[0, q0:ln, :HEAD_DIM] = dq.astype(BF16)
            dk_acc[:ln, :] = dk_acc[:ln, :] + lax.dot_general(dsb, q_b, tn, preferred_element_type=F32)
            df_ref[0, 0, :, :ln] = df_ref[0, 0, :, :ln] - jnp.sum(ds, axis=0, keepdims=True)
        dz_ref[0, :, HEAD_DIM:2 * HEAD_DIM] = dk_acc[...].astype(BF16)
        dz_ref[0, :, 2 * HEAD_DIM:] = dv_acc[...].astype(BF16)

    zblk = pl.BlockSpec((1, s, 3 * HEAD_DIM), lambda b, h: (b, 0, h))
    hblk = pl.BlockSpec((1, s, HEAD_DIM), lambda b, h: (b, 0, h))
    col = pl.BlockSpec((1, 1, s, 1), lambda b, h: (b, h, 0, 0))
    rowspec = pl.BlockSpec((1, 1, 1, s), lambda b, h: (b, h, 0, 0))
    return _run(
        body, [z, do, lse, fcol, frow, dz], exchange, name="attn_bwd", grid=(bsz, N_HEADS),
        in_specs=[zblk, hblk, col, col, rowspec, ANY],
        out_specs=[zblk, rowspec],
        out_shape=[jax.ShapeDtypeStruct(dz.shape, dz.dtype), jax.ShapeDtypeStruct((bsz, N_HEADS, 1, s), F32)],
        scratch_shapes=[pltpu.VMEM((s, HEAD_DIM), F32), pltpu.VMEM((s, HEAD_DIM), F32)],
        input_output_aliases={5: 0},
        compiler_params=_params())


def _merge_fwd(z2, pr, pa, *, tr=512):
    t = z2.shape[0]

    def body(mgr_ref, mga_ref, pr_ref, pa_ref, o_ref):
        gr_ = _sigmoid(mgr_ref[...])
        ga = _sigmoid(mga_ref[...])
        o_ref[...] = (gr_ * pr_ref[...] + ga * pa_ref[...]).astype(BF16)

    tok = pl.BlockSpec((tr, D_MODEL), lambda i: (i, 0))
    return _call(body, name="merge_fwd", grid=(t // tr,),
                 in_specs=[pl.BlockSpec((tr, D_MODEL), lambda i: (i, MG0R // D_MODEL)),
                           pl.BlockSpec((tr, D_MODEL), lambda i: (i, MG0R // D_MODEL + 1)), tok, tok],
                 out_specs=tok, out_shape=jax.ShapeDtypeStruct((t, D_MODEL), BF16),
                 compiler_params=_params())(z2, z2, pr, pa)


def _merge_bwd(dm, z2, pr, pa, *, tr=512):
    t = z2.shape[0]

    def body(dm_ref, mgr_ref, mga_ref, pr_ref, pa_ref, dz_ref, dpr_ref, dpa_ref):
        dmv = dm_ref[...]
        gr_ = _sigmoid(mgr_ref[...])
        ga = _sigmoid(mga_ref[...])
        dpr_ref[...] = (dmv * gr_).astype(BF16)
        dpa_ref[...] = (dmv * ga).astype(BF16)
        dz_ref[:, :D_MODEL] = (dmv * pr_ref[...] * gr_ * (1.0 - gr_)).astype(BF16)
        dz_ref[:, D_MODEL:] = (dmv * pa_ref[...] * ga * (1.0 - ga)).astype(BF16)

    tok = pl.BlockSpec((tr, D_MODEL), lambda i: (i, 0))
    mg = pl.BlockSpec((tr, 2 * D_MODEL), lambda i: (i, MG0 // (2 * D_MODEL)))
    return _call(body, name="merge_bwd", grid=(t // tr,),
                 in_specs=[tok, pl.BlockSpec((tr, D_MODEL), lambda i: (i, MG0R // D_MODEL)),
                           pl.BlockSpec((tr, D_MODEL), lambda i: (i, MG0R // D_MODEL + 1)), tok, tok],
                 out_specs=[mg, tok, tok],
                 out_shape=[jax.ShapeDtypeStruct((t, NZ), BF16), jax.ShapeDtypeStruct((t, D_MODEL), BF16),
                            jax.ShapeDtypeStruct((t, D_MODEL), BF16)],
                 compiler_params=_params())(dm, z2, z2, pr, pa)


def _fl_write(dfl_blk, dz):
    bsz, s, w = dfl_blk.shape

    def body(src_ref, dz_in, dz_ref):
        del dz_in
        dz_ref[...] = src_ref[...]

    return _call(body, name="fl_write", grid=(bsz,),
                 in_specs=[pl.BlockSpec((1, s, w), lambda b: (b, 0, 0)), ANY],
                 out_specs=pl.BlockSpec((1, s, w), lambda b: (b, 0, FL0 // w)),
                 out_shape=jax.ShapeDtypeStruct(dz.shape, dz.dtype), input_output_aliases={1: 0},
                 compiler_params=_params())(dfl_blk, dz)


def _ffn_fwd(zf, wc, bc):
    bsz, s, _ = zf.shape
    cw = 128

    def body(zf_ref, wc_ref, bc_ref, o_ref):
        gfp, uf = zf_ref[0, :, :cw].astype(F32), zf_ref[0, :, cw:].astype(F32)
        wcv = wc_ref[...]
        row = lax.broadcasted_iota(jnp.int32, (s, cw), 0)
        gf = bc_ref[...] + wcv[2:3] * gfp
        for j in (1, 2):
            gf = gf + wcv[2 - j:3 - j] * jnp.where(row >= j, pltpu.roll(gfp, j, 0), 0.0)
        o_ref[0] = (_gelu(gf) * uf).astype(BF16)

    return _call(body, name="ffn_fwd", grid=(bsz, D_FF // cw),
                 in_specs=[pl.BlockSpec((1, s, 2 * cw), lambda b, j: (b, 0, j)),
                           pl.BlockSpec((3, cw), lambda b, j: (0, j)), pl.BlockSpec((1, cw), lambda b, j: (0, j))],
                 out_specs=pl.BlockSpec((1, s, cw), lambda b, j: (b, 0, j)),
                 out_shape=jax.ShapeDtypeStruct((bsz, s, D_FF), BF16), compiler_params=_params())(zf, wc, bc)


def _ffn_bwd(dact, zf, wc, bc, exchange=None):
    bsz, s, _ = zf.shape
    cw = 128

    def body(da_ref, zf_ref, wc_ref, bc_ref, dzf_ref, dsm_ref):
        b = pl.program_id(1)
        gfp, uf = zf_ref[0, :, :cw].astype(F32), zf_ref[0, :, cw:].astype(F32)
        wcv = wc_ref[...]
        row = lax.broadcasted_iota(jnp.int32, (s, cw), 0)
        sh = [gfp] + [jnp.where(row >= j, pltpu.roll(gfp, j, 0), 0.0) for j in (1, 2)]
        gf = bc_ref[...]
        for kk in range(3):
            gf = gf + wcv[kk:kk + 1] * sh[2 - kk]
        ge, dge = _gelu_and_grad(gf)
        dav = da_ref[0].astype(F32)
        dgf = dav * uf * dge
        dgfp = wcv[2:3] * dgf
        for j in (1, 2):
            dgfp = dgfp + wcv[2 - j:3 - j] * jnp.where(row < s - j, pltpu.roll(dgf, s - j, 0), 0.0)
        dzf_ref[0] = jnp.concatenate([dgfp, dav * ge], axis=1).astype(BF16)
        rows = [jnp.sum(dgf * sh[2 - kk], axis=0, keepdims=True) for kk in range(3)]
        rows += [jnp.sum(dgf, axis=0, keepdims=True), jnp.zeros((4, cw), F32)]
        _acc(dsm_ref, slice(None), jnp.concatenate(rows, axis=0), b == 0)

    return _run(body, [dact, zf, wc, bc], exchange, name="ffn_bwd", grid=(D_FF // cw, bsz),
                in_specs=[pl.BlockSpec((1, s, cw), lambda j, b: (b, 0, j)),
                          pl.BlockSpec((1, s, 2 * cw), lambda j, b: (b, 0, j)),
                          pl.BlockSpec((3, cw), lambda j, b: (0, j)), pl.BlockSpec((1, cw), lambda j, b: (0, j))],
                out_specs=[pl.BlockSpec((1, s, 2 * cw), lambda j, b: (b, 0, j)),
                           pl.BlockSpec((8, cw), lambda j, b: (0, j))],
                out_shape=[jax.ShapeDtypeStruct(zf.shape, BF16), jax.ShapeDtypeStruct((8, D_FF), F32)],
                compiler_params=_params())


def _adamw(w, m, v, parts, *, name):
    r, c = w.shape
    p = parts.shape[0]
    tr = r if r <= 512 else 256
    assert r % tr == 0 and parts.shape[1:] == (r, c), (name, w.shape, parts.shape)
    c1 = 1.0 / (1.0 - ADAM_B1 ** ADAM_STEP)
    c2 = 1.0 / (1.0 - ADAM_B2 ** ADAM_STEP)

    def body(w_ref, m_ref, v_ref, p_ref, g_ref, d_ref, mo_ref, vo_ref):
        g = p_ref[0].astype(F32)
        for q in range(1, p):
            g = g + p_ref[q].astype(F32)
        mn = ADAM_B1 * m_ref[...] + (1.0 - ADAM_B1) * g
        vn = ADAM_B2 * v_ref[...] + (1.0 - ADAM_B2) * (g * g)
        g_ref[...] = g
        mo_ref[...] = mn
        vo_ref[...] = vn
        d_ref[...] = -ADAM_LR * ((mn * c1) / (jnp.sqrt(vn * c2) + ADAM_EPS) + ADAM_WD * w_ref[...])

    spec = pl.BlockSpec((tr, c), lambda i: (i, 0))
    shp = jax.ShapeDtypeStruct((r, c), F32)
    return _call(body, name=name, grid=(r // tr,),
                 in_specs=[spec, spec, spec, pl.BlockSpec((p, tr, c), lambda i: (0, i, 0))],
                 out_specs=[spec] * 4, out_shape=[shp] * 4, compiler_params=_params())(w, m, v, parts)


def _interleave(w, groups, width):
    n = w.shape[1] // (groups * width)
    return jnp.concatenate([w[:, (g * n + j) * width:(g * n + j + 1) * width]
                            for j in range(n) for g in range(groups)], axis=1)


def _deinterleave(w, groups, width):
    n = w.shape[1] // (groups * width)
    return jnp.concatenate([w[:, (j * groups + g) * width:(j * groups + g + 1) * width]
                            for g in range(groups) for j in range(n)], axis=1)


def _to_z_layout(w):
    xg = _interleave(w[:, :2 * D_RNN], 2, RNN_BLK)
    qkv = _interleave(w[:, 2 * D_RNN:2 * D_RNN + 3 * D_MODEL], 3, HEAD_DIM)
    fl0 = 2 * D_RNN + 3 * D_MODEL
    fl = w[:, fl0:fl0 + N_HEADS]
    mg = w[:, fl0 + N_HEADS:]
    pad = jnp.zeros((w.shape[0], MG0 - FL0 - N_HEADS), w.dtype)
    return jnp.concatenate([qkv, xg, fl, pad, mg], axis=1)


def _from_z_layout(w):
    qkv = _deinterleave(w[:, QKV0:XG0], 3, HEAD_DIM)
    xg = _deinterleave(w[:, XG0:FL0], 2, RNN_BLK)
    return jnp.concatenate([xg, qkv, w[:, FL0:FL0 + N_HEADS], w[:, MG0:]], axis=1)


def _col_gather(g):
    return g.transpose(1, 0, 2).reshape(g.shape[1], -1)


def _col_scatter(w):
    return w.reshape(w.shape[0], N_DEV, -1).transpose(1, 0, 2)


def kernel(x, c, w_ada, b_ada, g_norm1, w_in, w_rnn_conv, b_rnn_conv, w_lru_a, b_lru_a, w_lru_i, b_lru_i, lru_lambda, b_fgate, w_proj_rnn, w_proj_attn, w_out, g_norm2, w_ffn_up, w_ffn_conv, b_ffn_conv, w_ffn_down, w_ada_final, b_ada_final, g_final, loss_target, m_w_ada, m_b_ada, m_g_norm1, m_w_in, m_w_rnn_conv, m_b_rnn_conv, m_w_lru_a, m_b_lru_a, m_w_lru_i, m_b_lru_i, m_lru_lambda, m_b_fgate, m_w_proj_rnn, m_w_proj_attn, m_w_out, m_g_norm2, m_w_ffn_up, m_w_ffn_conv, m_b_ffn_conv, m_w_ffn_down, m_w_ada_final, m_b_ada_final, m_g_final, v_w_ada, v_b_ada, v_g_norm1, v_w_in, v_w_rnn_conv, v_b_rnn_conv, v_w_lru_a, v_b_lru_a, v_w_lru_i, v_b_lru_i, v_lru_lambda, v_b_fgate, v_w_proj_rnn, v_w_proj_attn, v_w_out, v_g_norm2, v_w_ffn_up, v_w_ffn_conv, v_b_ffn_conv, v_w_ffn_down, v_w_ada_final, v_b_ada_final, v_g_final):
    args = dict(locals())
    bsz, s, _ = x.shape
    t = bsz * s
    nb = N_DEV * bsz
    me = 4 * lax.axis_index("x") + 2 * lax.axis_index("y") + lax.axis_index("c")
    tm = min(2048, t)

    ex_a = _gather_two_level([c, w_in[0].astype(BF16), w_rnn_conv[0], w_ffn_conv[0]], "gather_first")
    c_all = ex_a[0].reshape(nb, D_MODEL)
    win_z = _to_z_layout(_col_gather(ex_a[1]))
    wrc = _col_gather(ex_a[2])
    wfc = _col_gather(ex_a[3])

    nmod = w_ada.shape[2]
    nmodf = w_ada_final.shape[1]
    mod_cols = _mm(c_all, w_ada[0], name="mod_cols", tm=nb, tn=nmod, tk=D_MODEL, silu_a=True,
                   bias=lax.dynamic_slice(b_ada, (0, me * nmod), (1, nmod)))
    modf_cols = _mm(c_all, w_ada_final, name="modf_cols", tm=nb, tn=nmodf, tk=D_MODEL, silu_a=True,
                    bias=lax.dynamic_slice(b_ada_final.reshape(1, -1), (0, me * nmodf), (1, nmodf)))
    ex_b = _exchange([mod_cols, modf_cols], [True, True], "gather_mod")
    mod = lax.dynamic_slice(ex_b[0], (0, me * bsz, 0), (N_DEV, bsz, nmod)).transpose(1, 0, 2).reshape(bsz, 6, 1, D_MODEL)
    modf = lax.dynamic_slice(ex_b[1], (0, me * bsz, 0), (N_DEV, bsz, nmodf)).transpose(1, 0, 2).reshape(bsz, 2, 1, D_MODEL)
    shift1, scale1, gate1, shift2, scale2, gate2 = [mod[:, i] for i in range(6)]
    shift_f, scale_f = modf[:, 0], modf[:, 1]

    h1 = _norm_fwd(x, g_norm1, shift1, scale1, name="norm1_fwd")[0]
    h1f = h1.reshape(t, D_MODEL)
    zq = _mm(h1f, win_z, name="mm_in_qkv", out_dtype=BF16, tm=tm, tn=1024, tk=D_MODEL,
             b_cols=(QKV0, XG0)).reshape(bsz, s, XG0)
    zr = _mm(h1f, win_z, name="mm_in_rest", tm=tm, tn=1024, tk=D_MODEL, b_cols=(XG0, NZR)).reshape(bsz, s, NZR)
    h_rnn, y_rnn, got = _rnn_fwd(zr, wrc, b_rnn_conv, w_lru_a[0], b_lru_a, w_lru_i[0], b_lru_i, lru_lambda,
                                 exchange=([w_ffn_up[0].astype(BF16)], ["gather_zf"]))
    wup_z = got[0]
    flt = zr[:, :, FL0R:FL0R + N_HEADS].transpose(0, 2, 1)
    bfg = b_fgate.reshape(N_HEADS, 1)
    fcum = _fgate_fwd(flt, bfg)
    fcol = fcum.reshape(bsz, N_HEADS, s, 1)
    frow = fcum.reshape(bsz, N_HEADS, 1, s)
    o_attn, lse, got = _attn_fwd(
        zq, fcol, frow, exchange=([w_proj_rnn[0].astype(BF16), w_proj_attn[0].astype(BF16), w_out[0].astype(BF16),
                                  w_ffn_down[0].astype(BF16)], [True] * 4))
    wpr = got[0].reshape(D_RNN, D_MODEL)
    wpa = got[1].reshape(D_MODEL, D_MODEL)
    wout = got[2].reshape(D_MODEL, D_MODEL)
    wdown = got[3].reshape(D_FF, D_MODEL)
    pr = _mm(y_rnn.reshape(t, D_RNN), wpr, name="mm_proj_rnn", tm=tm, tn=D_MODEL, tk=D_RNN)
    pa = _mm(o_attn.reshape(t, D_MODEL), wpa, name="mm_proj_attn", tm=tm, tn=D_MODEL, tk=D_MODEL)
    z2 = zr.reshape(t, NZR)
    merged = _merge_fwd(z2, pr, pa)
    mo = _mm(merged, wout, name="mm_out", tm=tm, tn=D_MODEL, tk=D_MODEL).reshape(bsz, s, D_MODEL)
    h2, x1 = _norm_fwd(x, g_norm2, shift2, scale2, name="norm2_fwd", yprev=mo, gate=gate1)
    zf = _mm(h2.reshape(t, D_MODEL), wup_z, name="mm_up", out_dtype=BF16, tm=tm, tn=1024,
             tk=D_MODEL).reshape(bsz, s, 2 * D_FF)
    act = _ffn_fwd(zf, wfc, b_ffn_conv)
    yf = _mm(act.reshape(t, D_FF), wdown, name="mm_down", tm=tm, tn=D_MODEL, tk=1024).reshape(bsz, s, D_MODEL)
    lp, dx2, dyf, dshift_f, dscale_f, dgate2, dg_final = _final(x1, yf, gate2, g_final.reshape(1, -1), shift_f, scale_f, loss_target)
    loss = lax.psum(jnp.sum(lp[:, :, 0, 0]) * (0.5 / D_MODEL), ("x", "y", "c"))

    dyf2 = dyf.reshape(t, D_MODEL)
    act2 = act.reshape(t, D_FF)
    dact = _mm(dyf2, wdown, name="mm_down_dx", tb=True, out_dtype=BF16, tm=tm, tn=1024, tk=D_MODEL).reshape(bsz, s, D_FF)
    g_wdown = _mm(act2, dyf2, name="mm_down_dw", ta=True, out_dtype=BF16, tm=1024, tn=D_MODEL, tk=tm)
    dzf, dsm_ffn, got = _ffn_bwd(dact, zf, wfc, b_ffn_conv, exchange=([g_wdown.reshape(N_DEV, -1, D_MODEL)], [False]))
    p_wdown = got[0]
    dzf2 = dzf.reshape(t, 2 * D_FF)
    dh2 = _mm(dzf2, wup_z, name="mm_up_dx", tb=True, tm=tm, tn=D_MODEL, tk=1024).reshape(bsz, s, D_MODEL)
    g_wup_z = _mm(h2.reshape(t, D_MODEL), dzf2, name="mm_up_dw", ta=True, out_dtype=BF16, tm=D_MODEL, tn=1024, tk=tm)
    dx1, dg_norm2, dshift2, dscale2, dmo, dgate1 = _norm_bwd(x1, dh2, g_norm2, scale2, dx2, name="norm2_bwd", yprev=mo, gate=gate1)
    dmo2 = dmo.reshape(t, D_MODEL)
    dmerged = _mm(dmo2, wout, name="mm_out_dx", tb=True, tm=tm, tn=D_MODEL, tk=D_MODEL)
    g_wout = _mm(merged, dmo2, name="mm_out_dw", ta=True, out_dtype=BF16, tm=D_MODEL, tn=D_MODEL, tk=tm)
    dz2, dpr, dpa = _merge_bwd(dmerged, z2, pr, pa)
    dy_rnn = _mm(dpr, wpr, name="mm_proj_rnn_dx", tb=True, tm=tm, tn=D_RNN, tk=D_MODEL).reshape(bsz, s, D_RNN)
    g_wpr = _mm(y_rnn.reshape(t, D_RNN), dpr, name="mm_proj_rnn_dw", ta=True, out_dtype=BF16, tm=D_RNN, tn=D_MODEL, tk=tm)
    do = _mm(dpa, wpa, name="mm_proj_attn_dx", tb=True, out_dtype=BF16, tm=tm, tn=D_MODEL, tk=D_MODEL).reshape(bsz, s, D_MODEL)
    g_wpa = _mm(o_attn.reshape(t, D_MODEL), dpa, name="mm_proj_attn_dw", ta=True, out_dtype=BF16, tm=D_MODEL, tn=D_MODEL, tk=tm)
    dz, dfrow, got = _attn_bwd(zq, do, lse, fcol, frow, dz2.reshape(bsz, s, NZ),
                               exchange=([g_wup_z, g_wout.reshape(N_DEV, -1, D_MODEL)], ["scatter_zf", False]))
    p_wup, p_wout = got
    dflt, db_fgate = _fgate_bwd(dfrow.reshape(bsz, N_HEADS, s), flt, bfg)
    dfl_blk = jnp.pad(dflt.transpose(0, 2, 1).astype(BF16), ((0, 0), (0, 0), (0, MG0 - FL0 - N_HEADS)))
    dz = _fl_write(dfl_blk, dz)
    dz, g_wa, g_wi, dsm_rnn, got = _rnn_bwd(
        zr, h_rnn, dy_rnn, dz, wrc, b_rnn_conv, w_lru_a[0], b_lru_a, w_lru_i[0], b_lru_i, lru_lambda,
        exchange=([g_wpr.reshape(N_DEV, -1, D_MODEL), g_wpa.reshape(N_DEV, -1, D_MODEL)], [False, False]))
    p_wpr, p_wpa = got
    dz2 = dz.reshape(t, NZ)
    small = [dg_norm2, dg_final, dsm_rnn, g_wa, g_wi, db_fgate, dsm_ffn]
    small_sizes = [a.size for a in small]
    n_pad = -sum(small_sizes) % 1024
    packed = jnp.concatenate([a.reshape(-1) for a in small] + [jnp.zeros((n_pad,), F32)]).reshape(-1, 128)
    g_win_z, got = _mm(h1f, dz2, name="mm_in_dw", ta=True, out_dtype=BF16, tm=D_MODEL, tn=1024, tk=tm,
                       exchange=([packed], [True]))
    flat = got[0].reshape(N_DEV, -1)
    dh1, got = _mm(dz2, win_z, name="mm_in_dx", tb=True, tm=tm, tn=D_MODEL, tk=1024,
                   exchange=([_col_scatter(_from_z_layout(g_win_z))], [False]))
    p_win = got[0]
    grad_x, dg_norm1, dshift1, dscale1 = _norm_bwd(x, dh1.reshape(bsz, s, D_MODEL), g_norm1, scale1, dx1, name="norm1_bwd")

    dmod = jnp.concatenate([dshift1, dscale1, dgate1, dshift2, dscale2, dgate2], axis=1).reshape(bsz, 6 * D_MODEL)
    dmodf = jnp.concatenate([dshift_f, dscale_f], axis=1).reshape(bsz, 2 * D_MODEL)
    ex_c = _exchange([dmod, dmodf, dg_norm1], [True, True, True], "gather_last_grads")
    dmod_all = ex_c[0].reshape(nb, 6 * D_MODEL)
    dmodf_all = ex_c[1].reshape(nb, 2 * D_MODEL)
    p_g1 = ex_c[2]

    res = {}

    def upd(name, parts, shape2):
        w, m, v = args[name], args["m_" + name], args["v_" + name]
        outs = _adamw(w.reshape(shape2), m.reshape(shape2), v.reshape(shape2), parts, name="adamw_" + name)
        res[name] = [o.reshape(w.shape) for o in outs]

    g_wada = _mm(c_all, lax.dynamic_slice(dmod_all, (0, me * nmod), (nb, nmod)), name="mm_ada_dw", ta=True,
                 silu_a=True, tm=D_MODEL, tn=nmod, tk=nb)
    g_wadaf = _mm(c_all, lax.dynamic_slice(dmodf_all, (0, me * nmodf), (nb, nmodf)), name="mm_adaf_dw", ta=True,
                  silu_a=True, tm=D_MODEL, tn=nmodf, tk=nb)
    upd("w_ada", g_wada[None], (D_MODEL, nmod))
    upd("w_ada_final", g_wadaf[None], (D_MODEL, nmodf))
    upd("b_ada", dmod_all.reshape(nb, 1, -1), (1, 6 * D_MODEL))
    upd("b_ada_final", dmodf_all.reshape(nb, 1, -1), (1, 2 * D_MODEL))
    upd("w_in", p_win, w_in.shape[1:])
    upd("w_proj_rnn", p_wpr, w_proj_rnn.shape[1:])
    upd("w_proj_attn", p_wpa, w_proj_attn.shape[1:])
    upd("w_out", p_wout, w_out.shape[1:])
    upd("w_ffn_up", p_wup, w_ffn_up.shape[1:])
    upd("w_ffn_down", p_wdown, w_ffn_down.shape[1:])

    parts, off = [], 0
    for a, n in zip(small, small_sizes):
        parts.append(flat[:, off:off + n].reshape((N_DEV,) + a.shape))
        off += n
    p_g2, p_gf, p_rnn, p_wa, p_wi, p_bfg, p_ffn = parts
    upd("g_norm1", p_g1, (1, D_MODEL))
    upd("g_norm2", p_g2, (1, D_MODEL))
    upd("g_final", p_gf, (1, D_MODEL))
    nrc = w_rnn_conv.shape[2]
    upd("w_rnn_conv", lax.dynamic_slice(p_rnn, (0, 0, me * nrc), (N_DEV, 4, nrc)), (4, nrc))
    upd("b_rnn_conv", p_rnn[:, 4:5], (1, D_RNN))
    upd("b_lru_a", p_rnn[:, 5:6], (1, D_RNN))
    upd("b_lru_i", p_rnn[:, 6:7], (1, D_RNN))
    upd("lru_lambda", p_rnn[:, 7:8], (1, D_RNN))
    upd("w_lru_a", p_wa.reshape(N_DEV, D_RNN, RNN_BLK), (D_RNN, RNN_BLK))
    upd("w_lru_i", p_wi.reshape(N_DEV, D_RNN, RNN_BLK), (D_RNN, RNN_BLK))
    upd("b_fgate", p_bfg.reshape(N_DEV, 1, N_HEADS), (1, N_HEADS))
    nfc = w_ffn_conv.shape[2]
    upd("w_ffn_conv", lax.dynamic_slice(p_ffn, (0, 0, me * nfc), (N_DEV, 3, nfc)), (3, nfc))
    upd("b_ffn_conv", p_ffn[:, 3:4], (1, D_FF))

    names = ['w_ada', 'b_ada', 'g_norm1', 'w_in', 'w_rnn_conv', 'b_rnn_conv', 'w_lru_a', 'b_lru_a', 'w_lru_i',
             'b_lru_i', 'lru_lambda', 'b_fgate', 'w_proj_rnn', 'w_proj_attn', 'w_out', 'g_norm2', 'w_ffn_up',
             'w_ffn_conv', 'b_ffn_conv', 'w_ffn_down', 'w_ada_final', 'b_ada_final', 'g_final']
    return (loss, grad_x, *[res[n][0] for n in names], *[res[n][1] for n in names],
            *[res[n][2] for n in names], *[res[n][3] for n in names])
```

```python
import jax
import jax.numpy as jnp
from jax import lax
from jax.experimental import pallas as pl
from jax.experimental.pallas import tpu as pltpu

F32, BF16 = jnp.float32, jnp.bfloat16
D_MODEL, D_RNN, N_BLK, RNN_BLK = 1024, 1280, 10, 128
N_HEADS, HEAD_DIM, D_FF = 8, 128, 3072
N_DEV = 8
NZ = 8192
QKV0, XG0, FL0, MG0 = 0, 3072, 5632, 6144
NZR = NZ - XG0
XG0R, FL0R, MG0R = 0, FL0 - XG0, MG0 - XG0
RMS_EPS = 1e-6
LRU_C = 8.0
ADAM_LR, ADAM_B1, ADAM_B2, ADAM_EPS, ADAM_WD, ADAM_STEP = 0.001, 0.9, 0.999, 1e-08, 0.01, 10
VMEM_LIMIT = 56 << 20
SCAN_CHUNK = 256
ATTN_TQ = 256
CUM_BLK = 512
MESH = pl.DeviceIdType.MESH
ANY = pl.BlockSpec(memory_space=pl.ANY)


def _call(body, **kw):
    return pl.pallas_call(body, **kw)


def _params():
    return pltpu.CompilerParams(vmem_limit_bytes=VMEM_LIMIT)


ZF_PIECES = 2 * D_FF // 128 // N_DEV


def _zf_cols(owner, p):
    ob = owner * ZF_PIECES + p
    blk = 2 * (ob % (D_FF // 128)) + ob // (D_FF // 128)
    return pl.ds(pl.multiple_of(blk * 128, 128), 128)


def _n_pieces(mode):
    return ZF_PIECES if mode in ("gather_zf", "scatter_zf") else 1


def _exchanged_shape(a, mode):
    if mode is True:
        return (N_DEV,) + a.shape
    if mode is False:
        return a.shape
    if mode == "gather_zf":
        return (a.shape[0], N_DEV * a.shape[1])
    return (N_DEV, a.shape[0], a.shape[1] // N_DEV)


def _direct_copies(ins, outs, sems, modes):
    send_sems, recv_sems, loc_sems = sems
    x, y, c = lax.axis_index("x"), lax.axis_index("y"), lax.axis_index("c")
    me = 4 * x + 2 * y + c

    def ends(a, to, p):
        mode = modes[a]
        if mode is True:
            return ins[a], outs[a].at[me]
        if mode is False:
            return ins[a].at[to], outs[a].at[me]
        mine = slice(p * 128, (p + 1) * 128)
        if mode == "gather_zf":
            return ins[a].at[:, mine], outs[a].at[:, _zf_cols(me, p)]
        return ins[a].at[:, _zf_cols(to, p)], outs[a].at[me, :, mine]

    local, remote = [], []
    nloc = 0
    for a in range(len(ins)):
        for p in range(_n_pieces(modes[a])):
            src, dst = ends(a, me, p)
            local.append(pltpu.make_async_copy(src, dst, loc_sems.at[nloc]))
            nloc += 1
    for k in range(1, N_DEV):
        px = 1 - x if k & 4 else x
        py = 1 - y if k & 2 else y
        pc = 1 - c if k & 1 else c
        peer = 4 * px + 2 * py + pc
        nrem = (k - 1) * nloc
        for a in range(len(ins)):
            for p in range(_n_pieces(modes[a])):
                src, dst = ends(a, peer, p)
                remote.append(pltpu.make_async_remote_copy(
                    src_ref=src, dst_ref=dst, send_sem=send_sems.at[nrem], recv_sem=recv_sems.at[nrem],
                    device_id=(px, py, pc), device_id_type=MESH))
                nrem += 1
    return local, remote


def _with_exchange(body, args, arrs, bcast, *, name, grid, in_specs, out_specs, out_shape, scratch_shapes=(), **kw):
    n_in, n_out, n_sc, nc = len(args), len(out_shape), len(scratch_shapes), len(arrs)

    def wrapped(*refs):
        core_in, cin = refs[:n_in], refs[n_in:n_in + nc]
        core_out = refs[n_in + nc:n_in + nc + n_out]
        cout = refs[n_in + nc + n_out:n_in + 2 * nc + n_out]
        rest = refs[n_in + 2 * nc + n_out:]
        core_sc, sems = rest[:n_sc], rest[n_sc:]

        def start():
            local, remote = _direct_copies(cin, cout, sems, bcast)
            for cp in local + remote:
                cp.start()

        def wait():
            local, remote = _direct_copies(cin, cout, sems, bcast)
            for cp in remote + local:
                cp.wait()

        if not grid:
            start()
            body(*core_in, *core_out, *core_sc)
            wait()
            return
        first, last = None, None
        for d, g in enumerate(grid):
            i = pl.program_id(d)
            first = (i == 0) if first is None else jnp.logical_and(first, i == 0)
            last = (i == g - 1) if last is None else jnp.logical_and(last, i == g - 1)
        pl.when(first)(start)
        body(*core_in, *core_out, *core_sc)
        pl.when(last)(wait)

    ex_shape = [jax.ShapeDtypeStruct(_exchanged_shape(a, mode), a.dtype) for a, mode in zip(arrs, bcast)]
    ncopies = sum(_n_pieces(mode) for mode in bcast)
    sems = [pltpu.SemaphoreType.DMA((ncopies * (N_DEV - 1),)), pltpu.SemaphoreType.DMA((ncopies * (N_DEV - 1),)),
            pltpu.SemaphoreType.DMA((ncopies,))]
    res = _call(wrapped, name=name, grid=grid, in_specs=list(in_specs) + [ANY] * nc,
                out_specs=list(out_specs) + [ANY] * nc, out_shape=list(out_shape) + ex_shape,
                scratch_shapes=list(scratch_shapes) + sems, **kw)(*args, *arrs)
    return list(res[:n_out]), list(res[n_out:])


def _run(body, args, exchange, **kw):
    if exchange is None:
        return list(_call(body, **kw)(*args))
    outs, got = _with_exchange(body, args, *exchange, **kw)
    return outs + [got]


def _exchange(arrs, bcast, name):
    return _with_exchange(lambda: None, [], arrs, bcast, name=name, grid=(), in_specs=[], out_specs=[], out_shape=[],
                          compiler_params=pltpu.CompilerParams(has_side_effects=True))[1]


def _gather_two_level(arrs, name):
    n = len(arrs)

    def body(*refs):
        ins, outs = refs[:n], refs[n:2 * n]
        send_sems, recv_sems, loc_sems = refs[2 * n:]
        x, y, c = lax.axis_index("x"), lax.axis_index("y"), lax.axis_index("c")
        me, sibling = (x, y, c), (x, y, 1 - c)
        chips = [(1 - x, y), (x, 1 - y), (1 - x, 1 - y)]

        def slot(a, p):
            return outs[a].at[4 * p[0] + 2 * p[1] + p[2]]

        def copy(a, k, block, to, src=None):
            return pltpu.make_async_remote_copy(
                src_ref=slot(a, block) if src is None else src, dst_ref=slot(a, block),
                send_sem=send_sems.at[a * 7 + k], recv_sem=recv_sems.at[a * 7 + k],
                device_id=to, device_id_type=MESH)

        mine = [pltpu.make_async_copy(ins[a], slot(a, me), loc_sems.at[a]) for a in range(n)]
        first = []
        for a in range(n):
            first.append(copy(a, 0, me, sibling, src=ins[a]))
            first += [copy(a, 1 + j, me, (*chip, c), src=ins[a]) for j, chip in enumerate(chips)]
        for cp in mine + first:
            cp.start()
        passed = []
        for j, chip in enumerate(chips):
            for a in range(n):
                copy(a, 1 + j, (*chip, c), me).wait_recv()
                fwd = copy(a, 4 + j, (*chip, c), sibling)
                fwd.start()
                passed.append(fwd)
        for a in range(n):
            copy(a, 0, sibling, me).wait_recv()
        for j, chip in enumerate(chips):
            for a in range(n):
                copy(a, 4 + j, (*chip, 1 - c), me).wait_recv()
        for cp in first + passed:
            cp.wait_send()
        for cp in mine:
            cp.wait()

    return _call(
        body, name=name, in_specs=[ANY] * n, out_specs=[ANY] * n,
        out_shape=[jax.ShapeDtypeStruct((N_DEV,) + a.shape, a.dtype) for a in arrs],
        scratch_shapes=[pltpu.SemaphoreType.DMA((n * 7,)), pltpu.SemaphoreType.DMA((n * 7,)),
                        pltpu.SemaphoreType.DMA((n,))],
        compiler_params=pltpu.CompilerParams(has_side_effects=True),
    )(*arrs)


def _sigmoid(x):
    return 1.0 / (1.0 + jnp.exp(-x))


def _log1p(e):
    u = 1.0 + e
    d = u - 1.0
    return jnp.where(d == 0.0, e, jnp.log(u) * (e / jnp.where(d == 0.0, 1.0, d)))


def _softplus(x):
    return jnp.maximum(x, 0.0) + _log1p(jnp.exp(-jnp.abs(x)))


_GELU_C = 0.7978845608028654


def _gelu(x):
    t = jnp.tanh(_GELU_C * (x + 0.044715 * x * x * x))
    return 0.5 * x * (1.0 + t)


def _gelu_and_grad(x):
    t = jnp.tanh(_GELU_C * (x + 0.044715 * x * x * x))
    g = 0.5 * x * (1.0 + t)
    dg = 0.5 * (1.0 + t) + 0.5 * x * (1.0 - t * t) * _GELU_C * (1.0 + 3.0 * 0.044715 * x * x)
    return g, dg


def _acc(ref, idx, val, first):
    @pl.when(first)
    def _():
        ref[idx] = val

    @pl.when(jnp.logical_not(first))
    def _():
        ref[idx] = ref[idx] + val


def _split3(x):
    hi = x.astype(BF16)
    r1 = x - hi.astype(F32)
    mid = r1.astype(BF16)
    lo = (r1 - mid.astype(F32)).astype(BF16)
    return hi, mid, lo


def _mm(a, b, *, name, ta=False, tb=False, out_dtype=F32, tm, tn, tk, bias=None, silu_a=False, exchange=None,
        b_cols=None):
    m, k = (a.shape[1], a.shape[0]) if ta else a.shape
    col0, n = b_cols if b_cols is not None else (0, b.shape[0] if tb else b.shape[1])
    assert m % tm == 0 and n % tn == 0 and k % tk == 0 and col0 % tn == 0, (name, m, n, k)
    jo = col0 // tn
    nk = k // tk
    dn = (((0 if ta else 1,), (1 if tb else 0,)), ((), ()))
    use_acc = nk > 1 and out_dtype != F32

    def body(*refs):
        a_ref, b_ref = refs[0], refs[1]
        bias_ref = refs[2] if bias is not None else None
        o_ref = refs[3 if bias is not None else 2]
        av = a_ref[...]
        if silu_a:
            av = av * _sigmoid(av)
        p = lax.dot_general(av.astype(BF16), b_ref[...].astype(BF16), dn, preferred_element_type=F32)
        if bias is not None:
            p = p + bias_ref[...]
        if nk == 1:
            o_ref[...] = p.astype(out_dtype)
            return
        kk = pl.program_id(2)
        acc = refs[-1] if use_acc else o_ref

        @pl.when(kk == 0)
        def _():
            acc[...] = p

        @pl.when(kk > 0)
        def _():
            acc[...] = acc[...] + p

        if use_acc:
            @pl.when(kk == nk - 1)
            def _():
                o_ref[...] = acc[...].astype(out_dtype)

    a_spec = pl.BlockSpec((tk, tm), lambda i, j, q: (q, i)) if ta else pl.BlockSpec((tm, tk), lambda i, j, q: (i, q))
    b_spec = (pl.BlockSpec((tn, tk), lambda i, j, q: (j + jo, q)) if tb
              else pl.BlockSpec((tk, tn), lambda i, j, q: (q, j + jo)))
    in_specs, args = [a_spec, b_spec], [a, b]
    if bias is not None:
        in_specs.append(pl.BlockSpec((1, tn), lambda i, j, q: (0, j)))
        args.append(bias)
    kw = dict(name=name, grid=(m // tm, n // tn, nk), in_specs=in_specs,
              out_specs=[pl.BlockSpec((tm, tn), lambda i, j, q: (i, j))],
              out_shape=[jax.ShapeDtypeStruct((m, n), out_dtype)],
              scratch_shapes=[pltpu.VMEM((tm, tn), F32)] if use_acc else [],
              compiler_params=_params())
    if exchange is None:
        return _call(body, **kw)(*args)[0]
    outs, got = _with_exchange(body, args, *exchange, **kw)
    return outs[0], got


def _tok_spec(ts):
    return pl.BlockSpec((1, ts, D_MODEL), lambda b, s: (b, s, 0))


def _seq_spec():
    return pl.BlockSpec((1, 1, D_MODEL), lambda b, s: (b, 0, 0))


def _vec_spec():
    return pl.BlockSpec((1, D_MODEL), lambda b, s: (0, 0))


def _norm_fwd(x, g, shift, scale, *, name, yprev=None, gate=None, ts=512):
    bsz, s, _ = x.shape
    res = yprev is not None

    def body(*refs):
        if res:
            x_ref, g_ref, sh_ref, sc_ref, y_ref, gt_ref, h_ref, xn_ref = refs
            xv = x_ref[0] + gt_ref[0] * y_ref[0]
            xn_ref[0] = xv
        else:
            x_ref, g_ref, sh_ref, sc_ref, h_ref = refs
            xv = x_ref[0]
        rstd = lax.rsqrt(jnp.mean(xv * xv, axis=-1, keepdims=True) + RMS_EPS)
        nv = xv * rstd * g_ref[...]
        h_ref[0] = (nv * (1.0 + sc_ref[0]) + sh_ref[0]).astype(BF16)

    in_specs = [_tok_spec(ts), _vec_spec(), _seq_spec(), _seq_spec()]
    args = [x, g, shift, scale]
    out_specs = [_tok_spec(ts)]
    out_shape = [jax.ShapeDtypeStruct(x.shape, BF16)]
    if res:
        in_specs += [_tok_spec(ts), _seq_spec()]
        args += [yprev, gate]
        out_specs.append(_tok_spec(ts))
        out_shape.append(jax.ShapeDtypeStruct(x.shape, F32))
    return _call(body, name=name, grid=(bsz, s // ts), in_specs=in_specs, out_specs=out_specs,
                 out_shape=out_shape, compiler_params=_params())(*args)


def _final(x1, yf, gate2, g, shift, scale, target, *, ts=512):
    bsz, s, _ = x1.shape
    ns = s // ts

    def body(x_ref, y_ref, gt_ref, g_ref, sh_ref, sc_ref, t_ref,
             lp_ref, dx_ref, dyf_ref, dsh_ref, dsc_ref, dgt_ref, dg_ref):
        b, si = pl.program_id(0), pl.program_id(1)
        yv, gt, gv = y_ref[0], gt_ref[0], g_ref[...]
        xv = x_ref[0] + gt * yv
        rstd = lax.rsqrt(jnp.mean(xv * xv, axis=-1, keepdims=True) + RMS_EPS)
        xhat = xv * rstd
        nv = xhat * gv
        sc1 = 1.0 + sc_ref[0]
        err = nv * sc1 + sh_ref[0] - t_ref[0]
        lp_ref[...] = jnp.zeros((1, 1, 8, 128), F32) + jnp.sum(err * err)
        dy = err * (1.0 / D_MODEL)
        dn = dy * sc1
        dxh = dn * gv
        dx = rstd * (dxh - xhat * jnp.mean(dxh * xhat, axis=-1, keepdims=True))
        dx_ref[0] = dx
        dyf_ref[0] = (dx * gt).astype(BF16)
        _acc(dsh_ref, 0, jnp.sum(dy, axis=0, keepdims=True), si == 0)
        _acc(dsc_ref, 0, jnp.sum(dy * nv, axis=0, keepdims=True), si == 0)
        _acc(dgt_ref, 0, jnp.sum(dx * yv, axis=0, keepdims=True), si == 0)
        _acc(dg_ref, slice(None), jnp.sum(dn * xhat, axis=0, keepdims=True), jnp.logical_and(b == 0, si == 0))

    seq_shape = jax.ShapeDtypeStruct((bsz, 1, D_MODEL), F32)
    return _call(
        body, name="final_loss", grid=(bsz, ns),
        in_specs=[_tok_spec(ts), _tok_spec(ts), _seq_spec(), _vec_spec(), _seq_spec(), _seq_spec(), _tok_spec(ts)],
        out_specs=[pl.BlockSpec((1, 1, 8, 128), lambda b, s: (b, s, 0, 0)), _tok_spec(ts), _tok_spec(ts),
                   _seq_spec(), _seq_spec(), _seq_spec(), _vec_spec()],
        out_shape=[jax.ShapeDtypeStruct((bsz, ns, 8, 128), F32), jax.ShapeDtypeStruct(x1.shape, F32),
                   jax.ShapeDtypeStruct(x1.shape, BF16), seq_shape, seq_shape, seq_shape,
                   jax.ShapeDtypeStruct((1, D_MODEL), F32)],
        compiler_params=_params(),
    )(x1, yf, gate2, g, shift, scale, target)


def _norm_bwd(x, dh, g, scale, dres, *, name, yprev=None, gate=None, ts=512):
    bsz, s, _ = x.shape
    res = yprev is not None

    def body(*refs):
        if res:
            (x_ref, dh_ref, g_ref, sc_ref, dr_ref, y_ref, gt_ref,
             dx_ref, dg_ref, dsh_ref, dsc_ref, dy_ref, dgt_ref) = refs
        else:
            x_ref, dh_ref, g_ref, sc_ref, dr_ref, dx_ref, dg_ref, dsh_ref, dsc_ref = refs
        b, si = pl.program_id(0), pl.program_id(1)
        xv, dhv, gv = x_ref[0], dh_ref[0], g_ref[...]
        rstd = lax.rsqrt(jnp.mean(xv * xv, axis=-1, keepdims=True) + RMS_EPS)
        xhat = xv * rstd
        dn = dhv * (1.0 + sc_ref[0])
        dxh = dn * gv
        dx = dr_ref[0] + rstd * (dxh - xhat * jnp.mean(dxh * xhat, axis=-1, keepdims=True))
        dx_ref[0] = dx
        _acc(dsh_ref, 0, jnp.sum(dhv, axis=0, keepdims=True), si == 0)
        _acc(dsc_ref, 0, jnp.sum(dhv * (xhat * gv), axis=0, keepdims=True), si == 0)
        _acc(dg_ref, slice(None), jnp.sum(dn * xhat, axis=0, keepdims=True), jnp.logical_and(b == 0, si == 0))
        if res:
            dy_ref[0] = (dx * gt_ref[0]).astype(BF16)
            _acc(dgt_ref, 0, jnp.sum(dx * y_ref[0], axis=0, keepdims=True), si == 0)

    seq_shape = jax.ShapeDtypeStruct((bsz, 1, D_MODEL), F32)
    in_specs = [_tok_spec(ts), _tok_spec(ts), _vec_spec(), _seq_spec(), _tok_spec(ts)]
    args = [x, dh, g, scale, dres]
    out_specs = [_tok_spec(ts), _vec_spec(), _seq_spec(), _seq_spec()]
    out_shape = [jax.ShapeDtypeStruct(x.shape, F32), jax.ShapeDtypeStruct((1, D_MODEL), F32), seq_shape, seq_shape]
    if res:
        in_specs += [_tok_spec(ts), _seq_spec()]
        args += [yprev, gate]
        out_specs += [_tok_spec(ts), _seq_spec()]
        out_shape += [jax.ShapeDtypeStruct(x.shape, BF16), seq_shape]
    return _call(body, name=name, grid=(bsz, s // ts), in_specs=in_specs, out_specs=out_specs,
                 out_shape=out_shape, compiler_params=_params())(*args)


def _lru_gates(xr, wa, ba, wi, bi, sp):
    xb = xr.astype(BF16)
    r = _sigmoid(jnp.dot(xb, wa, preferred_element_type=F32) + ba)
    i = _sigmoid(jnp.dot(xb, wi, preferred_element_type=F32) + bi)
    la = -LRU_C * r * sp
    a = jnp.exp(la)
    mult = jnp.sqrt(-jnp.tanh(la) * (a * a + 1.0))
    return r, i, a, mult


def _scan_forward(a, u, h_in, row):
    n = a.shape[0]
    r8 = row & 7
    for k in (1, 2, 4):
        m = r8 >= k
        u = jnp.where(m, a * pltpu.roll(u, k, 0) + u, u)
        a = jnp.where(m, a * pltpu.roll(a, k, 0), a)
    carry = jnp.broadcast_to(h_in, (8, a.shape[1]))
    out = []
    for j in range(n // 8):
        hj = u[8 * j:8 * j + 8] + a[8 * j:8 * j + 8] * carry
        out.append(hj)
        carry = jnp.broadcast_to(hj[7:8], hj.shape)
    return jnp.concatenate(out, axis=0)


def _scan_backward(a_up, g, g_in, row):
    n = g.shape[0]
    r8 = row & 7
    for k in (1, 2, 4):
        m = r8 < 8 - k
        g = jnp.where(m, a_up * pltpu.roll(g, n - k, 0) + g, g)
        a_up = jnp.where(m, a_up * pltpu.roll(a_up, n - k, 0), a_up)
    carry = jnp.broadcast_to(g_in, (8, g.shape[1]))
    out = [None] * (n // 8)
    for j in reversed(range(n // 8)):
        gj = g[8 * j:8 * j + 8] + a_up[8 * j:8 * j + 8] * carry
        out[j] = gj
        carry = jnp.broadcast_to(gj[0:1], gj.shape)
    return jnp.concatenate(out, axis=0)


def _rnn_fwd(z, wc, bc, wa, ba, wi, bi, lam, exchange=None):
    bsz, s, _ = z.shape
    ch = min(SCAN_CHUNK, s)
    nch = s // ch

    def body(z_ref, wc_ref, bc_ref, wa_ref, ba_ref, wi_ref, bi_ref, lam_ref, h_ref, y_ref):
        wcv, bcv = wc_ref[...], bc_ref[...]
        wav, wiv = wa_ref[0].astype(BF16), wi_ref[0].astype(BF16)
        bav, biv = ba_ref[...], bi_ref[...]
        sp = _softplus(-lam_ref[...])
        row = lax.broadcasted_iota(jnp.int32, (ch, RNN_BLK), 0)

        def chunk(ci, carry):
            prev8, hc = carry
            st = pl.multiple_of(ci * ch, ch)
            blk = z_ref[0, pl.ds(st, ch), :]
            xp, gr = blk[:, :RNN_BLK], blk[:, RNN_BLK:]
            xe = jnp.concatenate([prev8, xp], axis=0)
            xr = bcv + wcv[3:4] * xp
            for j in range(1, 4):
                xr = xr + wcv[3 - j:4 - j] * pltpu.roll(xe, j, 0)[8:]
            _, i, a, mult = _lru_gates(xr, wav, bav, wiv, biv, sp)
            hh = _scan_forward(a, mult * (i * xr), hc, row)
            h_ref[0, pl.ds(st, ch), :] = hh
            y_ref[0, pl.ds(st, ch), :] = (_gelu(gr) * hh).astype(BF16)
            return xp[ch - 8:], hh[ch - 1:]

        lax.fori_loop(0, nch, chunk, (jnp.zeros((8, RNN_BLK), F32), jnp.zeros((1, RNN_BLK), F32)))

    vec = pl.BlockSpec((1, RNN_BLK), lambda b, n: (0, n))
    mat = pl.BlockSpec((1, RNN_BLK, RNN_BLK), lambda b, n: (n, 0, 0))
    out = pl.BlockSpec((1, s, RNN_BLK), lambda b, n: (b, 0, n))
    return _run(
        body, [z, wc, bc, wa, ba, wi, bi, lam], exchange, name="rnn_fwd", grid=(bsz, N_BLK),
        in_specs=[pl.BlockSpec((1, s, 2 * RNN_BLK), lambda b, n: (b, 0, XG0R // (2 * RNN_BLK) + n)),
                  pl.BlockSpec((4, RNN_BLK), lambda b, n: (0, n)), vec, mat, vec, mat, vec, vec],
        out_specs=[out, out],
        out_shape=[jax.ShapeDtypeStruct((bsz, s, D_RNN), F32), jax.ShapeDtypeStruct((bsz, s, D_RNN), BF16)],
        compiler_params=_params())


def _rnn_bwd(z, h, dy, dz, wc, bc, wa, ba, wi, bi, lam, exchange=None):
    bsz, s, _ = z.shape
    ch = min(SCAN_CHUNK, s)
    nch = s // ch

    def body(z_ref, h_ref, dy_ref, dz_in, wc_ref, bc_ref, wa_ref, ba_ref, wi_ref, bi_ref, lam_ref,
             dz_ref, dwa_ref, dwi_ref, dsm_ref):
        del dz_in
        b = pl.program_id(1)
        wcv, bcv = wc_ref[...], bc_ref[...]
        wav, wiv = wa_ref[0].astype(BF16), wi_ref[0].astype(BF16)
        bav, biv = ba_ref[...], bi_ref[...]
        lamv = lam_ref[...]
        sp = _softplus(-lamv)
        row = lax.broadcasted_iota(jnp.int32, (ch, RNN_BLK), 0)

        @pl.when(b == 0)
        def _():
            dwa_ref[...] = jnp.zeros_like(dwa_ref)
            dwi_ref[...] = jnp.zeros_like(dwi_ref)
            dsm_ref[...] = jnp.zeros_like(dsm_ref)

        def chunk(k, carry):
            a_next, g_next, dxr_next8 = carry
            ci = nch - 1 - k
            st = pl.multiple_of(ci * ch, ch)
            pst = pl.multiple_of(jnp.maximum(st - 8, 0), 8)
            has_prev = (ci > 0).astype(F32)
            blk = z_ref[0, pl.ds(st, ch), :]
            xp, gr = blk[:, :RNN_BLK], blk[:, RNN_BLK:]
            xprev8 = z_ref[0, pl.ds(pst, 8), :][:, :RNN_BLK] * has_prev
            hprev8 = h_ref[0, pl.ds(pst, 8), :] * has_prev
            xe = jnp.concatenate([xprev8, xp], axis=0)
            xs = [xp] + [pltpu.roll(xe, j, 0)[8:] for j in range(1, 4)]
            xr = bcv
            for kk in range(4):
                xr = xr + wcv[kk:kk + 1] * xs[3 - kk]
            r, i, a, mult = _lru_gates(xr, wav, bav, wiv, biv, sp)
            hh = h_ref[0, pl.ds(st, ch), :]
            hm1 = pltpu.roll(jnp.concatenate([hprev8, hh], axis=0), 1, 0)[8:]
            dyv = dy_ref[0, pl.ds(st, ch), :]
            ge, dge = _gelu_and_grad(gr)
            dgr = dyv * hh * dge
            gg = _scan_backward(jnp.where(row == ch - 1, a_next, pltpu.roll(a, ch - 1, 0)), dyv * ge, g_next, row)
            da = gg * hm1
            ix = i * xr
            dmult = gg * ix
            di = gg * mult * xr
            dxr = gg * mult * i
            dla = da * a - dmult * (a * a) / mult
            dr = dla * (-LRU_C * sp)
            dsp = jnp.sum(dla * (-LRU_C * r), axis=0, keepdims=True)
            dpa = dr * r * (1.0 - r)
            dpi = di * i * (1.0 - i)
            dpab, dpib, xrb = dpa.astype(BF16), dpi.astype(BF16), xr.astype(BF16)
            nt = (((1,), (1,)), ((), ()))
            tn = (((0,), (0,)), ((), ()))
            dxr = dxr + lax.dot_general(dpab, wav, nt, preferred_element_type=F32)
            dxr = dxr + lax.dot_general(dpib, wiv, nt, preferred_element_type=F32)
            dwa_ref[0] = dwa_ref[0] + lax.dot_general(xrb, dpab, tn, preferred_element_type=F32)
            dwi_ref[0] = dwi_ref[0] + lax.dot_general(xrb, dpib, tn, preferred_element_type=F32)
            rows = [jnp.sum(dxr * xs[3 - q], axis=0, keepdims=True) for q in range(4)]
            rows += [jnp.sum(dxr, axis=0, keepdims=True), jnp.sum(dpa, axis=0, keepdims=True),
                     jnp.sum(dpi, axis=0, keepdims=True), dsp * (-_sigmoid(-lamv))]
            dsm_ref[...] = dsm_ref[...] + jnp.concatenate(rows, axis=0)
            dxe = jnp.concatenate([dxr, dxr_next8], axis=0)
            dxp = wcv[3:4] * dxr
            for j in range(1, 4):
                dxp = dxp + wcv[3 - j:4 - j] * pltpu.roll(dxe, ch + 8 - j, 0)[:ch]
            dz_ref[0, pl.ds(st, ch), :] = jnp.concatenate([dxp, dgr], axis=1).astype(BF16)
            return a[0:1], gg[0:1], dxr[0:8]

        lax.fori_loop(0, nch, chunk, (jnp.zeros((1, RNN_BLK), F32), jnp.zeros((1, RNN_BLK), F32),
                                      jnp.zeros((8, RNN_BLK), F32)))

    vec = pl.BlockSpec((1, RNN_BLK), lambda n, b: (0, n))
    mat = pl.BlockSpec((1, RNN_BLK, RNN_BLK), lambda n, b: (n, 0, 0))
    tok = pl.BlockSpec((1, s, RNN_BLK), lambda n, b: (b, 0, n))
    zblk = pl.BlockSpec((1, s, 2 * RNN_BLK), lambda n, b: (b, 0, XG0R // (2 * RNN_BLK) + n))
    dzblk = pl.BlockSpec((1, s, 2 * RNN_BLK), lambda n, b: (b, 0, XG0 // (2 * RNN_BLK) + n))
    return _run(
        body, [z, h, dy, dz, wc, bc, wa, ba, wi, bi, lam], exchange, name="rnn_bwd", grid=(N_BLK, bsz),
        in_specs=[zblk, tok, tok, ANY, pl.BlockSpec((4, RNN_BLK), lambda n, b: (0, n)), vec, mat, vec, mat, vec, vec],
        out_specs=[dzblk, mat, mat, pl.BlockSpec((8, RNN_BLK), lambda n, b: (0, n))],
        out_shape=[jax.ShapeDtypeStruct(dz.shape, dz.dtype), jax.ShapeDtypeStruct((N_BLK, RNN_BLK, RNN_BLK), F32),
                   jax.ShapeDtypeStruct((N_BLK, RNN_BLK, RNN_BLK), F32), jax.ShapeDtypeStruct((8, D_RNN), F32)],
        input_output_aliases={3: 0},
        compiler_params=_params())


def _tri(n, upper):
    r = lax.broadcasted_iota(jnp.int32, (n, n), 0)
    c = lax.broadcasted_iota(jnp.int32, (n, n), 1)
    return (r <= c if upper else r >= c).astype(BF16)


def _exact_dot(x, t):
    hi, mid, lo = _split3(x)
    return (jnp.dot(hi, t, preferred_element_type=F32) + jnp.dot(mid, t, preferred_element_type=F32)
            + jnp.dot(lo, t, preferred_element_type=F32))


def _fgate_fwd(flt, bfg):
    bsz, nh, s = flt.shape
    blk = min(CUM_BLK, s)

    def body(fl_ref, b_ref, f_ref):
        tri = _tri(blk, True)
        carry = jnp.zeros((nh, 1), F32)
        for q in range(s // blk):
            xq = fl_ref[0, :, q * blk:(q + 1) * blk] + b_ref[...]
            fq = _exact_dot(-_softplus(-xq), tri) + carry
            f_ref[0, :, q * blk:(q + 1) * blk] = fq
            carry = fq[:, blk - 1:blk]

    spec = pl.BlockSpec((1, nh, s), lambda b: (b, 0, 0))
    return _call(body, name="fgate_fwd", grid=(bsz,), in_specs=[spec, pl.BlockSpec((nh, 1), lambda b: (0, 0))],
                 out_specs=spec, out_shape=jax.ShapeDtypeStruct(flt.shape, F32), compiler_params=_params())(flt, bfg)


def _fgate_bwd(dft, flt, bfg):
    bsz, nh, s = flt.shape
    blk = min(CUM_BLK, s)

    def body(df_ref, fl_ref, b_ref, dfl_ref, db_ref):
        b = pl.program_id(0)
        tri = _tri(blk, False)
        carry = jnp.zeros((nh, 1), F32)
        tot = jnp.zeros((nh, 1), F32)
        for q in reversed(range(s // blk)):
            sl = slice(q * blk, (q + 1) * blk)
            dlq = _exact_dot(df_ref[0, :, sl], tri) + carry
            carry = dlq[:, 0:1]
            dflq = dlq * _sigmoid(-(fl_ref[0, :, sl] + b_ref[...]))
            dfl_ref[0, :, sl] = dflq
            tot = tot + jnp.sum(dflq, axis=1, keepdims=True)
        _acc(db_ref, slice(None), tot, b == 0)

    spec = pl.BlockSpec((1, nh, s), lambda b: (b, 0, 0))
    small = pl.BlockSpec((nh, 1), lambda b: (0, 0))
    return _call(body, name="fgate_bwd", grid=(bsz,), in_specs=[spec, spec, small], out_specs=[spec, small],
                 out_shape=[jax.ShapeDtypeStruct(flt.shape, F32), jax.ShapeDtypeStruct((nh, 1), F32)],
                 compiler_params=_params())(dft, flt, bfg)


_NT = (((1,), (1,)), ((), ()))
_TN = (((0,), (0,)), ((), ()))


def _attn_block_scores(q_b, z_ref, frow_ref, k0, tq):
    kk = z_ref[0, k0:k0 + tq, HEAD_DIM:2 * HEAD_DIM]
    return lax.dot_general(q_b, kk, _NT, preferred_element_type=F32) - frow_ref[0, 0, :, k0:k0 + tq]


def _causal(tq):
    return lax.broadcasted_iota(jnp.int32, (tq, tq), 0) >= lax.broadcasted_iota(jnp.int32, (tq, tq), 1)


def _attn_fwd(z, fcol, frow, exchange=None):
    bsz, s, _ = z.shape
    tq = min(ATTN_TQ, s)
    scale = HEAD_DIM ** -0.5

    def body(z_ref, fcol_ref, frow_ref, o_ref, lse_ref):
        for qb in range(s // tq):
            q0 = qb * tq
            q_b = (z_ref[0, q0:q0 + tq, :HEAD_DIM].astype(F32) * scale).astype(BF16)
            m = jnp.full((tq, 1), -jnp.inf, F32)
            l = jnp.zeros((tq, 1), F32)
            acc = jnp.zeros((tq, HEAD_DIM), F32)
            for kb in range(qb + 1):
                k0 = kb * tq
                sc = _attn_block_scores(q_b, z_ref, frow_ref, k0, tq)
                if kb == qb:
                    sc = jnp.where(_causal(tq), sc, -jnp.inf)
                m_new = jnp.maximum(m, jnp.max(sc, axis=-1, keepdims=True))
                alpha = jnp.exp(m - m_new)
                p = jnp.exp(sc - m_new)
                l = alpha * l + jnp.sum(p, axis=-1, keepdims=True)
                vv = z_ref[0, k0:k0 + tq, 2 * HEAD_DIM:]
                acc = alpha * acc + jnp.dot(p.astype(BF16), vv, preferred_element_type=F32)
                m = m_new
            o_ref[0, q0:q0 + tq, :] = (acc / l).astype(BF16)
            lse_ref[0, 0, q0:q0 + tq, :] = m + jnp.log(l) + fcol_ref[0, 0, q0:q0 + tq, :]

    return _run(
        body, [z, fcol, frow], exchange, name="attn_fwd", grid=(bsz, N_HEADS),
        in_specs=[pl.BlockSpec((1, s, 3 * HEAD_DIM), lambda b, h: (b, 0, h)),
                  pl.BlockSpec((1, 1, s, 1), lambda b, h: (b, h, 0, 0)),
                  pl.BlockSpec((1, 1, 1, s), lambda b, h: (b, h, 0, 0))],
        out_specs=[pl.BlockSpec((1, s, HEAD_DIM), lambda b, h: (b, 0, h)),
                   pl.BlockSpec((1, 1, s, 1), lambda b, h: (b, h, 0, 0))],
        out_shape=[jax.ShapeDtypeStruct((bsz, s, N_HEADS * HEAD_DIM), BF16),
                   jax.ShapeDtypeStruct((bsz, N_HEADS, s, 1), F32)],
        compiler_params=_params())


def _attn_bwd(z, do, lse, fcol, frow, dz, exchange=None):
    bsz, s, _ = z.shape
    tq = min(ATTN_TQ, s)
    scale = HEAD_DIM ** -0.5

    def body(z_ref, do_ref, lse_ref, fcol_ref, frow_ref, dz_in, dz_ref, df_ref, dk_acc, dv_acc, p_buf, dp_buf):
        del dz_in
        dk_acc[...] = jnp.zeros_like(dk_acc)
        dv_acc[...] = jnp.zeros_like(dv_acc)
        df_ref[...] = jnp.zeros_like(df_ref)
        for qb in range(s // tq):
            q0 = qb * tq
            q_b = (z_ref[0, q0:q0 + tq, :HEAD_DIM].astype(F32) * scale).astype(BF16)
            do_b = do_ref[0, q0:q0 + tq, :]
            lse_q = lse_ref[0, 0, q0:q0 + tq, :] - fcol_ref[0, 0, q0:q0 + tq, :]
            delta = jnp.zeros((tq, 1), F32)
            for kb in range(qb + 1):
                k0 = kb * tq
                p = jnp.exp(_attn_block_scores(q_b, z_ref, frow_ref, k0, tq) - lse_q)
                if kb == qb:
                    p = jnp.where(_causal(tq), p, 0.0)
                vv = z_ref[0, k0:k0 + tq, 2 * HEAD_DIM:]
                dp = lax.dot_general(do_b, vv, _NT, preferred_element_type=F32)
                delta = delta + jnp.sum(p * dp, axis=-1, keepdims=True)
                p_buf[:, k0:k0 + tq] = p
                dp_buf[:, k0:k0 + tq] = dp
                dv_acc[k0:k0 + tq, :] = dv_acc[k0:k0 + tq, :] + lax.dot_general(
                    p.astype(BF16), do_b, _TN, preferred_element_type=F32)
            dq = jnp.zeros((tq, HEAD_DIM), F32)
            for kb in range(qb + 1):
                k0 = kb * tq
                ds = p_buf[:, k0:k0 + tq] * (dp_buf[:, k0:k0 + tq] - delta)
                dsb = ds.astype(BF16)
                kk = z_ref[0, k0:k0 + tq, HEAD_DIM:2 * HEAD_DIM]
                dq = dq + jnp.dot(dsb, kk, preferred_element_type=F32)
                dk_acc[k0:k0 + tq, :] = dk_acc[k0:k0 + tq, :] + lax.dot_general(
                    dsb, q_b, _TN, preferred_element_type=F32)
                df_ref[0, 0, :, k0:k0 + tq] = df_ref[0, 0, :, k0:k0 + tq] - jnp.sum(ds, axis=0, keepdims=True)
            dz_ref[0, q0:q0 + tq, :HEAD_DIM] = (dq * scale).astype(BF16)
        dz_ref[0, :, HEAD_DIM:2 * HEAD_DIM] = dk_acc[...].astype(BF16)
        dz_ref[0, :, 2 * HEAD_DIM:] = dv_acc[...].astype(BF16)

    zblk = pl.BlockSpec((1, s, 3 * HEAD_DIM), lambda b, h: (b, 0, h))
    hblk = pl.BlockSpec((1, s, HEAD_DIM), lambda b, h: (b, 0, h))
    col = pl.BlockSpec((1, 1, s, 1), lambda b, h: (b, h, 0, 0))
    rowspec = pl.BlockSpec((1, 1, 1, s), lambda b, h: (b, h, 0, 0))
    return _run(
        body, [z, do, lse, fcol, frow, dz], exchange, name="attn_bwd", grid=(bsz, N_HEADS),
        in_specs=[zblk, hblk, col, col, rowspec, ANY],
        out_specs=[zblk, rowspec],
        out_shape=[jax.ShapeDtypeStruct(dz.shape, dz.dtype), jax.ShapeDtypeStruct((bsz, N_HEADS, 1, s), F32)],
        scratch_shapes=[pltpu.VMEM((s, HEAD_DIM), F32), pltpu.VMEM((s, HEAD_DIM), F32),
                        pltpu.VMEM((tq, s), F32), pltpu.VMEM((tq, s), F32)],
        input_output_aliases={5: 0},
        compiler_params=_params())


def _merge_fwd(z2, pr, pa, *, tr=512):
    t = z2.shape[0]

    def body(mgr_ref, mga_ref, pr_ref, pa_ref, o_ref):
        gr_ = _sigmoid(mgr_ref[...])
        ga = _sigmoid(mga_ref[...])
        o_ref[...] = (gr_ * pr_ref[...] + ga * pa_ref[...]).astype(BF16)

    tok = pl.BlockSpec((tr, D_MODEL), lambda i: (i, 0))
    return _call(body, name="merge_fwd", grid=(t // tr,),
                 in_specs=[pl.BlockSpec((tr, D_MODEL), lambda i: (i, MG0R // D_MODEL)),
                           pl.BlockSpec((tr, D_MODEL), lambda i: (i, MG0R // D_MODEL + 1)), tok, tok],
                 out_specs=tok, out_shape=jax.ShapeDtypeStruct((t, D_MODEL), BF16),
                 compiler_params=_params())(z2, z2, pr, pa)


def _merge_bwd(dm, z2, pr, pa, *, tr=512):
    t = z2.shape[0]

    def body(dm_ref, mgr_ref, mga_ref, pr_ref, pa_ref, dz_ref, dpr_ref, dpa_ref):
        dmv = dm_ref[...]
        gr_ = _sigmoid(mgr_ref[...])
        ga = _sigmoid(mga_ref[...])
        dpr_ref[...] = (dmv * gr_).astype(BF16)
        dpa_ref[...] = (dmv * ga).astype(BF16)
        dz_ref[:, :D_MODEL] = (dmv * pr_ref[...] * gr_ * (1.0 - gr_)).astype(BF16)
        dz_ref[:, D_MODEL:] = (dmv * pa_ref[...] * ga * (1.0 - ga)).astype(BF16)

    tok = pl.BlockSpec((tr, D_MODEL), lambda i: (i, 0))
    mg = pl.BlockSpec((tr, 2 * D_MODEL), lambda i: (i, MG0 // (2 * D_MODEL)))
    return _call(body, name="merge_bwd", grid=(t // tr,),
                 in_specs=[tok, pl.BlockSpec((tr, D_MODEL), lambda i: (i, MG0R // D_MODEL)),
                           pl.BlockSpec((tr, D_MODEL), lambda i: (i, MG0R // D_MODEL + 1)), tok, tok],
                 out_specs=[mg, tok, tok],
                 out_shape=[jax.ShapeDtypeStruct((t, NZ), BF16), jax.ShapeDtypeStruct((t, D_MODEL), BF16),
                            jax.ShapeDtypeStruct((t, D_MODEL), BF16)],
                 compiler_params=_params())(dm, z2, z2, pr, pa)


def _fl_write(dfl_blk, dz):
    bsz, s, w = dfl_blk.shape

    def body(src_ref, dz_in, dz_ref):
        del dz_in
        dz_ref[...] = src_ref[...]

    return _call(body, name="fl_write", grid=(bsz,),
                 in_specs=[pl.BlockSpec((1, s, w), lambda b: (b, 0, 0)), ANY],
                 out_specs=pl.BlockSpec((1, s, w), lambda b: (b, 0, FL0 // w)),
                 out_shape=jax.ShapeDtypeStruct(dz.shape, dz.dtype), input_output_aliases={1: 0},
                 compiler_params=_params())(dfl_blk, dz)


def _ffn_fwd(zf, wc, bc):
    bsz, s, _ = zf.shape
    cw = 128

    def body(zf_ref, wc_ref, bc_ref, o_ref):
        gfp, uf = zf_ref[0, :, :cw].astype(F32), zf_ref[0, :, cw:].astype(F32)
        wcv = wc_ref[...]
        row = lax.broadcasted_iota(jnp.int32, (s, cw), 0)
        gf = bc_ref[...] + wcv[2:3] * gfp
        for j in (1, 2):
            gf = gf + wcv[2 - j:3 - j] * jnp.where(row >= j, pltpu.roll(gfp, j, 0), 0.0)
        o_ref[0] = (_gelu(gf) * uf).astype(BF16)

    return _call(body, name="ffn_fwd", grid=(bsz, D_FF // cw),
                 in_specs=[pl.BlockSpec((1, s, 2 * cw), lambda b, j: (b, 0, j)),
                           pl.BlockSpec((3, cw), lambda b, j: (0, j)), pl.BlockSpec((1, cw), lambda b, j: (0, j))],
                 out_specs=pl.BlockSpec((1, s, cw), lambda b, j: (b, 0, j)),
                 out_shape=jax.ShapeDtypeStruct((bsz, s, D_FF), BF16), compiler_params=_params())(zf, wc, bc)


def _ffn_bwd(dact, zf, wc, bc, exchange=None):
    bsz, s, _ = zf.shape
    cw = 128

    def body(da_ref, zf_ref, wc_ref, bc_ref, dzf_ref, dsm_ref):
        b = pl.program_id(1)
        gfp, uf = zf_ref[0, :, :cw].astype(F32), zf_ref[0, :, cw:].astype(F32)
        wcv = wc_ref[...]
        row = lax.broadcasted_iota(jnp.int32, (s, cw), 0)
        sh = [gfp] + [jnp.where(row >= j, pltpu.roll(gfp, j, 0), 0.0) for j in (1, 2)]
        gf = bc_ref[...]
        for kk in range(3):
            gf = gf + wcv[kk:kk + 1] * sh[2 - kk]
        ge, dge = _gelu_and_grad(gf)
        dav = da_ref[0].astype(F32)
        dgf = dav * uf * dge
        dgfp = wcv[2:3] * dgf
        for j in (1, 2):
            dgfp = dgfp + wcv[2 - j:3 - j] * jnp.where(row < s - j, pltpu.roll(dgf, s - j, 0), 0.0)
        dzf_ref[0] = jnp.concatenate([dgfp, dav * ge], axis=1).astype(BF16)
        rows = [jnp.sum(dgf * sh[2 - kk], axis=0, keepdims=True) for kk in range(3)]
        rows += [jnp.sum(dgf, axis=0, keepdims=True), jnp.zeros((4, cw), F32)]
        _acc(dsm_ref, slice(None), jnp.concatenate(rows, axis=0), b == 0)

    return _run(body, [dact, zf, wc, bc], exchange, name="ffn_bwd", grid=(D_FF // cw, bsz),
                in_specs=[pl.BlockSpec((1, s, cw), lambda j, b: (b, 0, j)),
                          pl.BlockSpec((1, s, 2 * cw), lambda j, b: (b, 0, j)),
                          pl.BlockSpec((3, cw), lambda j, b: (0, j)), pl.BlockSpec((1, cw), lambda j, b: (0, j))],
                out_specs=[pl.BlockSpec((1, s, 2 * cw), lambda j, b: (b, 0, j)),
                           pl.BlockSpec((8, cw), lambda j, b: (0, j))],
                out_shape=[jax.ShapeDtypeStruct(zf.shape, BF16), jax.ShapeDtypeStruct((8, D_FF), F32)],
                compiler_params=_params())


def _adamw(w, m, v, parts, *, name):
    r, c = w.shape
    p = parts.shape[0]
    tr = r if r <= 512 else 256
    assert r % tr == 0 and parts.shape[1:] == (r, c), (name, w.shape, parts.shape)
    c1 = 1.0 / (1.0 - ADAM_B1 ** ADAM_STEP)
    c2 = 1.0 / (1.0 - ADAM_B2 ** ADAM_STEP)

    def body(w_ref, m_ref, v_ref, p_ref, g_ref, d_ref, mo_ref, vo_ref):
        g = p_ref[0].astype(F32)
        for q in range(1, p):
            g = g + p_ref[q].astype(F32)
        mn = ADAM_B1 * m_ref[...] + (1.0 - ADAM_B1) * g
        vn = ADAM_B2 * v_ref[...] + (1.0 - ADAM_B2) * (g * g)
        g_ref[...] = g
        mo_ref[...] = mn
        vo_ref[...] = vn
        d_ref[...] = -ADAM_LR * ((mn * c1) / (jnp.sqrt(vn * c2) + ADAM_EPS) + ADAM_WD * w_ref[...])

    spec = pl.BlockSpec((tr, c), lambda i: (i, 0))
    shp = jax.ShapeDtypeStruct((r, c), F32)
    return _call(body, name=name, grid=(r // tr,),
                 in_specs=[spec, spec, spec, pl.BlockSpec((p, tr, c), lambda i: (0, i, 0))],
                 out_specs=[spec] * 4, out_shape=[shp] * 4, compiler_params=_params())(w, m, v, parts)


def _interleave(w, groups, width):
    n = w.shape[1] // (groups * width)
    return jnp.concatenate([w[:, (g * n + j) * width:(g * n + j + 1) * width]
                            for j in range(n) for g in range(groups)], axis=1)


def _deinterleave(w, groups, width):
    n = w.shape[1] // (groups * width)
    return jnp.concatenate([w[:, (j * groups + g) * width:(j * groups + g + 1) * width]
                            for g in range(groups) for j in range(n)], axis=1)


def _to_z_layout(w):
    xg = _interleave(w[:, :2 * D_RNN], 2, RNN_BLK)
    qkv = _interleave(w[:, 2 * D_RNN:2 * D_RNN + 3 * D_MODEL], 3, HEAD_DIM)
    fl0 = 2 * D_RNN + 3 * D_MODEL
    fl = w[:, fl0:fl0 + N_HEADS]
    mg = w[:, fl0 + N_HEADS:]
    pad = jnp.zeros((w.shape[0], MG0 - FL0 - N_HEADS), w.dtype)
    return jnp.concatenate([qkv, xg, fl, pad, mg], axis=1)


def _from_z_layout(w):
    qkv = _deinterleave(w[:, QKV0:XG0], 3, HEAD_DIM)
    xg = _deinterleave(w[:, XG0:FL0], 2, RNN_BLK)
    return jnp.concatenate([xg, qkv, w[:, FL0:FL0 + N_HEADS], w[:, MG0:]], axis=1)


def _col_gather(g):
    return g.transpose(1, 0, 2).reshape(g.shape[1], -1)


def _col_scatter(w):
    return w.reshape(w.shape[0], N_DEV, -1).transpose(1, 0, 2)


def kernel(x, c, w_ada, b_ada, g_norm1, w_in, w_rnn_conv, b_rnn_conv, w_lru_a, b_lru_a, w_lru_i, b_lru_i, lru_lambda, b_fgate, w_proj_rnn, w_proj_attn, w_out, g_norm2, w_ffn_up, w_ffn_conv, b_ffn_conv, w_ffn_down, w_ada_final, b_ada_final, g_final, loss_target, m_w_ada, m_b_ada, m_g_norm1, m_w_in, m_w_rnn_conv, m_b_rnn_conv, m_w_lru_a, m_b_lru_a, m_w_lru_i, m_b_lru_i, m_lru_lambda, m_b_fgate, m_w_proj_rnn, m_w_proj_attn, m_w_out, m_g_norm2, m_w_ffn_up, m_w_ffn_conv, m_b_ffn_conv, m_w_ffn_down, m_w_ada_final, m_b_ada_final, m_g_final, v_w_ada, v_b_ada, v_g_norm1, v_w_in, v_w_rnn_conv, v_b_rnn_conv, v_w_lru_a, v_b_lru_a, v_w_lru_i, v_b_lru_i, v_lru_lambda, v_b_fgate, v_w_proj_rnn, v_w_proj_attn, v_w_out, v_g_norm2, v_w_ffn_up, v_w_ffn_conv, v_b_ffn_conv, v_w_ffn_down, v_w_ada_final, v_b_ada_final, v_g_final):
    args = dict(locals())
    bsz, s, _ = x.shape
    t = bsz * s
    nb = N_DEV * bsz
    me = 4 * lax.axis_index("x") + 2 * lax.axis_index("y") + lax.axis_index("c")
    tm = min(2048, t)

    ex_a = _gather_two_level([c, w_in[0].astype(BF16), w_rnn_conv[0], w_ffn_conv[0]], "gather_first")
    c_all = ex_a[0].reshape(nb, D_MODEL)
    win_z = _to_z_layout(_col_gather(ex_a[1]))
    wrc = _col_gather(ex_a[2])
    wfc = _col_gather(ex_a[3])

    nmod = w_ada.shape[2]
    nmodf = w_ada_final.shape[1]
    mod_cols = _mm(c_all, w_ada[0], name="mod_cols", tm=nb, tn=nmod, tk=D_MODEL, silu_a=True,
                   bias=lax.dynamic_slice(b_ada, (0, me * nmod), (1, nmod)))
    modf_cols = _mm(c_all, w_ada_final, name="modf_cols", tm=nb, tn=nmodf, tk=D_MODEL, silu_a=True,
                    bias=lax.dynamic_slice(b_ada_final.reshape(1, -1), (0, me * nmodf), (1, nmodf)))
    ex_b = _exchange([mod_cols, modf_cols], [True, True], "gather_mod")
    mod = lax.dynamic_slice(ex_b[0], (0, me * bsz, 0), (N_DEV, bsz, nmod)).transpose(1, 0, 2).reshape(bsz, 6, 1, D_MODEL)
    modf = lax.dynamic_slice(ex_b[1], (0, me * bsz, 0), (N_DEV, bsz, nmodf)).transpose(1, 0, 2).reshape(bsz, 2, 1, D_MODEL)
    shift1, scale1, gate1, shift2, scale2, gate2 = [mod[:, i] for i in range(6)]
    shift_f, scale_f = modf[:, 0], modf[:, 1]

    h1 = _norm_fwd(x, g_norm1, shift1, scale1, name="norm1_fwd")[0]
    h1f = h1.reshape(t, D_MODEL)
    zq = _mm(h1f, win_z, name="mm_in_qkv", out_dtype=BF16, tm=tm, tn=1024, tk=D_MODEL,
             b_cols=(QKV0, XG0)).reshape(bsz, s, XG0)
    zr = _mm(h1f, win_z, name="mm_in_rest", tm=tm, tn=1024, tk=D_MODEL, b_cols=(XG0, NZR)).reshape(bsz, s, NZR)
    h_rnn, y_rnn, got = _rnn_fwd(zr, wrc, b_rnn_conv, w_lru_a[0], b_lru_a, w_lru_i[0], b_lru_i, lru_lambda,
                                 exchange=([w_ffn_up[0].astype(BF16)], ["gather_zf"]))
    wup_z = got[0]
    flt = zr[:, :, FL0R:FL0R + N_HEADS].transpose(0, 2, 1)
    bfg = b_fgate.reshape(N_HEADS, 1)
    fcum = _fgate_fwd(flt, bfg)
    fcol = fcum.reshape(bsz, N_HEADS, s, 1)
    frow = fcum.reshape(bsz, N_HEADS, 1, s)
    o_attn, lse, got = _attn_fwd(
        zq, fcol, frow, exchange=([w_proj_rnn[0].astype(BF16), w_proj_attn[0].astype(BF16), w_out[0].astype(BF16),
                                  w_ffn_down[0].astype(BF16)], [True] * 4))
    wpr = got[0].reshape(D_RNN, D_MODEL)
    wpa = got[1].reshape(D_MODEL, D_MODEL)
    wout = got[2].reshape(D_MODEL, D_MODEL)
    wdown = got[3].reshape(D_FF, D_MODEL)
    pr = _mm(y_rnn.reshape(t, D_RNN), wpr, name="mm_proj_rnn", tm=tm, tn=D_MODEL, tk=D_RNN)
    pa = _mm(o_attn.reshape(t, D_MODEL), wpa, name="mm_proj_attn", tm=tm, tn=D_MODEL, tk=D_MODEL)
    z2 = zr.reshape(t, NZR)
    merged = _merge_fwd(z2, pr, pa)
    mo = _mm(merged, wout, name="mm_out", tm=tm, tn=D_MODEL, tk=D_MODEL).reshape(bsz, s, D_MODEL)
    h2, x1 = _norm_fwd(x, g_norm2, shift2, scale2, name="norm2_fwd", yprev=mo, gate=gate1)
    zf = _mm(h2.reshape(t, D_MODEL), wup_z, name="mm_up", out_dtype=BF16, tm=tm, tn=1024,
             tk=D_MODEL).reshape(bsz, s, 2 * D_FF)
    act = _ffn_fwd(zf, wfc, b_ffn_conv)
    yf = _mm(act.reshape(t, D_FF), wdown, name="mm_down", tm=tm, tn=D_MODEL, tk=D_FF // 2).reshape(bsz, s, D_MODEL)
    lp, dx2, dyf, dshift_f, dscale_f, dgate2, dg_final = _final(x1, yf, gate2, g_final.reshape(1, -1), shift_f, scale_f, loss_target)
    loss = lax.psum(jnp.sum(lp[:, :, 0, 0]) * (0.5 / D_MODEL), ("x", "y", "c"))

    dyf2 = dyf.reshape(t, D_MODEL)
    act2 = act.reshape(t, D_FF)
    dact = _mm(dyf2, wdown, name="mm_down_dx", tb=True, out_dtype=BF16, tm=tm, tn=1024, tk=D_MODEL).reshape(bsz, s, D_FF)
    g_wdown = _mm(act2, dyf2, name="mm_down_dw", ta=True, out_dtype=BF16, tm=1024, tn=D_MODEL, tk=tm)
    dzf, dsm_ffn, got = _ffn_bwd(dact, zf, wfc, b_ffn_conv, exchange=([g_wdown.reshape(N_DEV, -1, D_MODEL)], [False]))
    p_wdown = got[0]
    dzf2 = dzf.reshape(t, 2 * D_FF)
    dh2 = _mm(dzf2, wup_z, name="mm_up_dx", tb=True, tm=tm, tn=D_MODEL, tk=2048).reshape(bsz, s, D_MODEL)
    g_wup_z = _mm(h2.reshape(t, D_MODEL), dzf2, name="mm_up_dw", ta=True, out_dtype=BF16, tm=D_MODEL, tn=1024, tk=tm)
    dx1, dg_norm2, dshift2, dscale2, dmo, dgate1 = _norm_bwd(x1, dh2, g_norm2, scale2, dx2, name="norm2_bwd", yprev=mo, gate=gate1)
    dmo2 = dmo.reshape(t, D_MODEL)
    dmerged = _mm(dmo2, wout, name="mm_out_dx", tb=True, tm=tm, tn=D_MODEL, tk=D_MODEL)
    g_wout = _mm(merged, dmo2, name="mm_out_dw", ta=True, out_dtype=BF16, tm=D_MODEL, tn=D_MODEL, tk=tm)
    dz2, dpr, dpa = _merge_bwd(dmerged, z2, pr, pa)
    dy_rnn = _mm(dpr, wpr, name="mm_proj_rnn_dx", tb=True, tm=tm, tn=D_RNN, tk=D_MODEL).reshape(bsz, s, D_RNN)
    g_wpr = _mm(y_rnn.reshape(t, D_RNN), dpr, name="mm_proj_rnn_dw", ta=True, out_dtype=BF16, tm=D_RNN, tn=D_MODEL, tk=tm)
    do = _mm(dpa, wpa, name="mm_proj_attn_dx", tb=True, out_dtype=BF16, tm=tm, tn=D_MODEL, tk=D_MODEL).reshape(bsz, s, D_MODEL)
    g_wpa = _mm(o_attn.reshape(t, D_MODEL), dpa, name="mm_proj_attn_dw", ta=True, out_dtype=BF16, tm=D_MODEL, tn=D_MODEL, tk=tm)
    dz, dfrow, got = _attn_bwd(zq, do, lse, fcol, frow, dz2.reshape(bsz, s, NZ),
                               exchange=([g_wup_z, g_wout.reshape(N_DEV, -1, D_MODEL)], ["scatter_zf", False]))
    p_wup, p_wout = got
    dflt, db_fgate = _fgate_bwd(dfrow.reshape(bsz, N_HEADS, s), flt, bfg)
    dfl_blk = jnp.pad(dflt.transpose(0, 2, 1).astype(BF16), ((0, 0), (0, 0), (0, MG0 - FL0 - N_HEADS)))
    dz = _fl_write(dfl_blk, dz)
    dz, g_wa, g_wi, dsm_rnn, got = _rnn_bwd(
        zr, h_rnn, dy_rnn, dz, wrc, b_rnn_conv, w_lru_a[0], b_lru_a, w_lru_i[0], b_lru_i, lru_lambda,
        exchange=([g_wpr.reshape(N_DEV, -1, D_MODEL), g_wpa.reshape(N_DEV, -1, D_MODEL)], [False, False]))
    p_wpr, p_wpa = got
    dz2 = dz.reshape(t, NZ)
    small = [dg_norm2, dg_final, dsm_rnn, g_wa, g_wi, db_fgate, dsm_ffn]
    small_sizes = [a.size for a in small]
    n_pad = -sum(small_sizes) % 1024
    packed = jnp.concatenate([a.reshape(-1) for a in small] + [jnp.zeros((n_pad,), F32)]).reshape(-1, 128)
    g_win_z, got = _mm(h1f, dz2, name="mm_in_dw", ta=True, out_dtype=BF16, tm=D_MODEL, tn=1024, tk=tm,
                       exchange=([packed], [True]))
    flat = got[0].reshape(N_DEV, -1)
    dh1, got = _mm(dz2, win_z, name="mm_in_dx", tb=True, tm=tm, tn=D_MODEL, tk=2048,
                   exchange=([_col_scatter(_from_z_layout(g_win_z))], [False]))
    p_win = got[0]
    grad_x, dg_norm1, dshift1, dscale1 = _norm_bwd(x, dh1.reshape(bsz, s, D_MODEL), g_norm1, scale1, dx1, name="norm1_bwd")

    dmod = jnp.concatenate([dshift1, dscale1, dgate1, dshift2, dscale2, dgate2], axis=1).reshape(bsz, 6 * D_MODEL)
    dmodf = jnp.concatenate([dshift_f, dscale_f], axis=1).reshape(bsz, 2 * D_MODEL)
    ex_c = _exchange([dmod, dmodf, dg_norm1], [True, True, True], "gather_last_grads")
    dmod_all = ex_c[0].reshape(nb, 6 * D_MODEL)
    dmodf_all = ex_c[1].reshape(nb, 2 * D_MODEL)
    p_g1 = ex_c[2]

    res = {}

    def upd(name, parts, shape2):
        w, m, v = args[name], args["m_" + name], args["v_" + name]
        outs = _adamw(w.reshape(shape2), m.reshape(shape2), v.reshape(shape2), parts, name="adamw_" + name)
        res[name] = [o.reshape(w.shape) for o in outs]

    g_wada = _mm(c_all, lax.dynamic_slice(dmod_all, (0, me * nmod), (nb, nmod)), name="mm_ada_dw", ta=True,
                 silu_a=True, tm=D_MODEL, tn=nmod, tk=nb)
    g_wadaf = _mm(c_all, lax.dynamic_slice(dmodf_all, (0, me * nmodf), (nb, nmodf)), name="mm_adaf_dw", ta=True,
                  silu_a=True, tm=D_MODEL, tn=nmodf, tk=nb)
    upd("w_ada", g_wada[None], (D_MODEL, nmod))
    upd("w_ada_final", g_wadaf[None], (D_MODEL, nmodf))
    upd("b_ada", dmod_all.reshape(nb, 1, -1), (1, 6 * D_MODEL))
    upd("b_ada_final", dmodf_all.reshape(nb, 1, -1), (1, 2 * D_MODEL))
    upd("w_in", p_win, w_in.shape[1:])
    upd("w_proj_rnn", p_wpr, w_proj_rnn.shape[1:])
    upd("w_proj_attn", p_wpa, w_proj_attn.shape[1:])
    upd("w_out", p_wout, w_out.shape[1:])
    upd("w_ffn_up", p_wup, w_ffn_up.shape[1:])
    upd("w_ffn_down", p_wdown, w_ffn_down.shape[1:])

    parts, off = [], 0
    for a, n in zip(small, small_sizes):
        parts.append(flat[:, off:off + n].reshape((N_DEV,) + a.shape))
        off += n
    p_g2, p_gf, p_rnn, p_wa, p_wi, p_bfg, p_ffn = parts
    upd("g_norm1", p_g1, (1, D_MODEL))
    upd("g_norm2", p_g2, (1, D_MODEL))
    upd("g_final", p_gf, (1, D_MODEL))
    nrc = w_rnn_conv.shape[2]
    upd("w_rnn_conv", lax.dynamic_slice(p_rnn, (0, 0, me * nrc), (N_DEV, 4, nrc)), (4, nrc))
    upd("b_rnn_conv", p_rnn[:, 4:5], (1, D_RNN))
    upd("b_lru_a", p_rnn[:, 5:6], (1, D_RNN))
    upd("b_lru_i", p_rnn[:, 6:7], (1, D_RNN))
    upd("lru_lambda", p_rnn[:, 7:8], (1, D_RNN))
    upd("w_lru_a", p_wa.reshape(N_DEV, D_RNN, RNN_BLK), (D_RNN, RNN_BLK))
    upd("w_lru_i", p_wi.reshape(N_DEV, D_RNN, RNN_BLK), (D_RNN, RNN_BLK))
    upd("b_fgate", p_bfg.reshape(N_DEV, 1, N_HEADS), (1, N_HEADS))
    nfc = w_ffn_conv.shape[2]
    upd("w_ffn_conv", lax.dynamic_slice(p_ffn, (0, 0, me * nfc), (N_DEV, 3, nfc)), (3, nfc))
    upd("b_ffn_conv", p_ffn[:, 3:4], (1, D_FF))

    names = ['w_ada', 'b_ada', 'g_norm1', 'w_in', 'w_rnn_conv', 'b_rnn_conv', 'w_lru_a', 'b_lru_a', 'w_lru_i',
             'b_lru_i', 'lru_lambda', 'b_fgate', 'w_proj_rnn', 'w_proj_attn', 'w_out', 'g_norm2', 'w_ffn_up',
             'w_ffn_conv', 'b_ffn_conv', 'w_ffn_down', 'w_ada_final', 'b_ada_final', 'g_final']
    return (loss, grad_x, *[res[n][0] for n in names], *[res[n][1] for n in names],
            *[res[n][2] for n in names], *[res[n][3] for n in names])
```

```python
import jax
import jax.numpy as jnp
from jax import lax
from jax.experimental import pallas as pl
from jax.experimental.pallas import tpu as pltpu

F32, BF16 = jnp.float32, jnp.bfloat16
D_MODEL, D_RNN, N_BLK, RNN_BLK = 1024, 1280, 10, 128
N_HEADS, HEAD_DIM, D_FF = 8, 128, 3072
N_DEV = 8
NZ = 8192
QKV0, XG0, FL0, MG0 = 0, 3072, 5632, 6144
NZR = NZ - XG0
XG0R, FL0R, MG0R = 0, FL0 - XG0, MG0 - XG0
RMS_EPS = 1e-6
LRU_C = 8.0
ADAM_LR, ADAM_B1, ADAM_B2, ADAM_EPS, ADAM_WD, ADAM_STEP = 0.001, 0.9, 0.999, 1e-08, 0.01, 10
VMEM_LIMIT = 56 << 20
SCAN_CHUNK = 256
ATTN_TQ = 256
CUM_BLK = 512
MESH = pl.DeviceIdType.MESH
ANY = pl.BlockSpec(memory_space=pl.ANY)


def _call(body, **kw):
    return pl.pallas_call(body, **kw)


def _params():
    return pltpu.CompilerParams(vmem_limit_bytes=VMEM_LIMIT)


ZF_PIECES = 2 * D_FF // 128 // N_DEV


def _zf_cols(owner, p):
    ob = owner * ZF_PIECES + p
    blk = 2 * (ob % (D_FF // 128)) + ob // (D_FF // 128)
    return pl.ds(pl.multiple_of(blk * 128, 128), 128)


def _n_pieces(mode):
    return ZF_PIECES if mode in ("gather_zf", "scatter_zf") else 1


def _exchanged_shape(a, mode):
    if mode is True:
        return (N_DEV,) + a.shape
    if mode is False:
        return a.shape
    if mode == "gather_zf":
        return (a.shape[0], N_DEV * a.shape[1])
    return (N_DEV, a.shape[0], a.shape[1] // N_DEV)


def _direct_copies(ins, outs, sems, modes):
    send_sems, recv_sems, loc_sems = sems
    x, y, c = lax.axis_index("x"), lax.axis_index("y"), lax.axis_index("c")
    me = 4 * x + 2 * y + c

    def ends(a, to, p):
        mode = modes[a]
        if mode is True:
            return ins[a], outs[a].at[me]
        if mode is False:
            return ins[a].at[to], outs[a].at[me]
        mine = slice(p * 128, (p + 1) * 128)
        if mode == "gather_zf":
            return ins[a].at[:, mine], outs[a].at[:, _zf_cols(me, p)]
        return ins[a].at[:, _zf_cols(to, p)], outs[a].at[me, :, mine]

    local, remote = [], []
    nloc = 0
    for a in range(len(ins)):
        for p in range(_n_pieces(modes[a])):
            src, dst = ends(a, me, p)
            local.append(pltpu.make_async_copy(src, dst, loc_sems.at[nloc]))
            nloc += 1
    for k in range(1, N_DEV):
        px = 1 - x if k & 4 else x
        py = 1 - y if k & 2 else y
        pc = 1 - c if k & 1 else c
        peer = 4 * px + 2 * py + pc
        nrem = (k - 1) * nloc
        for a in range(len(ins)):
            for p in range(_n_pieces(modes[a])):
                src, dst = ends(a, peer, p)
                remote.append(pltpu.make_async_remote_copy(
                    src_ref=src, dst_ref=dst, send_sem=send_sems.at[nrem], recv_sem=recv_sems.at[nrem],
                    device_id=(px, py, pc), device_id_type=MESH))
                nrem += 1
    return local, remote


def _with_exchange(body, args, arrs, bcast, *, name, grid, in_specs, out_specs, out_shape, scratch_shapes=(), **kw):
    n_in, n_out, n_sc, nc = len(args), len(out_shape), len(scratch_shapes), len(arrs)

    def wrapped(*refs):
        core_in, cin = refs[:n_in], refs[n_in:n_in + nc]
        core_out = refs[n_in + nc:n_in + nc + n_out]
        cout = refs[n_in + nc + n_out:n_in + 2 * nc + n_out]
        rest = refs[n_in + 2 * nc + n_out:]
        core_sc, sems = rest[:n_sc], rest[n_sc:]

        def start():
            local, remote = _direct_copies(cin, cout, sems, bcast)
            for cp in local + remote:
                cp.start()

        def wait():
            local, remote = _direct_copies(cin, cout, sems, bcast)
            for cp in remote + local:
                cp.wait()

        if not grid:
            start()
            body(*core_in, *core_out, *core_sc)
            wait()
            return
        first, last = None, None
        for d, g in enumerate(grid):
            i = pl.program_id(d)
            first = (i == 0) if first is None else jnp.logical_and(first, i == 0)
            last = (i == g - 1) if last is None else jnp.logical_and(last, i == g - 1)
        pl.when(first)(start)
        body(*core_in, *core_out, *core_sc)
        pl.when(last)(wait)

    ex_shape = [jax.ShapeDtypeStruct(_exchanged_shape(a, mode), a.dtype) for a, mode in zip(arrs, bcast)]
    ncopies = sum(_n_pieces(mode) for mode in bcast)
    sems = [pltpu.SemaphoreType.DMA((ncopies * (N_DEV - 1),)), pltpu.SemaphoreType.DMA((ncopies * (N_DEV - 1),)),
            pltpu.SemaphoreType.DMA((ncopies,))]
    res = _call(wrapped, name=name, grid=grid, in_specs=list(in_specs) + [ANY] * nc,
                out_specs=list(out_specs) + [ANY] * nc, out_shape=list(out_shape) + ex_shape,
                scratch_shapes=list(scratch_shapes) + sems, **kw)(*args, *arrs)
    return list(res[:n_out]), list(res[n_out:])


def _run(body, args, exchange, **kw):
    if exchange is None:
        return list(_call(body, **kw)(*args))
    outs, got = _with_exchange(body, args, *exchange, **kw)
    return outs + [got]


def _exchange(arrs, bcast, name):
    return _with_exchange(lambda: None, [], arrs, bcast, name=name, grid=(), in_specs=[], out_specs=[], out_shape=[],
                          compiler_params=pltpu.CompilerParams(has_side_effects=True))[1]


def _gather_two_level(arrs, name):
    n = len(arrs)

    def body(*refs):
        ins, outs = refs[:n], refs[n:2 * n]
        send_sems, recv_sems, loc_sems = refs[2 * n:]
        x, y, c = lax.axis_index("x"), lax.axis_index("y"), lax.axis_index("c")
        me, sibling = (x, y, c), (x, y, 1 - c)
        chips = [(1 - x, y), (x, 1 - y), (1 - x, 1 - y)]

        def slot(a, p):
            return outs[a].at[4 * p[0] + 2 * p[1] + p[2]]

        def copy(a, k, block, to, src=None):
            return pltpu.make_async_remote_copy(
                src_ref=slot(a, block) if src is None else src, dst_ref=slot(a, block),
                send_sem=send_sems.at[a * 7 + k], recv_sem=recv_sems.at[a * 7 + k],
                device_id=to, device_id_type=MESH)

        mine = [pltpu.make_async_copy(ins[a], slot(a, me), loc_sems.at[a]) for a in range(n)]
        first = []
        for a in range(n):
            first.append(copy(a, 0, me, sibling, src=ins[a]))
            first += [copy(a, 1 + j, me, (*chip, c), src=ins[a]) for j, chip in enumerate(chips)]
        for cp in mine + first:
            cp.start()
        passed = []
        for j, chip in enumerate(chips):
            for a in range(n):
                copy(a, 1 + j, (*chip, c), me).wait_recv()
                fwd = copy(a, 4 + j, (*chip, c), sibling)
                fwd.start()
                passed.append(fwd)
        for a in range(n):
            copy(a, 0, sibling, me).wait_recv()
        for j, chip in enumerate(chips):
            for a in range(n):
                copy(a, 4 + j, (*chip, 1 - c), me).wait_recv()
        for cp in first + passed:
            cp.wait_send()
        for cp in mine:
            cp.wait()

    return _call(
        body, name=name, in_specs=[ANY] * n, out_specs=[ANY] * n,
        out_shape=[jax.ShapeDtypeStruct((N_DEV,) + a.shape, a.dtype) for a in arrs],
        scratch_shapes=[pltpu.SemaphoreType.DMA((n * 7,)), pltpu.SemaphoreType.DMA((n * 7,)),
                        pltpu.SemaphoreType.DMA((n,))],
        compiler_params=pltpu.CompilerParams(has_side_effects=True),
    )(*arrs)


def _sigmoid(x):
    return 1.0 / (1.0 + jnp.exp(-x))


def _log1p(e):
    u = 1.0 + e
    d = u - 1.0
    return jnp.where(d == 0.0, e, jnp.log(u) * (e / jnp.where(d == 0.0, 1.0, d)))


def _softplus(x):
    return jnp.maximum(x, 0.0) + _log1p(jnp.exp(-jnp.abs(x)))


_GELU_C = 0.7978845608028654


def _gelu(x):
    t = jnp.tanh(_GELU_C * (x + 0.044715 * x * x * x))
    return 0.5 * x * (1.0 + t)


def _gelu_and_grad(x):
    t = jnp.tanh(_GELU_C * (x + 0.044715 * x * x * x))
    g = 0.5 * x * (1.0 + t)
    dg = 0.5 * (1.0 + t) + 0.5 * x * (1.0 - t * t) * _GELU_C * (1.0 + 3.0 * 0.044715 * x * x)
    return g, dg


def _acc(ref, idx, val, first):
    @pl.when(first)
    def _():
        ref[idx] = val

    @pl.when(jnp.logical_not(first))
    def _():
        ref[idx] = ref[idx] + val


def _split3(x):
    hi = x.astype(BF16)
    r1 = x - hi.astype(F32)
    mid = r1.astype(BF16)
    lo = (r1 - mid.astype(F32)).astype(BF16)
    return hi, mid, lo


def _mm(a, b, *, name, ta=False, tb=False, out_dtype=F32, tm, tn, tk, bias=None, silu_a=False, exchange=None,
        b_cols=None):
    m, k = (a.shape[1], a.shape[0]) if ta else a.shape
    col0, n = b_cols if b_cols is not None else (0, b.shape[0] if tb else b.shape[1])
    assert m % tm == 0 and n % tn == 0 and k % tk == 0 and col0 % tn == 0, (name, m, n, k)
    jo = col0 // tn
    nk = k // tk
    dn = (((0 if ta else 1,), (1 if tb else 0,)), ((), ()))
    use_acc = nk > 1 and out_dtype != F32

    def body(*refs):
        a_ref, b_ref = refs[0], refs[1]
        bias_ref = refs[2] if bias is not None else None
        o_ref = refs[3 if bias is not None else 2]
        av = a_ref[...]
        if silu_a:
            av = av * _sigmoid(av)
        p = lax.dot_general(av.astype(BF16), b_ref[...].astype(BF16), dn, preferred_element_type=F32)
        if bias is not None:
            p = p + bias_ref[...]
        if nk == 1:
            o_ref[...] = p.astype(out_dtype)
            return
        kk = pl.program_id(2)
        acc = refs[-1] if use_acc else o_ref

        @pl.when(kk == 0)
        def _():
            acc[...] = p

        @pl.when(kk > 0)
        def _():
            acc[...] = acc[...] + p

        if use_acc:
            @pl.when(kk == nk - 1)
            def _():
                o_ref[...] = acc[...].astype(out_dtype)

    a_spec = pl.BlockSpec((tk, tm), lambda i, j, q: (q, i)) if ta else pl.BlockSpec((tm, tk), lambda i, j, q: (i, q))
    b_spec = (pl.BlockSpec((tn, tk), lambda i, j, q: (j + jo, q)) if tb
              else pl.BlockSpec((tk, tn), lambda i, j, q: (q, j + jo)))
    in_specs, args = [a_spec, b_spec], [a, b]
    if bias is not None:
        in_specs.append(pl.BlockSpec((1, tn), lambda i, j, q: (0, j)))
        args.append(bias)
    kw = dict(name=name, grid=(m // tm, n // tn, nk), in_specs=in_specs,
              out_specs=[pl.BlockSpec((tm, tn), lambda i, j, q: (i, j))],
              out_shape=[jax.ShapeDtypeStruct((m, n), out_dtype)],
              scratch_shapes=[pltpu.VMEM((tm, tn), F32)] if use_acc else [],
              compiler_params=_params())
    if exchange is None:
        return _call(body, **kw)(*args)[0]
    outs, got = _with_exchange(body, args, *exchange, **kw)
    return outs[0], got


def _tok_spec(ts):
    return pl.BlockSpec((1, ts, D_MODEL), lambda b, s: (b, s, 0))


def _seq_spec():
    return pl.BlockSpec((1, 1, D_MODEL), lambda b, s: (b, 0, 0))


def _vec_spec():
    return pl.BlockSpec((1, D_MODEL), lambda b, s: (0, 0))


def _norm_fwd(x, g, shift, scale, *, name, yprev=None, gate=None, ts=512):
    bsz, s, _ = x.shape
    res = yprev is not None

    def body(*refs):
        if res:
            x_ref, g_ref, sh_ref, sc_ref, y_ref, gt_ref, h_ref, xn_ref = refs
            xv = x_ref[0] + gt_ref[0] * y_ref[0]
            xn_ref[0] = xv
        else:
            x_ref, g_ref, sh_ref, sc_ref, h_ref = refs
            xv = x_ref[0]
        rstd = lax.rsqrt(jnp.mean(xv * xv, axis=-1, keepdims=True) + RMS_EPS)
        nv = xv * rstd * g_ref[...]
        h_ref[0] = (nv * (1.0 + sc_ref[0]) + sh_ref[0]).astype(BF16)

    in_specs = [_tok_spec(ts), _vec_spec(), _seq_spec(), _seq_spec()]
    args = [x, g, shift, scale]
    out_specs = [_tok_spec(ts)]
    out_shape = [jax.ShapeDtypeStruct(x.shape, BF16)]
    if res:
        in_specs += [_tok_spec(ts), _seq_spec()]
        args += [yprev, gate]
        out_specs.append(_tok_spec(ts))
        out_shape.append(jax.ShapeDtypeStruct(x.shape, F32))
    return _call(body, name=name, grid=(bsz, s // ts), in_specs=in_specs, out_specs=out_specs,
                 out_shape=out_shape, compiler_params=_params())(*args)


def _final(x1, yf, gate2, g, shift, scale, target, *, ts=512):
    bsz, s, _ = x1.shape
    ns = s // ts

    def body(x_ref, y_ref, gt_ref, g_ref, sh_ref, sc_ref, t_ref,
             lp_ref, dx_ref, dyf_ref, dsh_ref, dsc_ref, dgt_ref, dg_ref):
        b, si = pl.program_id(0), pl.program_id(1)
        yv, gt, gv = y_ref[0], gt_ref[0], g_ref[...]
        xv = x_ref[0] + gt * yv
        rstd = lax.rsqrt(jnp.mean(xv * xv, axis=-1, keepdims=True) + RMS_EPS)
        xhat = xv * rstd
        nv = xhat * gv
        sc1 = 1.0 + sc_ref[0]
        err = nv * sc1 + sh_ref[0] - t_ref[0]
        lp_ref[...] = jnp.zeros((1, 1, 8, 128), F32) + jnp.sum(err * err)
        dy = err * (1.0 / D_MODEL)
        dn = dy * sc1
        dxh = dn * gv
        dx = rstd * (dxh - xhat * jnp.mean(dxh * xhat, axis=-1, keepdims=True))
        dx_ref[0] = dx
        dyf_ref[0] = (dx * gt).astype(BF16)
        _acc(dsh_ref, 0, jnp.sum(dy, axis=0, keepdims=True), si == 0)
        _acc(dsc_ref, 0, jnp.sum(dy * nv, axis=0, keepdims=True), si == 0)
        _acc(dgt_ref, 0, jnp.sum(dx * yv, axis=0, keepdims=True), si == 0)
        _acc(dg_ref, slice(None), jnp.sum(dn * xhat, axis=0, keepdims=True), jnp.logical_and(b == 0, si == 0))

    seq_shape = jax.ShapeDtypeStruct((bsz, 1, D_MODEL), F32)
    return _call(
        body, name="final_loss", grid=(bsz, ns),
        in_specs=[_tok_spec(ts), _tok_spec(ts), _seq_spec(), _vec_spec(), _seq_spec(), _seq_spec(), _tok_spec(ts)],
        out_specs=[pl.BlockSpec((1, 1, 8, 128), lambda b, s: (b, s, 0, 0)), _tok_spec(ts), _tok_spec(ts),
                   _seq_spec(), _seq_spec(), _seq_spec(), _vec_spec()],
        out_shape=[jax.ShapeDtypeStruct((bsz, ns, 8, 128), F32), jax.ShapeDtypeStruct(x1.shape, F32),
                   jax.ShapeDtypeStruct(x1.shape, BF16), seq_shape, seq_shape, seq_shape,
                   jax.ShapeDtypeStruct((1, D_MODEL), F32)],
        compiler_params=_params(),
    )(x1, yf, gate2, g, shift, scale, target)


def _norm_bwd(x, dh, g, scale, dres, *, name, yprev=None, gate=None, ts=512):
    bsz, s, _ = x.shape
    res = yprev is not None

    def body(*refs):
        if res:
            (x_ref, dh_ref, g_ref, sc_ref, dr_ref, y_ref, gt_ref,
             dx_ref, dg_ref, dsh_ref, dsc_ref, dy_ref, dgt_ref) = refs
        else:
            x_ref, dh_ref, g_ref, sc_ref, dr_ref, dx_ref, dg_ref, dsh_ref, dsc_ref = refs
        b, si = pl.program_id(0), pl.program_id(1)
        xv, dhv, gv = x_ref[0], dh_ref[0], g_ref[...]
        rstd = lax.rsqrt(jnp.mean(xv * xv, axis=-1, keepdims=True) + RMS_EPS)
        xhat = xv * rstd
        dn = dhv * (1.0 + sc_ref[0])
        dxh = dn * gv
        dx = dr_ref[0] + rstd * (dxh - xhat * jnp.mean(dxh * xhat, axis=-1, keepdims=True))
        dx_ref[0] = dx
        _acc(dsh_ref, 0, jnp.sum(dhv, axis=0, keepdims=True), si == 0)
        _acc(dsc_ref, 0, jnp.sum(dhv * (xhat * gv), axis=0, keepdims=True), si == 0)
        _acc(dg_ref, slice(None), jnp.sum(dn * xhat, axis=0, keepdims=True), jnp.logical_and(b == 0, si == 0))
        if res:
            dy_ref[0] = (dx * gt_ref[0]).astype(BF16)
            _acc(dgt_ref, 0, jnp.sum(dx * y_ref[0], axis=0, keepdims=True), si == 0)

    seq_shape = jax.ShapeDtypeStruct((bsz, 1, D_MODEL), F32)
    in_specs = [_tok_spec(ts), _tok_spec(ts), _vec_spec(), _seq_spec(), _tok_spec(ts)]
    args = [x, dh, g, scale, dres]
    out_specs = [_tok_spec(ts), _vec_spec(), _seq_spec(), _seq_spec()]
    out_shape = [jax.ShapeDtypeStruct(x.shape, F32), jax.ShapeDtypeStruct((1, D_MODEL), F32), seq_shape, seq_shape]
    if res:
        in_specs += [_tok_spec(ts), _seq_spec()]
        args += [yprev, gate]
        out_specs += [_tok_spec(ts), _seq_spec()]
        out_shape += [jax.ShapeDtypeStruct(x.shape, BF16), seq_shape]
    return _call(body, name=name, grid=(bsz, s // ts), in_specs=in_specs, out_specs=out_specs,
                 out_shape=out_shape, compiler_params=_params())(*args)


def _lru_gates(xr, wa, ba, wi, bi, sp):
    xb = xr.astype(BF16)
    r = _sigmoid(jnp.dot(xb, wa, preferred_element_type=F32) + ba)
    i = _sigmoid(jnp.dot(xb, wi, preferred_element_type=F32) + bi)
    la = -LRU_C * r * sp
    a = jnp.exp(la)
    mult = jnp.sqrt(-jnp.tanh(la) * (a * a + 1.0))
    return r, i, a, mult


def _scan_forward(a, u, h_in, row):
    n = a.shape[0]
    r8 = row & 7
    for k in (1, 2, 4):
        m = r8 >= k
        u = jnp.where(m, a * pltpu.roll(u, k, 0) + u, u)
        a = jnp.where(m, a * pltpu.roll(a, k, 0), a)
    carry = jnp.broadcast_to(h_in, (8, a.shape[1]))
    out = []
    for j in range(n // 8):
        hj = u[8 * j:8 * j + 8] + a[8 * j:8 * j + 8] * carry
        out.append(hj)
        carry = jnp.broadcast_to(hj[7:8], hj.shape)
    return jnp.concatenate(out, axis=0)


def _scan_backward(a_up, g, g_in, row):
    n = g.shape[0]
    r8 = row & 7
    for k in (1, 2, 4):
        m = r8 < 8 - k
        g = jnp.where(m, a_up * pltpu.roll(g, n - k, 0) + g, g)
        a_up = jnp.where(m, a_up * pltpu.roll(a_up, n - k, 0), a_up)
    carry = jnp.broadcast_to(g_in, (8, g.shape[1]))
    out = [None] * (n // 8)
    for j in reversed(range(n // 8)):
        gj = g[8 * j:8 * j + 8] + a_up[8 * j:8 * j + 8] * carry
        out[j] = gj
        carry = jnp.broadcast_to(gj[0:1], gj.shape)
    return jnp.concatenate(out, axis=0)


def _rnn_fwd(z, wc, bc, wa, ba, wi, bi, lam, exchange=None):
    bsz, s, _ = z.shape
    ch = min(SCAN_CHUNK, s)
    nch = s // ch

    def body(z_ref, wc_ref, bc_ref, wa_ref, ba_ref, wi_ref, bi_ref, lam_ref, h_ref, y_ref):
        wcv, bcv = wc_ref[...], bc_ref[...]
        wav, wiv = wa_ref[0].astype(BF16), wi_ref[0].astype(BF16)
        bav, biv = ba_ref[...], bi_ref[...]
        sp = _softplus(-lam_ref[...])
        row = lax.broadcasted_iota(jnp.int32, (ch, RNN_BLK), 0)

        def chunk(ci, carry):
            prev8, hc = carry
            st = pl.multiple_of(ci * ch, ch)
            blk = z_ref[0, pl.ds(st, ch), :]
            xp, gr = blk[:, :RNN_BLK], blk[:, RNN_BLK:]
            xe = jnp.concatenate([prev8, xp], axis=0)
            xr = bcv + wcv[3:4] * xp
            for j in range(1, 4):
                xr = xr + wcv[3 - j:4 - j] * pltpu.roll(xe, j, 0)[8:]
            _, i, a, mult = _lru_gates(xr, wav, bav, wiv, biv, sp)
            hh = _scan_forward(a, mult * (i * xr), hc, row)
            h_ref[0, pl.ds(st, ch), :] = hh
            y_ref[0, pl.ds(st, ch), :] = (_gelu(gr) * hh).astype(BF16)
            return xp[ch - 8:], hh[ch - 1:]

        lax.fori_loop(0, nch, chunk, (jnp.zeros((8, RNN_BLK), F32), jnp.zeros((1, RNN_BLK), F32)))

    vec = pl.BlockSpec((1, RNN_BLK), lambda b, n: (0, n))
    mat = pl.BlockSpec((1, RNN_BLK, RNN_BLK), lambda b, n: (n, 0, 0))
    out = pl.BlockSpec((1, s, RNN_BLK), lambda b, n: (b, 0, n))
    return _run(
        body, [z, wc, bc, wa, ba, wi, bi, lam], exchange, name="rnn_fwd", grid=(bsz, N_BLK),
        in_specs=[pl.BlockSpec((1, s, 2 * RNN_BLK), lambda b, n: (b, 0, XG0R // (2 * RNN_BLK) + n)),
                  pl.BlockSpec((4, RNN_BLK), lambda b, n: (0, n)), vec, mat, vec, mat, vec, vec],
        out_specs=[out, out],
        out_shape=[jax.ShapeDtypeStruct((bsz, s, D_RNN), F32), jax.ShapeDtypeStruct((bsz, s, D_RNN), BF16)],
        compiler_params=_params())


def _rnn_bwd(z, h, dy, dz, wc, bc, wa, ba, wi, bi, lam, exchange=None):
    bsz, s, _ = z.shape
    ch = min(SCAN_CHUNK, s)
    nch = s // ch

    def body(z_ref, h_ref, dy_ref, dz_in, wc_ref, bc_ref, wa_ref, ba_ref, wi_ref, bi_ref, lam_ref,
             dz_ref, dwa_ref, dwi_ref, dsm_ref):
        del dz_in
        b = pl.program_id(1)
        wcv, bcv = wc_ref[...], bc_ref[...]
        wav, wiv = wa_ref[0].astype(BF16), wi_ref[0].astype(BF16)
        bav, biv = ba_ref[...], bi_ref[...]
        lamv = lam_ref[...]
        sp = _softplus(-lamv)
        row = lax.broadcasted_iota(jnp.int32, (ch, RNN_BLK), 0)

        @pl.when(b == 0)
        def _():
            dwa_ref[...] = jnp.zeros_like(dwa_ref)
            dwi_ref[...] = jnp.zeros_like(dwi_ref)
            dsm_ref[...] = jnp.zeros_like(dsm_ref)

        def chunk(k, carry):
            a_next, g_next, dxr_next8 = carry
            ci = nch - 1 - k
            st = pl.multiple_of(ci * ch, ch)
            pst = pl.multiple_of(jnp.maximum(st - 8, 0), 8)
            has_prev = (ci > 0).astype(F32)
            blk = z_ref[0, pl.ds(st, ch), :]
            xp, gr = blk[:, :RNN_BLK], blk[:, RNN_BLK:]
            xprev8 = z_ref[0, pl.ds(pst, 8), :][:, :RNN_BLK] * has_prev
            hprev8 = h_ref[0, pl.ds(pst, 8), :] * has_prev
            xe = jnp.concatenate([xprev8, xp], axis=0)
            xs = [xp] + [pltpu.roll(xe, j, 0)[8:] for j in range(1, 4)]
            xr = bcv
            for kk in range(4):
                xr = xr + wcv[kk:kk + 1] * xs[3 - kk]
            r, i, a, mult = _lru_gates(xr, wav, bav, wiv, biv, sp)
            hh = h_ref[0, pl.ds(st, ch), :]
            hm1 = pltpu.roll(jnp.concatenate([hprev8, hh], axis=0), 1, 0)[8:]
            dyv = dy_ref[0, pl.ds(st, ch), :]
            ge, dge = _gelu_and_grad(gr)
            dgr = dyv * hh * dge
            gg = _scan_backward(jnp.where(row == ch - 1, a_next, pltpu.roll(a, ch - 1, 0)), dyv * ge, g_next, row)
            da = gg * hm1
            ix = i * xr
            dmult = gg * ix
            di = gg * mult * xr
            dxr = gg * mult * i
            dla = da * a - dmult * (a * a) / mult
            dr = dla * (-LRU_C * sp)
            dsp = jnp.sum(dla * (-LRU_C * r), axis=0, keepdims=True)
            dpa = dr * r * (1.0 - r)
            dpi = di * i * (1.0 - i)
            dpab, dpib, xrb = dpa.astype(BF16), dpi.astype(BF16), xr.astype(BF16)
            nt = (((1,), (1,)), ((), ()))
            tn = (((0,), (0,)), ((), ()))
            dxr = dxr + lax.dot_general(dpab, wav, nt, preferred_element_type=F32)
            dxr = dxr + lax.dot_general(dpib, wiv, nt, preferred_element_type=F32)
            dwa_ref[0] = dwa_ref[0] + lax.dot_general(xrb, dpab, tn, preferred_element_type=F32)
            dwi_ref[0] = dwi_ref[0] + lax.dot_general(xrb, dpib, tn, preferred_element_type=F32)
            rows = [jnp.sum(dxr * xs[3 - q], axis=0, keepdims=True) for q in range(4)]
            rows += [jnp.sum(dxr, axis=0, keepdims=True), jnp.sum(dpa, axis=0, keepdims=True),
                     jnp.sum(dpi, axis=0, keepdims=True), dsp * (-_sigmoid(-lamv))]
            dsm_ref[...] = dsm_ref[...] + jnp.concatenate(rows, axis=0)
            dxe = jnp.concatenate([dxr, dxr_next8], axis=0)
            dxp = wcv[3:4] * dxr
            for j in range(1, 4):
                dxp = dxp + wcv[3 - j:4 - j] * pltpu.roll(dxe, ch + 8 - j, 0)[:ch]
            dz_ref[0, pl.ds(st, ch), :] = jnp.concatenate([dxp, dgr], axis=1).astype(BF16)
            return a[0:1], gg[0:1], dxr[0:8]

        lax.fori_loop(0, nch, chunk, (jnp.zeros((1, RNN_BLK), F32), jnp.zeros((1, RNN_BLK), F32),
                                      jnp.zeros((8, RNN_BLK), F32)))

    vec = pl.BlockSpec((1, RNN_BLK), lambda n, b: (0, n))
    mat = pl.BlockSpec((1, RNN_BLK, RNN_BLK), lambda n, b: (n, 0, 0))
    tok = pl.BlockSpec((1, s, RNN_BLK), lambda n, b: (b, 0, n))
    zblk = pl.BlockSpec((1, s, 2 * RNN_BLK), lambda n, b: (b, 0, XG0R // (2 * RNN_BLK) + n))
    dzblk = pl.BlockSpec((1, s, 2 * RNN_BLK), lambda n, b: (b, 0, XG0 // (2 * RNN_BLK) + n))
    return _run(
        body, [z, h, dy, dz, wc, bc, wa, ba, wi, bi, lam], exchange, name="rnn_bwd", grid=(N_BLK, bsz),
        in_specs=[zblk, tok, tok, ANY, pl.BlockSpec((4, RNN_BLK), lambda n, b: (0, n)), vec, mat, vec, mat, vec, vec],
        out_specs=[dzblk, mat, mat, pl.BlockSpec((8, RNN_BLK), lambda n, b: (0, n))],
        out_shape=[jax.ShapeDtypeStruct(dz.shape, dz.dtype), jax.ShapeDtypeStruct((N_BLK, RNN_BLK, RNN_BLK), F32),
                   jax.ShapeDtypeStruct((N_BLK, RNN_BLK, RNN_BLK), F32), jax.ShapeDtypeStruct((8, D_RNN), F32)],
        input_output_aliases={3: 0},
        compiler_params=_params())


def _tri(n, upper):
    r = lax.broadcasted_iota(jnp.int32, (n, n), 0)
    c = lax.broadcasted_iota(jnp.int32, (n, n), 1)
    return (r <= c if upper else r >= c).astype(BF16)


def _exact_dot(x, t):
    hi, mid, lo = _split3(x)
    return (jnp.dot(hi, t, preferred_element_type=F32) + jnp.dot(mid, t, preferred_element_type=F32)
            + jnp.dot(lo, t, preferred_element_type=F32))


def _fgate_fwd(flt, bfg):
    bsz, nh, s = flt.shape
    blk = min(CUM_BLK, s)

    def body(fl_ref, b_ref, f_ref):
        tri = _tri(blk, True)
        carry = jnp.zeros((nh, 1), F32)
        for q in range(s // blk):
            xq = fl_ref[0, :, q * blk:(q + 1) * blk] + b_ref[...]
            fq = _exact_dot(-_softplus(-xq), tri) + carry
            f_ref[0, :, q * blk:(q + 1) * blk] = fq
            carry = fq[:, blk - 1:blk]

    spec = pl.BlockSpec((1, nh, s), lambda b: (b, 0, 0))
    return _call(body, name="fgate_fwd", grid=(bsz,), in_specs=[spec, pl.BlockSpec((nh, 1), lambda b: (0, 0))],
                 out_specs=spec, out_shape=jax.ShapeDtypeStruct(flt.shape, F32), compiler_params=_params())(flt, bfg)


def _fgate_bwd(dft, flt, bfg):
    bsz, nh, s = flt.shape
    blk = min(CUM_BLK, s)

    def body(df_ref, fl_ref, b_ref, dfl_ref, db_ref):
        b = pl.program_id(0)
        tri = _tri(blk, False)
        carry = jnp.zeros((nh, 1), F32)
        tot = jnp.zeros((nh, 1), F32)
        for q in reversed(range(s // blk)):
            sl = slice(q * blk, (q + 1) * blk)
            dlq = _exact_dot(df_ref[0, :, sl], tri) + carry
            carry = dlq[:, 0:1]
            dflq = dlq * _sigmoid(-(fl_ref[0, :, sl] + b_ref[...]))
            dfl_ref[0, :, sl] = dflq
            tot = tot + jnp.sum(dflq, axis=1, keepdims=True)
        _acc(db_ref, slice(None), tot, b == 0)

    spec = pl.BlockSpec((1, nh, s), lambda b: (b, 0, 0))
    small = pl.BlockSpec((nh, 1), lambda b: (0, 0))
    return _call(body, name="fgate_bwd", grid=(bsz,), in_specs=[spec, spec, small], out_specs=[spec, small],
                 out_shape=[jax.ShapeDtypeStruct(flt.shape, F32), jax.ShapeDtypeStruct((nh, 1), F32)],
                 compiler_params=_params())(dft, flt, bfg)


_NT = (((1,), (1,)), ((), ()))
_TN = (((0,), (0,)), ((), ()))


def _attn_block_scores(q_b, z_ref, frow_ref, k0, tq):
    kk = z_ref[0, k0:k0 + tq, HEAD_DIM:2 * HEAD_DIM]
    return lax.dot_general(q_b, kk, _NT, preferred_element_type=F32) - frow_ref[0, 0, :, k0:k0 + tq]


def _causal(tq):
    return lax.broadcasted_iota(jnp.int32, (tq, tq), 0) >= lax.broadcasted_iota(jnp.int32, (tq, tq), 1)


def _attn_fwd(z, fcol, frow, exchange=None):
    bsz, s, _ = z.shape
    tq = min(ATTN_TQ, s)
    scale = HEAD_DIM ** -0.5

    def body(z_ref, fcol_ref, frow_ref, o_ref, lse_ref):
        for qb in range(s // tq):
            q0 = qb * tq
            q_b = (z_ref[0, q0:q0 + tq, :HEAD_DIM].astype(F32) * scale).astype(BF16)
            m = jnp.full((tq, 1), -jnp.inf, F32)
            l = jnp.zeros((tq, 1), F32)
            acc = jnp.zeros((tq, HEAD_DIM), F32)
            for kb in range(qb + 1):
                k0 = kb * tq
                sc = _attn_block_scores(q_b, z_ref, frow_ref, k0, tq)
                if kb == qb:
                    sc = jnp.where(_causal(tq), sc, -jnp.inf)
                m_new = jnp.maximum(m, jnp.max(sc, axis=-1, keepdims=True))
                alpha = jnp.exp(m - m_new)
                p = jnp.exp(sc - m_new)
                l = alpha * l + jnp.sum(p, axis=-1, keepdims=True)
                vv = z_ref[0, k0:k0 + tq, 2 * HEAD_DIM:]
                acc = alpha * acc + jnp.dot(p.astype(BF16), vv, preferred_element_type=F32)
                m = m_new
            o_ref[0, q0:q0 + tq, :] = (acc / l).astype(BF16)
            lse_ref[0, 0, q0:q0 + tq, :] = m + jnp.log(l) + fcol_ref[0, 0, q0:q0 + tq, :]

    return _run(
        body, [z, fcol, frow], exchange, name="attn_fwd", grid=(bsz, N_HEADS),
        in_specs=[pl.BlockSpec((1, s, 3 * HEAD_DIM), lambda b, h: (b, 0, h)),
                  pl.BlockSpec((1, 1, s, 1), lambda b, h: (b, h, 0, 0)),
                  pl.BlockSpec((1, 1, 1, s), lambda b, h: (b, h, 0, 0))],
        out_specs=[pl.BlockSpec((1, s, HEAD_DIM), lambda b, h: (b, 0, h)),
                   pl.BlockSpec((1, 1, s, 1), lambda b, h: (b, h, 0, 0))],
        out_shape=[jax.ShapeDtypeStruct((bsz, s, N_HEADS * HEAD_DIM), BF16),
                   jax.ShapeDtypeStruct((bsz, N_HEADS, s, 1), F32)],
        compiler_params=_params())


def _attn_bwd(z, do, lse, fcol, frow, dz, exchange=None):
    bsz, s, _ = z.shape
    tq = min(ATTN_TQ, s)
    scale = HEAD_DIM ** -0.5

    def body(z_ref, do_ref, lse_ref, fcol_ref, frow_ref, dz_in, dz_ref, df_ref, dk_acc, dv_acc):
        del dz_in
        dk_acc[...] = jnp.zeros_like(dk_acc)
        dv_acc[...] = jnp.zeros_like(dv_acc)
        df_ref[...] = jnp.zeros_like(df_ref)
        for qb in range(s // tq):
            q0, ln = qb * tq, (qb + 1) * tq
            q_b = (z_ref[0, q0:ln, :HEAD_DIM].astype(F32) * scale).astype(BF16)
            kk = z_ref[0, :ln, HEAD_DIM:2 * HEAD_DIM]
            vv = z_ref[0, :ln, 2 * HEAD_DIM:]
            do_b = do_ref[0, q0:ln, :]
            lse_q = lse_ref[0, 0, q0:ln, :] - fcol_ref[0, 0, q0:ln, :]
            sc = lax.dot_general(q_b, kk, _NT, preferred_element_type=F32) - frow_ref[0, 0, :, :ln] - lse_q
            mask = (q0 + lax.broadcasted_iota(jnp.int32, (tq, ln), 0)) >= lax.broadcasted_iota(jnp.int32, (tq, ln), 1)
            p = jnp.where(mask, jnp.exp(sc), 0.0)
            pb = p.astype(BF16)
            dv_acc[:ln, :] = dv_acc[:ln, :] + lax.dot_general(pb, do_b, _TN, preferred_element_type=F32)
            dp = lax.dot_general(do_b, vv, _NT, preferred_element_type=F32)
            ds = p * (dp - jnp.sum(p * dp, axis=-1, keepdims=True))
            dsb = ds.astype(BF16)
            dq = jnp.dot(dsb, kk, preferred_element_type=F32) * scale
            dz_ref[0, q0:ln, :HEAD_DIM] = dq.astype(BF16)
            dk_acc[:ln, :] = dk_acc[:ln, :] + lax.dot_general(dsb, q_b, _TN, preferred_element_type=F32)
            df_ref[0, 0, :, :ln] = df_ref[0, 0, :, :ln] - jnp.sum(ds, axis=0, keepdims=True)
        dz_ref[0, :, HEAD_DIM:2 * HEAD_DIM] = dk_acc[...].astype(BF16)
        dz_ref[0, :, 2 * HEAD_DIM:] = dv_acc[...].astype(BF16)

    zblk = pl.BlockSpec((1, s, 3 * HEAD_DIM), lambda b, h: (b, 0, h))
    hblk = pl.BlockSpec((1, s, HEAD_DIM), lambda b, h: (b, 0, h))
    col = pl.BlockSpec((1, 1, s, 1), lambda b, h: (b, h, 0, 0))
    rowspec = pl.BlockSpec((1, 1, 1, s), lambda b, h: (b, h, 0, 0))
    return _run(
        body, [z, do, lse, fcol, frow, dz], exchange, name="attn_bwd", grid=(bsz, N_HEADS),
        in_specs=[zblk, hblk, col, col, rowspec, ANY],
        out_specs=[zblk, rowspec],
        out_shape=[jax.ShapeDtypeStruct(dz.shape, dz.dtype), jax.ShapeDtypeStruct((bsz, N_HEADS, 1, s), F32)],
        scratch_shapes=[pltpu.VMEM((s, HEAD_DIM), F32), pltpu.VMEM((s, HEAD_DIM), F32)],
        input_output_aliases={5: 0},
        compiler_params=_params())


def _merge_fwd(z2, pr, pa, *, tr=512):
    t = z2.shape[0]

    def body(mgr_ref, mga_ref, pr_ref, pa_ref, o_ref):
        gr_ = _sigmoid(mgr_ref[...])
        ga = _sigmoid(mga_ref[...])
        o_ref[...] = (gr_ * pr_ref[...] + ga * pa_ref[...]).astype(BF16)

    tok = pl.BlockSpec((tr, D_MODEL), lambda i: (i, 0))
    return _call(body, name="merge_fwd", grid=(t // tr,),
                 in_specs=[pl.BlockSpec((tr, D_MODEL), lambda i: (i, MG0R // D_MODEL)),
                           pl.BlockSpec((tr, D_MODEL), lambda i: (i, MG0R // D_MODEL + 1)), tok, tok],
                 out_specs=tok, out_shape=jax.ShapeDtypeStruct((t, D_MODEL), BF16),
                 compiler_params=_params())(z2, z2, pr, pa)


def _merge_bwd(dm, z2, pr, pa, *, tr=512):
    t = z2.shape[0]

    def body(dm_ref, mgr_ref, mga_ref, pr_ref, pa_ref, dz_ref, dpr_ref, dpa_ref):
        dmv = dm_ref[...]
        gr_ = _sigmoid(mgr_ref[...])
        ga = _sigmoid(mga_ref[...])
        dpr_ref[...] = (dmv * gr_).astype(BF16)
        dpa_ref[...] = (dmv * ga).astype(BF16)
        dz_ref[:, :D_MODEL] = (dmv * pr_ref[...] * gr_ * (1.0 - gr_)).astype(BF16)
        dz_ref[:, D_MODEL:] = (dmv * pa_ref[...] * ga * (1.0 - ga)).astype(BF16)

    tok = pl.BlockSpec((tr, D_MODEL), lambda i: (i, 0))
    mg = pl.BlockSpec((tr, 2 * D_MODEL), lambda i: (i, MG0 // (2 * D_MODEL)))
    return _call(body, name="merge_bwd", grid=(t // tr,),
                 in_specs=[tok, pl.BlockSpec((tr, D_MODEL), lambda i: (i, MG0R // D_MODEL)),
                           pl.BlockSpec((tr, D_MODEL), lambda i: (i, MG0R // D_MODEL + 1)), tok, tok],
                 out_specs=[mg, tok, tok],
                 out_shape=[jax.ShapeDtypeStruct((t, NZ), BF16), jax.ShapeDtypeStruct((t, D_MODEL), BF16),
                            jax.ShapeDtypeStruct((t, D_MODEL), BF16)],
                 compiler_params=_params())(dm, z2, z2, pr, pa)


def _fl_write(dfl_blk, dz):
    bsz, s, w = dfl_blk.shape

    def body(src_ref, dz_in, dz_ref):
        del dz_in
        dz_ref[...] = src_ref[...]

    return _call(body, name="fl_write", grid=(bsz,),
                 in_specs=[pl.BlockSpec((1, s, w), lambda b: (b, 0, 0)), ANY],
                 out_specs=pl.BlockSpec((1, s, w), lambda b: (b, 0, FL0 // w)),
                 out_shape=jax.ShapeDtypeStruct(dz.shape, dz.dtype), input_output_aliases={1: 0},
                 compiler_params=_params())(dfl_blk, dz)


def _ffn_fwd(zf, wc, bc):
    bsz, s, _ = zf.shape
    cw = 128

    def body(zf_ref, wc_ref, bc_ref, o_ref):
        gfp, uf = zf_ref[0, :, :cw].astype(F32), zf_ref[0, :, cw:].astype(F32)
        wcv = wc_ref[...]
        row = lax.broadcasted_iota(jnp.int32, (s, cw), 0)
        gf = bc_ref[...] + wcv[2:3] * gfp
        for j in (1, 2):
            gf = gf + wcv[2 - j:3 - j] * jnp.where(row >= j, pltpu.roll(gfp, j, 0), 0.0)
        o_ref[0] = (_gelu(gf) * uf).astype(BF16)

    return _call(body, name="ffn_fwd", grid=(bsz, D_FF // cw),
                 in_specs=[pl.BlockSpec((1, s, 2 * cw), lambda b, j: (b, 0, j)),
                           pl.BlockSpec((3, cw), lambda b, j: (0, j)), pl.BlockSpec((1, cw), lambda b, j: (0, j))],
                 out_specs=pl.BlockSpec((1, s, cw), lambda b, j: (b, 0, j)),
                 out_shape=jax.ShapeDtypeStruct((bsz, s, D_FF), BF16), compiler_params=_params())(zf, wc, bc)


def _ffn_bwd(dact, zf, wc, bc, exchange=None):
    bsz, s, _ = zf.shape
    cw = 128

    def body(da_ref, zf_ref, wc_ref, bc_ref, dzf_ref, dsm_ref):
        b = pl.program_id(1)
        gfp, uf = zf_ref[0, :, :cw].astype(F32), zf_ref[0, :, cw:].astype(F32)
        wcv = wc_ref[...]
        row = lax.broadcasted_iota(jnp.int32, (s, cw), 0)
        sh = [gfp] + [jnp.where(row >= j, pltpu.roll(gfp, j, 0), 0.0) for j in (1, 2)]
        gf = bc_ref[...]
        for kk in range(3):
            gf = gf + wcv[kk:kk + 1] * sh[2 - kk]
        ge, dge = _gelu_and_grad(gf)
        dav = da_ref[0].astype(F32)
        dgf = dav * uf * dge
        dgfp = wcv[2:3] * dgf
        for j in (1, 2):
            dgfp = dgfp + wcv[2 - j:3 - j] * jnp.where(row < s - j, pltpu.roll(dgf, s - j, 0), 0.0)
        dzf_ref[0] = jnp.concatenate([dgfp, dav * ge], axis=1).astype(BF16)
        rows = [jnp.sum(dgf * sh[2 - kk], axis=0, keepdims=True) for kk in range(3)]
        rows += [jnp.sum(dgf, axis=0, keepdims=True), jnp.zeros((4, cw), F32)]
        _acc(dsm_ref, slice(None), jnp.concatenate(rows, axis=0), b == 0)

    return _run(body, [dact, zf, wc, bc], exchange, name="ffn_bwd", grid=(D_FF // cw, bsz),
                in_specs=[pl.BlockSpec((1, s, cw), lambda j, b: (b, 0, j)),
                          pl.BlockSpec((1, s, 2 * cw), lambda j, b: (b, 0, j)),
                          pl.BlockSpec((3, cw), lambda j, b: (0, j)), pl.BlockSpec((1, cw), lambda j, b: (0, j))],
                out_specs=[pl.BlockSpec((1, s, 2 * cw), lambda j, b: (b, 0, j)),
                           pl.BlockSpec((8, cw), lambda j, b: (0, j))],
                out_shape=[jax.ShapeDtypeStruct(zf.shape, BF16), jax.ShapeDtypeStruct((8, D_FF), F32)],
                compiler_params=_params())


def _adamw(w, m, v, parts, *, name):
    r, c = w.shape
    p = parts.shape[0]
    tr = r if r <= 512 else 256
    assert r % tr == 0 and parts.shape[1:] == (r, c), (name, w.shape, parts.shape)
    c1 = 1.0 / (1.0 - ADAM_B1 ** ADAM_STEP)
    c2 = 1.0 / (1.0 - ADAM_B2 ** ADAM_STEP)

    def body(w_ref, m_ref, v_ref, p_ref, g_ref, d_ref, mo_ref, vo_ref):
        g = p_ref[0].astype(F32)
        for q in range(1, p):
            g = g + p_ref[q].astype(F32)
        mn = ADAM_B1 * m_ref[...] + (1.0 - ADAM_B1) * g
        vn = ADAM_B2 * v_ref[...] + (1.0 - ADAM_B2) * (g * g)
        g_ref[...] = g
        mo_ref[...] = mn
        vo_ref[...] = vn
        d_ref[...] = -ADAM_LR * ((mn * c1) / (jnp.sqrt(vn * c2) + ADAM_EPS) + ADAM_WD * w_ref[...])

    spec = pl.BlockSpec((tr, c), lambda i: (i, 0))
    shp = jax.ShapeDtypeStruct((r, c), F32)
    return _call(body, name=name, grid=(r // tr,),
                 in_specs=[spec, spec, spec, pl.BlockSpec((p, tr, c), lambda i: (0, i, 0))],
                 out_specs=[spec] * 4, out_shape=[shp] * 4, compiler_params=_params())(w, m, v, parts)


def _interleave(w, groups, width):
    n = w.shape[1] // (groups * width)
    return jnp.concatenate([w[:, (g * n + j) * width:(g * n + j + 1) * width]
                            for j in range(n) for g in range(groups)], axis=1)


def _deinterleave(w, groups, width):
    n = w.shape[1] // (groups * width)
    return jnp.concatenate([w[:, (j * groups + g) * width:(j * groups + g + 1) * width]
                            for g in range(groups) for j in range(n)], axis=1)


def _to_z_layout(w):
    xg = _interleave(w[:, :2 * D_RNN], 2, RNN_BLK)
    qkv = _interleave(w[:, 2 * D_RNN:2 * D_RNN + 3 * D_MODEL], 3, HEAD_DIM)
    fl0 = 2 * D_RNN + 3 * D_MODEL
    fl = w[:, fl0:fl0 + N_HEADS]
    mg = w[:, fl0 + N_HEADS:]
    pad = jnp.zeros((w.shape[0], MG0 - FL0 - N_HEADS), w.dtype)
    return jnp.concatenate([qkv, xg, fl, pad, mg], axis=1)


def _from_z_layout(w):
    qkv = _deinterleave(w[:, QKV0:XG0], 3, HEAD_DIM)
    xg = _deinterleave(w[:, XG0:FL0], 2, RNN_BLK)
    return jnp.concatenate([xg, qkv, w[:, FL0:FL0 + N_HEADS], w[:, MG0:]], axis=1)


def _col_gather(g):
    return g.transpose(1, 0, 2).reshape(g.shape[1], -1)


def _col_scatter(w):
    return w.reshape(w.shape[0], N_DEV, -1).transpose(1, 0, 2)


def kernel(x, c, w_ada, b_ada, g_norm1, w_in, w_rnn_conv, b_rnn_conv, w_lru_a, b_lru_a, w_lru_i, b_lru_i, lru_lambda, b_fgate, w_proj_rnn, w_proj_attn, w_out, g_norm2, w_ffn_up, w_ffn_conv, b_ffn_conv, w_ffn_down, w_ada_final, b_ada_final, g_final, loss_target, m_w_ada, m_b_ada, m_g_norm1, m_w_in, m_w_rnn_conv, m_b_rnn_conv, m_w_lru_a, m_b_lru_a, m_w_lru_i, m_b_lru_i, m_lru_lambda, m_b_fgate, m_w_proj_rnn, m_w_proj_attn, m_w_out, m_g_norm2, m_w_ffn_up, m_w_ffn_conv, m_b_ffn_conv, m_w_ffn_down, m_w_ada_final, m_b_ada_final, m_g_final, v_w_ada, v_b_ada, v_g_norm1, v_w_in, v_w_rnn_conv, v_b_rnn_conv, v_w_lru_a, v_b_lru_a, v_w_lru_i, v_b_lru_i, v_lru_lambda, v_b_fgate, v_w_proj_rnn, v_w_proj_attn, v_w_out, v_g_norm2, v_w_ffn_up, v_w_ffn_conv, v_b_ffn_conv, v_w_ffn_down, v_w_ada_final, v_b_ada_final, v_g_final):
    args = dict(locals())
    bsz, s, _ = x.shape
    t = bsz * s
    nb = N_DEV * bsz
    me = 4 * lax.axis_index("x") + 2 * lax.axis_index("y") + lax.axis_index("c")
    tm = min(2048, t)

    ex_a = _gather_two_level([c, w_in[0].astype(BF16), w_rnn_conv[0], w_ffn_conv[0]], "gather_first")
    c_all = ex_a[0].reshape(nb, D_MODEL)
    win_z = _to_z_layout(_col_gather(ex_a[1]))
    wrc = _col_gather(ex_a[2])
    wfc = _col_gather(ex_a[3])

    nmod = w_ada.shape[2]
    nmodf = w_ada_final.shape[1]
    mod_cols = _mm(c_all, w_ada[0], name="mod_cols", tm=nb, tn=nmod, tk=D_MODEL, silu_a=True,
                   bias=lax.dynamic_slice(b_ada, (0, me * nmod), (1, nmod)))
    modf_cols = _mm(c_all, w_ada_final, name="modf_cols", tm=nb, tn=nmodf, tk=D_MODEL, silu_a=True,
                    bias=lax.dynamic_slice(b_ada_final.reshape(1, -1), (0, me * nmodf), (1, nmodf)))
    ex_b = _exchange([mod_cols, modf_cols], [True, True], "gather_mod")
    mod = lax.dynamic_slice(ex_b[0], (0, me * bsz, 0), (N_DEV, bsz, nmod)).transpose(1, 0, 2).reshape(bsz, 6, 1, D_MODEL)
    modf = lax.dynamic_slice(ex_b[1], (0, me * bsz, 0), (N_DEV, bsz, nmodf)).transpose(1, 0, 2).reshape(bsz, 2, 1, D_MODEL)
    shift1, scale1, gate1, shift2, scale2, gate2 = [mod[:, i] for i in range(6)]
    shift_f, scale_f = modf[:, 0], modf[:, 1]

    h1 = _norm_fwd(x, g_norm1, shift1, scale1, name="norm1_fwd")[0]
    h1f = h1.reshape(t, D_MODEL)
    zq = _mm(h1f, win_z, name="mm_in_qkv", out_dtype=BF16, tm=tm, tn=1024, tk=D_MODEL,
             b_cols=(QKV0, XG0)).reshape(bsz, s, XG0)
    zr = _mm(h1f, win_z, name="mm_in_rest", tm=tm, tn=1024, tk=D_MODEL, b_cols=(XG0, NZR)).reshape(bsz, s, NZR)
    h_rnn, y_rnn, got = _rnn_fwd(zr, wrc, b_rnn_conv, w_lru_a[0], b_lru_a, w_lru_i[0], b_lru_i, lru_lambda,
                                 exchange=([w_ffn_up[0].astype(BF16)], ["gather_zf"]))
    wup_z = got[0]
    flt = zr[:, :, FL0R:FL0R + N_HEADS].transpose(0, 2, 1)
    bfg = b_fgate.reshape(N_HEADS, 1)
    fcum = _fgate_fwd(flt, bfg)
    fcol = fcum.reshape(bsz, N_HEADS, s, 1)
    frow = fcum.reshape(bsz, N_HEADS, 1, s)
    o_attn, lse, got = _attn_fwd(
        zq, fcol, frow, exchange=([w_proj_rnn[0].astype(BF16), w_proj_attn[0].astype(BF16), w_out[0].astype(BF16),
                                  w_ffn_down[0].astype(BF16)], [True] * 4))
    wpr = got[0].reshape(D_RNN, D_MODEL)
    wpa = got[1].reshape(D_MODEL, D_MODEL)
    wout = got[2].reshape(D_MODEL, D_MODEL)
    wdown = got[3].reshape(D_FF, D_MODEL)
    pr = _mm(y_rnn.reshape(t, D_RNN), wpr, name="mm_proj_rnn", out_dtype=BF16, tm=tm, tn=D_MODEL, tk=D_RNN)
    pa = _mm(o_attn.reshape(t, D_MODEL), wpa, name="mm_proj_attn", out_dtype=BF16, tm=tm, tn=D_MODEL, tk=D_MODEL)
    z2 = zr.reshape(t, NZR)
    merged = _merge_fwd(z2, pr, pa)
    mo = _mm(merged, wout, name="mm_out", out_dtype=BF16, tm=tm, tn=D_MODEL, tk=D_MODEL).reshape(bsz, s, D_MODEL)
    h2, x1 = _norm_fwd(x, g_norm2, shift2, scale2, name="norm2_fwd", yprev=mo, gate=gate1)
    zf = _mm(h2.reshape(t, D_MODEL), wup_z, name="mm_up", out_dtype=BF16, tm=tm, tn=1024,
             tk=D_MODEL).reshape(bsz, s, 2 * D_FF)
    act = _ffn_fwd(zf, wfc, b_ffn_conv)
    yf = _mm(act.reshape(t, D_FF), wdown, name="mm_down", out_dtype=BF16, tm=tm, tn=D_MODEL,
             tk=D_FF // 2).reshape(bsz, s, D_MODEL)
    lp, dx2, dyf, dshift_f, dscale_f, dgate2, dg_final = _final(x1, yf, gate2, g_final.reshape(1, -1), shift_f, scale_f, loss_target)
    loss = lax.psum(jnp.sum(lp[:, :, 0, 0]) * (0.5 / D_MODEL), ("x", "y", "c"))

    dyf2 = dyf.reshape(t, D_MODEL)
    act2 = act.reshape(t, D_FF)
    dact = _mm(dyf2, wdown, name="mm_down_dx", tb=True, out_dtype=BF16, tm=tm, tn=1024, tk=D_MODEL).reshape(bsz, s, D_FF)
    g_wdown = _mm(act2, dyf2, name="mm_down_dw", ta=True, out_dtype=BF16, tm=1024, tn=D_MODEL, tk=tm)
    dzf, dsm_ffn, got = _ffn_bwd(dact, zf, wfc, b_ffn_conv, exchange=([g_wdown.reshape(N_DEV, -1, D_MODEL)], [False]))
    p_wdown = got[0]
    dzf2 = dzf.reshape(t, 2 * D_FF)
    dh2 = _mm(dzf2, wup_z, name="mm_up_dx", tb=True, tm=tm, tn=D_MODEL, tk=2048).reshape(bsz, s, D_MODEL)
    g_wup_z = _mm(h2.reshape(t, D_MODEL), dzf2, name="mm_up_dw", ta=True, out_dtype=BF16, tm=D_MODEL, tn=1024, tk=tm)
    dx1, dg_norm2, dshift2, dscale2, dmo, dgate1 = _norm_bwd(x1, dh2, g_norm2, scale2, dx2, name="norm2_bwd", yprev=mo, gate=gate1)
    dmo2 = dmo.reshape(t, D_MODEL)
    dmerged = _mm(dmo2, wout, name="mm_out_dx", tb=True, out_dtype=BF16, tm=tm, tn=D_MODEL, tk=D_MODEL)
    g_wout = _mm(merged, dmo2, name="mm_out_dw", ta=True, out_dtype=BF16, tm=D_MODEL, tn=D_MODEL, tk=tm)
    dz2, dpr, dpa = _merge_bwd(dmerged, z2, pr, pa)
    dy_rnn = _mm(dpr, wpr, name="mm_proj_rnn_dx", tb=True, out_dtype=BF16, tm=tm, tn=D_RNN,
                 tk=D_MODEL).reshape(bsz, s, D_RNN)
    g_wpr = _mm(y_rnn.reshape(t, D_RNN), dpr, name="mm_proj_rnn_dw", ta=True, out_dtype=BF16, tm=D_RNN, tn=D_MODEL, tk=tm)
    do = _mm(dpa, wpa, name="mm_proj_attn_dx", tb=True, out_dtype=BF16, tm=tm, tn=D_MODEL, tk=D_MODEL).reshape(bsz, s, D_MODEL)
    g_wpa = _mm(o_attn.reshape(t, D_MODEL), dpa, name="mm_proj_attn_dw", ta=True, out_dtype=BF16, tm=D_MODEL, tn=D_MODEL, tk=tm)
    dz, dfrow, got = _attn_bwd(zq, do, lse, fcol, frow, dz2.reshape(bsz, s, NZ),
                               exchange=([g_wup_z, g_wout.reshape(N_DEV, -1, D_MODEL)], ["scatter_zf", False]))
    p_wup, p_wout = got
    dflt, db_fgate = _fgate_bwd(dfrow.reshape(bsz, N_HEADS, s), flt, bfg)
    dfl_blk = jnp.pad(dflt.transpose(0, 2, 1).astype(BF16), ((0, 0), (0, 0), (0, MG0 - FL0 - N_HEADS)))
    dz = _fl_write(dfl_blk, dz)
    dz, g_wa, g_wi, dsm_rnn, got = _rnn_bwd(
        zr, h_rnn, dy_rnn, dz, wrc, b_rnn_conv, w_lru_a[0], b_lru_a, w_lru_i[0], b_lru_i, lru_lambda,
        exchange=([g_wpr.reshape(N_DEV, -1, D_MODEL), g_wpa.reshape(N_DEV, -1, D_MODEL)], [False, False]))
    p_wpr, p_wpa = got
    dz2 = dz.reshape(t, NZ)
    small = [dg_norm2, dg_final, dsm_rnn, g_wa, g_wi, db_fgate, dsm_ffn]
    small_sizes = [a.size for a in small]
    n_pad = -sum(small_sizes) % 1024
    packed = jnp.concatenate([a.reshape(-1) for a in small] + [jnp.zeros((n_pad,), F32)]).reshape(-1, 128)
    g_win_z, got = _mm(h1f, dz2, name="mm_in_dw", ta=True, out_dtype=BF16, tm=D_MODEL, tn=1024, tk=tm,
                       exchange=([packed], [True]))
    flat = got[0].reshape(N_DEV, -1)
    dh1, got = _mm(dz2, win_z, name="mm_in_dx", tb=True, tm=tm, tn=D_MODEL, tk=2048,
                   exchange=([_col_scatter(_from_z_layout(g_win_z))], [False]))
    p_win = got[0]
    grad_x, dg_norm1, dshift1, dscale1 = _norm_bwd(x, dh1.reshape(bsz, s, D_MODEL), g_norm1, scale1, dx1, name="norm1_bwd")

    dmod = jnp.concatenate([dshift1, dscale1, dgate1, dshift2, dscale2, dgate2], axis=1).reshape(bsz, 6 * D_MODEL)
    dmodf = jnp.concatenate([dshift_f, dscale_f], axis=1).reshape(bsz, 2 * D_MODEL)
    ex_c = _exchange([dmod, dmodf, dg_norm1], [True, True, True], "gather_last_grads")
    dmod_all = ex_c[0].reshape(nb, 6 * D_MODEL)
    dmodf_all = ex_c[1].reshape(nb, 2 * D_MODEL)
    p_g1 = ex_c[2]

    res = {}

    def upd(name, parts, shape2):
        w, m, v = args[name], args["m_" + name], args["v_" + name]
        outs = _adamw(w.reshape(shape2), m.reshape(shape2), v.reshape(shape2), parts, name="adamw_" + name)
        res[name] = [o.reshape(w.shape) for o in outs]

    g_wada = _mm(c_all, lax.dynamic_slice(dmod_all, (0, me * nmod), (nb, nmod)), name="mm_ada_dw", ta=True,
                 silu_a=True, tm=D_MODEL, tn=nmod, tk=nb)
    g_wadaf = _mm(c_all, lax.dynamic_slice(dmodf_all, (0, me * nmodf), (nb, nmodf)), name="mm_adaf_dw", ta=True,
                  silu_a=True, tm=D_MODEL, tn=nmodf, tk=nb)
    upd("w_ada", g_wada[None], (D_MODEL, nmod))
    upd("w_ada_final", g_wadaf[None], (D_MODEL, nmodf))
    upd("b_ada", dmod_all.reshape(nb, 1, -1), (1, 6 * D_MODEL))
    upd("b_ada_final", dmodf_all.reshape(nb, 1, -1), (1, 2 * D_MODEL))
    upd("w_in", p_win, w_in.shape[1:])
    upd("w_proj_rnn", p_wpr, w_proj_rnn.shape[1:])
    upd("w_proj_attn", p_wpa, w_proj_attn.shape[1:])
    upd("w_out", p_wout, w_out.shape[1:])
    upd("w_ffn_up", p_wup, w_ffn_up.shape[1:])
    upd("w_ffn_down", p_wdown, w_ffn_down.shape[1:])

    parts, off = [], 0
    for a, n in zip(small, small_sizes):
        parts.append(flat[:, off:off + n].reshape((N_DEV,) + a.shape))
        off += n
    p_g2, p_gf, p_rnn, p_wa, p_wi, p_bfg, p_ffn = parts
    upd("g_norm1", p_g1, (1, D_MODEL))
    upd("g_norm2", p_g2, (1, D_MODEL))
    upd("g_final", p_gf, (1, D_MODEL))
    nrc = w_rnn_conv.shape[2]
    upd("w_rnn_conv", lax.dynamic_slice(p_rnn, (0, 0, me * nrc), (N_DEV, 4, nrc)), (4, nrc))
    upd("b_rnn_conv", p_rnn[:, 4:5], (1, D_RNN))
    upd("b_lru_a", p_rnn[:, 5:6], (1, D_RNN))
    upd("b_lru_i", p_rnn[:, 6:7], (1, D_RNN))
    upd("lru_lambda", p_rnn[:, 7:8], (1, D_RNN))
    upd("w_lru_a", p_wa.reshape(N_DEV, D_RNN, RNN_BLK), (D_RNN, RNN_BLK))
    upd("w_lru_i", p_wi.reshape(N_DEV, D_RNN, RNN_BLK), (D_RNN, RNN_BLK))
    upd("b_fgate", p_bfg.reshape(N_DEV, 1, N_HEADS), (1, N_HEADS))
    nfc = w_ffn_conv.shape[2]
    upd("w_ffn_conv", lax.dynamic_slice(p_ffn, (0, 0, me * nfc), (N_DEV, 3, nfc)), (3, nfc))
    upd("b_ffn_conv", p_ffn[:, 3:4], (1, D_FF))

    names = ['w_ada', 'b_ada', 'g_norm1', 'w_in', 'w_rnn_conv', 'b_rnn_conv', 'w_lru_a', 'b_lru_a', 'w_lru_i',
             'b_lru_i', 'lru_lambda', 'b_fgate', 'w_proj_rnn', 'w_proj_attn', 'w_out', 'g_norm2', 'w_ffn_up',
             'w_ffn_conv', 'b_ffn_conv', 'w_ffn_down', 'w_ada_final', 'b_ada_final', 'g_final']
    return (loss, grad_x, *[res[n][0] for n in names], *[res[n][1] for n in names],
            *[res[n][2] for n in names], *[res[n][3] for n in names])
```

```python
import jax
import jax.numpy as jnp
from jax import lax
from jax.experimental import pallas as pl
from jax.experimental.pallas import tpu as pltpu

F32, BF16 = jnp.float32, jnp.bfloat16
D_MODEL, D_RNN, N_BLK, RNN_BLK = 1024, 1280, 10, 128
N_HEADS, HEAD_DIM, D_FF = 8, 128, 3072
N_DEV = 8
NZ = 8192
QKV0, XG0, FL0, MG0 = 0, 3072, 5632, 6144
NZR = NZ - XG0
XG0R, FL0R, MG0R = 0, FL0 - XG0, MG0 - XG0
RMS_EPS = 1e-6
LRU_C = 8.0
ADAM_LR, ADAM_B1, ADAM_B2, ADAM_EPS, ADAM_WD, ADAM_STEP = 0.001, 0.9, 0.999, 1e-08, 0.01, 10
VMEM_LIMIT = 56 << 20
SCAN_CHUNK = 256
ATTN_TQ = 256
CUM_BLK = 512
MESH = pl.DeviceIdType.MESH
ANY = pl.BlockSpec(memory_space=pl.ANY)


def _call(body, **kw):
    return pl.pallas_call(body, **kw)


def _params():
    return pltpu.CompilerParams(vmem_limit_bytes=VMEM_LIMIT)


ZF_PIECES = 2 * D_FF // 128 // N_DEV


def _zf_cols(owner, p):
    ob = owner * ZF_PIECES + p
    blk = 2 * (ob % (D_FF // 128)) + ob // (D_FF // 128)
    return pl.ds(pl.multiple_of(blk * 128, 128), 128)


def _n_pieces(mode):
    return ZF_PIECES if mode in ("gather_zf", "scatter_zf") else 1


def _exchanged_shape(a, mode):
    if mode is True:
        return (N_DEV,) + a.shape
    if mode is False:
        return a.shape
    if mode == "gather_zf":
        return (a.shape[0], N_DEV * a.shape[1])
    return (N_DEV, a.shape[0], a.shape[1] // N_DEV)


def _direct_copies(ins, outs, sems, modes):
    send_sems, recv_sems, loc_sems = sems
    x, y, c = lax.axis_index("x"), lax.axis_index("y"), lax.axis_index("c")
    me = 4 * x + 2 * y + c

    def ends(a, to, p):
        mode = modes[a]
        if mode is True:
            return ins[a], outs[a].at[me]
        if mode is False:
            return ins[a].at[to], outs[a].at[me]
        mine = slice(p * 128, (p + 1) * 128)
        if mode == "gather_zf":
            return ins[a].at[:, mine], outs[a].at[:, _zf_cols(me, p)]
        return ins[a].at[:, _zf_cols(to, p)], outs[a].at[me, :, mine]

    local, remote = [], []
    nloc = 0
    for a in range(len(ins)):
        for p in range(_n_pieces(modes[a])):
            src, dst = ends(a, me, p)
            local.append(pltpu.make_async_copy(src, dst, loc_sems.at[nloc]))
            nloc += 1
    for k in range(1, N_DEV):
        px = 1 - x if k & 4 else x
        py = 1 - y if k & 2 else y
        pc = 1 - c if k & 1 else c
        peer = 4 * px + 2 * py + pc
        nrem = (k - 1) * nloc
        for a in range(len(ins)):
            for p in range(_n_pieces(modes[a])):
                src, dst = ends(a, peer, p)
                remote.append(pltpu.make_async_remote_copy(
                    src_ref=src, dst_ref=dst, send_sem=send_sems.at[nrem], recv_sem=recv_sems.at[nrem],
                    device_id=(px, py, pc), device_id_type=MESH))
                nrem += 1
    return local, remote


def _with_exchange(body, args, arrs, bcast, *, name, grid, in_specs, out_specs, out_shape, scratch_shapes=(), **kw):
    n_in, n_out, n_sc, nc = len(args), len(out_shape), len(scratch_shapes), len(arrs)

    def wrapped(*refs):
        core_in, cin = refs[:n_in], refs[n_in:n_in + nc]
        core_out = refs[n_in + nc:n_in + nc + n_out]
        cout = refs[n_in + nc + n_out:n_in + 2 * nc + n_out]
        rest = refs[n_in + 2 * nc + n_out:]
        core_sc, sems = rest[:n_sc], rest[n_sc:]

        def start():
            local, remote = _direct_copies(cin, cout, sems, bcast)
            for cp in local + remote:
                cp.start()

        def wait():
            local, remote = _direct_copies(cin, cout, sems, bcast)
            for cp in remote + local:
                cp.wait()

        if not grid:
            start()
            body(*core_in, *core_out, *core_sc)
            wait()
            return
        first, last = None, None
        for d, g in enumerate(grid):
            i = pl.program_id(d)
            first = (i == 0) if first is None else jnp.logical_and(first, i == 0)
            last = (i == g - 1) if last is None else jnp.logical_and(last, i == g - 1)
        pl.when(first)(start)
        body(*core_in, *core_out, *core_sc)
        pl.when(last)(wait)

    ex_shape = [jax.ShapeDtypeStruct(_exchanged_shape(a, mode), a.dtype) for a, mode in zip(arrs, bcast)]
    ncopies = sum(_n_pieces(mode) for mode in bcast)
    sems = [pltpu.SemaphoreType.DMA((ncopies * (N_DEV - 1),)), pltpu.SemaphoreType.DMA((ncopies * (N_DEV - 1),)),
            pltpu.SemaphoreType.DMA((ncopies,))]
    res = _call(wrapped, name=name, grid=grid, in_specs=list(in_specs) + [ANY] * nc,
                out_specs=list(out_specs) + [ANY] * nc, out_shape=list(out_shape) + ex_shape,
                scratch_shapes=list(scratch_shapes) + sems, **kw)(*args, *arrs)
    return list(res[:n_out]), list(res[n_out:])


def _run(body, args, exchange, **kw):
    if exchange is None:
        return list(_call(body, **kw)(*args))
    outs, got = _with_exchange(body, args, *exchange, **kw)
    return outs + [got]


def _exchange(arrs, bcast, name):
    return _with_exchange(lambda: None, [], arrs, bcast, name=name, grid=(), in_specs=[], out_specs=[], out_shape=[],
                          compiler_params=pltpu.CompilerParams(has_side_effects=True))[1]


def _gather_two_level(arrs, name):
    n = len(arrs)

    def body(*refs):
        ins, outs = refs[:n], refs[n:2 * n]
        send_sems, recv_sems, loc_sems = refs[2 * n:]
        x, y, c = lax.axis_index("x"), lax.axis_index("y"), lax.axis_index("c")
        me, sibling = (x, y, c), (x, y, 1 - c)
        chips = [(1 - x, y), (x, 1 - y), (1 - x, 1 - y)]

        def slot(a, p):
            return outs[a].at[4 * p[0] + 2 * p[1] + p[2]]

        def copy(a, k, block, to, src=None):
            return pltpu.make_async_remote_copy(
                src_ref=slot(a, block) if src is None else src, dst_ref=slot(a, block),
                send_sem=send_sems.at[a * 7 + k], recv_sem=recv_sems.at[a * 7 + k],
                device_id=to, device_id_type=MESH)

        mine = [pltpu.make_async_copy(ins[a], slot(a, me), loc_sems.at[a]) for a in range(n)]
        first = []
        for a in range(n):
            first.append(copy(a, 0, me, sibling, src=ins[a]))
            first += [copy(a, 1 + j, me, (*chip, c), src=ins[a]) for j, chip in enumerate(chips)]
        for cp in mine + first:
            cp.start()
        passed = []
        for j, chip in enumerate(chips):
            for a in range(n):
                copy(a, 1 + j, (*chip, c), me).wait_recv()
                fwd = copy(a, 4 + j, (*chip, c), sibling)
                fwd.start()
                passed.append(fwd)
        for a in range(n):
            copy(a, 0, sibling, me).wait_recv()
        for j, chip in enumerate(chips):
            for a in range(n):
                copy(a, 4 + j, (*chip, 1 - c), me).wait_recv()
        for cp in first + passed:
            cp.wait_send()
        for cp in mine:
            cp.wait()

    return _call(
        body, name=name, in_specs=[ANY] * n, out_specs=[ANY] * n,
        out_shape=[jax.ShapeDtypeStruct((N_DEV,) + a.shape, a.dtype) for a in arrs],
        scratch_shapes=[pltpu.SemaphoreType.DMA((n * 7,)), pltpu.SemaphoreType.DMA((n * 7,)),
                        pltpu.SemaphoreType.DMA((n,))],
        compiler_params=pltpu.CompilerParams(has_side_effects=True),
    )(*arrs)


def _sigmoid(x):
    return 1.0 / (1.0 + jnp.exp(-x))


def _log1p(e):
    u = 1.0 + e
    d = u - 1.0
    return jnp.where(d == 0.0, e, jnp.log(u) * (e / jnp.where(d == 0.0, 1.0, d)))


def _softplus(x):
    return jnp.maximum(x, 0.0) + _log1p(jnp.exp(-jnp.abs(x)))


_GELU_C = 0.7978845608028654


def _gelu(x):
    t = jnp.tanh(_GELU_C * (x + 0.044715 * x * x * x))
    return 0.5 * x * (1.0 + t)


def _gelu_and_grad(x):
    t = jnp.tanh(_GELU_C * (x + 0.044715 * x * x * x))
    g = 0.5 * x * (1.0 + t)
    dg = 0.5 * (1.0 + t) + 0.5 * x * (1.0 - t * t) * _GELU_C * (1.0 + 3.0 * 0.044715 * x * x)
    return g, dg


def _acc(ref, idx, val, first):
    @pl.when(first)
    def _():
        ref[idx] = val

    @pl.when(jnp.logical_not(first))
    def _():
        ref[idx] = ref[idx] + val


def _split3(x):
    hi = x.astype(BF16)
    r1 = x - hi.astype(F32)
    mid = r1.astype(BF16)
    lo = (r1 - mid.astype(F32)).astype(BF16)
    return hi, mid, lo


def _mm(a, b, *, name, ta=False, tb=False, out_dtype=F32, tm, tn, tk, bias=None, silu_a=False, exchange=None,
        b_cols=None):
    m, k = (a.shape[1], a.shape[0]) if ta else a.shape
    col0, n = b_cols if b_cols is not None else (0, b.shape[0] if tb else b.shape[1])
    assert m % tm == 0 and n % tn == 0 and k % tk == 0 and col0 % tn == 0, (name, m, n, k)
    jo = col0 // tn
    nk = k // tk
    dn = (((0 if ta else 1,), (1 if tb else 0,)), ((), ()))
    use_acc = nk > 1 and out_dtype != F32

    def body(*refs):
        a_ref, b_ref = refs[0], refs[1]
        bias_ref = refs[2] if bias is not None else None
        o_ref = refs[3 if bias is not None else 2]
        av = a_ref[...]
        if silu_a:
            av = av * _sigmoid(av)
        p = lax.dot_general(av.astype(BF16), b_ref[...].astype(BF16), dn, preferred_element_type=F32)
        if bias is not None:
            p = p + bias_ref[...]
        if nk == 1:
            o_ref[...] = p.astype(out_dtype)
            return
        kk = pl.program_id(2)
        acc = refs[-1] if use_acc else o_ref

        @pl.when(kk == 0)
        def _():
            acc[...] = p

        @pl.when(kk > 0)
        def _():
            acc[...] = acc[...] + p

        if use_acc:
            @pl.when(kk == nk - 1)
            def _():
                o_ref[...] = acc[...].astype(out_dtype)

    a_spec = pl.BlockSpec((tk, tm), lambda i, j, q: (q, i)) if ta else pl.BlockSpec((tm, tk), lambda i, j, q: (i, q))
    b_spec = (pl.BlockSpec((tn, tk), lambda i, j, q: (j + jo, q)) if tb
              else pl.BlockSpec((tk, tn), lambda i, j, q: (q, j + jo)))
    in_specs, args = [a_spec, b_spec], [a, b]
    if bias is not None:
        in_specs.append(pl.BlockSpec((1, tn), lambda i, j, q: (0, j)))
        args.append(bias)
    kw = dict(name=name, grid=(m // tm, n // tn, nk), in_specs=in_specs,
              out_specs=[pl.BlockSpec((tm, tn), lambda i, j, q: (i, j))],
              out_shape=[jax.ShapeDtypeStruct((m, n), out_dtype)],
              scratch_shapes=[pltpu.VMEM((tm, tn), F32)] if use_acc else [],
              compiler_params=_params())
    if exchange is None:
        return _call(body, **kw)(*args)[0]
    outs, got = _with_exchange(body, args, *exchange, **kw)
    return outs[0], got


def _tok_spec(ts):
    return pl.BlockSpec((1, ts, D_MODEL), lambda b, s: (b, s, 0))


def _seq_spec():
    return pl.BlockSpec((1, 1, D_MODEL), lambda b, s: (b, 0, 0))


def _vec_spec():
    return pl.BlockSpec((1, D_MODEL), lambda b, s: (0, 0))


def _norm_fwd(x, g, shift, scale, *, name, yprev=None, gate=None, ts=512):
    bsz, s, _ = x.shape
    res = yprev is not None

    def body(*refs):
        if res:
            x_ref, g_ref, sh_ref, sc_ref, y_ref, gt_ref, h_ref, xn_ref = refs
            xv = x_ref[0] + gt_ref[0] * y_ref[0]
            xn_ref[0] = xv
        else:
            x_ref, g_ref, sh_ref, sc_ref, h_ref = refs
            xv = x_ref[0]
        rstd = lax.rsqrt(jnp.mean(xv * xv, axis=-1, keepdims=True) + RMS_EPS)
        nv = xv * rstd * g_ref[...]
        h_ref[0] = (nv * (1.0 + sc_ref[0]) + sh_ref[0]).astype(BF16)

    in_specs = [_tok_spec(ts), _vec_spec(), _seq_spec(), _seq_spec()]
    args = [x, g, shift, scale]
    out_specs = [_tok_spec(ts)]
    out_shape = [jax.ShapeDtypeStruct(x.shape, BF16)]
    if res:
        in_specs += [_tok_spec(ts), _seq_spec()]
        args += [yprev, gate]
        out_specs.append(_tok_spec(ts))
        out_shape.append(jax.ShapeDtypeStruct(x.shape, F32))
    return _call(body, name=name, grid=(bsz, s // ts), in_specs=in_specs, out_specs=out_specs,
                 out_shape=out_shape, compiler_params=_params())(*args)


def _final(x1, yf, gate2, g, shift, scale, target, *, ts=512):
    bsz, s, _ = x1.shape
    ns = s // ts

    def body(x_ref, y_ref, gt_ref, g_ref, sh_ref, sc_ref, t_ref,
             lp_ref, dx_ref, dyf_ref, dsh_ref, dsc_ref, dgt_ref, dg_ref):
        b, si = pl.program_id(0), pl.program_id(1)
        yv, gt, gv = y_ref[0], gt_ref[0], g_ref[...]
        xv = x_ref[0] + gt * yv
        rstd = lax.rsqrt(jnp.mean(xv * xv, axis=-1, keepdims=True) + RMS_EPS)
        xhat = xv * rstd
        nv = xhat * gv
        sc1 = 1.0 + sc_ref[0]
        err = nv * sc1 + sh_ref[0] - t_ref[0]
        lp_ref[...] = jnp.zeros((1, 1, 8, 128), F32) + jnp.sum(err * err)
        dy = err * (1.0 / D_MODEL)
        dn = dy * sc1
        dxh = dn * gv
        dx = rstd * (dxh - xhat * jnp.mean(dxh * xhat, axis=-1, keepdims=True))
        dx_ref[0] = dx
        dyf_ref[0] = (dx * gt).astype(BF16)
        _acc(dsh_ref, 0, jnp.sum(dy, axis=0, keepdims=True), si == 0)
        _acc(dsc_ref, 0, jnp.sum(dy * nv, axis=0, keepdims=True), si == 0)
        _acc(dgt_ref, 0, jnp.sum(dx * yv, axis=0, keepdims=True), si == 0)
        _acc(dg_ref, slice(None), jnp.sum(dn * xhat, axis=0, keepdims=True), jnp.logical_and(b == 0, si == 0))

    seq_shape = jax.ShapeDtypeStruct((bsz, 1, D_MODEL), F32)
    return _call(
        body, name="final_loss", grid=(bsz, ns),
        in_specs=[_tok_spec(ts), _tok_spec(ts), _seq_spec(), _vec_spec(), _seq_spec(), _seq_spec(), _tok_spec(ts)],
        out_specs=[pl.BlockSpec((1, 1, 8, 128), lambda b, s: (b, s, 0, 0)), _tok_spec(ts), _tok_spec(ts),
                   _seq_spec(), _seq_spec(), _seq_spec(), _vec_spec()],
        out_shape=[jax.ShapeDtypeStruct((bsz, ns, 8, 128), F32), jax.ShapeDtypeStruct(x1.shape, F32),
                   jax.ShapeDtypeStruct(x1.shape, BF16), seq_shape, seq_shape, seq_shape,
                   jax.ShapeDtypeStruct((1, D_MODEL), F32)],
        compiler_params=_params(),
    )(x1, yf, gate2, g, shift, scale, target)


def _norm_bwd(x, dh, g, scale, dres, *, name, yprev=None, gate=None, ts=512):
    bsz, s, _ = x.shape
    res = yprev is not None

    def body(*refs):
        if res:
            (x_ref, dh_ref, g_ref, sc_ref, dr_ref, y_ref, gt_ref,
             dx_ref, dg_ref, dsh_ref, dsc_ref, dy_ref, dgt_ref) = refs
        else:
            x_ref, dh_ref, g_ref, sc_ref, dr_ref, dx_ref, dg_ref, dsh_ref, dsc_ref = refs
        b, si = pl.program_id(0), pl.program_id(1)
        xv, dhv, gv = x_ref[0], dh_ref[0], g_ref[...]
        rstd = lax.rsqrt(jnp.mean(xv * xv, axis=-1, keepdims=True) + RMS_EPS)
        xhat = xv * rstd
        dn = dhv * (1.0 + sc_ref[0])
        dxh = dn * gv
        dx = dr_ref[0] + rstd * (dxh - xhat * jnp.mean(dxh * xhat, axis=-1, keepdims=True))
        dx_ref[0] = dx
        _acc(dsh_ref, 0, jnp.sum(dhv, axis=0, keepdims=True), si == 0)
        _acc(dsc_ref, 0, jnp.sum(dhv * (xhat * gv), axis=0, keepdims=True), si == 0)
        _acc(dg_ref, slice(None), jnp.sum(dn * xhat, axis=0, keepdims=True), jnp.logical_and(b == 0, si == 0))
        if res:
            dy_ref[0] = (dx * gt_ref[0]).astype(BF16)
            _acc(dgt_ref, 0, jnp.sum(dx * y_ref[0], axis=0, keepdims=True), si == 0)

    seq_shape = jax.ShapeDtypeStruct((bsz, 1, D_MODEL), F32)
    in_specs = [_tok_spec(ts), _tok_spec(ts), _vec_spec(), _seq_spec(), _tok_spec(ts)]
    args = [x, dh, g, scale, dres]
    out_specs = [_tok_spec(ts), _vec_spec(), _seq_spec(), _seq_spec()]
    out_shape = [jax.ShapeDtypeStruct(x.shape, F32), jax.ShapeDtypeStruct((1, D_MODEL), F32), seq_shape, seq_shape]
    if res:
        in_specs += [_tok_spec(ts), _seq_spec()]
        args += [yprev, gate]
        out_specs += [_tok_spec(ts), _seq_spec()]
        out_shape += [jax.ShapeDtypeStruct(x.shape, BF16), seq_shape]
    return _call(body, name=name, grid=(bsz, s // ts), in_specs=in_specs, out_specs=out_specs,
                 out_shape=out_shape, compiler_params=_params())(*args)


def _lru_gates(xr, wa, ba, wi, bi, sp):
    xb = xr.astype(BF16)
    r = _sigmoid(jnp.dot(xb, wa, preferred_element_type=F32) + ba)
    i = _sigmoid(jnp.dot(xb, wi, preferred_element_type=F32) + bi)
    la = -LRU_C * r * sp
    a = jnp.exp(la)
    mult = jnp.sqrt(-jnp.tanh(la) * (a * a + 1.0))
    return r, i, a, mult


def _scan_forward(a, u, h_in, row):
    n = a.shape[0]
    r8 = row & 7
    for k in (1, 2, 4):
        m = r8 >= k
        u = jnp.where(m, a * pltpu.roll(u, k, 0) + u, u)
        a = jnp.where(m, a * pltpu.roll(a, k, 0), a)
    carry = jnp.broadcast_to(h_in, (8, a.shape[1]))
    out = []
    for j in range(n // 8):
        hj = u[8 * j:8 * j + 8] + a[8 * j:8 * j + 8] * carry
        out.append(hj)
        carry = jnp.broadcast_to(hj[7:8], hj.shape)
    return jnp.concatenate(out, axis=0)


def _scan_backward(a_up, g, g_in, row):
    n = g.shape[0]
    r8 = row & 7
    for k in (1, 2, 4):
        m = r8 < 8 - k
        g = jnp.where(m, a_up * pltpu.roll(g, n - k, 0) + g, g)
        a_up = jnp.where(m, a_up * pltpu.roll(a_up, n - k, 0), a_up)
    carry = jnp.broadcast_to(g_in, (8, g.shape[1]))
    out = [None] * (n // 8)
    for j in reversed(range(n // 8)):
        gj = g[8 * j:8 * j + 8] + a_up[8 * j:8 * j + 8] * carry
        out[j] = gj
        carry = jnp.broadcast_to(gj[0:1], gj.shape)
    return jnp.concatenate(out, axis=0)


def _rnn_fwd(z, wc, bc, wa, ba, wi, bi, lam, exchange=None):
    bsz, s, _ = z.shape
    ch = min(SCAN_CHUNK, s)
    nch = s // ch

    def body(z_ref, wc_ref, bc_ref, wa_ref, ba_ref, wi_ref, bi_ref, lam_ref, h_ref, y_ref):
        wcv, bcv = wc_ref[...], bc_ref[...]
        wav, wiv = wa_ref[0].astype(BF16), wi_ref[0].astype(BF16)
        bav, biv = ba_ref[...], bi_ref[...]
        sp = _softplus(-lam_ref[...])
        row = lax.broadcasted_iota(jnp.int32, (ch, RNN_BLK), 0)

        def chunk(ci, carry):
            prev8, hc = carry
            st = pl.multiple_of(ci * ch, ch)
            blk = z_ref[0, pl.ds(st, ch), :]
            xp, gr = blk[:, :RNN_BLK], blk[:, RNN_BLK:]
            xe = jnp.concatenate([prev8, xp], axis=0)
            xr = bcv + wcv[3:4] * xp
            for j in range(1, 4):
                xr = xr + wcv[3 - j:4 - j] * pltpu.roll(xe, j, 0)[8:]
            _, i, a, mult = _lru_gates(xr, wav, bav, wiv, biv, sp)
            hh = _scan_forward(a, mult * (i * xr), hc, row)
            h_ref[0, pl.ds(st, ch), :] = hh
            y_ref[0, pl.ds(st, ch), :] = (_gelu(gr) * hh).astype(BF16)
            return xp[ch - 8:], hh[ch - 1:]

        lax.fori_loop(0, nch, chunk, (jnp.zeros((8, RNN_BLK), F32), jnp.zeros((1, RNN_BLK), F32)))

    vec = pl.BlockSpec((1, RNN_BLK), lambda b, n: (0, n))
    mat = pl.BlockSpec((1, RNN_BLK, RNN_BLK), lambda b, n: (n, 0, 0))
    out = pl.BlockSpec((1, s, RNN_BLK), lambda b, n: (b, 0, n))
    return _run(
        body, [z, wc, bc, wa, ba, wi, bi, lam], exchange, name="rnn_fwd", grid=(bsz, N_BLK),
        in_specs=[pl.BlockSpec((1, s, 2 * RNN_BLK), lambda b, n: (b, 0, XG0R // (2 * RNN_BLK) + n)),
                  pl.BlockSpec((4, RNN_BLK), lambda b, n: (0, n)), vec, mat, vec, mat, vec, vec],
        out_specs=[out, out],
        out_shape=[jax.ShapeDtypeStruct((bsz, s, D_RNN), F32), jax.ShapeDtypeStruct((bsz, s, D_RNN), BF16)],
        compiler_params=_params())


def _rnn_bwd(z, h, dy, dz, wc, bc, wa, ba, wi, bi, lam, exchange=None):
    bsz, s, _ = z.shape
    ch = min(SCAN_CHUNK, s)
    nch = s // ch

    def body(z_ref, h_ref, dy_ref, dz_in, wc_ref, bc_ref, wa_ref, ba_ref, wi_ref, bi_ref, lam_ref,
             dz_ref, dwa_ref, dwi_ref, dsm_ref):
        del dz_in
        b = pl.program_id(1)
        wcv, bcv = wc_ref[...], bc_ref[...]
        wav, wiv = wa_ref[0].astype(BF16), wi_ref[0].astype(BF16)
        bav, biv = ba_ref[...], bi_ref[...]
        lamv = lam_ref[...]
        sp = _softplus(-lamv)
        row = lax.broadcasted_iota(jnp.int32, (ch, RNN_BLK), 0)

        @pl.when(b == 0)
        def _():
            dwa_ref[...] = jnp.zeros_like(dwa_ref)
            dwi_ref[...] = jnp.zeros_like(dwi_ref)
            dsm_ref[...] = jnp.zeros_like(dsm_ref)

        def chunk(k, carry):
            a_next, g_next, dxr_next8 = carry
            ci = nch - 1 - k
            st = pl.multiple_of(ci * ch, ch)
            pst = pl.multiple_of(jnp.maximum(st - 8, 0), 8)
            has_prev = (ci > 0).astype(F32)
            blk = z_ref[0, pl.ds(st, ch), :]
            xp, gr = blk[:, :RNN_BLK], blk[:, RNN_BLK:]
            xprev8 = z_ref[0, pl.ds(pst, 8), :][:, :RNN_BLK] * has_prev
            hprev8 = h_ref[0, pl.ds(pst, 8), :] * has_prev
            xe = jnp.concatenate([xprev8, xp], axis=0)
            xs = [xp] + [pltpu.roll(xe, j, 0)[8:] for j in range(1, 4)]
            xr = bcv
            for kk in range(4):
                xr = xr + wcv[kk:kk + 1] * xs[3 - kk]
            r, i, a, mult = _lru_gates(xr, wav, bav, wiv, biv, sp)
            hh = h_ref[0, pl.ds(st, ch), :]
            hm1 = pltpu.roll(jnp.concatenate([hprev8, hh], axis=0), 1, 0)[8:]
            dyv = dy_ref[0, pl.ds(st, ch), :]
            ge, dge = _gelu_and_grad(gr)
            dgr = dyv * hh * dge
            gg = _scan_backward(jnp.where(row == ch - 1, a_next, pltpu.roll(a, ch - 1, 0)), dyv * ge, g_next, row)
            da = gg * hm1
            ix = i * xr
            dmult = gg * ix
            di = gg * mult * xr
            dxr = gg * mult * i
            dla = da * a - dmult * (a * a) / mult
            dr = dla * (-LRU_C * sp)
            dsp = jnp.sum(dla * (-LRU_C * r), axis=0, keepdims=True)
            dpa = dr * r * (1.0 - r)
            dpi = di * i * (1.0 - i)
            dpab, dpib, xrb = dpa.astype(BF16), dpi.astype(BF16), xr.astype(BF16)
            nt = (((1,), (1,)), ((), ()))
            tn = (((0,), (0,)), ((), ()))
            dxr = dxr + lax.dot_general(dpab, wav, nt, preferred_element_type=F32)
            dxr = dxr + lax.dot_general(dpib, wiv, nt, preferred_element_type=F32)
            dwa_ref[0] = dwa_ref[0] + lax.dot_general(xrb, dpab, tn, preferred_element_type=F32)
            dwi_ref[0] = dwi_ref[0] + lax.dot_general(xrb, dpib, tn, preferred_element_type=F32)
            rows = [jnp.sum(dxr * xs[3 - q], axis=0, keepdims=True) for q in range(4)]
            rows += [jnp.sum(dxr, axis=0, keepdims=True), jnp.sum(dpa, axis=0, keepdims=True),
                     jnp.sum(dpi, axis=0, keepdims=True), dsp * (-_sigmoid(-lamv))]
            dsm_ref[...] = dsm_ref[...] + jnp.concatenate(rows, axis=0)
            dxe = jnp.concatenate([dxr, dxr_next8], axis=0)
            dxp = wcv[3:4] * dxr
            for j in range(1, 4):
                dxp = dxp + wcv[3 - j:4 - j] * pltpu.roll(dxe, ch + 8 - j, 0)[:ch]
            dz_ref[0, pl.ds(st, ch), :] = jnp.concatenate([dxp, dgr], axis=1).astype(BF16)
            return a[0:1], gg[0:1], dxr[0:8]

        lax.fori_loop(0, nch, chunk, (jnp.zeros((1, RNN_BLK), F32), jnp.zeros((1, RNN_BLK), F32),
                                      jnp.zeros((8, RNN_BLK), F32)))

    vec = pl.BlockSpec((1, RNN_BLK), lambda n, b: (0, n))
    mat = pl.BlockSpec((1, RNN_BLK, RNN_BLK), lambda n, b: (n, 0, 0))
    tok = pl.BlockSpec((1, s, RNN_BLK), lambda n, b: (b, 0, n))
    zblk = pl.BlockSpec((1, s, 2 * RNN_BLK), lambda n, b: (b, 0, XG0R // (2 * RNN_BLK) + n))
    dzblk = pl.BlockSpec((1, s, 2 * RNN_BLK), lambda n, b: (b, 0, XG0 // (2 * RNN_BLK) + n))
    return _run(
        body, [z, h, dy, dz, wc, bc, wa, ba, wi, bi, lam], exchange, name="rnn_bwd", grid=(N_BLK, bsz),
        in_specs=[zblk, tok, tok, ANY, pl.BlockSpec((4, RNN_BLK), lambda n, b: (0, n)), vec, mat, vec, mat, vec, vec],
        out_specs=[dzblk, mat, mat, pl.BlockSpec((8, RNN_BLK), lambda n, b: (0, n))],
        out_shape=[jax.ShapeDtypeStruct(dz.shape, dz.dtype), jax.ShapeDtypeStruct((N_BLK, RNN_BLK, RNN_BLK), F32),
                   jax.ShapeDtypeStruct((N_BLK, RNN_BLK, RNN_BLK), F32), jax.ShapeDtypeStruct((8, D_RNN), F32)],
        input_output_aliases={3: 0},
        compiler_params=_params())


def _tri(n, upper):
    r = lax.broadcasted_iota(jnp.int32, (n, n), 0)
    c = lax.broadcasted_iota(jnp.int32, (n, n), 1)
    return (r <= c if upper else r >= c).astype(BF16)


def _exact_dot(x, t):
    hi, mid, lo = _split3(x)
    return (jnp.dot(hi, t, preferred_element_type=F32) + jnp.dot(mid, t, preferred_element_type=F32)
            + jnp.dot(lo, t, preferred_element_type=F32))


def _fgate_fwd(flt, bfg):
    bsz, nh, s = flt.shape
    blk = min(CUM_BLK, s)

    def body(fl_ref, b_ref, f_ref):
        tri = _tri(blk, True)
        carry = jnp.zeros((nh, 1), F32)
        for q in range(s // blk):
            xq = fl_ref[0, :, q * blk:(q + 1) * blk] + b_ref[...]
            fq = _exact_dot(-_softplus(-xq), tri) + carry
            f_ref[0, :, q * blk:(q + 1) * blk] = fq
            carry = fq[:, blk - 1:blk]

    spec = pl.BlockSpec((1, nh, s), lambda b: (b, 0, 0))
    return _call(body, name="fgate_fwd", grid=(bsz,), in_specs=[spec, pl.BlockSpec((nh, 1), lambda b: (0, 0))],
                 out_specs=spec, out_shape=jax.ShapeDtypeStruct(flt.shape, F32), compiler_params=_params())(flt, bfg)


def _fgate_bwd(dft, flt, bfg):
    bsz, nh, s = flt.shape
    blk = min(CUM_BLK, s)

    def body(df_ref, fl_ref, b_ref, dfl_ref, db_ref):
        b = pl.program_id(0)
        tri = _tri(blk, False)
        carry = jnp.zeros((nh, 1), F32)
        tot = jnp.zeros((nh, 1), F32)
        for q in reversed(range(s // blk)):
            sl = slice(q * blk, (q + 1) * blk)
            dlq = _exact_dot(df_ref[0, :, sl], tri) + carry
            carry = dlq[:, 0:1]
            dflq = dlq * _sigmoid(-(fl_ref[0, :, sl] + b_ref[...]))
            dfl_ref[0, :, sl] = dflq
            tot = tot + jnp.sum(dflq, axis=1, keepdims=True)
        _acc(db_ref, slice(None), tot, b == 0)

    spec = pl.BlockSpec((1, nh, s), lambda b: (b, 0, 0))
    small = pl.BlockSpec((nh, 1), lambda b: (0, 0))
    return _call(body, name="fgate_bwd", grid=(bsz,), in_specs=[spec, spec, small], out_specs=[spec, small],
                 out_shape=[jax.ShapeDtypeStruct(flt.shape, F32), jax.ShapeDtypeStruct((nh, 1), F32)],
                 compiler_params=_params())(dft, flt, bfg)


_NT = (((1,), (1,)), ((), ()))
_TN = (((0,), (0,)), ((), ()))


def _attn_block_scores(q_b, z_ref, frow_ref, k0, tq):
    kk = z_ref[0, k0:k0 + tq, HEAD_DIM:2 * HEAD_DIM]
    return lax.dot_general(q_b, kk, _NT, preferred_element_type=F32) - frow_ref[0, 0, :, k0:k0 + tq]


def _causal(tq):
    return lax.broadcasted_iota(jnp.int32, (tq, tq), 0) >= lax.broadcasted_iota(jnp.int32, (tq, tq), 1)


def _attn_fwd(z, fcol, frow, exchange=None):
    bsz, s, _ = z.shape
    tq = min(ATTN_TQ, s)
    scale = HEAD_DIM ** -0.5

    def body(z_ref, fcol_ref, frow_ref, o_ref, lse_ref):
        for qb in range(s // tq):
            q0 = qb * tq
            q_b = (z_ref[0, q0:q0 + tq, :HEAD_DIM].astype(F32) * scale).astype(BF16)
            m = jnp.full((tq, 1), -jnp.inf, F32)
            l = jnp.zeros((tq, 1), F32)
            acc = jnp.zeros((tq, HEAD_DIM), F32)
            for kb in range(qb + 1):
                k0 = kb * tq
                sc = _attn_block_scores(q_b, z_ref, frow_ref, k0, tq)
                if kb == qb:
                    sc = jnp.where(_causal(tq), sc, -jnp.inf)
                m_new = jnp.maximum(m, jnp.max(sc, axis=-1, keepdims=True))
                alpha = jnp.exp(m - m_new)
                p = jnp.exp(sc - m_new)
                l = alpha * l + jnp.sum(p, axis=-1, keepdims=True)
                vv = z_ref[0, k0:k0 + tq, 2 * HEAD_DIM:]
                acc = alpha * acc + jnp.dot(p.astype(BF16), vv, preferred_element_type=F32)
                m = m_new
            o_ref[0, q0:q0 + tq, :] = (acc / l).astype(BF16)
            lse_ref[0, 0, q0:q0 + tq, :] = m + jnp.log(l) + fcol_ref[0, 0, q0:q0 + tq, :]

    return _run(
        body, [z, fcol, frow], exchange, name="attn_fwd", grid=(bsz, N_HEADS),
        in_specs=[pl.BlockSpec((1, s, 3 * HEAD_DIM), lambda b, h: (b, 0, h)),
                  pl.BlockSpec((1, 1, s, 1), lambda b, h: (b, h, 0, 0)),
                  pl.BlockSpec((1, 1, 1, s), lambda b, h: (b, h, 0, 0))],
        out_specs=[pl.BlockSpec((1, s, HEAD_DIM), lambda b, h: (b, 0, h)),
                   pl.BlockSpec((1, 1, s, 1), lambda b, h: (b, h, 0, 0))],
        out_shape=[jax.ShapeDtypeStruct((bsz, s, N_HEADS * HEAD_DIM), BF16),
                   jax.ShapeDtypeStruct((bsz, N_HEADS, s, 1), F32)],
        compiler_params=_params())


def _attn_bwd(z, do, lse, fcol, frow, dz, exchange=None):
    bsz, s, _ = z.shape
    tq = min(ATTN_TQ, s)
    scale = HEAD_DIM ** -0.5

    def body(z_ref, do_ref, lse_ref, fcol_ref, frow_ref, dz_in, dz_ref, df_ref, dk_acc, dv_acc):
        del dz_in
        dk_acc[...] = jnp.zeros_like(dk_acc)
        dv_acc[...] = jnp.zeros_like(dv_acc)
        df_ref[...] = jnp.zeros_like(df_ref)
        for qb in range(s // tq):
            q0, ln = qb * tq, (qb + 1) * tq
            q_b = (z_ref[0, q0:ln, :HEAD_DIM].astype(F32) * scale).astype(BF16)
            kk = z_ref[0, :ln, HEAD_DIM:2 * HEAD_DIM]
            vv = z_ref[0, :ln, 2 * HEAD_DIM:]
            do_b = do_ref[0, q0:ln, :]
            lse_q = lse_ref[0, 0, q0:ln, :] - fcol_ref[0, 0, q0:ln, :]
            sc = lax.dot_general(q_b, kk, _NT, preferred_element_type=F32) - frow_ref[0, 0, :, :ln] - lse_q
            mask = (q0 + lax.broadcasted_iota(jnp.int32, (tq, ln), 0)) >= lax.broadcasted_iota(jnp.int32, (tq, ln), 1)
            p = jnp.where(mask, jnp.exp(sc), 0.0)
            pb = p.astype(BF16)
            dv_acc[:ln, :] = dv_acc[:ln, :] + lax.dot_general(pb, do_b, _TN, preferred_element_type=F32)
            dp = lax.dot_general(do_b, vv, _NT, preferred_element_type=F32)
            ds = p * (dp - jnp.sum(p * dp, axis=-1, keepdims=True))
            dsb = ds.astype(BF16)
            dq = jnp.dot(dsb, kk, preferred_element_type=F32) * scale
            dz_ref[0, q0:ln, :HEAD_DIM] = dq.astype(BF16)
            dk_acc[:ln, :] = dk_acc[:ln, :] + lax.dot_general(dsb, q_b, _TN, preferred_element_type=F32)
            df_ref[0, 0, :, :ln] = df_ref[0, 0, :, :ln] - jnp.sum(ds, axis=0, keepdims=True)
        dz_ref[0, :, HEAD_DIM:2 * HEAD_DIM] = dk_acc[...].astype(BF16)
        dz_ref[0, :, 2 * HEAD_DIM:] = dv_acc[...].astype(BF16)

    zblk = pl.BlockSpec((1, s, 3 * HEAD_DIM), lambda b, h: (b, 0, h))
    hblk = pl.BlockSpec((1, s, HEAD_DIM), lambda b, h: (b, 0, h))
    col = pl.BlockSpec((1, 1, s, 1), lambda b, h: (b, h, 0, 0))
    rowspec = pl.BlockSpec((1, 1, 1, s), lambda b, h: (b, h, 0, 0))
    return _run(
        body, [z, do, lse, fcol, frow, dz], exchange, name="attn_bwd", grid=(bsz, N_HEADS),
        in_specs=[zblk, hblk, col, col, rowspec, ANY],
        out_specs=[zblk, rowspec],
        out_shape=[jax.ShapeDtypeStruct(dz.shape, dz.dtype), jax.ShapeDtypeStruct((bsz, N_HEADS, 1, s), F32)],
        scratch_shapes=[pltpu.VMEM((s, HEAD_DIM), F32), pltpu.VMEM((s, HEAD_DIM), F32)],
        input_output_aliases={5: 0},
        compiler_params=_params())


def _merge_fwd(z2, pr, pa, *, tr=512):
    t = z2.shape[0]

    def body(mgr_ref, mga_ref, pr_ref, pa_ref, o_ref):
        gr_ = _sigmoid(mgr_ref[...])
        ga = _sigmoid(mga_ref[...])
        o_ref[...] = (gr_ * pr_ref[...] + ga * pa_ref[...]).astype(BF16)

    tok = pl.BlockSpec((tr, D_MODEL), lambda i: (i, 0))
    return _call(body, name="merge_fwd", grid=(t // tr,),
                 in_specs=[pl.BlockSpec((tr, D_MODEL), lambda i: (i, MG0R // D_MODEL)),
                           pl.BlockSpec((tr, D_MODEL), lambda i: (i, MG0R // D_MODEL + 1)), tok, tok],
                 out_specs=tok, out_shape=jax.ShapeDtypeStruct((t, D_MODEL), BF16),
                 compiler_params=_params())(z2, z2, pr, pa)


def _merge_bwd(dm, z2, pr, pa, *, tr=512):
    t = z2.shape[0]

    def body(dm_ref, mgr_ref, mga_ref, pr_ref, pa_ref, dz_ref, dpr_ref, dpa_ref):
        dmv = dm_ref[...]
        gr_ = _sigmoid(mgr_ref[...])
        ga = _sigmoid(mga_ref[...])
        dpr_ref[...] = (dmv * gr_).astype(BF16)
        dpa_ref[...] = (dmv * ga).astype(BF16)
        dz_ref[:, :D_MODEL] = (dmv * pr_ref[...] * gr_ * (1.0 - gr_)).astype(BF16)
        dz_ref[:, D_MODEL:] = (dmv * pa_ref[...] * ga * (1.0 - ga)).astype(BF16)

    tok = pl.BlockSpec((tr, D_MODEL), lambda i: (i, 0))
    mg = pl.BlockSpec((tr, 2 * D_MODEL), lambda i: (i, MG0 // (2 * D_MODEL)))
    return _call(body, name="merge_bwd", grid=(t // tr,),
                 in_specs=[tok, pl.BlockSpec((tr, D_MODEL), lambda i: (i, MG0R // D_MODEL)),
                           pl.BlockSpec((tr, D_MODEL), lambda i: (i, MG0R // D_MODEL + 1)), tok, tok],
                 out_specs=[mg, tok, tok],
                 out_shape=[jax.ShapeDtypeStruct((t, NZ), BF16), jax.ShapeDtypeStruct((t, D_MODEL), BF16),
                            jax.ShapeDtypeStruct((t, D_MODEL), BF16)],
                 compiler_params=_params())(dm, z2, z2, pr, pa)


def _fl_write(dfl_blk, dz):
    bsz, s, w = dfl_blk.shape

    def body(src_ref, dz_in, dz_ref):
        del dz_in
        dz_ref[...] = src_ref[...]

    return _call(body, name="fl_write", grid=(bsz,),
                 in_specs=[pl.BlockSpec((1, s, w), lambda b: (b, 0, 0)), ANY],
                 out_specs=pl.BlockSpec((1, s, w), lambda b: (b, 0, FL0 // w)),
                 out_shape=jax.ShapeDtypeStruct(dz.shape, dz.dtype), input_output_aliases={1: 0},
                 compiler_params=_params())(dfl_blk, dz)


def _ffn_fwd(zf, wc, bc):
    bsz, s, _ = zf.shape
    cw = 128

    def body(zf_ref, wc_ref, bc_ref, o_ref):
        gfp, uf = zf_ref[0, :, :cw].astype(F32), zf_ref[0, :, cw:].astype(F32)
        wcv = wc_ref[...]
        row = lax.broadcasted_iota(jnp.int32, (s, cw), 0)
        gf = bc_ref[...] + wcv[2:3] * gfp
        for j in (1, 2):
            gf = gf + wcv[2 - j:3 - j] * jnp.where(row >= j, pltpu.roll(gfp, j, 0), 0.0)
        o_ref[0] = (_gelu(gf) * uf).astype(BF16)

    return _call(body, name="ffn_fwd", grid=(bsz, D_FF // cw),
                 in_specs=[pl.BlockSpec((1, s, 2 * cw), lambda b, j: (b, 0, j)),
                           pl.BlockSpec((3, cw), lambda b, j: (0, j)), pl.BlockSpec((1, cw), lambda b, j: (0, j))],
                 out_specs=pl.BlockSpec((1, s, cw), lambda b, j: (b, 0, j)),
                 out_shape=jax.ShapeDtypeStruct((bsz, s, D_FF), BF16), compiler_params=_params())(zf, wc, bc)


def _ffn_bwd(dact, zf, wc, bc, exchange=None):
    bsz, s, _ = zf.shape
    cw = 128

    def body(da_ref, zf_ref, wc_ref, bc_ref, dzf_ref, dsm_ref):
        b = pl.program_id(1)
        gfp, uf = zf_ref[0, :, :cw].astype(F32), zf_ref[0, :, cw:].astype(F32)
        wcv = wc_ref[...]
        row = lax.broadcasted_iota(jnp.int32, (s, cw), 0)
        sh = [gfp] + [jnp.where(row >= j, pltpu.roll(gfp, j, 0), 0.0) for j in (1, 2)]
        gf = bc_ref[...]
        for kk in range(3):
            gf = gf + wcv[kk:kk + 1] * sh[2 - kk]
        ge, dge = _gelu_and_grad(gf)
        dav = da_ref[0].astype(F32)
        dgf = dav * uf * dge
        dgfp = wcv[2:3] * dgf
        for j in (1, 2):
            dgfp = dgfp + wcv[2 - j:3 - j] * jnp.where(row < s - j, pltpu.roll(dgf, s - j, 0), 0.0)
        dzf_ref[0] = jnp.concatenate([dgfp, dav * ge], axis=1).astype(BF16)
        rows = [jnp.sum(dgf * sh[2 - kk], axis=0, keepdims=True) for kk in range(3)]
        rows += [jnp.sum(dgf, axis=0, keepdims=True), jnp.zeros((4, cw), F32)]
        _acc(dsm_ref, slice(None), jnp.concatenate(rows, axis=0), b == 0)

    return _run(body, [dact, zf, wc, bc], exchange, name="ffn_bwd", grid=(D_FF // cw, bsz),
                in_specs=[pl.BlockSpec((1, s, cw), lambda j, b: (b, 0, j)),
                          pl.BlockSpec((1, s, 2 * cw), lambda j, b: (b, 0, j)),
                          pl.BlockSpec((3, cw), lambda j, b: (0, j)), pl.BlockSpec((1, cw), lambda j, b: (0, j))],
                out_specs=[pl.BlockSpec((1, s, 2 * cw), lambda j, b: (b, 0, j)),
                           pl.BlockSpec((8, cw), lambda j, b: (0, j))],
                out_shape=[jax.ShapeDtypeStruct(zf.shape, BF16), jax.ShapeDtypeStruct((8, D_FF), F32)],
                compiler_params=_params())


def _adamw(w, m, v, parts, *, name):
    r, c = w.shape
    p = parts.shape[0]
    by_cols = r > 512 and r % 256 != 0
    tr, tc = (r, 256) if by_cols else (r if r <= 512 else 256, c)
    assert r % tr == 0 and c % tc == 0 and parts.shape[1:] == (r, c), (name, w.shape, parts.shape)
    c1 = 1.0 / (1.0 - ADAM_B1 ** ADAM_STEP)
    c2 = 1.0 / (1.0 - ADAM_B2 ** ADAM_STEP)

    def body(w_ref, m_ref, v_ref, p_ref, g_ref, d_ref, mo_ref, vo_ref):
        g = p_ref[0].astype(F32)
        for q in range(1, p):
            g = g + p_ref[q].astype(F32)
        mn = ADAM_B1 * m_ref[...] + (1.0 - ADAM_B1) * g
        vn = ADAM_B2 * v_ref[...] + (1.0 - ADAM_B2) * (g * g)
        g_ref[...] = g
        mo_ref[...] = mn
        vo_ref[...] = vn
        d_ref[...] = -ADAM_LR * ((mn * c1) / (jnp.sqrt(vn * c2) + ADAM_EPS) + ADAM_WD * w_ref[...])

    spec = pl.BlockSpec((tr, tc), (lambda i: (0, i)) if by_cols else (lambda i: (i, 0)))
    shp = jax.ShapeDtypeStruct((r, c), F32)
    return _call(body, name=name, grid=(c // tc if by_cols else r // tr,),
                 in_specs=[spec, spec, spec,
                           pl.BlockSpec((p, tr, tc), (lambda i: (0, 0, i)) if by_cols else (lambda i: (0, i, 0)))],
                 out_specs=[spec] * 4, out_shape=[shp] * 4, compiler_params=_params())(w, m, v, parts)


def _interleave(w, groups, width):
    n = w.shape[0] // (groups * width)
    return jnp.concatenate([w[(g * n + j) * width:(g * n + j + 1) * width]
                            for j in range(n) for g in range(groups)], axis=0)


def _deinterleave(w, groups, width):
    n = w.shape[0] // (groups * width)
    return jnp.concatenate([w[(j * groups + g) * width:(j * groups + g + 1) * width]
                            for g in range(groups) for j in range(n)], axis=0)


def _to_z_layout(wt):
    xg = _interleave(wt[:2 * D_RNN], 2, RNN_BLK)
    qkv = _interleave(wt[2 * D_RNN:2 * D_RNN + 3 * D_MODEL], 3, HEAD_DIM)
    fl0 = 2 * D_RNN + 3 * D_MODEL
    pad = jnp.zeros((MG0 - FL0 - N_HEADS, wt.shape[1]), wt.dtype)
    return jnp.concatenate([qkv, xg, wt[fl0:fl0 + N_HEADS], pad, wt[fl0 + N_HEADS:]], axis=0)


def _from_z_layout(wt):
    qkv = _deinterleave(wt[QKV0:XG0], 3, HEAD_DIM)
    xg = _deinterleave(wt[XG0:FL0], 2, RNN_BLK)
    return jnp.concatenate([xg, qkv, wt[FL0:FL0 + N_HEADS], wt[MG0:]], axis=0)


def _col_gather(g):
    return g.transpose(1, 0, 2).reshape(g.shape[1], -1)


def _col_scatter(w):
    return w.reshape(w.shape[0], N_DEV, -1).transpose(1, 0, 2)


def kernel(x, c, w_ada, b_ada, g_norm1, w_in, w_rnn_conv, b_rnn_conv, w_lru_a, b_lru_a, w_lru_i, b_lru_i, lru_lambda, b_fgate, w_proj_rnn, w_proj_attn, w_out, g_norm2, w_ffn_up, w_ffn_conv, b_ffn_conv, w_ffn_down, w_ada_final, b_ada_final, g_final, loss_target, m_w_ada, m_b_ada, m_g_norm1, m_w_in, m_w_rnn_conv, m_b_rnn_conv, m_w_lru_a, m_b_lru_a, m_w_lru_i, m_b_lru_i, m_lru_lambda, m_b_fgate, m_w_proj_rnn, m_w_proj_attn, m_w_out, m_g_norm2, m_w_ffn_up, m_w_ffn_conv, m_b_ffn_conv, m_w_ffn_down, m_w_ada_final, m_b_ada_final, m_g_final, v_w_ada, v_b_ada, v_g_norm1, v_w_in, v_w_rnn_conv, v_b_rnn_conv, v_w_lru_a, v_b_lru_a, v_w_lru_i, v_b_lru_i, v_lru_lambda, v_b_fgate, v_w_proj_rnn, v_w_proj_attn, v_w_out, v_g_norm2, v_w_ffn_up, v_w_ffn_conv, v_b_ffn_conv, v_w_ffn_down, v_w_ada_final, v_b_ada_final, v_g_final):
    args = dict(locals())
    bsz, s, _ = x.shape
    t = bsz * s
    nb = N_DEV * bsz
    me = 4 * lax.axis_index("x") + 2 * lax.axis_index("y") + lax.axis_index("c")
    tm = min(2048, t)

    ex_a = _gather_two_level([c, w_in[0].T.astype(BF16), w_rnn_conv[0], w_ffn_conv[0]], "gather_first")
    c_all = ex_a[0].reshape(nb, D_MODEL)
    win_zt = _to_z_layout(ex_a[1].reshape(-1, D_MODEL))
    wrc = _col_gather(ex_a[2])
    wfc = _col_gather(ex_a[3])

    nmod = w_ada.shape[2]
    nmodf = w_ada_final.shape[1]
    mod_cols = _mm(c_all, w_ada[0], name="mod_cols", tm=nb, tn=nmod, tk=D_MODEL, silu_a=True,
                   bias=lax.dynamic_slice(b_ada, (0, me * nmod), (1, nmod)))
    modf_cols = _mm(c_all, w_ada_final, name="modf_cols", tm=nb, tn=nmodf, tk=D_MODEL, silu_a=True,
                    bias=lax.dynamic_slice(b_ada_final.reshape(1, -1), (0, me * nmodf), (1, nmodf)))
    ex_b = _exchange([mod_cols, modf_cols], [True, True], "gather_mod")
    mod = lax.dynamic_slice(ex_b[0], (0, me * bsz, 0), (N_DEV, bsz, nmod)).transpose(1, 0, 2).reshape(bsz, 6, 1, D_MODEL)
    modf = lax.dynamic_slice(ex_b[1], (0, me * bsz, 0), (N_DEV, bsz, nmodf)).transpose(1, 0, 2).reshape(bsz, 2, 1, D_MODEL)
    shift1, scale1, gate1, shift2, scale2, gate2 = [mod[:, i] for i in range(6)]
    shift_f, scale_f = modf[:, 0], modf[:, 1]

    h1 = _norm_fwd(x, g_norm1, shift1, scale1, name="norm1_fwd")[0]
    h1f = h1.reshape(t, D_MODEL)
    zq = _mm(h1f, win_zt, name="mm_in_qkv", tb=True, out_dtype=BF16, tm=tm, tn=1024, tk=D_MODEL,
             b_cols=(QKV0, XG0)).reshape(bsz, s, XG0)
    zr = _mm(h1f, win_zt, name="mm_in_rest", tb=True, tm=tm, tn=1024, tk=D_MODEL,
             b_cols=(XG0, NZR)).reshape(bsz, s, NZR)
    h_rnn, y_rnn, got = _rnn_fwd(zr, wrc, b_rnn_conv, w_lru_a[0], b_lru_a, w_lru_i[0], b_lru_i, lru_lambda,
                                 exchange=([w_ffn_up[0].astype(BF16)], ["gather_zf"]))
    wup_z = got[0]
    flt = zr[:, :, FL0R:FL0R + N_HEADS].transpose(0, 2, 1)
    bfg = b_fgate.reshape(N_HEADS, 1)
    fcum = _fgate_fwd(flt, bfg)
    fcol = fcum.reshape(bsz, N_HEADS, s, 1)
    frow = fcum.reshape(bsz, N_HEADS, 1, s)
    o_attn, lse, got = _attn_fwd(
        zq, fcol, frow, exchange=([w_proj_rnn[0].astype(BF16), w_proj_attn[0].astype(BF16), w_out[0].astype(BF16),
                                  w_ffn_down[0].astype(BF16)], [True] * 4))
    wpr = got[0].reshape(D_RNN, D_MODEL)
    wpa = got[1].reshape(D_MODEL, D_MODEL)
    wout = got[2].reshape(D_MODEL, D_MODEL)
    wdown = got[3].reshape(D_FF, D_MODEL)
    pr = _mm(y_rnn.reshape(t, D_RNN), wpr, name="mm_proj_rnn", out_dtype=BF16, tm=tm, tn=D_MODEL, tk=D_RNN)
    pa = _mm(o_attn.reshape(t, D_MODEL), wpa, name="mm_proj_attn", out_dtype=BF16, tm=tm, tn=D_MODEL, tk=D_MODEL)
    z2 = zr.reshape(t, NZR)
    merged = _merge_fwd(z2, pr, pa)
    mo = _mm(merged, wout, name="mm_out", out_dtype=BF16, tm=tm, tn=D_MODEL, tk=D_MODEL).reshape(bsz, s, D_MODEL)
    h2, x1 = _norm_fwd(x, g_norm2, shift2, scale2, name="norm2_fwd", yprev=mo, gate=gate1)
    zf = _mm(h2.reshape(t, D_MODEL), wup_z, name="mm_up", out_dtype=BF16, tm=tm, tn=1024,
             tk=D_MODEL).reshape(bsz, s, 2 * D_FF)
    act = _ffn_fwd(zf, wfc, b_ffn_conv)
    yf = _mm(act.reshape(t, D_FF), wdown, name="mm_down", out_dtype=BF16, tm=tm, tn=D_MODEL,
             tk=D_FF // 2).reshape(bsz, s, D_MODEL)
    lp, dx2, dyf, dshift_f, dscale_f, dgate2, dg_final = _final(x1, yf, gate2, g_final.reshape(1, -1), shift_f, scale_f, loss_target)
    loss = lax.psum(jnp.sum(lp[:, :, 0, 0]) * (0.5 / D_MODEL), ("x", "y", "c"))

    dyf2 = dyf.reshape(t, D_MODEL)
    act2 = act.reshape(t, D_FF)
    dact = _mm(dyf2, wdown, name="mm_down_dx", tb=True, out_dtype=BF16, tm=tm, tn=1024, tk=D_MODEL).reshape(bsz, s, D_FF)
    g_wdown = _mm(act2, dyf2, name="mm_down_dw", ta=True, out_dtype=BF16, tm=1024, tn=D_MODEL, tk=tm)
    dzf, dsm_ffn, got = _ffn_bwd(dact, zf, wfc, b_ffn_conv, exchange=([g_wdown.reshape(N_DEV, -1, D_MODEL)], [False]))
    p_wdown = got[0]
    dzf2 = dzf.reshape(t, 2 * D_FF)
    dh2 = _mm(dzf2, wup_z, name="mm_up_dx", tb=True, tm=tm, tn=D_MODEL, tk=2048).reshape(bsz, s, D_MODEL)
    g_wup_z = _mm(h2.reshape(t, D_MODEL), dzf2, name="mm_up_dw", ta=True, out_dtype=BF16, tm=D_MODEL, tn=1024, tk=tm)
    dx1, dg_norm2, dshift2, dscale2, dmo, dgate1 = _norm_bwd(x1, dh2, g_norm2, scale2, dx2, name="norm2_bwd", yprev=mo, gate=gate1)
    dmo2 = dmo.reshape(t, D_MODEL)
    dmerged = _mm(dmo2, wout, name="mm_out_dx", tb=True, out_dtype=BF16, tm=tm, tn=D_MODEL, tk=D_MODEL)
    g_wout = _mm(merged, dmo2, name="mm_out_dw", ta=True, out_dtype=BF16, tm=D_MODEL, tn=D_MODEL, tk=tm)
    dz2, dpr, dpa = _merge_bwd(dmerged, z2, pr, pa)
    dy_rnn = _mm(dpr, wpr, name="mm_proj_rnn_dx", tb=True, out_dtype=BF16, tm=tm, tn=D_RNN,
                 tk=D_MODEL).reshape(bsz, s, D_RNN)
    g_wpr = _mm(y_rnn.reshape(t, D_RNN), dpr, name="mm_proj_rnn_dw", ta=True, out_dtype=BF16, tm=D_RNN, tn=D_MODEL, tk=tm)
    do = _mm(dpa, wpa, name="mm_proj_attn_dx", tb=True, out_dtype=BF16, tm=tm, tn=D_MODEL, tk=D_MODEL).reshape(bsz, s, D_MODEL)
    g_wpa = _mm(o_attn.reshape(t, D_MODEL), dpa, name="mm_proj_attn_dw", ta=True, out_dtype=BF16, tm=D_MODEL, tn=D_MODEL, tk=tm)
    dz, dfrow, got = _attn_bwd(zq, do, lse, fcol, frow, dz2.reshape(bsz, s, NZ),
                               exchange=([g_wup_z, g_wout.reshape(N_DEV, -1, D_MODEL)], ["scatter_zf", False]))
    p_wup, p_wout = got
    dflt, db_fgate = _fgate_bwd(dfrow.reshape(bsz, N_HEADS, s), flt, bfg)
    dfl_blk = jnp.pad(dflt.transpose(0, 2, 1).astype(BF16), ((0, 0), (0, 0), (0, MG0 - FL0 - N_HEADS)))
    dz = _fl_write(dfl_blk, dz)
    dz, g_wa, g_wi, dsm_rnn, got = _rnn_bwd(
        zr, h_rnn, dy_rnn, dz, wrc, b_rnn_conv, w_lru_a[0], b_lru_a, w_lru_i[0], b_lru_i, lru_lambda,
        exchange=([g_wpr.reshape(N_DEV, -1, D_MODEL), g_wpa.reshape(N_DEV, -1, D_MODEL)], [False, False]))
    p_wpr, p_wpa = got
    dz2 = dz.reshape(t, NZ)
    small = [dg_norm2, dg_final, dsm_rnn, g_wa, g_wi, db_fgate, dsm_ffn]
    small_sizes = [a.size for a in small]
    n_pad = -sum(small_sizes) % 1024
    packed = jnp.concatenate([a.reshape(-1) for a in small] + [jnp.zeros((n_pad,), F32)]).reshape(-1, 128)
    g_win_zt, got = _mm(dz2, h1f, name="mm_in_dw", ta=True, out_dtype=BF16, tm=1024, tn=D_MODEL, tk=tm,
                        exchange=([packed], [True]))
    flat = got[0].reshape(N_DEV, -1)
    dh1, got = _mm(dz2, win_zt, name="mm_in_dx", tm=tm, tn=D_MODEL, tk=2048,
                   exchange=([_from_z_layout(g_win_zt).reshape(N_DEV, -1, D_MODEL)], [False]))
    p_win = got[0]
    grad_x, dg_norm1, dshift1, dscale1 = _norm_bwd(x, dh1.reshape(bsz, s, D_MODEL), g_norm1, scale1, dx1, name="norm1_bwd")

    dmod = jnp.concatenate([dshift1, dscale1, dgate1, dshift2, dscale2, dgate2], axis=1).reshape(bsz, 6 * D_MODEL)
    dmodf = jnp.concatenate([dshift_f, dscale_f], axis=1).reshape(bsz, 2 * D_MODEL)
    ex_c = _exchange([dmod, dmodf, dg_norm1], [True, True, True], "gather_last_grads")
    dmod_all = ex_c[0].reshape(nb, 6 * D_MODEL)
    dmodf_all = ex_c[1].reshape(nb, 2 * D_MODEL)
    p_g1 = ex_c[2]

    res = {}

    def upd(name, parts, shape2):
        w, m, v = args[name], args["m_" + name], args["v_" + name]
        outs = _adamw(w.reshape(shape2), m.reshape(shape2), v.reshape(shape2), parts, name="adamw_" + name)
        res[name] = [o.reshape(w.shape) for o in outs]

    g_wada = _mm(c_all, lax.dynamic_slice(dmod_all, (0, me * nmod), (nb, nmod)), name="mm_ada_dw", ta=True,
                 silu_a=True, tm=D_MODEL, tn=nmod, tk=nb)
    g_wadaf = _mm(c_all, lax.dynamic_slice(dmodf_all, (0, me * nmodf), (nb, nmodf)), name="mm_adaf_dw", ta=True,
                  silu_a=True, tm=D_MODEL, tn=nmodf, tk=nb)
    upd("w_ada", g_wada[None], (D_MODEL, nmod))
    upd("w_ada_final", g_wadaf[None], (D_MODEL, nmodf))
    upd("b_ada", dmod_all.reshape(nb, 1, -1), (1, 6 * D_MODEL))
    upd("b_ada_final", dmodf_all.reshape(nb, 1, -1), (1, 2 * D_MODEL))
    outs = _adamw(w_in[0].T, m_w_in[0].T, v_w_in[0].T, p_win, name="adamw_w_in")
    res["w_in"] = [o.T[None] for o in outs]
    upd("w_proj_rnn", p_wpr, w_proj_rnn.shape[1:])
    upd("w_proj_attn", p_wpa, w_proj_attn.shape[1:])
    upd("w_out", p_wout, w_out.shape[1:])
    upd("w_ffn_up", p_wup, w_ffn_up.shape[1:])
    upd("w_ffn_down", p_wdown, w_ffn_down.shape[1:])

    parts, off = [], 0
    for a, n in zip(small, small_sizes):
        parts.append(flat[:, off:off + n].reshape((N_DEV,) + a.shape))
        off += n
    p_g2, p_gf, p_rnn, p_wa, p_wi, p_bfg, p_ffn = parts
    upd("g_norm1", p_g1, (1, D_MODEL))
    upd("g_norm2", p_g2, (1, D_MODEL))
    upd("g_final", p_gf, (1, D_MODEL))
    nrc = w_rnn_conv.shape[2]
    upd("w_rnn_conv", lax.dynamic_slice(p_rnn, (0, 0, me * nrc), (N_DEV, 4, nrc)), (4, nrc))
    upd("b_rnn_conv", p_rnn[:, 4:5], (1, D_RNN))
    upd("b_lru_a", p_rnn[:, 5:6], (1, D_RNN))
    upd("b_lru_i", p_rnn[:, 6:7], (1, D_RNN))
    upd("lru_lambda", p_rnn[:, 7:8], (1, D_RNN))
    upd("w_lru_a", p_wa.reshape(N_DEV, D_RNN, RNN_BLK), (D_RNN, RNN_BLK))
    upd("w_lru_i", p_wi.reshape(N_DEV, D_RNN, RNN_BLK), (D_RNN, RNN_BLK))
    upd("b_fgate", p_bfg.reshape(N_DEV, 1, N_HEADS), (1, N_HEADS))
    nfc = w_ffn_conv.shape[2]
    upd("w_ffn_conv", lax.dynamic_slice(p_ffn, (0, 0, me * nfc), (N_DEV, 3, nfc)), (3, nfc))
    upd("b_ffn_conv", p_ffn[:, 3:4], (1, D_FF))

    names = ['w_ada', 'b_ada', 'g_norm1', 'w_in', 'w_rnn_conv', 'b_rnn_conv', 'w_lru_a', 'b_lru_a', 'w_lru_i',
             'b_lru_i', 'lru_lambda', 'b_fgate', 'w_proj_rnn', 'w_proj_attn', 'w_out', 'g_norm2', 'w_ffn_up',
             'w_ffn_conv', 'b_ffn_conv', 'w_ffn_down', 'w_ada_final', 'b_ada_final', 'g_final']
    return (loss, grad_x, *[res[n][0] for n in names], *[res[n][1] for n in names],
            *[res[n][2] for n in names], *[res[n][3] for n in names])
```

```python
import jax
import jax.numpy as jnp
from jax import lax
from jax.experimental import pallas as pl
from jax.experimental.pallas import tpu as pltpu

F32, BF16 = jnp.float32, jnp.bfloat16
D_MODEL, D_RNN, N_BLK, RNN_BLK = 1024, 1280, 10, 128
N_HEADS, HEAD_DIM, D_FF = 8, 128, 3072
N_DEV = 8
NZ = 8192
QKV0, XG0, FL0, MG0 = 0, 3072, 5632, 6144
NZR = NZ - XG0
XG0R, FL0R, MG0R = 0, FL0 - XG0, MG0 - XG0
RMS_EPS = 1e-6
LRU_C = 8.0
ADAM_LR, ADAM_B1, ADAM_B2, ADAM_EPS, ADAM_WD, ADAM_STEP = 0.001, 0.9, 0.999, 1e-08, 0.01, 10
VMEM_LIMIT = 56 << 20
SCAN_CHUNK = 256
ATTN_TQ = 256
CUM_BLK = 512
MESH = pl.DeviceIdType.MESH
ANY = pl.BlockSpec(memory_space=pl.ANY)


def _call(body, **kw):
    return pl.pallas_call(body, **kw)


def _params():
    return pltpu.CompilerParams(vmem_limit_bytes=VMEM_LIMIT)


ZF_PIECES = 2 * D_FF // 128 // N_DEV


def _zf_cols(owner, p):
    ob = owner * ZF_PIECES + p
    blk = 2 * (ob % (D_FF // 128)) + ob // (D_FF // 128)
    return pl.ds(pl.multiple_of(blk * 128, 128), 128)


def _n_pieces(mode):
    return ZF_PIECES if mode in ("gather_zf", "scatter_zf") else 1


def _exchanged_shape(a, mode):
    if mode is True:
        return (N_DEV,) + a.shape
    if mode is False:
        return a.shape
    if mode == "gather_zf":
        return (a.shape[0], N_DEV * a.shape[1])
    return (N_DEV, a.shape[0], a.shape[1] // N_DEV)


def _direct_copies(ins, outs, sems, modes):
    send_sems, recv_sems, loc_sems = sems
    x, y, c = lax.axis_index("x"), lax.axis_index("y"), lax.axis_index("c")
    me = 4 * x + 2 * y + c

    def ends(a, to, p):
        mode = modes[a]
        if mode is True:
            return ins[a], outs[a].at[me]
        if mode is False:
            return ins[a].at[to], outs[a].at[me]
        mine = slice(p * 128, (p + 1) * 128)
        if mode == "gather_zf":
            return ins[a].at[:, mine], outs[a].at[:, _zf_cols(me, p)]
        return ins[a].at[:, _zf_cols(to, p)], outs[a].at[me, :, mine]

    local, remote = [], []
    nloc = 0
    for a in range(len(ins)):
        for p in range(_n_pieces(modes[a])):
            src, dst = ends(a, me, p)
            local.append(pltpu.make_async_copy(src, dst, loc_sems.at[nloc]))
            nloc += 1
    for k in range(1, N_DEV):
        px = 1 - x if k & 4 else x
        py = 1 - y if k & 2 else y
        pc = 1 - c if k & 1 else c
        peer = 4 * px + 2 * py + pc
        nrem = (k - 1) * nloc
        for a in range(len(ins)):
            for p in range(_n_pieces(modes[a])):
                src, dst = ends(a, peer, p)
                remote.append(pltpu.make_async_remote_copy(
                    src_ref=src, dst_ref=dst, send_sem=send_sems.at[nrem], recv_sem=recv_sems.at[nrem],
                    device_id=(px, py, pc), device_id_type=MESH))
                nrem += 1
    return local, remote


def _with_exchange(body, args, arrs, bcast, *, name, grid, in_specs, out_specs, out_shape, scratch_shapes=(), **kw):
    n_in, n_out, n_sc, nc = len(args), len(out_shape), len(scratch_shapes), len(arrs)

    def wrapped(*refs):
        core_in, cin = refs[:n_in], refs[n_in:n_in + nc]
        core_out = refs[n_in + nc:n_in + nc + n_out]
        cout = refs[n_in + nc + n_out:n_in + 2 * nc + n_out]
        rest = refs[n_in + 2 * nc + n_out:]
        core_sc, sems = rest[:n_sc], rest[n_sc:]

        def start():
            local, remote = _direct_copies(cin, cout, sems, bcast)
            for cp in local + remote:
                cp.start()

        def wait():
            local, remote = _direct_copies(cin, cout, sems, bcast)
            for cp in remote + local:
                cp.wait()

        if not grid:
            start()
            body(*core_in, *core_out, *core_sc)
            wait()
            return
        first, last = None, None
        for d, g in enumerate(grid):
            i = pl.program_id(d)
            first = (i == 0) if first is None else jnp.logical_and(first, i == 0)
            last = (i == g - 1) if last is None else jnp.logical_and(last, i == g - 1)
        pl.when(first)(start)
        body(*core_in, *core_out, *core_sc)
        pl.when(last)(wait)

    ex_shape = [jax.ShapeDtypeStruct(_exchanged_shape(a, mode), a.dtype) for a, mode in zip(arrs, bcast)]
    ncopies = sum(_n_pieces(mode) for mode in bcast)
    sems = [pltpu.SemaphoreType.DMA((ncopies * (N_DEV - 1),)), pltpu.SemaphoreType.DMA((ncopies * (N_DEV - 1),)),
            pltpu.SemaphoreType.DMA((ncopies,))]
    res = _call(wrapped, name=name, grid=grid, in_specs=list(in_specs) + [ANY] * nc,
                out_specs=list(out_specs) + [ANY] * nc, out_shape=list(out_shape) + ex_shape,
                scratch_shapes=list(scratch_shapes) + sems, **kw)(*args, *arrs)
    return list(res[:n_out]), list(res[n_out:])


def _run(body, args, exchange, **kw):
    if exchange is None:
        return list(_call(body, **kw)(*args))
    outs, got = _with_exchange(body, args, *exchange, **kw)
    return outs + [got]


def _exchange(arrs, bcast, name):
    return _with_exchange(lambda: None, [], arrs, bcast, name=name, grid=(), in_specs=[], out_specs=[], out_shape=[],
                          compiler_params=pltpu.CompilerParams(has_side_effects=True))[1]


def _gather_two_level(arrs, name):
    n = len(arrs)

    def body(*refs):
        ins, outs = refs[:n], refs[n:2 * n]
        send_sems, recv_sems, loc_sems = refs[2 * n:]
        x, y, c = lax.axis_index("x"), lax.axis_index("y"), lax.axis_index("c")
        me, sibling = (x, y, c), (x, y, 1 - c)
        chips = [(1 - x, y), (x, 1 - y), (1 - x, 1 - y)]

        def slot(a, p):
            return outs[a].at[4 * p[0] + 2 * p[1] + p[2]]

        def copy(a, k, block, to, src=None):
            return pltpu.make_async_remote_copy(
                src_ref=slot(a, block) if src is None else src, dst_ref=slot(a, block),
                send_sem=send_sems.at[a * 7 + k], recv_sem=recv_sems.at[a * 7 + k],
                device_id=to, device_id_type=MESH)

        mine = [pltpu.make_async_copy(ins[a], slot(a, me), loc_sems.at[a]) for a in range(n)]
        first = []
        for a in range(n):
            first.append(copy(a, 0, me, sibling, src=ins[a]))
            first += [copy(a, 1 + j, me, (*chip, c), src=ins[a]) for j, chip in enumerate(chips)]
        for cp in mine + first:
            cp.start()
        passed = []
        for j, chip in enumerate(chips):
            for a in range(n):
                copy(a, 1 + j, (*chip, c), me).wait_recv()
                fwd = copy(a, 4 + j, (*chip, c), sibling)
                fwd.start()
                passed.append(fwd)
        for a in range(n):
            copy(a, 0, sibling, me).wait_recv()
        for j, chip in enumerate(chips):
            for a in range(n):
                copy(a, 4 + j, (*chip, 1 - c), me).wait_recv()
        for cp in first + passed:
            cp.wait_send()
        for cp in mine:
            cp.wait()

    return _call(
        body, name=name, in_specs=[ANY] * n, out_specs=[ANY] * n,
        out_shape=[jax.ShapeDtypeStruct((N_DEV,) + a.shape, a.dtype) for a in arrs],
        scratch_shapes=[pltpu.SemaphoreType.DMA((n * 7,)), pltpu.SemaphoreType.DMA((n * 7,)),
                        pltpu.SemaphoreType.DMA((n,))],
        compiler_params=pltpu.CompilerParams(has_side_effects=True),
    )(*arrs)


def _sigmoid(x):
    return 1.0 / (1.0 + jnp.exp(-x))


def _log1p(e):
    u = 1.0 + e
    d = u - 1.0
    return jnp.where(d == 0.0, e, jnp.log(u) * (e / jnp.where(d == 0.0, 1.0, d)))


def _softplus(x):
    return jnp.maximum(x, 0.0) + _log1p(jnp.exp(-jnp.abs(x)))


_GELU_C = 0.7978845608028654


def _gelu(x):
    t = jnp.tanh(_GELU_C * (x + 0.044715 * x * x * x))
    return 0.5 * x * (1.0 + t)


def _gelu_and_grad(x):
    t = jnp.tanh(_GELU_C * (x + 0.044715 * x * x * x))
    g = 0.5 * x * (1.0 + t)
    dg = 0.5 * (1.0 + t) + 0.5 * x * (1.0 - t * t) * _GELU_C * (1.0 + 3.0 * 0.044715 * x * x)
    return g, dg


def _acc(ref, idx, val, first):
    @pl.when(first)
    def _():
        ref[idx] = val

    @pl.when(jnp.logical_not(first))
    def _():
        ref[idx] = ref[idx] + val


def _split3(x):
    hi = x.astype(BF16)
    r1 = x - hi.astype(F32)
    mid = r1.astype(BF16)
    lo = (r1 - mid.astype(F32)).astype(BF16)
    return hi, mid, lo


def _mm(a, b, *, name, ta=False, tb=False, out_dtype=F32, tm, tn, tk, bias=None, silu_a=False, exchange=None,
        b_cols=None):
    m, k = (a.shape[1], a.shape[0]) if ta else a.shape
    col0, n = b_cols if b_cols is not None else (0, b.shape[0] if tb else b.shape[1])
    assert m % tm == 0 and n % tn == 0 and k % tk == 0 and col0 % tn == 0, (name, m, n, k)
    jo = col0 // tn
    nk = k // tk
    dn = (((0 if ta else 1,), (1 if tb else 0,)), ((), ()))
    use_acc = nk > 1 and out_dtype != F32

    def body(*refs):
        a_ref, b_ref = refs[0], refs[1]
        bias_ref = refs[2] if bias is not None else None
        o_ref = refs[3 if bias is not None else 2]
        av = a_ref[...]
        if silu_a:
            av = av * _sigmoid(av)
        p = lax.dot_general(av.astype(BF16), b_ref[...].astype(BF16), dn, preferred_element_type=F32)
        if bias is not None:
            p = p + bias_ref[...]
        if nk == 1:
            o_ref[...] = p.astype(out_dtype)
            return
        kk = pl.program_id(2)
        acc = refs[-1] if use_acc else o_ref

        @pl.when(kk == 0)
        def _():
            acc[...] = p

        @pl.when(kk > 0)
        def _():
            acc[...] = acc[...] + p

        if use_acc:
            @pl.when(kk == nk - 1)
            def _():
                o_ref[...] = acc[...].astype(out_dtype)

    a_spec = pl.BlockSpec((tk, tm), lambda i, j, q: (q, i)) if ta else pl.BlockSpec((tm, tk), lambda i, j, q: (i, q))
    b_spec = (pl.BlockSpec((tn, tk), lambda i, j, q: (j + jo, q)) if tb
              else pl.BlockSpec((tk, tn), lambda i, j, q: (q, j + jo)))
    in_specs, args = [a_spec, b_spec], [a, b]
    if bias is not None:
        in_specs.append(pl.BlockSpec((1, tn), lambda i, j, q: (0, j)))
        args.append(bias)
    kw = dict(name=name, grid=(m // tm, n // tn, nk), in_specs=in_specs,
              out_specs=[pl.BlockSpec((tm, tn), lambda i, j, q: (i, j))],
              out_shape=[jax.ShapeDtypeStruct((m, n), out_dtype)],
              scratch_shapes=[pltpu.VMEM((tm, tn), F32)] if use_acc else [],
              compiler_params=_params())
    if exchange is None:
        return _call(body, **kw)(*args)[0]
    outs, got = _with_exchange(body, args, *exchange, **kw)
    return outs[0], got


def _tok_spec(ts):
    return pl.BlockSpec((1, ts, D_MODEL), lambda b, s: (b, s, 0))


def _seq_spec():
    return pl.BlockSpec((1, 1, D_MODEL), lambda b, s: (b, 0, 0))


def _vec_spec():
    return pl.BlockSpec((1, D_MODEL), lambda b, s: (0, 0))


def _norm_fwd(x, g, shift, scale, *, name, yprev=None, gate=None, ts=512):
    bsz, s, _ = x.shape
    res = yprev is not None

    def body(*refs):
        if res:
            x_ref, g_ref, sh_ref, sc_ref, y_ref, gt_ref, h_ref, xn_ref = refs
            xv = x_ref[0] + gt_ref[0] * y_ref[0]
            xn_ref[0] = xv
        else:
            x_ref, g_ref, sh_ref, sc_ref, h_ref = refs
            xv = x_ref[0]
        rstd = lax.rsqrt(jnp.mean(xv * xv, axis=-1, keepdims=True) + RMS_EPS)
        nv = xv * rstd * g_ref[...]
        h_ref[0] = (nv * (1.0 + sc_ref[0]) + sh_ref[0]).astype(BF16)

    in_specs = [_tok_spec(ts), _vec_spec(), _seq_spec(), _seq_spec()]
    args = [x, g, shift, scale]
    out_specs = [_tok_spec(ts)]
    out_shape = [jax.ShapeDtypeStruct(x.shape, BF16)]
    if res:
        in_specs += [_tok_spec(ts), _seq_spec()]
        args += [yprev, gate]
        out_specs.append(_tok_spec(ts))
        out_shape.append(jax.ShapeDtypeStruct(x.shape, F32))
    return _call(body, name=name, grid=(bsz, s // ts), in_specs=in_specs, out_specs=out_specs,
                 out_shape=out_shape, compiler_params=_params())(*args)


def _final(x1, yf, gate2, g, shift, scale, target, *, ts=512):
    bsz, s, _ = x1.shape
    ns = s // ts

    def body(x_ref, y_ref, gt_ref, g_ref, sh_ref, sc_ref, t_ref,
             lp_ref, dx_ref, dyf_ref, dsh_ref, dsc_ref, dgt_ref, dg_ref):
        b, si = pl.program_id(0), pl.program_id(1)
        yv, gt, gv = y_ref[0], gt_ref[0], g_ref[...]
        xv = x_ref[0] + gt * yv
        rstd = lax.rsqrt(jnp.mean(xv * xv, axis=-1, keepdims=True) + RMS_EPS)
        xhat = xv * rstd
        nv = xhat * gv
        sc1 = 1.0 + sc_ref[0]
        err = nv * sc1 + sh_ref[0] - t_ref[0]
        lp_ref[...] = jnp.zeros((1, 1, 8, 128), F32) + jnp.sum(err * err)
        dy = err * (1.0 / D_MODEL)
        dn = dy * sc1
        dxh = dn * gv
        dx = rstd * (dxh - xhat * jnp.mean(dxh * xhat, axis=-1, keepdims=True))
        dx_ref[0] = dx
        dyf_ref[0] = (dx * gt).astype(BF16)
        _acc(dsh_ref, 0, jnp.sum(dy, axis=0, keepdims=True), si == 0)
        _acc(dsc_ref, 0, jnp.sum(dy * nv, axis=0, keepdims=True), si == 0)
        _acc(dgt_ref, 0, jnp.sum(dx * yv, axis=0, keepdims=True), si == 0)
        _acc(dg_ref, slice(None), jnp.sum(dn * xhat, axis=0, keepdims=True), jnp.logical_and(b == 0, si == 0))

    seq_shape = jax.ShapeDtypeStruct((bsz, 1, D_MODEL), F32)
    return _call(
        body, name="final_loss", grid=(bsz, ns),
        in_specs=[_tok_spec(ts), _tok_spec(ts), _seq_spec(), _vec_spec(), _seq_spec(), _seq_spec(), _tok_spec(ts)],
        out_specs=[pl.BlockSpec((1, 1, 8, 128), lambda b, s: (b, s, 0, 0)), _tok_spec(ts), _tok_spec(ts),
                   _seq_spec(), _seq_spec(), _seq_spec(), _vec_spec()],
        out_shape=[jax.ShapeDtypeStruct((bsz, ns, 8, 128), F32), jax.ShapeDtypeStruct(x1.shape, F32),
                   jax.ShapeDtypeStruct(x1.shape, BF16), seq_shape, seq_shape, seq_shape,
                   jax.ShapeDtypeStruct((1, D_MODEL), F32)],
        compiler_params=_params(),
    )(x1, yf, gate2, g, shift, scale, target)


def _norm_bwd(x, dh, g, scale, dres, *, name, yprev=None, gate=None, ts=512):
    bsz, s, _ = x.shape
    res = yprev is not None

    def body(*refs):
        if res:
            (x_ref, dh_ref, g_ref, sc_ref, dr_ref, y_ref, gt_ref,
             dx_ref, dg_ref, dsh_ref, dsc_ref, dy_ref, dgt_ref) = refs
        else:
            x_ref, dh_ref, g_ref, sc_ref, dr_ref, dx_ref, dg_ref, dsh_ref, dsc_ref = refs
        b, si = pl.program_id(0), pl.program_id(1)
        xv, dhv, gv = x_ref[0], dh_ref[0], g_ref[...]
        rstd = lax.rsqrt(jnp.mean(xv * xv, axis=-1, keepdims=True) + RMS_EPS)
        xhat = xv * rstd
        dn = dhv * (1.0 + sc_ref[0])
        dxh = dn * gv
        dx = dr_ref[0] + rstd * (dxh - xhat * jnp.mean(dxh * xhat, axis=-1, keepdims=True))
        dx_ref[0] = dx
        _acc(dsh_ref, 0, jnp.sum(dhv, axis=0, keepdims=True), si == 0)
        _acc(dsc_ref, 0, jnp.sum(dhv * (xhat * gv), axis=0, keepdims=True), si == 0)
        _acc(dg_ref, slice(None), jnp.sum(dn * xhat, axis=0, keepdims=True), jnp.logical_and(b == 0, si == 0))
        if res:
            dy_ref[0] = (dx * gt_ref[0]).astype(BF16)
            _acc(dgt_ref, 0, jnp.sum(dx * y_ref[0], axis=0, keepdims=True), si == 0)

    seq_shape = jax.ShapeDtypeStruct((bsz, 1, D_MODEL), F32)
    in_specs = [_tok_spec(ts), _tok_spec(ts), _vec_spec(), _seq_spec(), _tok_spec(ts)]
    args = [x, dh, g, scale, dres]
    out_specs = [_tok_spec(ts), _vec_spec(), _seq_spec(), _seq_spec()]
    out_shape = [jax.ShapeDtypeStruct(x.shape, F32), jax.ShapeDtypeStruct((1, D_MODEL), F32), seq_shape, seq_shape]
    if res:
        in_specs += [_tok_spec(ts), _seq_spec()]
        args += [yprev, gate]
        out_specs += [_tok_spec(ts), _seq_spec()]
        out_shape += [jax.ShapeDtypeStruct(x.shape, BF16), seq_shape]
    return _call(body, name=name, grid=(bsz, s // ts), in_specs=in_specs, out_specs=out_specs,
                 out_shape=out_shape, compiler_params=_params())(*args)


def _lru_gates(xr, wa, ba, wi, bi, sp):
    xb = xr.astype(BF16)
    r = _sigmoid(jnp.dot(xb, wa, preferred_element_type=F32) + ba)
    i = _sigmoid(jnp.dot(xb, wi, preferred_element_type=F32) + bi)
    la = -LRU_C * r * sp
    a = jnp.exp(la)
    mult = jnp.sqrt(-jnp.tanh(la) * (a * a + 1.0))
    return r, i, a, mult


def _scan_forward(a, u, h_in, row):
    n = a.shape[0]
    r8 = row & 7
    for k in (1, 2, 4):
        m = r8 >= k
        u = jnp.where(m, a * pltpu.roll(u, k, 0) + u, u)
        a = jnp.where(m, a * pltpu.roll(a, k, 0), a)
    carry = jnp.broadcast_to(h_in, (8, a.shape[1]))
    out = []
    for j in range(n // 8):
        hj = u[8 * j:8 * j + 8] + a[8 * j:8 * j + 8] * carry
        out.append(hj)
        carry = jnp.broadcast_to(hj[7:8], hj.shape)
    return jnp.concatenate(out, axis=0)


def _scan_backward(a_up, g, g_in, row):
    n = g.shape[0]
    r8 = row & 7
    for k in (1, 2, 4):
        m = r8 < 8 - k
        g = jnp.where(m, a_up * pltpu.roll(g, n - k, 0) + g, g)
        a_up = jnp.where(m, a_up * pltpu.roll(a_up, n - k, 0), a_up)
    carry = jnp.broadcast_to(g_in, (8, g.shape[1]))
    out = [None] * (n // 8)
    for j in reversed(range(n // 8)):
        gj = g[8 * j:8 * j + 8] + a_up[8 * j:8 * j + 8] * carry
        out[j] = gj
        carry = jnp.broadcast_to(gj[0:1], gj.shape)
    return jnp.concatenate(out, axis=0)


def _rnn_fwd(z, wc, bc, wa, ba, wi, bi, lam, exchange=None):
    bsz, s, _ = z.shape
    ch = min(SCAN_CHUNK, s)
    nch = s // ch

    def body(z_ref, wc_ref, bc_ref, wa_ref, ba_ref, wi_ref, bi_ref, lam_ref, h_ref, y_ref):
        wcv, bcv = wc_ref[...], bc_ref[...]
        wav, wiv = wa_ref[0].astype(BF16), wi_ref[0].astype(BF16)
        bav, biv = ba_ref[...], bi_ref[...]
        sp = _softplus(-lam_ref[...])
        row = lax.broadcasted_iota(jnp.int32, (ch, RNN_BLK), 0)

        def chunk(ci, carry):
            prev8, hc = carry
            st = pl.multiple_of(ci * ch, ch)
            blk = z_ref[0, pl.ds(st, ch), :]
            xp, gr = blk[:, :RNN_BLK], blk[:, RNN_BLK:]
            xe = jnp.concatenate([prev8, xp], axis=0)
            xr = bcv + wcv[3:4] * xp
            for j in range(1, 4):
                xr = xr + wcv[3 - j:4 - j] * pltpu.roll(xe, j, 0)[8:]
            _, i, a, mult = _lru_gates(xr, wav, bav, wiv, biv, sp)
            hh = _scan_forward(a, mult * (i * xr), hc, row)
            h_ref[0, pl.ds(st, ch), :] = hh
            y_ref[0, pl.ds(st, ch), :] = (_gelu(gr) * hh).astype(BF16)
            return xp[ch - 8:], hh[ch - 1:]

        lax.fori_loop(0, nch, chunk, (jnp.zeros((8, RNN_BLK), F32), jnp.zeros((1, RNN_BLK), F32)))

    vec = pl.BlockSpec((1, RNN_BLK), lambda b, n: (0, n))
    mat = pl.BlockSpec((1, RNN_BLK, RNN_BLK), lambda b, n: (n, 0, 0))
    out = pl.BlockSpec((1, s, RNN_BLK), lambda b, n: (b, 0, n))
    return _run(
        body, [z, wc, bc, wa, ba, wi, bi, lam], exchange, name="rnn_fwd", grid=(bsz, N_BLK),
        in_specs=[pl.BlockSpec((1, s, 2 * RNN_BLK), lambda b, n: (b, 0, XG0R // (2 * RNN_BLK) + n)),
                  pl.BlockSpec((4, RNN_BLK), lambda b, n: (0, n)), vec, mat, vec, mat, vec, vec],
        out_specs=[out, out],
        out_shape=[jax.ShapeDtypeStruct((bsz, s, D_RNN), F32), jax.ShapeDtypeStruct((bsz, s, D_RNN), BF16)],
        compiler_params=_params())


def _rnn_bwd(z, h, dy, dz, wc, bc, wa, ba, wi, bi, lam, exchange=None):
    bsz, s, _ = z.shape
    ch = min(SCAN_CHUNK, s)
    nch = s // ch

    def body(z_ref, h_ref, dy_ref, dz_in, wc_ref, bc_ref, wa_ref, ba_ref, wi_ref, bi_ref, lam_ref,
             dz_ref, dwa_ref, dwi_ref, dsm_ref):
        del dz_in
        b = pl.program_id(1)
        wcv, bcv = wc_ref[...], bc_ref[...]
        wav, wiv = wa_ref[0].astype(BF16), wi_ref[0].astype(BF16)
        bav, biv = ba_ref[...], bi_ref[...]
        lamv = lam_ref[...]
        sp = _softplus(-lamv)
        row = lax.broadcasted_iota(jnp.int32, (ch, RNN_BLK), 0)

        @pl.when(b == 0)
        def _():
            dwa_ref[...] = jnp.zeros_like(dwa_ref)
            dwi_ref[...] = jnp.zeros_like(dwi_ref)
            dsm_ref[...] = jnp.zeros_like(dsm_ref)

        def chunk(k, carry):
            a_next, g_next, dxr_next8 = carry
            ci = nch - 1 - k
            st = pl.multiple_of(ci * ch, ch)
            pst = pl.multiple_of(jnp.maximum(st - 8, 0), 8)
            has_prev = (ci > 0).astype(F32)
            blk = z_ref[0, pl.ds(st, ch), :]
            xp, gr = blk[:, :RNN_BLK], blk[:, RNN_BLK:]
            xprev8 = z_ref[0, pl.ds(pst, 8), :][:, :RNN_BLK] * has_prev
            hprev8 = h_ref[0, pl.ds(pst, 8), :] * has_prev
            xe = jnp.concatenate([xprev8, xp], axis=0)
            xs = [xp] + [pltpu.roll(xe, j, 0)[8:] for j in range(1, 4)]
            xr = bcv
            for kk in range(4):
                xr = xr + wcv[kk:kk + 1] * xs[3 - kk]
            r, i, a, mult = _lru_gates(xr, wav, bav, wiv, biv, sp)
            hh = h_ref[0, pl.ds(st, ch), :]
            hm1 = pltpu.roll(jnp.concatenate([hprev8, hh], axis=0), 1, 0)[8:]
            dyv = dy_ref[0, pl.ds(st, ch), :]
            ge, dge = _gelu_and_grad(gr)
            dgr = dyv * hh * dge
            gg = _scan_backward(jnp.where(row == ch - 1, a_next, pltpu.roll(a, ch - 1, 0)), dyv * ge, g_next, row)
            da = gg * hm1
            ix = i * xr
            dmult = gg * ix
            di = gg * mult * xr
            dxr = gg * mult * i
            dla = da * a - dmult * (a * a) / mult
            dr = dla * (-LRU_C * sp)
            dsp = jnp.sum(dla * (-LRU_C * r), axis=0, keepdims=True)
            dpa = dr * r * (1.0 - r)
            dpi = di * i * (1.0 - i)
            dpab, dpib, xrb = dpa.astype(BF16), dpi.astype(BF16), xr.astype(BF16)
            nt = (((1,), (1,)), ((), ()))
            tn = (((0,), (0,)), ((), ()))
            dxr = dxr + lax.dot_general(dpab, wav, nt, preferred_element_type=F32)
            dxr = dxr + lax.dot_general(dpib, wiv, nt, preferred_element_type=F32)
            dwa_ref[0] = dwa_ref[0] + lax.dot_general(xrb, dpab, tn, preferred_element_type=F32)
            dwi_ref[0] = dwi_ref[0] + lax.dot_general(xrb, dpib, tn, preferred_element_type=F32)
            rows = [jnp.sum(dxr * xs[3 - q], axis=0, keepdims=True) for q in range(4)]
            rows += [jnp.sum(dxr, axis=0, keepdims=True), jnp.sum(dpa, axis=0, keepdims=True),
                     jnp.sum(dpi, axis=0, keepdims=True), dsp * (-_sigmoid(-lamv))]
            dsm_ref[...] = dsm_ref[...] + jnp.concatenate(rows, axis=0)
            dxe = jnp.concatenate([dxr, dxr_next8], axis=0)
            dxp = wcv[3:4] * dxr
            for j in range(1, 4):
                dxp = dxp + wcv[3 - j:4 - j] * pltpu.roll(dxe, ch + 8 - j, 0)[:ch]
            dz_ref[0, pl.ds(st, ch), :] = jnp.concatenate([dxp, dgr], axis=1).astype(BF16)
            return a[0:1], gg[0:1], dxr[0:8]

        lax.fori_loop(0, nch, chunk, (jnp.zeros((1, RNN_BLK), F32), jnp.zeros((1, RNN_BLK), F32),
                                      jnp.zeros((8, RNN_BLK), F32)))

    vec = pl.BlockSpec((1, RNN_BLK), lambda n, b: (0, n))
    mat = pl.BlockSpec((1, RNN_BLK, RNN_BLK), lambda n, b: (n, 0, 0))
    tok = pl.BlockSpec((1, s, RNN_BLK), lambda n, b: (b, 0, n))
    zblk = pl.BlockSpec((1, s, 2 * RNN_BLK), lambda n, b: (b, 0, XG0R // (2 * RNN_BLK) + n))
    dzblk = pl.BlockSpec((1, s, 2 * RNN_BLK), lambda n, b: (b, 0, XG0 // (2 * RNN_BLK) + n))
    return _run(
        body, [z, h, dy, dz, wc, bc, wa, ba, wi, bi, lam], exchange, name="rnn_bwd", grid=(N_BLK, bsz),
        in_specs=[zblk, tok, tok, ANY, pl.BlockSpec((4, RNN_BLK), lambda n, b: (0, n)), vec, mat, vec, mat, vec, vec],
        out_specs=[dzblk, mat, mat, pl.BlockSpec((8, RNN_BLK), lambda n, b: (0, n))],
        out_shape=[jax.ShapeDtypeStruct(dz.shape, dz.dtype), jax.ShapeDtypeStruct((N_BLK, RNN_BLK, RNN_BLK), F32),
                   jax.ShapeDtypeStruct((N_BLK, RNN_BLK, RNN_BLK), F32), jax.ShapeDtypeStruct((8, D_RNN), F32)],
        input_output_aliases={3: 0},
        compiler_params=_params())


def _tri(n, upper):
    r = lax.broadcasted_iota(jnp.int32, (n, n), 0)
    c = lax.broadcasted_iota(jnp.int32, (n, n), 1)
    return (r <= c if upper else r >= c).astype(BF16)


def _exact_dot(x, t):
    hi, mid, lo = _split3(x)
    return (jnp.dot(hi, t, preferred_element_type=F32) + jnp.dot(mid, t, preferred_element_type=F32)
            + jnp.dot(lo, t, preferred_element_type=F32))


def _fgate_fwd(flt, bfg):
    bsz, nh, s = flt.shape
    blk = min(CUM_BLK, s)

    def body(fl_ref, b_ref, f_ref):
        tri = _tri(blk, True)
        carry = jnp.zeros((nh, 1), F32)
        for q in range(s // blk):
            xq = fl_ref[0, :, q * blk:(q + 1) * blk] + b_ref[...]
            fq = _exact_dot(-_softplus(-xq), tri) + carry
            f_ref[0, :, q * blk:(q + 1) * blk] = fq
            carry = fq[:, blk - 1:blk]

    spec = pl.BlockSpec((1, nh, s), lambda b: (b, 0, 0))
    return _call(body, name="fgate_fwd", grid=(bsz,), in_specs=[spec, pl.BlockSpec((nh, 1), lambda b: (0, 0))],
                 out_specs=spec, out_shape=jax.ShapeDtypeStruct(flt.shape, F32), compiler_params=_params())(flt, bfg)


def _fgate_bwd(dft, flt, bfg):
    bsz, nh, s = flt.shape
    blk = min(CUM_BLK, s)

    def body(df_ref, fl_ref, b_ref, dfl_ref, db_ref):
        b = pl.program_id(0)
        tri = _tri(blk, False)
        carry = jnp.zeros((nh, 1), F32)
        tot = jnp.zeros((nh, 1), F32)
        for q in reversed(range(s // blk)):
            sl = slice(q * blk, (q + 1) * blk)
            dlq = _exact_dot(df_ref[0, :, sl], tri) + carry
            carry = dlq[:, 0:1]
            dflq = dlq * _sigmoid(-(fl_ref[0, :, sl] + b_ref[...]))
            dfl_ref[0, :, sl] = dflq
            tot = tot + jnp.sum(dflq, axis=1, keepdims=True)
        _acc(db_ref, slice(None), tot, b == 0)

    spec = pl.BlockSpec((1, nh, s), lambda b: (b, 0, 0))
    small = pl.BlockSpec((nh, 1), lambda b: (0, 0))
    return _call(body, name="fgate_bwd", grid=(bsz,), in_specs=[spec, spec, small], out_specs=[spec, small],
                 out_shape=[jax.ShapeDtypeStruct(flt.shape, F32), jax.ShapeDtypeStruct((nh, 1), F32)],
                 compiler_params=_params())(dft, flt, bfg)


_NT = (((1,), (1,)), ((), ()))
_TN = (((0,), (0,)), ((), ()))


def _attn_block_scores(q_b, z_ref, frow_ref, k0, tq):
    kk = z_ref[0, k0:k0 + tq, HEAD_DIM:2 * HEAD_DIM]
    return lax.dot_general(q_b, kk, _NT, preferred_element_type=F32) - frow_ref[0, 0, :, k0:k0 + tq]


def _causal(tq):
    return lax.broadcasted_iota(jnp.int32, (tq, tq), 0) >= lax.broadcasted_iota(jnp.int32, (tq, tq), 1)


def _attn_fwd(z, fcol, frow, exchange=None):
    bsz, s, _ = z.shape
    tq = min(ATTN_TQ, s)
    scale = HEAD_DIM ** -0.5

    def body(z_ref, fcol_ref, frow_ref, o_ref, lse_ref):
        for qb in range(s // tq):
            q0 = qb * tq
            q_b = (z_ref[0, q0:q0 + tq, :HEAD_DIM].astype(F32) * scale).astype(BF16)
            m = jnp.full((tq, 1), -jnp.inf, F32)
            l = jnp.zeros((tq, 1), F32)
            acc = jnp.zeros((tq, HEAD_DIM), F32)
            for kb in range(qb + 1):
                k0 = kb * tq
                sc = _attn_block_scores(q_b, z_ref, frow_ref, k0, tq)
                if kb == qb:
                    sc = jnp.where(_causal(tq), sc, -jnp.inf)
                m_new = jnp.maximum(m, jnp.max(sc, axis=-1, keepdims=True))
                alpha = jnp.exp(m - m_new)
                p = jnp.exp(sc - m_new)
                l = alpha * l + jnp.sum(p, axis=-1, keepdims=True)
                vv = z_ref[0, k0:k0 + tq, 2 * HEAD_DIM:]
                acc = alpha * acc + jnp.dot(p.astype(BF16), vv, preferred_element_type=F32)
                m = m_new
            o_ref[0, q0:q0 + tq, :] = (acc / l).astype(BF16)
            lse_ref[0, 0, q0:q0 + tq, :] = m + jnp.log(l) + fcol_ref[0, 0, q0:q0 + tq, :]

    return _run(
        body, [z, fcol, frow], exchange, name="attn_fwd", grid=(bsz, N_HEADS),
        in_specs=[pl.BlockSpec((1, s, 3 * HEAD_DIM), lambda b, h: (b, 0, h)),
                  pl.BlockSpec((1, 1, s, 1), lambda b, h: (b, h, 0, 0)),
                  pl.BlockSpec((1, 1, 1, s), lambda b, h: (b, h, 0, 0))],
        out_specs=[pl.BlockSpec((1, s, HEAD_DIM), lambda b, h: (b, 0, h)),
                   pl.BlockSpec((1, 1, s, 1), lambda b, h: (b, h, 0, 0))],
        out_shape=[jax.ShapeDtypeStruct((bsz, s, N_HEADS * HEAD_DIM), BF16),
                   jax.ShapeDtypeStruct((bsz, N_HEADS, s, 1), F32)],
        compiler_params=_params())


def _attn_bwd(z, do, lse, fcol, frow, dz, exchange=None):
    bsz, s, _ = z.shape
    tq = min(ATTN_TQ, s)
    scale = HEAD_DIM ** -0.5

    def body(z_ref, do_ref, lse_ref, fcol_ref, frow_ref, dz_in, dz_ref, df_ref, dk_acc, dv_acc):
        del dz_in
        dk_acc[...] = jnp.zeros_like(dk_acc)
        dv_acc[...] = jnp.zeros_like(dv_acc)
        df_ref[...] = jnp.zeros_like(df_ref)
        for qb in range(s // tq):
            q0, ln = qb * tq, (qb + 1) * tq
            q_b = (z_ref[0, q0:ln, :HEAD_DIM].astype(F32) * scale).astype(BF16)
            kk = z_ref[0, :ln, HEAD_DIM:2 * HEAD_DIM]
            vv = z_ref[0, :ln, 2 * HEAD_DIM:]
            do_b = do_ref[0, q0:ln, :]
            lse_q = lse_ref[0, 0, q0:ln, :] - fcol_ref[0, 0, q0:ln, :]
            sc = lax.dot_general(q_b, kk, _NT, preferred_element_type=F32) - frow_ref[0, 0, :, :ln] - lse_q
            mask = (q0 + lax.broadcasted_iota(jnp.int32, (tq, ln), 0)) >= lax.broadcasted_iota(jnp.int32, (tq, ln), 1)
            p = jnp.where(mask, jnp.exp(sc), 0.0)
            pb = p.astype(BF16)
            dv_acc[:ln, :] = dv_acc[:ln, :] + lax.dot_general(pb, do_b, _TN, preferred_element_type=F32)
            dp = lax.dot_general(do_b, vv, _NT, preferred_element_type=F32)
            ds = p * (dp - jnp.sum(p * dp, axis=-1, keepdims=True))
            dsb = ds.astype(BF16)
            dq = jnp.dot(dsb, kk, preferred_element_type=F32) * scale
            dz_ref[0, q0:ln, :HEAD_DIM] = dq.astype(BF16)
            dk_acc[:ln, :] = dk_acc[:ln, :] + lax.dot_general(dsb, q_b, _TN, preferred_element_type=F32)
            df_ref[0, 0, :, :ln] = df_ref[0, 0, :, :ln] - jnp.sum(ds, axis=0, keepdims=True)
        dz_ref[0, :, HEAD_DIM:2 * HEAD_DIM] = dk_acc[...].astype(BF16)
        dz_ref[0, :, 2 * HEAD_DIM:] = dv_acc[...].astype(BF16)

    zblk = pl.BlockSpec((1, s, 3 * HEAD_DIM), lambda b, h: (b, 0, h))
    hblk = pl.BlockSpec((1, s, HEAD_DIM), lambda b, h: (b, 0, h))
    col = pl.BlockSpec((1, 1, s, 1), lambda b, h: (b, h, 0, 0))
    rowspec = pl.BlockSpec((1, 1, 1, s), lambda b, h: (b, h, 0, 0))
    return _run(
        body, [z, do, lse, fcol, frow, dz], exchange, name="attn_bwd", grid=(bsz, N_HEADS),
        in_specs=[zblk, hblk, col, col, rowspec, ANY],
        out_specs=[zblk, rowspec],
        out_shape=[jax.ShapeDtypeStruct(dz.shape, dz.dtype), jax.ShapeDtypeStruct((bsz, N_HEADS, 1, s), F32)],
        scratch_shapes=[pltpu.VMEM((s, HEAD_DIM), F32), pltpu.VMEM((s, HEAD_DIM), F32)],
        input_output_aliases={5: 0},
        compiler_params=_params())


def _merge_fwd(z2, pr, pa, *, tr=512):
    t = z2.shape[0]

    def body(mgr_ref, mga_ref, pr_ref, pa_ref, o_ref):
        gr_ = _sigmoid(mgr_ref[...])
        ga = _sigmoid(mga_ref[...])
        o_ref[...] = (gr_ * pr_ref[...] + ga * pa_ref[...]).astype(BF16)

    tok = pl.BlockSpec((tr, D_MODEL), lambda i: (i, 0))
    return _call(body, name="merge_fwd", grid=(t // tr,),
                 in_specs=[pl.BlockSpec((tr, D_MODEL), lambda i: (i, MG0R // D_MODEL)),
                           pl.BlockSpec((tr, D_MODEL), lambda i: (i, MG0R // D_MODEL + 1)), tok, tok],
                 out_specs=tok, out_shape=jax.ShapeDtypeStruct((t, D_MODEL), BF16),
                 compiler_params=_params())(z2, z2, pr, pa)


def _merge_bwd(dm, z2, pr, pa, *, tr=512):
    t = z2.shape[0]

    def body(dm_ref, mgr_ref, mga_ref, pr_ref, pa_ref, dz_ref, dpr_ref, dpa_ref):
        dmv = dm_ref[...]
        gr_ = _sigmoid(mgr_ref[...])
        ga = _sigmoid(mga_ref[...])
        dpr_ref[...] = (dmv * gr_).astype(BF16)
        dpa_ref[...] = (dmv * ga).astype(BF16)
        dz_ref[:, :D_MODEL] = (dmv * pr_ref[...] * gr_ * (1.0 - gr_)).astype(BF16)
        dz_ref[:, D_MODEL:] = (dmv * pa_ref[...] * ga * (1.0 - ga)).astype(BF16)

    tok = pl.BlockSpec((tr, D_MODEL), lambda i: (i, 0))
    mg = pl.BlockSpec((tr, 2 * D_MODEL), lambda i: (i, MG0 // (2 * D_MODEL)))
    return _call(body, name="merge_bwd", grid=(t // tr,),
                 in_specs=[tok, pl.BlockSpec((tr, D_MODEL), lambda i: (i, MG0R // D_MODEL)),
                           pl.BlockSpec((tr, D_MODEL), lambda i: (i, MG0R // D_MODEL + 1)), tok, tok],
                 out_specs=[mg, tok, tok],
                 out_shape=[jax.ShapeDtypeStruct((t, NZ), BF16), jax.ShapeDtypeStruct((t, D_MODEL), BF16),
                            jax.ShapeDtypeStruct((t, D_MODEL), BF16)],
                 compiler_params=_params())(dm, z2, z2, pr, pa)


def _fl_write(dfl_blk, dz):
    bsz, s, w = dfl_blk.shape

    def body(src_ref, dz_in, dz_ref):
        del dz_in
        dz_ref[...] = src_ref[...]

    return _call(body, name="fl_write", grid=(bsz,),
                 in_specs=[pl.BlockSpec((1, s, w), lambda b: (b, 0, 0)), ANY],
                 out_specs=pl.BlockSpec((1, s, w), lambda b: (b, 0, FL0 // w)),
                 out_shape=jax.ShapeDtypeStruct(dz.shape, dz.dtype), input_output_aliases={1: 0},
                 compiler_params=_params())(dfl_blk, dz)


def _ffn_fwd(zf, wc, bc):
    bsz, s, _ = zf.shape
    cw = 128

    def body(zf_ref, wc_ref, bc_ref, o_ref):
        gfp, uf = zf_ref[0, :, :cw].astype(F32), zf_ref[0, :, cw:].astype(F32)
        wcv = wc_ref[...]
        row = lax.broadcasted_iota(jnp.int32, (s, cw), 0)
        gf = bc_ref[...] + wcv[2:3] * gfp
        for j in (1, 2):
            gf = gf + wcv[2 - j:3 - j] * jnp.where(row >= j, pltpu.roll(gfp, j, 0), 0.0)
        o_ref[0] = (_gelu(gf) * uf).astype(BF16)

    return _call(body, name="ffn_fwd", grid=(bsz, D_FF // cw),
                 in_specs=[pl.BlockSpec((1, s, 2 * cw), lambda b, j: (b, 0, j)),
                           pl.BlockSpec((3, cw), lambda b, j: (0, j)), pl.BlockSpec((1, cw), lambda b, j: (0, j))],
                 out_specs=pl.BlockSpec((1, s, cw), lambda b, j: (b, 0, j)),
                 out_shape=jax.ShapeDtypeStruct((bsz, s, D_FF), BF16), compiler_params=_params())(zf, wc, bc)


def _ffn_bwd(dact, zf, wc, bc, exchange=None):
    bsz, s, _ = zf.shape
    cw = 128

    def body(da_ref, zf_ref, wc_ref, bc_ref, dzf_ref, dsm_ref):
        b = pl.program_id(1)
        gfp, uf = zf_ref[0, :, :cw].astype(F32), zf_ref[0, :, cw:].astype(F32)
        wcv = wc_ref[...]
        row = lax.broadcasted_iota(jnp.int32, (s, cw), 0)
        sh = [gfp] + [jnp.where(row >= j, pltpu.roll(gfp, j, 0), 0.0) for j in (1, 2)]
        gf = bc_ref[...]
        for kk in range(3):
            gf = gf + wcv[kk:kk + 1] * sh[2 - kk]
        ge, dge = _gelu_and_grad(gf)
        dav = da_ref[0].astype(F32)
        dgf = dav * uf * dge
        dgfp = wcv[2:3] * dgf
        for j in (1, 2):
            dgfp = dgfp + wcv[2 - j:3 - j] * jnp.where(row < s - j, pltpu.roll(dgf, s - j, 0), 0.0)
        dzf_ref[0] = jnp.concatenate([dgfp, dav * ge], axis=1).astype(BF16)
        rows = [jnp.sum(dgf * sh[2 - kk], axis=0, keepdims=True) for kk in range(3)]
        rows += [jnp.sum(dgf, axis=0, keepdims=True), jnp.zeros((4, cw), F32)]
        _acc(dsm_ref, slice(None), jnp.concatenate(rows, axis=0), b == 0)

    return _run(body, [dact, zf, wc, bc], exchange, name="ffn_bwd", grid=(D_FF // cw, bsz),
                in_specs=[pl.BlockSpec((1, s, cw), lambda j, b: (b, 0, j)),
                          pl.BlockSpec((1, s, 2 * cw), lambda j, b: (b, 0, j)),
                          pl.BlockSpec((3, cw), lambda j, b: (0, j)), pl.BlockSpec((1, cw), lambda j, b: (0, j))],
                out_specs=[pl.BlockSpec((1, s, 2 * cw), lambda j, b: (b, 0, j)),
                           pl.BlockSpec((8, cw), lambda j, b: (0, j))],
                out_shape=[jax.ShapeDtypeStruct(zf.shape, BF16), jax.ShapeDtypeStruct((8, D_FF), F32)],
                compiler_params=_params())


def _adamw(w, m, v, parts, *, name):
    r, c = w.shape
    p = parts.shape[0]
    by_cols = r > 512 and r % 256 != 0
    tr, tc = (r, 256) if by_cols else (r if r <= 512 else 256, c)
    assert r % tr == 0 and c % tc == 0 and parts.shape[1:] == (r, c), (name, w.shape, parts.shape)
    c1 = 1.0 / (1.0 - ADAM_B1 ** ADAM_STEP)
    c2 = 1.0 / (1.0 - ADAM_B2 ** ADAM_STEP)

    def body(w_ref, m_ref, v_ref, p_ref, g_ref, d_ref, mo_ref, vo_ref):
        g = p_ref[0].astype(F32)
        for q in range(1, p):
            g = g + p_ref[q].astype(F32)
        mn = ADAM_B1 * m_ref[...] + (1.0 - ADAM_B1) * g
        vn = ADAM_B2 * v_ref[...] + (1.0 - ADAM_B2) * (g * g)
        g_ref[...] = g
        mo_ref[...] = mn
        vo_ref[...] = vn
        d_ref[...] = -ADAM_LR * ((mn * c1) / (jnp.sqrt(vn * c2) + ADAM_EPS) + ADAM_WD * w_ref[...])

    spec = pl.BlockSpec((tr, tc), (lambda i: (0, i)) if by_cols else (lambda i: (i, 0)))
    shp = jax.ShapeDtypeStruct((r, c), F32)
    return _call(body, name=name, grid=(c // tc if by_cols else r // tr,),
                 in_specs=[spec, spec, spec,
                           pl.BlockSpec((p, tr, tc), (lambda i: (0, 0, i)) if by_cols else (lambda i: (0, i, 0)))],
                 out_specs=[spec] * 4, out_shape=[shp] * 4, compiler_params=_params())(w, m, v, parts)


def _move_rows(srcs, moves, out_rows, name):
    ns = len(srcs)

    def body(*refs):
        ins, out, sems = refs[:ns], refs[ns], refs[ns + 1]
        copies = [pltpu.make_async_copy(ins[i].at[pl.ds(s0, n)], out.at[pl.ds(d0, n)], sems.at[k])
                  for k, (i, s0, d0, n) in enumerate(moves)]
        for cp in copies:
            cp.start()
        for cp in copies:
            cp.wait()

    assert sum(m[3] for m in moves) == out_rows, name
    return _call(body, name=name, in_specs=[ANY] * ns, out_specs=ANY,
                 out_shape=jax.ShapeDtypeStruct((out_rows, srcs[0].shape[1]), srcs[0].dtype),
                 scratch_shapes=[pltpu.SemaphoreType.DMA((len(moves),))])(*srcs)


def _z_slabs():
    q0 = 2 * D_RNN
    slabs = [(q0 + (r * N_HEADS + h) * HEAD_DIM, QKV0 + (3 * h + r) * HEAD_DIM) for h in range(N_HEADS) for r in range(3)]
    slabs += [((r * N_BLK + n) * RNN_BLK, XG0 + (2 * n + r) * RNN_BLK) for n in range(N_BLK) for r in range(2)]
    return slabs


def _to_z_layout(wt):
    fl0 = 2 * D_RNN + 3 * D_MODEL
    flpad = jnp.pad(wt[fl0:fl0 + N_HEADS], ((0, MG0 - FL0 - N_HEADS), (0, 0)))
    moves = [(0, s, d, 128) for s, d in _z_slabs()] + [(1, 0, FL0, MG0 - FL0), (2, 0, MG0, NZ - MG0)]
    return _move_rows([wt, flpad, wt[fl0 + N_HEADS:]], moves, NZ, "to_z_layout")


def _from_z_layout(wt):
    fl0 = 2 * D_RNN + 3 * D_MODEL
    main = _move_rows([wt], [(0, d, s, 128) for s, d in _z_slabs()], fl0, "from_z_layout")
    return jnp.concatenate([main, wt[FL0:FL0 + N_HEADS], wt[MG0:]], axis=0)


def _col_gather(g):
    return g.transpose(1, 0, 2).reshape(g.shape[1], -1)


def _col_scatter(w):
    return w.reshape(w.shape[0], N_DEV, -1).transpose(1, 0, 2)


def kernel(x, c, w_ada, b_ada, g_norm1, w_in, w_rnn_conv, b_rnn_conv, w_lru_a, b_lru_a, w_lru_i, b_lru_i, lru_lambda, b_fgate, w_proj_rnn, w_proj_attn, w_out, g_norm2, w_ffn_up, w_ffn_conv, b_ffn_conv, w_ffn_down, w_ada_final, b_ada_final, g_final, loss_target, m_w_ada, m_b_ada, m_g_norm1, m_w_in, m_w_rnn_conv, m_b_rnn_conv, m_w_lru_a, m_b_lru_a, m_w_lru_i, m_b_lru_i, m_lru_lambda, m_b_fgate, m_w_proj_rnn, m_w_proj_attn, m_w_out, m_g_norm2, m_w_ffn_up, m_w_ffn_conv, m_b_ffn_conv, m_w_ffn_down, m_w_ada_final, m_b_ada_final, m_g_final, v_w_ada, v_b_ada, v_g_norm1, v_w_in, v_w_rnn_conv, v_b_rnn_conv, v_w_lru_a, v_b_lru_a, v_w_lru_i, v_b_lru_i, v_lru_lambda, v_b_fgate, v_w_proj_rnn, v_w_proj_attn, v_w_out, v_g_norm2, v_w_ffn_up, v_w_ffn_conv, v_b_ffn_conv, v_w_ffn_down, v_w_ada_final, v_b_ada_final, v_g_final):
    args = dict(locals())
    bsz, s, _ = x.shape
    t = bsz * s
    nb = N_DEV * bsz
    me = 4 * lax.axis_index("x") + 2 * lax.axis_index("y") + lax.axis_index("c")
    tm = min(2048, t)

    ex_a = _gather_two_level([c, w_in[0].T.astype(BF16), w_rnn_conv[0], w_ffn_conv[0]], "gather_first")
    c_all = ex_a[0].reshape(nb, D_MODEL)
    win_zt = _to_z_layout(ex_a[1].reshape(-1, D_MODEL))
    wrc = _col_gather(ex_a[2])
    wfc = _col_gather(ex_a[3])

    nmod = w_ada.shape[2]
    nmodf = w_ada_final.shape[1]
    mod_cols = _mm(c_all, w_ada[0], name="mod_cols", tm=nb, tn=nmod, tk=D_MODEL, silu_a=True,
                   bias=lax.dynamic_slice(b_ada, (0, me * nmod), (1, nmod)))
    modf_cols = _mm(c_all, w_ada_final, name="modf_cols", tm=nb, tn=nmodf, tk=D_MODEL, silu_a=True,
                    bias=lax.dynamic_slice(b_ada_final.reshape(1, -1), (0, me * nmodf), (1, nmodf)))
    ex_b = _exchange([mod_cols, modf_cols], [True, True], "gather_mod")
    mod = lax.dynamic_slice(ex_b[0], (0, me * bsz, 0), (N_DEV, bsz, nmod)).transpose(1, 0, 2).reshape(bsz, 6, 1, D_MODEL)
    modf = lax.dynamic_slice(ex_b[1], (0, me * bsz, 0), (N_DEV, bsz, nmodf)).transpose(1, 0, 2).reshape(bsz, 2, 1, D_MODEL)
    shift1, scale1, gate1, shift2, scale2, gate2 = [mod[:, i] for i in range(6)]
    shift_f, scale_f = modf[:, 0], modf[:, 1]

    h1 = _norm_fwd(x, g_norm1, shift1, scale1, name="norm1_fwd")[0]
    h1f = h1.reshape(t, D_MODEL)
    zq = _mm(h1f, win_zt, name="mm_in_qkv", tb=True, out_dtype=BF16, tm=tm, tn=1024, tk=D_MODEL,
             b_cols=(QKV0, XG0)).reshape(bsz, s, XG0)
    zr = _mm(h1f, win_zt, name="mm_in_rest", tb=True, tm=tm, tn=1024, tk=D_MODEL,
             b_cols=(XG0, NZR)).reshape(bsz, s, NZR)
    h_rnn, y_rnn, got = _rnn_fwd(zr, wrc, b_rnn_conv, w_lru_a[0], b_lru_a, w_lru_i[0], b_lru_i, lru_lambda,
                                 exchange=([w_ffn_up[0].astype(BF16)], ["gather_zf"]))
    wup_z = got[0]
    flt = zr[:, :, FL0R:FL0R + N_HEADS].transpose(0, 2, 1)
    bfg = b_fgate.reshape(N_HEADS, 1)
    fcum = _fgate_fwd(flt, bfg)
    fcol = fcum.reshape(bsz, N_HEADS, s, 1)
    frow = fcum.reshape(bsz, N_HEADS, 1, s)
    o_attn, lse, got = _attn_fwd(
        zq, fcol, frow, exchange=([w_proj_rnn[0].astype(BF16), w_proj_attn[0].astype(BF16), w_out[0].astype(BF16),
                                  w_ffn_down[0].astype(BF16)], [True] * 4))
    wpr = got[0].reshape(D_RNN, D_MODEL)
    wpa = got[1].reshape(D_MODEL, D_MODEL)
    wout = got[2].reshape(D_MODEL, D_MODEL)
    wdown = got[3].reshape(D_FF, D_MODEL)
    pr = _mm(y_rnn.reshape(t, D_RNN), wpr, name="mm_proj_rnn", out_dtype=BF16, tm=tm, tn=D_MODEL, tk=D_RNN)
    pa = _mm(o_attn.reshape(t, D_MODEL), wpa, name="mm_proj_attn", out_dtype=BF16, tm=tm, tn=D_MODEL, tk=D_MODEL)
    z2 = zr.reshape(t, NZR)
    merged = _merge_fwd(z2, pr, pa)
    mo = _mm(merged, wout, name="mm_out", out_dtype=BF16, tm=tm, tn=D_MODEL, tk=D_MODEL).reshape(bsz, s, D_MODEL)
    h2, x1 = _norm_fwd(x, g_norm2, shift2, scale2, name="norm2_fwd", yprev=mo, gate=gate1)
    zf = _mm(h2.reshape(t, D_MODEL), wup_z, name="mm_up", out_dtype=BF16, tm=tm, tn=1024,
             tk=D_MODEL).reshape(bsz, s, 2 * D_FF)
    act = _ffn_fwd(zf, wfc, b_ffn_conv)
    yf = _mm(act.reshape(t, D_FF), wdown, name="mm_down", out_dtype=BF16, tm=tm, tn=D_MODEL,
             tk=D_FF // 2).reshape(bsz, s, D_MODEL)
    lp, dx2, dyf, dshift_f, dscale_f, dgate2, dg_final = _final(x1, yf, gate2, g_final.reshape(1, -1), shift_f, scale_f, loss_target)
    loss = lax.psum(jnp.sum(lp[:, :, 0, 0]) * (0.5 / D_MODEL), ("x", "y", "c"))

    dyf2 = dyf.reshape(t, D_MODEL)
    act2 = act.reshape(t, D_FF)
    dact = _mm(dyf2, wdown, name="mm_down_dx", tb=True, out_dtype=BF16, tm=tm, tn=1024, tk=D_MODEL).reshape(bsz, s, D_FF)
    g_wdown = _mm(act2, dyf2, name="mm_down_dw", ta=True, out_dtype=BF16, tm=1024, tn=D_MODEL, tk=tm)
    dzf, dsm_ffn, got = _ffn_bwd(dact, zf, wfc, b_ffn_conv, exchange=([g_wdown.reshape(N_DEV, -1, D_MODEL)], [False]))
    p_wdown = got[0]
    dzf2 = dzf.reshape(t, 2 * D_FF)
    dh2 = _mm(dzf2, wup_z, name="mm_up_dx", tb=True, tm=tm, tn=D_MODEL, tk=2048).reshape(bsz, s, D_MODEL)
    g_wup_z = _mm(h2.reshape(t, D_MODEL), dzf2, name="mm_up_dw", ta=True, out_dtype=BF16, tm=D_MODEL, tn=1024, tk=tm)
    dx1, dg_norm2, dshift2, dscale2, dmo, dgate1 = _norm_bwd(x1, dh2, g_norm2, scale2, dx2, name="norm2_bwd", yprev=mo, gate=gate1)
    dmo2 = dmo.reshape(t, D_MODEL)
    dmerged = _mm(dmo2, wout, name="mm_out_dx", tb=True, out_dtype=BF16, tm=tm, tn=D_MODEL, tk=D_MODEL)
    g_wout = _mm(merged, dmo2, name="mm_out_dw", ta=True, out_dtype=BF16, tm=D_MODEL, tn=D_MODEL, tk=tm)
    dz2, dpr, dpa = _merge_bwd(dmerged, z2, pr, pa)
    dy_rnn = _mm(dpr, wpr, name="mm_proj_rnn_dx", tb=True, out_dtype=BF16, tm=tm, tn=D_RNN,
                 tk=D_MODEL).reshape(bsz, s, D_RNN)
    g_wpr = _mm(y_rnn.reshape(t, D_RNN), dpr, name="mm_proj_rnn_dw", ta=True, out_dtype=BF16, tm=D_RNN, tn=D_MODEL, tk=tm)
    do = _mm(dpa, wpa, name="mm_proj_attn_dx", tb=True, out_dtype=BF16, tm=tm, tn=D_MODEL, tk=D_MODEL).reshape(bsz, s, D_MODEL)
    g_wpa = _mm(o_attn.reshape(t, D_MODEL), dpa, name="mm_proj_attn_dw", ta=True, out_dtype=BF16, tm=D_MODEL, tn=D_MODEL, tk=tm)
    dz, dfrow, got = _attn_bwd(zq, do, lse, fcol, frow, dz2.reshape(bsz, s, NZ),
                               exchange=([g_wup_z, g_wout.reshape(N_DEV, -1, D_MODEL)], ["scatter_zf", False]))
    p_wup, p_wout = got
    dflt, db_fgate = _fgate_bwd(dfrow.reshape(bsz, N_HEADS, s), flt, bfg)
    dfl_blk = jnp.pad(dflt.transpose(0, 2, 1).astype(BF16), ((0, 0), (0, 0), (0, MG0 - FL0 - N_HEADS)))
    dz = _fl_write(dfl_blk, dz)
    dz, g_wa, g_wi, dsm_rnn, got = _rnn_bwd(
        zr, h_rnn, dy_rnn, dz, wrc, b_rnn_conv, w_lru_a[0], b_lru_a, w_lru_i[0], b_lru_i, lru_lambda,
        exchange=([g_wpr.reshape(N_DEV, -1, D_MODEL), g_wpa.reshape(N_DEV, -1, D_MODEL)], [False, False]))
    p_wpr, p_wpa = got
    dz2 = dz.reshape(t, NZ)
    small = [dg_norm2, dg_final, dsm_rnn, g_wa, g_wi, db_fgate, dsm_ffn]
    small_sizes = [a.size for a in small]
    n_pad = -sum(small_sizes) % 1024
    packed = jnp.concatenate([a.reshape(-1) for a in small] + [jnp.zeros((n_pad,), F32)]).reshape(-1, 128)
    g_win_zt, got = _mm(dz2, h1f, name="mm_in_dw", ta=True, out_dtype=BF16, tm=1024, tn=D_MODEL, tk=tm,
                        exchange=([packed], [True]))
    flat = got[0].reshape(N_DEV, -1)
    dh1, got = _mm(dz2, win_zt, name="mm_in_dx", tm=tm, tn=D_MODEL, tk=2048,
                   exchange=([_from_z_layout(g_win_zt).reshape(N_DEV, -1, D_MODEL)], [False]))
    p_win = got[0]
    grad_x, dg_norm1, dshift1, dscale1 = _norm_bwd(x, dh1.reshape(bsz, s, D_MODEL), g_norm1, scale1, dx1, name="norm1_bwd")

    dmod = jnp.concatenate([dshift1, dscale1, dgate1, dshift2, dscale2, dgate2], axis=1).reshape(bsz, 6 * D_MODEL)
    dmodf = jnp.concatenate([dshift_f, dscale_f], axis=1).reshape(bsz, 2 * D_MODEL)
    ex_c = _exchange([dmod, dmodf, dg_norm1], [True, True, True], "gather_last_grads")
    dmod_all = ex_c[0].reshape(nb, 6 * D_MODEL)
    dmodf_all = ex_c[1].reshape(nb, 2 * D_MODEL)
    p_g1 = ex_c[2]

    res = {}

    def upd(name, parts, shape2):
        w, m, v = args[name], args["m_" + name], args["v_" + name]
        outs = _adamw(w.reshape(shape2), m.reshape(shape2), v.reshape(shape2), parts, name="adamw_" + name)
        res[name] = [o.reshape(w.shape) for o in outs]

    g_wada = _mm(c_all, lax.dynamic_slice(dmod_all, (0, me * nmod), (nb, nmod)), name="mm_ada_dw", ta=True,
                 silu_a=True, tm=D_MODEL, tn=nmod, tk=nb)
    g_wadaf = _mm(c_all, lax.dynamic_slice(dmodf_all, (0, me * nmodf), (nb, nmodf)), name="mm_adaf_dw", ta=True,
                  silu_a=True, tm=D_MODEL, tn=nmodf, tk=nb)
    upd("w_ada", g_wada[None], (D_MODEL, nmod))
    upd("w_ada_final", g_wadaf[None], (D_MODEL, nmodf))
    upd("b_ada", dmod_all.reshape(nb, 1, -1), (1, 6 * D_MODEL))
    upd("b_ada_final", dmodf_all.reshape(nb, 1, -1), (1, 2 * D_MODEL))
    outs = _adamw(w_in[0].T, m_w_in[0].T, v_w_in[0].T, p_win, name="adamw_w_in")
    res["w_in"] = [o.T[None] for o in outs]
    upd("w_proj_rnn", p_wpr, w_proj_rnn.shape[1:])
    upd("w_proj_attn", p_wpa, w_proj_attn.shape[1:])
    upd("w_out", p_wout, w_out.shape[1:])
    upd("w_ffn_up", p_wup, w_ffn_up.shape[1:])
    upd("w_ffn_down", p_wdown, w_ffn_down.shape[1:])

    parts, off = [], 0
    for a, n in zip(small, small_sizes):
        parts.append(flat[:, off:off + n].reshape((N_DEV,) + a.shape))
        off += n
    p_g2, p_gf, p_rnn, p_wa, p_wi, p_bfg, p_ffn = parts
    upd("g_norm1", p_g1, (1, D_MODEL))
    upd("g_norm2", p_g2, (1, D_MODEL))
    upd("g_final", p_gf, (1, D_MODEL))
    nrc = w_rnn_conv.shape[2]
    upd("w_rnn_conv", lax.dynamic_slice(p_rnn, (0, 0, me * nrc), (N_DEV, 4, nrc)), (4, nrc))
    upd("b_rnn_conv", p_rnn[:, 4:5], (1, D_RNN))
    upd("b_lru_a", p_rnn[:, 5:6], (1, D_RNN))
    upd("b_lru_i", p_rnn[:, 6:7], (1, D_RNN))
    upd("lru_lambda", p_rnn[:, 7:8], (1, D_RNN))
    upd("w_lru_a", p_wa.reshape(N_DEV, D_RNN, RNN_BLK), (D_RNN, RNN_BLK))
    upd("w_lru_i", p_wi.reshape(N_DEV, D_RNN, RNN_BLK), (D_RNN, RNN_BLK))
    upd("b_fgate", p_bfg.reshape(N_DEV, 1, N_HEADS), (1, N_HEADS))
    nfc = w_ffn_conv.shape[2]
    upd("w_ffn_conv", lax.dynamic_slice(p_ffn, (0, 0, me * nfc), (N_DEV, 3, nfc)), (3, nfc))
    upd("b_ffn_conv", p_ffn[:, 3:4], (1, D_FF))

    names = ['w_ada', 'b_ada', 'g_norm1', 'w_in', 'w_rnn_conv', 'b_rnn_conv', 'w_lru_a', 'b_lru_a', 'w_lru_i',
             'b_lru_i', 'lru_lambda', 'b_fgate', 'w_proj_rnn', 'w_proj_attn', 'w_out', 'g_norm2', 'w_ffn_up',
             'w_ffn_conv', 'b_ffn_conv', 'w_ffn_down', 'w_ada_final', 'b_ada_final', 'g_final']
    return (loss, grad_x, *[res[n][0] for n in names], *[res[n][1] for n in names],
            *[res[n][2] for n in names], *[res[n][3] for n in names])
```

```python
import jax
import jax.numpy as jnp
from jax import lax
from jax.experimental import pallas as pl
from jax.experimental.pallas import tpu as pltpu

F32, BF16 = jnp.float32, jnp.bfloat16
D_MODEL, D_RNN, N_BLK, RNN_BLK = 1024, 1280, 10, 128
N_HEADS, HEAD_DIM, D_FF = 8, 128, 3072
N_DEV = 8
NZ = 8192
QKV0, XG0, FL0, MG0 = 0, 3072, 5632, 6144
NZR = NZ - XG0
XG0R, FL0R, MG0R = 0, FL0 - XG0, MG0 - XG0
RMS_EPS = 1e-6
LRU_C = 8.0
ADAM_LR, ADAM_B1, ADAM_B2, ADAM_EPS, ADAM_WD, ADAM_STEP = 0.001, 0.9, 0.999, 1e-08, 0.01, 10
VMEM_LIMIT = 56 << 20
SCAN_CHUNK = 256
ATTN_TQ = 256
CUM_BLK = 512
MESH = pl.DeviceIdType.MESH
ANY = pl.BlockSpec(memory_space=pl.ANY)


def _call(body, **kw):
    return pl.pallas_call(body, **kw)


def _params():
    return pltpu.CompilerParams(vmem_limit_bytes=VMEM_LIMIT)


ZF_PIECES = 2 * D_FF // 128 // N_DEV


def _zf_cols(owner, p):
    ob = owner * ZF_PIECES + p
    blk = 2 * (ob % (D_FF // 128)) + ob // (D_FF // 128)
    return pl.ds(pl.multiple_of(blk * 128, 128), 128)


def _n_pieces(mode):
    return ZF_PIECES if mode in ("gather_zf", "scatter_zf") else 1


def _exchanged_shape(a, mode):
    if mode is True:
        return (N_DEV,) + a.shape
    if mode is False:
        return a.shape
    if mode == "gather_zf":
        return (a.shape[0], N_DEV * a.shape[1])
    return (N_DEV, a.shape[0], a.shape[1] // N_DEV)


def _direct_copies(ins, outs, sems, modes):
    send_sems, recv_sems, loc_sems = sems
    x, y, c = lax.axis_index("x"), lax.axis_index("y"), lax.axis_index("c")
    me = 4 * x + 2 * y + c

    def ends(a, to, p):
        mode = modes[a]
        if mode is True:
            return ins[a], outs[a].at[me]
        if mode is False:
            return ins[a].at[to], outs[a].at[me]
        mine = slice(p * 128, (p + 1) * 128)
        if mode == "gather_zf":
            return ins[a].at[:, mine], outs[a].at[:, _zf_cols(me, p)]
        return ins[a].at[:, _zf_cols(to, p)], outs[a].at[me, :, mine]

    local, remote = [], []
    nloc = 0
    for a in range(len(ins)):
        for p in range(_n_pieces(modes[a])):
            src, dst = ends(a, me, p)
            local.append(pltpu.make_async_copy(src, dst, loc_sems.at[nloc]))
            nloc += 1
    for k in range(1, N_DEV):
        px = 1 - x if k & 4 else x
        py = 1 - y if k & 2 else y
        pc = 1 - c if k & 1 else c
        peer = 4 * px + 2 * py + pc
        nrem = (k - 1) * nloc
        for a in range(len(ins)):
            for p in range(_n_pieces(modes[a])):
                src, dst = ends(a, peer, p)
                remote.append(pltpu.make_async_remote_copy(
                    src_ref=src, dst_ref=dst, send_sem=send_sems.at[nrem], recv_sem=recv_sems.at[nrem],
                    device_id=(px, py, pc), device_id_type=MESH))
                nrem += 1
    return local, remote


def _with_exchange(body, args, arrs, bcast, *, name, grid, in_specs, out_specs, out_shape, scratch_shapes=(), **kw):
    n_in, n_out, n_sc, nc = len(args), len(out_shape), len(scratch_shapes), len(arrs)

    def wrapped(*refs):
        core_in, cin = refs[:n_in], refs[n_in:n_in + nc]
        core_out = refs[n_in + nc:n_in + nc + n_out]
        cout = refs[n_in + nc + n_out:n_in + 2 * nc + n_out]
        rest = refs[n_in + 2 * nc + n_out:]
        core_sc, sems = rest[:n_sc], rest[n_sc:]

        def start():
            local, remote = _direct_copies(cin, cout, sems, bcast)
            for cp in local + remote:
                cp.start()

        def wait():
            local, remote = _direct_copies(cin, cout, sems, bcast)
            for cp in remote + local:
                cp.wait()

        if not grid:
            start()
            body(*core_in, *core_out, *core_sc)
            wait()
            return
        first, last = None, None
        for d, g in enumerate(grid):
            i = pl.program_id(d)
            first = (i == 0) if first is None else jnp.logical_and(first, i == 0)
            last = (i == g - 1) if last is None else jnp.logical_and(last, i == g - 1)
        pl.when(first)(start)
        body(*core_in, *core_out, *core_sc)
        pl.when(last)(wait)

    ex_shape = [jax.ShapeDtypeStruct(_exchanged_shape(a, mode), a.dtype) for a, mode in zip(arrs, bcast)]
    ncopies = sum(_n_pieces(mode) for mode in bcast)
    sems = [pltpu.SemaphoreType.DMA((ncopies * (N_DEV - 1),)), pltpu.SemaphoreType.DMA((ncopies * (N_DEV - 1),)),
            pltpu.SemaphoreType.DMA((ncopies,))]
    res = _call(wrapped, name=name, grid=grid, in_specs=list(in_specs) + [ANY] * nc,
                out_specs=list(out_specs) + [ANY] * nc, out_shape=list(out_shape) + ex_shape,
                scratch_shapes=list(scratch_shapes) + sems, **kw)(*args, *arrs)
    return list(res[:n_out]), list(res[n_out:])


def _run(body, args, exchange, **kw):
    if exchange is None:
        return list(_call(body, **kw)(*args))
    outs, got = _with_exchange(body, args, *exchange, **kw)
    return outs + [got]


def _exchange(arrs, bcast, name):
    return _with_exchange(lambda: None, [], arrs, bcast, name=name, grid=(), in_specs=[], out_specs=[], out_shape=[],
                          compiler_params=pltpu.CompilerParams(has_side_effects=True))[1]


def _gather_two_level(arrs, name):
    n = len(arrs)

    def body(*refs):
        ins, outs = refs[:n], refs[n:2 * n]
        send_sems, recv_sems, loc_sems = refs[2 * n:]
        x, y, c = lax.axis_index("x"), lax.axis_index("y"), lax.axis_index("c")
        me, sibling = (x, y, c), (x, y, 1 - c)
        chips = [(1 - x, y), (x, 1 - y), (1 - x, 1 - y)]

        def slot(a, p):
            return outs[a].at[4 * p[0] + 2 * p[1] + p[2]]

        def copy(a, k, block, to, src=None):
            return pltpu.make_async_remote_copy(
                src_ref=slot(a, block) if src is None else src, dst_ref=slot(a, block),
                send_sem=send_sems.at[a * 7 + k], recv_sem=recv_sems.at[a * 7 + k],
                device_id=to, device_id_type=MESH)

        mine = [pltpu.make_async_copy(ins[a], slot(a, me), loc_sems.at[a]) for a in range(n)]
        first = []
        for a in range(n):
            first.append(copy(a, 0, me, sibling, src=ins[a]))
            first += [copy(a, 1 + j, me, (*chip, c), src=ins[a]) for j, chip in enumerate(chips)]
        for cp in mine + first:
            cp.start()
        passed = []
        for j, chip in enumerate(chips):
            for a in range(n):
                copy(a, 1 + j, (*chip, c), me).wait_recv()
                fwd = copy(a, 4 + j, (*chip, c), sibling)
                fwd.start()
                passed.append(fwd)
        for a in range(n):
            copy(a, 0, sibling, me).wait_recv()
        for j, chip in enumerate(chips):
            for a in range(n):
                copy(a, 4 + j, (*chip, 1 - c), me).wait_recv()
        for cp in first + passed:
            cp.wait_send()
        for cp in mine:
            cp.wait()

    return _call(
        body, name=name, in_specs=[ANY] * n, out_specs=[ANY] * n,
        out_shape=[jax.ShapeDtypeStruct((N_DEV,) + a.shape, a.dtype) for a in arrs],
        scratch_shapes=[pltpu.SemaphoreType.DMA((n * 7,)), pltpu.SemaphoreType.DMA((n * 7,)),
                        pltpu.SemaphoreType.DMA((n,))],
        compiler_params=pltpu.CompilerParams(has_side_effects=True),
    )(*arrs)


def _sigmoid(x):
    return 1.0 / (1.0 + jnp.exp(-x))


def _log1p(e):
    u = 1.0 + e
    d = u - 1.0
    return jnp.where(d == 0.0, e, jnp.log(u) * (e / jnp.where(d == 0.0, 1.0, d)))


def _softplus(x):
    return jnp.maximum(x, 0.0) + _log1p(jnp.exp(-jnp.abs(x)))


_GELU_C = 0.7978845608028654


def _gelu(x):
    t = jnp.tanh(_GELU_C * (x + 0.044715 * x * x * x))
    return 0.5 * x * (1.0 + t)


def _gelu_and_grad(x):
    t = jnp.tanh(_GELU_C * (x + 0.044715 * x * x * x))
    g = 0.5 * x * (1.0 + t)
    dg = 0.5 * (1.0 + t) + 0.5 * x * (1.0 - t * t) * _GELU_C * (1.0 + 3.0 * 0.044715 * x * x)
    return g, dg


def _acc(ref, idx, val, first):
    @pl.when(first)
    def _():
        ref[idx] = val

    @pl.when(jnp.logical_not(first))
    def _():
        ref[idx] = ref[idx] + val


def _split3(x):
    hi = x.astype(BF16)
    r1 = x - hi.astype(F32)
    mid = r1.astype(BF16)
    lo = (r1 - mid.astype(F32)).astype(BF16)
    return hi, mid, lo


def _mm(a, b, *, name, ta=False, tb=False, out_dtype=F32, tm, tn, tk, bias=None, silu_a=False, exchange=None,
        b_cols=None):
    m, k = (a.shape[1], a.shape[0]) if ta else a.shape
    col0, n = b_cols if b_cols is not None else (0, b.shape[0] if tb else b.shape[1])
    assert m % tm == 0 and n % tn == 0 and k % tk == 0 and col0 % tn == 0, (name, m, n, k)
    jo = col0 // tn
    nk = k // tk
    dn = (((0 if ta else 1,), (1 if tb else 0,)), ((), ()))
    use_acc = nk > 1 and out_dtype != F32

    def body(*refs):
        a_ref, b_ref = refs[0], refs[1]
        bias_ref = refs[2] if bias is not None else None
        o_ref = refs[3 if bias is not None else 2]
        av = a_ref[...]
        if silu_a:
            av = av * _sigmoid(av)
        p = lax.dot_general(av.astype(BF16), b_ref[...].astype(BF16), dn, preferred_element_type=F32)
        if bias is not None:
            p = p + bias_ref[...]
        if nk == 1:
            o_ref[...] = p.astype(out_dtype)
            return
        kk = pl.program_id(2)
        acc = refs[-1] if use_acc else o_ref

        @pl.when(kk == 0)
        def _():
            acc[...] = p

        @pl.when(kk > 0)
        def _():
            acc[...] = acc[...] + p

        if use_acc:
            @pl.when(kk == nk - 1)
            def _():
                o_ref[...] = acc[...].astype(out_dtype)

    a_spec = pl.BlockSpec((tk, tm), lambda i, j, q: (q, i)) if ta else pl.BlockSpec((tm, tk), lambda i, j, q: (i, q))
    b_spec = (pl.BlockSpec((tn, tk), lambda i, j, q: (j + jo, q)) if tb
              else pl.BlockSpec((tk, tn), lambda i, j, q: (q, j + jo)))
    in_specs, args = [a_spec, b_spec], [a, b]
    if bias is not None:
        in_specs.append(pl.BlockSpec((1, tn), lambda i, j, q: (0, j)))
        args.append(bias)
    kw = dict(name=name, grid=(m // tm, n // tn, nk), in_specs=in_specs,
              out_specs=[pl.BlockSpec((tm, tn), lambda i, j, q: (i, j))],
              out_shape=[jax.ShapeDtypeStruct((m, n), out_dtype)],
              scratch_shapes=[pltpu.VMEM((tm, tn), F32)] if use_acc else [],
              compiler_params=_params())
    if exchange is None:
        return _call(body, **kw)(*args)[0]
    outs, got = _with_exchange(body, args, *exchange, **kw)
    return outs[0], got


def _tok_spec(ts):
    return pl.BlockSpec((1, ts, D_MODEL), lambda b, s: (b, s, 0))


def _seq_spec():
    return pl.BlockSpec((1, 1, D_MODEL), lambda b, s: (b, 0, 0))


def _vec_spec():
    return pl.BlockSpec((1, D_MODEL), lambda b, s: (0, 0))


def _norm_fwd(x, g, shift, scale, *, name, yprev=None, gate=None, ts=512):
    bsz, s, _ = x.shape
    res = yprev is not None

    def body(*refs):
        if res:
            x_ref, g_ref, sh_ref, sc_ref, y_ref, gt_ref, h_ref, xn_ref = refs
            xv = x_ref[0] + gt_ref[0] * y_ref[0]
            xn_ref[0] = xv
        else:
            x_ref, g_ref, sh_ref, sc_ref, h_ref = refs
            xv = x_ref[0]
        rstd = lax.rsqrt(jnp.mean(xv * xv, axis=-1, keepdims=True) + RMS_EPS)
        nv = xv * rstd * g_ref[...]
        h_ref[0] = (nv * (1.0 + sc_ref[0]) + sh_ref[0]).astype(BF16)

    in_specs = [_tok_spec(ts), _vec_spec(), _seq_spec(), _seq_spec()]
    args = [x, g, shift, scale]
    out_specs = [_tok_spec(ts)]
    out_shape = [jax.ShapeDtypeStruct(x.shape, BF16)]
    if res:
        in_specs += [_tok_spec(ts), _seq_spec()]
        args += [yprev, gate]
        out_specs.append(_tok_spec(ts))
        out_shape.append(jax.ShapeDtypeStruct(x.shape, F32))
    return _call(body, name=name, grid=(bsz, s // ts), in_specs=in_specs, out_specs=out_specs,
                 out_shape=out_shape, compiler_params=_params())(*args)


def _final(x1, yf, gate2, g, shift, scale, target, *, ts=512):
    bsz, s, _ = x1.shape
    ns = s // ts

    def body(x_ref, y_ref, gt_ref, g_ref, sh_ref, sc_ref, t_ref,
             lp_ref, dx_ref, dyf_ref, dsh_ref, dsc_ref, dgt_ref, dg_ref):
        b, si = pl.program_id(0), pl.program_id(1)
        yv, gt, gv = y_ref[0], gt_ref[0], g_ref[...]
        xv = x_ref[0] + gt * yv
        rstd = lax.rsqrt(jnp.mean(xv * xv, axis=-1, keepdims=True) + RMS_EPS)
        xhat = xv * rstd
        nv = xhat * gv
        sc1 = 1.0 + sc_ref[0]
        err = nv * sc1 + sh_ref[0] - t_ref[0]
        lp_ref[...] = jnp.zeros((1, 1, 8, 128), F32) + jnp.sum(err * err)
        dy = err * (1.0 / D_MODEL)
        dn = dy * sc1
        dxh = dn * gv
        dx = rstd * (dxh - xhat * jnp.mean(dxh * xhat, axis=-1, keepdims=True))
        dx_ref[0] = dx
        dyf_ref[0] = (dx * gt).astype(BF16)
        _acc(dsh_ref, 0, jnp.sum(dy, axis=0, keepdims=True), si == 0)
        _acc(dsc_ref, 0, jnp.sum(dy * nv, axis=0, keepdims=True), si == 0)
        _acc(dgt_ref, 0, jnp.sum(dx * yv, axis=0, keepdims=True), si == 0)
        _acc(dg_ref, slice(None), jnp.sum(dn * xhat, axis=0, keepdims=True), jnp.logical_and(b == 0, si == 0))

    seq_shape = jax.ShapeDtypeStruct((bsz, 1, D_MODEL), F32)
    return _call(
        body, name="final_loss", grid=(bsz, ns),
        in_specs=[_tok_spec(ts), _tok_spec(ts), _seq_spec(), _vec_spec(), _seq_spec(), _seq_spec(), _tok_spec(ts)],
        out_specs=[pl.BlockSpec((1, 1, 8, 128), lambda b, s: (b, s, 0, 0)), _tok_spec(ts), _tok_spec(ts),
                   _seq_spec(), _seq_spec(), _seq_spec(), _vec_spec()],
        out_shape=[jax.ShapeDtypeStruct((bsz, ns, 8, 128), F32), jax.ShapeDtypeStruct(x1.shape, F32),
                   jax.ShapeDtypeStruct(x1.shape, BF16), seq_shape, seq_shape, seq_shape,
                   jax.ShapeDtypeStruct((1, D_MODEL), F32)],
        compiler_params=_params(),
    )(x1, yf, gate2, g, shift, scale, target)


def _norm_bwd(x, dh, g, scale, dres, *, name, yprev=None, gate=None, ts=512):
    bsz, s, _ = x.shape
    res = yprev is not None

    def body(*refs):
        if res:
            (x_ref, dh_ref, g_ref, sc_ref, dr_ref, y_ref, gt_ref,
             dx_ref, dg_ref, dsh_ref, dsc_ref, dy_ref, dgt_ref) = refs
        else:
            x_ref, dh_ref, g_ref, sc_ref, dr_ref, dx_ref, dg_ref, dsh_ref, dsc_ref = refs
        b, si = pl.program_id(0), pl.program_id(1)
        xv, dhv, gv = x_ref[0], dh_ref[0], g_ref[...]
        rstd = lax.rsqrt(jnp.mean(xv * xv, axis=-1, keepdims=True) + RMS_EPS)
        xhat = xv * rstd
        dn = dhv * (1.0 + sc_ref[0])
        dxh = dn * gv
        dx = dr_ref[0] + rstd * (dxh - xhat * jnp.mean(dxh * xhat, axis=-1, keepdims=True))
        dx_ref[0] = dx
        _acc(dsh_ref, 0, jnp.sum(dhv, axis=0, keepdims=True), si == 0)
        _acc(dsc_ref, 0, jnp.sum(dhv * (xhat * gv), axis=0, keepdims=True), si == 0)
        _acc(dg_ref, slice(None), jnp.sum(dn * xhat, axis=0, keepdims=True), jnp.logical_and(b == 0, si == 0))
        if res:
            dy_ref[0] = (dx * gt_ref[0]).astype(BF16)
            _acc(dgt_ref, 0, jnp.sum(dx * y_ref[0], axis=0, keepdims=True), si == 0)

    seq_shape = jax.ShapeDtypeStruct((bsz, 1, D_MODEL), F32)
    in_specs = [_tok_spec(ts), _tok_spec(ts), _vec_spec(), _seq_spec(), _tok_spec(ts)]
    args = [x, dh, g, scale, dres]
    out_specs = [_tok_spec(ts), _vec_spec(), _seq_spec(), _seq_spec()]
    out_shape = [jax.ShapeDtypeStruct(x.shape, F32), jax.ShapeDtypeStruct((1, D_MODEL), F32), seq_shape, seq_shape]
    if res:
        in_specs += [_tok_spec(ts), _seq_spec()]
        args += [yprev, gate]
        out_specs += [_tok_spec(ts), _seq_spec()]
        out_shape += [jax.ShapeDtypeStruct(x.shape, BF16), seq_shape]
    return _call(body, name=name, grid=(bsz, s // ts), in_specs=in_specs, out_specs=out_specs,
                 out_shape=out_shape, compiler_params=_params())(*args)


def _lru_gates(xr, wa, ba, wi, bi, sp):
    xb = xr.astype(BF16)
    r = _sigmoid(jnp.dot(xb, wa, preferred_element_type=F32) + ba)
    i = _sigmoid(jnp.dot(xb, wi, preferred_element_type=F32) + bi)
    la = -LRU_C * r * sp
    a = jnp.exp(la)
    mult = jnp.sqrt(-jnp.tanh(la) * (a * a + 1.0))
    return r, i, a, mult


def _scan_forward(a, u, h_in, row):
    n = a.shape[0]
    r8 = row & 7
    for k in (1, 2, 4):
        m = r8 >= k
        u = jnp.where(m, a * pltpu.roll(u, k, 0) + u, u)
        a = jnp.where(m, a * pltpu.roll(a, k, 0), a)
    carry = jnp.broadcast_to(h_in, (8, a.shape[1]))
    out = []
    for j in range(n // 8):
        hj = u[8 * j:8 * j + 8] + a[8 * j:8 * j + 8] * carry
        out.append(hj)
        carry = jnp.broadcast_to(hj[7:8], hj.shape)
    return jnp.concatenate(out, axis=0)


def _scan_backward(a_up, g, g_in, row):
    n = g.shape[0]
    r8 = row & 7
    for k in (1, 2, 4):
        m = r8 < 8 - k
        g = jnp.where(m, a_up * pltpu.roll(g, n - k, 0) + g, g)
        a_up = jnp.where(m, a_up * pltpu.roll(a_up, n - k, 0), a_up)
    carry = jnp.broadcast_to(g_in, (8, g.shape[1]))
    out = [None] * (n // 8)
    for j in reversed(range(n // 8)):
        gj = g[8 * j:8 * j + 8] + a_up[8 * j:8 * j + 8] * carry
        out[j] = gj
        carry = jnp.broadcast_to(gj[0:1], gj.shape)
    return jnp.concatenate(out, axis=0)


def _rnn_fwd(z, wc, bc, wa, ba, wi, bi, lam, exchange=None):
    bsz, s, _ = z.shape
    ch = min(SCAN_CHUNK, s)
    nch = s // ch

    def body(z_ref, wc_ref, bc_ref, wa_ref, ba_ref, wi_ref, bi_ref, lam_ref, h_ref, y_ref):
        wcv, bcv = wc_ref[...], bc_ref[...]
        wav, wiv = wa_ref[0].astype(BF16), wi_ref[0].astype(BF16)
        bav, biv = ba_ref[...], bi_ref[...]
        sp = _softplus(-lam_ref[...])
        row = lax.broadcasted_iota(jnp.int32, (ch, RNN_BLK), 0)

        def chunk(ci, carry):
            prev8, hc = carry
            st = pl.multiple_of(ci * ch, ch)
            blk = z_ref[0, pl.ds(st, ch), :]
            xp, gr = blk[:, :RNN_BLK], blk[:, RNN_BLK:]
            xe = jnp.concatenate([prev8, xp], axis=0)
            xr = bcv + wcv[3:4] * xp
            for j in range(1, 4):
                xr = xr + wcv[3 - j:4 - j] * pltpu.roll(xe, j, 0)[8:]
            _, i, a, mult = _lru_gates(xr, wav, bav, wiv, biv, sp)
            hh = _scan_forward(a, mult * (i * xr), hc, row)
            h_ref[0, pl.ds(st, ch), :] = hh
            y_ref[0, pl.ds(st, ch), :] = (_gelu(gr) * hh).astype(BF16)
            return xp[ch - 8:], hh[ch - 1:]

        lax.fori_loop(0, nch, chunk, (jnp.zeros((8, RNN_BLK), F32), jnp.zeros((1, RNN_BLK), F32)))

    vec = pl.BlockSpec((1, RNN_BLK), lambda b, n: (0, n))
    mat = pl.BlockSpec((1, RNN_BLK, RNN_BLK), lambda b, n: (n, 0, 0))
    out = pl.BlockSpec((1, s, RNN_BLK), lambda b, n: (b, 0, n))
    return _run(
        body, [z, wc, bc, wa, ba, wi, bi, lam], exchange, name="rnn_fwd", grid=(bsz, N_BLK),
        in_specs=[pl.BlockSpec((1, s, 2 * RNN_BLK), lambda b, n: (b, 0, XG0R // (2 * RNN_BLK) + n)),
                  pl.BlockSpec((4, RNN_BLK), lambda b, n: (0, n)), vec, mat, vec, mat, vec, vec],
        out_specs=[out, out],
        out_shape=[jax.ShapeDtypeStruct((bsz, s, D_RNN), F32), jax.ShapeDtypeStruct((bsz, s, D_RNN), BF16)],
        compiler_params=_params())


def _rnn_bwd(z, h, dy, dz, wc, bc, wa, ba, wi, bi, lam, exchange=None):
    bsz, s, _ = z.shape
    ch = min(SCAN_CHUNK, s)
    nch = s // ch

    def body(z_ref, h_ref, dy_ref, dz_in, wc_ref, bc_ref, wa_ref, ba_ref, wi_ref, bi_ref, lam_ref,
             dz_ref, dwa_ref, dwi_ref, dsm_ref):
        del dz_in
        b = pl.program_id(1)
        wcv, bcv = wc_ref[...], bc_ref[...]
        wav, wiv = wa_ref[0].astype(BF16), wi_ref[0].astype(BF16)
        bav, biv = ba_ref[...], bi_ref[...]
        lamv = lam_ref[...]
        sp = _softplus(-lamv)
        row = lax.broadcasted_iota(jnp.int32, (ch, RNN_BLK), 0)

        @pl.when(b == 0)
        def _():
            dwa_ref[...] = jnp.zeros_like(dwa_ref)
            dwi_ref[...] = jnp.zeros_like(dwi_ref)
            dsm_ref[...] = jnp.zeros_like(dsm_ref)

        def chunk(k, carry):
            a_next, g_next, dxr_next8 = carry
            ci = nch - 1 - k
            st = pl.multiple_of(ci * ch, ch)
            pst = pl.multiple_of(jnp.maximum(st - 8, 0), 8)
            has_prev = (ci > 0).astype(F32)
            blk = z_ref[0, pl.ds(st, ch), :]
            xp, gr = blk[:, :RNN_BLK], blk[:, RNN_BLK:]
            xprev8 = z_ref[0, pl.ds(pst, 8), :][:, :RNN_BLK] * has_prev
            hprev8 = h_ref[0, pl.ds(pst, 8), :] * has_prev
            xe = jnp.concatenate([xprev8, xp], axis=0)
            xs = [xp] + [pltpu.roll(xe, j, 0)[8:] for j in range(1, 4)]
            xr = bcv
            for kk in range(4):
                xr = xr + wcv[kk:kk + 1] * xs[3 - kk]
            r, i, a, mult = _lru_gates(xr, wav, bav, wiv, biv, sp)
            hh = h_ref[0, pl.ds(st, ch), :]
            hm1 = pltpu.roll(jnp.concatenate([hprev8, hh], axis=0), 1, 0)[8:]
            dyv = dy_ref[0, pl.ds(st, ch), :]
            ge, dge = _gelu_and_grad(gr)
            dgr = dyv * hh * dge
            gg = _scan_backward(jnp.where(row == ch - 1, a_next, pltpu.roll(a, ch - 1, 0)), dyv * ge, g_next, row)
            da = gg * hm1
            ix = i * xr
            dmult = gg * ix
            di = gg * mult * xr
            dxr = gg * mult * i
            dla = da * a - dmult * (a * a) / mult
            dr = dla * (-LRU_C * sp)
            dsp = jnp.sum(dla * (-LRU_C * r), axis=0, keepdims=True)
            dpa = dr * r * (1.0 - r)
            dpi = di * i * (1.0 - i)
            dpab, dpib, xrb = dpa.astype(BF16), dpi.astype(BF16), xr.astype(BF16)
            nt = (((1,), (1,)), ((), ()))
            tn = (((0,), (0,)), ((), ()))
            dxr = dxr + lax.dot_general(dpab, wav, nt, preferred_element_type=F32)
            dxr = dxr + lax.dot_general(dpib, wiv, nt, preferred_element_type=F32)
            dwa_ref[0] = dwa_ref[0] + lax.dot_general(xrb, dpab, tn, preferred_element_type=F32)
            dwi_ref[0] = dwi_ref[0] + lax.dot_general(xrb, dpib, tn, preferred_element_type=F32)
            rows = [jnp.sum(dxr * xs[3 - q], axis=0, keepdims=True) for q in range(4)]
            rows += [jnp.sum(dxr, axis=0, keepdims=True), jnp.sum(dpa, axis=0, keepdims=True),
                     jnp.sum(dpi, axis=0, keepdims=True), dsp * (-_sigmoid(-lamv))]
            dsm_ref[...] = dsm_ref[...] + jnp.concatenate(rows, axis=0)
            dxe = jnp.concatenate([dxr, dxr_next8], axis=0)
            dxp = wcv[3:4] * dxr
            for j in range(1, 4):
                dxp = dxp + wcv[3 - j:4 - j] * pltpu.roll(dxe, ch + 8 - j, 0)[:ch]
            dz_ref[0, pl.ds(st, ch), :] = jnp.concatenate([dxp, dgr], axis=1).astype(BF16)
            return a[0:1], gg[0:1], dxr[0:8]

        lax.fori_loop(0, nch, chunk, (jnp.zeros((1, RNN_BLK), F32), jnp.zeros((1, RNN_BLK), F32),
                                      jnp.zeros((8, RNN_BLK), F32)))

    vec = pl.BlockSpec((1, RNN_BLK), lambda n, b: (0, n))
    mat = pl.BlockSpec((1, RNN_BLK, RNN_BLK), lambda n, b: (n, 0, 0))
    tok = pl.BlockSpec((1, s, RNN_BLK), lambda n, b: (b, 0, n))
    zblk = pl.BlockSpec((1, s, 2 * RNN_BLK), lambda n, b: (b, 0, XG0R // (2 * RNN_BLK) + n))
    dzblk = pl.BlockSpec((1, s, 2 * RNN_BLK), lambda n, b: (b, 0, XG0 // (2 * RNN_BLK) + n))
    return _run(
        body, [z, h, dy, dz, wc, bc, wa, ba, wi, bi, lam], exchange, name="rnn_bwd", grid=(N_BLK, bsz),
        in_specs=[zblk, tok, tok, ANY, pl.BlockSpec((4, RNN_BLK), lambda n, b: (0, n)), vec, mat, vec, mat, vec, vec],
        out_specs=[dzblk, mat, mat, pl.BlockSpec((8, RNN_BLK), lambda n, b: (0, n))],
        out_shape=[jax.ShapeDtypeStruct(dz.shape, dz.dtype), jax.ShapeDtypeStruct((N_BLK, RNN_BLK, RNN_BLK), F32),
                   jax.ShapeDtypeStruct((N_BLK, RNN_BLK, RNN_BLK), F32), jax.ShapeDtypeStruct((8, D_RNN), F32)],
        input_output_aliases={3: 0},
        compiler_params=_params())


def _tri(n, upper):
    r = lax.broadcasted_iota(jnp.int32, (n, n), 0)
    c = lax.broadcasted_iota(jnp.int32, (n, n), 1)
    return (r <= c if upper else r >= c).astype(BF16)


def _exact_dot(x, t):
    hi, mid, lo = _split3(x)
    return (jnp.dot(hi, t, preferred_element_type=F32) + jnp.dot(mid, t, preferred_element_type=F32)
            + jnp.dot(lo, t, preferred_element_type=F32))


def _fgate_fwd(flt, bfg):
    bsz, nh, s = flt.shape
    blk = min(CUM_BLK, s)

    def body(fl_ref, b_ref, f_ref):
        tri = _tri(blk, True)
        carry = jnp.zeros((nh, 1), F32)
        for q in range(s // blk):
            xq = fl_ref[0, :, q * blk:(q + 1) * blk] + b_ref[...]
            fq = _exact_dot(-_softplus(-xq), tri) + carry
            f_ref[0, :, q * blk:(q + 1) * blk] = fq
            carry = fq[:, blk - 1:blk]

    spec = pl.BlockSpec((1, nh, s), lambda b: (b, 0, 0))
    return _call(body, name="fgate_fwd", grid=(bsz,), in_specs=[spec, pl.BlockSpec((nh, 1), lambda b: (0, 0))],
                 out_specs=spec, out_shape=jax.ShapeDtypeStruct(flt.shape, F32), compiler_params=_params())(flt, bfg)


def _fgate_bwd(dft, flt, bfg):
    bsz, nh, s = flt.shape
    blk = min(CUM_BLK, s)

    def body(df_ref, fl_ref, b_ref, dfl_ref, db_ref):
        b = pl.program_id(0)
        tri = _tri(blk, False)
        carry = jnp.zeros((nh, 1), F32)
        tot = jnp.zeros((nh, 1), F32)
        for q in reversed(range(s // blk)):
            sl = slice(q * blk, (q + 1) * blk)
            dlq = _exact_dot(df_ref[0, :, sl], tri) + carry
            carry = dlq[:, 0:1]
            dflq = dlq * _sigmoid(-(fl_ref[0, :, sl] + b_ref[...]))
            dfl_ref[0, :, sl] = dflq
            tot = tot + jnp.sum(dflq, axis=1, keepdims=True)
        _acc(db_ref, slice(None), tot, b == 0)

    spec = pl.BlockSpec((1, nh, s), lambda b: (b, 0, 0))
    small = pl.BlockSpec((nh, 1), lambda b: (0, 0))
    return _call(body, name="fgate_bwd", grid=(bsz,), in_specs=[spec, spec, small], out_specs=[spec, small],
                 out_shape=[jax.ShapeDtypeStruct(flt.shape, F32), jax.ShapeDtypeStruct((nh, 1), F32)],
                 compiler_params=_params())(dft, flt, bfg)


_NT = (((1,), (1,)), ((), ()))
_TN = (((0,), (0,)), ((), ()))


def _attn_block_scores(q_b, z_ref, frow_ref, k0, tq):
    kk = z_ref[0, k0:k0 + tq, HEAD_DIM:2 * HEAD_DIM]
    return lax.dot_general(q_b, kk, _NT, preferred_element_type=F32) - frow_ref[0, 0, :, k0:k0 + tq]


def _causal(tq):
    return lax.broadcasted_iota(jnp.int32, (tq, tq), 0) >= lax.broadcasted_iota(jnp.int32, (tq, tq), 1)


def _attn_fwd(z, fcol, frow, exchange=None):
    bsz, s, _ = z.shape
    tq = min(ATTN_TQ, s)
    scale = HEAD_DIM ** -0.5

    def body(z_ref, fcol_ref, frow_ref, o_ref, lse_ref):
        for qb in range(s // tq):
            q0 = qb * tq
            q_b = (z_ref[0, q0:q0 + tq, :HEAD_DIM].astype(F32) * scale).astype(BF16)
            m = jnp.full((tq, 1), -jnp.inf, F32)
            l = jnp.zeros((tq, 1), F32)
            acc = jnp.zeros((tq, HEAD_DIM), F32)
            for kb in range(qb + 1):
                k0 = kb * tq
                sc = _attn_block_scores(q_b, z_ref, frow_ref, k0, tq)
                if kb == qb:
                    sc = jnp.where(_causal(tq), sc, -jnp.inf)
                m_new = jnp.maximum(m, jnp.max(sc, axis=-1, keepdims=True))
                alpha = jnp.exp(m - m_new)
                p = jnp.exp(sc - m_new)
                l = alpha * l + jnp.sum(p, axis=-1, keepdims=True)
                vv = z_ref[0, k0:k0 + tq, 2 * HEAD_DIM:]
                acc = alpha * acc + jnp.dot(p.astype(BF16), vv, preferred_element_type=F32)
                m = m_new
            o_ref[0, q0:q0 + tq, :] = (acc / l).astype(BF16)
            lse_ref[0, 0, q0:q0 + tq, :] = m + jnp.log(l) + fcol_ref[0, 0, q0:q0 + tq, :]

    return _run(
        body, [z, fcol, frow], exchange, name="attn_fwd", grid=(bsz, N_HEADS),
        in_specs=[pl.BlockSpec((1, s, 3 * HEAD_DIM), lambda b, h: (b, 0, h)),
                  pl.BlockSpec((1, 1, s, 1), lambda b, h: (b, h, 0, 0)),
                  pl.BlockSpec((1, 1, 1, s), lambda b, h: (b, h, 0, 0))],
        out_specs=[pl.BlockSpec((1, s, HEAD_DIM), lambda b, h: (b, 0, h)),
                   pl.BlockSpec((1, 1, s, 1), lambda b, h: (b, h, 0, 0))],
        out_shape=[jax.ShapeDtypeStruct((bsz, s, N_HEADS * HEAD_DIM), BF16),
                   jax.ShapeDtypeStruct((bsz, N_HEADS, s, 1), F32)],
        compiler_params=_params())


def _attn_bwd(z, do, lse, fcol, frow, dz, exchange=None):
    bsz, s, _ = z.shape
    tq = min(ATTN_TQ, s)
    scale = HEAD_DIM ** -0.5

    def body(z_ref, do_ref, lse_ref, fcol_ref, frow_ref, dz_in, dz_ref, df_ref, dk_acc, dv_acc):
        del dz_in
        dk_acc[...] = jnp.zeros_like(dk_acc)
        dv_acc[...] = jnp.zeros_like(dv_acc)
        df_ref[...] = jnp.zeros_like(df_ref)
        for qb in range(s // tq):
            q0, ln = qb * tq, (qb + 1) * tq
            q_b = (z_ref[0, q0:ln, :HEAD_DIM].astype(F32) * scale).astype(BF16)
            kk = z_ref[0, :ln, HEAD_DIM:2 * HEAD_DIM]
            vv = z_ref[0, :ln, 2 * HEAD_DIM:]
            do_b = do_ref[0, q0:ln, :]
            lse_q = lse_ref[0, 0, q0:ln, :] - fcol_ref[0, 0, q0:ln, :]
            sc = lax.dot_general(q_b, kk, _NT, preferred_element_type=F32) - frow_ref[0, 0, :, :ln] - lse_q
            mask = (q0 + lax.broadcasted_iota(jnp.int32, (tq, ln), 0)) >= lax.broadcasted_iota(jnp.int32, (tq, ln), 1)
            p = jnp.where(mask, jnp.exp(sc), 0.0)
            pb = p.astype(BF16)
            dv_acc[:ln, :] = dv_acc[:ln, :] + lax.dot_general(pb, do_b, _TN, preferred_element_type=F32)
            dp = lax.dot_general(do_b, vv, _NT, preferred_element_type=F32)
            ds = p * (dp - jnp.sum(p * dp, axis=-1, keepdims=True))
            dsb = ds.astype(BF16)
            dq = jnp.dot(dsb, kk, preferred_element_type=F32) * scale
            dz_ref[0, q0:ln, :HEAD_DIM] = dq.astype(BF16)
            dk_acc[:ln, :] = dk_acc[:ln, :] + lax.dot_general(dsb, q_b, _TN, preferred_element_type=F32)
            df_ref[0, 0, :, :ln] = df_ref[0, 0, :, :ln] - jnp.sum(ds, axis=0, keepdims=True)
        dz_ref[0, :, HEAD_DIM:2 * HEAD_DIM] = dk_acc[...].astype(BF16)
        dz_ref[0, :, 2 * HEAD_DIM:] = dv_acc[...].astype(BF16)

    zblk = pl.BlockSpec((1, s, 3 * HEAD_DIM), lambda b, h: (b, 0, h))
    hblk = pl.BlockSpec((1, s, HEAD_DIM), lambda b, h: (b, 0, h))
    col = pl.BlockSpec((1, 1, s, 1), lambda b, h: (b, h, 0, 0))
    rowspec = pl.BlockSpec((1, 1, 1, s), lambda b, h: (b, h, 0, 0))
    return _run(
        body, [z, do, lse, fcol, frow, dz], exchange, name="attn_bwd", grid=(bsz, N_HEADS),
        in_specs=[zblk, hblk, col, col, rowspec, ANY],
        out_specs=[zblk, rowspec],
        out_shape=[jax.ShapeDtypeStruct(dz.shape, dz.dtype), jax.ShapeDtypeStruct((bsz, N_HEADS, 1, s), F32)],
        scratch_shapes=[pltpu.VMEM((s, HEAD_DIM), F32), pltpu.VMEM((s, HEAD_DIM), F32)],
        input_output_aliases={5: 0},
        compiler_params=_params())


def _merge_fwd(z2, pr, pa, *, tr=512):
    t = z2.shape[0]

    def body(mgr_ref, mga_ref, pr_ref, pa_ref, o_ref):
        gr_ = _sigmoid(mgr_ref[...])
        ga = _sigmoid(mga_ref[...])
        o_ref[...] = (gr_ * pr_ref[...] + ga * pa_ref[...]).astype(BF16)

    tok = pl.BlockSpec((tr, D_MODEL), lambda i: (i, 0))
    return _call(body, name="merge_fwd", grid=(t // tr,),
                 in_specs=[pl.BlockSpec((tr, D_MODEL), lambda i: (i, MG0R // D_MODEL)),
                           pl.BlockSpec((tr, D_MODEL), lambda i: (i, MG0R // D_MODEL + 1)), tok, tok],
                 out_specs=tok, out_shape=jax.ShapeDtypeStruct((t, D_MODEL), BF16),
                 compiler_params=_params())(z2, z2, pr, pa)


def _merge_bwd(dm, z2, pr, pa, *, tr=512):
    t = z2.shape[0]

    def body(dm_ref, mgr_ref, mga_ref, pr_ref, pa_ref, dz_ref, dpr_ref, dpa_ref):
        dmv = dm_ref[...]
        gr_ = _sigmoid(mgr_ref[...])
        ga = _sigmoid(mga_ref[...])
        dpr_ref[...] = (dmv * gr_).astype(BF16)
        dpa_ref[...] = (dmv * ga).astype(BF16)
        dz_ref[:, :D_MODEL] = (dmv * pr_ref[...] * gr_ * (1.0 - gr_)).astype(BF16)
        dz_ref[:, D_MODEL:] = (dmv * pa_ref[...] * ga * (1.0 - ga)).astype(BF16)

    tok = pl.BlockSpec((tr, D_MODEL), lambda i: (i, 0))
    mg = pl.BlockSpec((tr, 2 * D_MODEL), lambda i: (i, MG0 // (2 * D_MODEL)))
    return _call(body, name="merge_bwd", grid=(t // tr,),
                 in_specs=[tok, pl.BlockSpec((tr, D_MODEL), lambda i: (i, MG0R // D_MODEL)),
                           pl.BlockSpec((tr, D_MODEL), lambda i: (i, MG0R // D_MODEL + 1)), tok, tok],
                 out_specs=[mg, tok, tok],
                 out_shape=[jax.ShapeDtypeStruct((t, NZ), BF16), jax.ShapeDtypeStruct((t, D_MODEL), BF16),
                            jax.ShapeDtypeStruct((t, D_MODEL), BF16)],
                 compiler_params=_params())(dm, z2, z2, pr, pa)


def _fl_write(dfl_blk, dz):
    bsz, s, w = dfl_blk.shape

    def body(src_ref, dz_in, dz_ref):
        del dz_in
        dz_ref[...] = src_ref[...]

    return _call(body, name="fl_write", grid=(bsz,),
                 in_specs=[pl.BlockSpec((1, s, w), lambda b: (b, 0, 0)), ANY],
                 out_specs=pl.BlockSpec((1, s, w), lambda b: (b, 0, FL0 // w)),
                 out_shape=jax.ShapeDtypeStruct(dz.shape, dz.dtype), input_output_aliases={1: 0},
                 compiler_params=_params())(dfl_blk, dz)


def _ffn_fwd(zf, wc, bc):
    bsz, s, _ = zf.shape
    cw = 128

    def body(zf_ref, wc_ref, bc_ref, o_ref):
        gfp, uf = zf_ref[0, :, :cw].astype(F32), zf_ref[0, :, cw:].astype(F32)
        wcv = wc_ref[...]
        row = lax.broadcasted_iota(jnp.int32, (s, cw), 0)
        gf = bc_ref[...] + wcv[2:3] * gfp
        for j in (1, 2):
            gf = gf + wcv[2 - j:3 - j] * jnp.where(row >= j, pltpu.roll(gfp, j, 0), 0.0)
        o_ref[0] = (_gelu(gf) * uf).astype(BF16)

    return _call(body, name="ffn_fwd", grid=(bsz, D_FF // cw),
                 in_specs=[pl.BlockSpec((1, s, 2 * cw), lambda b, j: (b, 0, j)),
                           pl.BlockSpec((3, cw), lambda b, j: (0, j)), pl.BlockSpec((1, cw), lambda b, j: (0, j))],
                 out_specs=pl.BlockSpec((1, s, cw), lambda b, j: (b, 0, j)),
                 out_shape=jax.ShapeDtypeStruct((bsz, s, D_FF), BF16), compiler_params=_params())(zf, wc, bc)


def _ffn_bwd(dact, zf, wc, bc, exchange=None):
    bsz, s, _ = zf.shape
    cw = 128

    def body(da_ref, zf_ref, wc_ref, bc_ref, dzf_ref, dsm_ref):
        b = pl.program_id(1)
        gfp, uf = zf_ref[0, :, :cw].astype(F32), zf_ref[0, :, cw:].astype(F32)
        wcv = wc_ref[...]
        row = lax.broadcasted_iota(jnp.int32, (s, cw), 0)
        sh = [gfp] + [jnp.where(row >= j, pltpu.roll(gfp, j, 0), 0.0) for j in (1, 2)]
        gf = bc_ref[...]
        for kk in range(3):
            gf = gf + wcv[kk:kk + 1] * sh[2 - kk]
        ge, dge = _gelu_and_grad(gf)
        dav = da_ref[0].astype(F32)
        dgf = dav * uf * dge
        dgfp = wcv[2:3] * dgf
        for j in (1, 2):
            dgfp = dgfp + wcv[2 - j:3 - j] * jnp.where(row < s - j, pltpu.roll(dgf, s - j, 0), 0.0)
        dzf_ref[0] = jnp.concatenate([dgfp, dav * ge], axis=1).astype(BF16)
        rows = [jnp.sum(dgf * sh[2 - kk], axis=0, keepdims=True) for kk in range(3)]
        rows += [jnp.sum(dgf, axis=0, keepdims=True), jnp.zeros((4, cw), F32)]
        _acc(dsm_ref, slice(None), jnp.concatenate(rows, axis=0), b == 0)

    return _run(body, [dact, zf, wc, bc], exchange, name="ffn_bwd", grid=(D_FF // cw, bsz),
                in_specs=[pl.BlockSpec((1, s, cw), lambda j, b: (b, 0, j)),
                          pl.BlockSpec((1, s, 2 * cw), lambda j, b: (b, 0, j)),
                          pl.BlockSpec((3, cw), lambda j, b: (0, j)), pl.BlockSpec((1, cw), lambda j, b: (0, j))],
                out_specs=[pl.BlockSpec((1, s, 2 * cw), lambda j, b: (b, 0, j)),
                           pl.BlockSpec((8, cw), lambda j, b: (0, j))],
                out_shape=[jax.ShapeDtypeStruct(zf.shape, BF16), jax.ShapeDtypeStruct((8, D_FF), F32)],
                compiler_params=_params())


def _adamw(w, m, v, parts, *, name):
    r, c = w.shape
    p = parts.shape[0]
    by_cols = r > 512 and r % 256 != 0
    tr, tc = (r, 256) if by_cols else (r if r <= 512 else 256, c)
    assert r % tr == 0 and c % tc == 0 and parts.shape[1:] == (r, c), (name, w.shape, parts.shape)
    c1 = 1.0 / (1.0 - ADAM_B1 ** ADAM_STEP)
    c2 = 1.0 / (1.0 - ADAM_B2 ** ADAM_STEP)

    def body(w_ref, m_ref, v_ref, p_ref, g_ref, d_ref, mo_ref, vo_ref):
        g = p_ref[0].astype(F32)
        for q in range(1, p):
            g = g + p_ref[q].astype(F32)
        mn = ADAM_B1 * m_ref[...] + (1.0 - ADAM_B1) * g
        vn = ADAM_B2 * v_ref[...] + (1.0 - ADAM_B2) * (g * g)
        g_ref[...] = g
        mo_ref[...] = mn
        vo_ref[...] = vn
        d_ref[...] = -ADAM_LR * ((mn * c1) / (jnp.sqrt(vn * c2) + ADAM_EPS) + ADAM_WD * w_ref[...])

    spec = pl.BlockSpec((tr, tc), (lambda i: (0, i)) if by_cols else (lambda i: (i, 0)))
    shp = jax.ShapeDtypeStruct((r, c), F32)
    return _call(body, name=name, grid=(c // tc if by_cols else r // tr,),
                 in_specs=[spec, spec, spec,
                           pl.BlockSpec((p, tr, tc), (lambda i: (0, 0, i)) if by_cols else (lambda i: (0, i, 0)))],
                 out_specs=[spec] * 4, out_shape=[shp] * 4, compiler_params=_params())(w, m, v, parts)


def _ref_slab(i):
    nq = 3 * N_HEADS
    j = i - nq
    return jnp.where(i < nq, 2 * D_RNN // 128 + (i % 3) * N_HEADS + i // 3, (j % 2) * N_BLK + j // 2)


def _permute_slabs(x, to_z, name):
    n = FL0 // 128
    cols = x.shape[1]

    def body(x_ref, o_ref):
        o_ref[...] = x_ref[...]

    there = pl.BlockSpec((128, cols), lambda i: (_ref_slab(i), 0))
    here = pl.BlockSpec((128, cols), lambda i: (i, 0))
    return _call(body, name=name, grid=(n,), in_specs=[there if to_z else here],
                 out_specs=here if to_z else there, out_shape=jax.ShapeDtypeStruct((FL0, cols), x.dtype),
                 compiler_params=_params())(x)


def _to_z_layout(wt):
    fl0 = 2 * D_RNN + 3 * D_MODEL
    pad = jnp.zeros((MG0 - FL0 - N_HEADS, wt.shape[1]), wt.dtype)
    return jnp.concatenate([_permute_slabs(wt, True, "to_z_layout"), wt[fl0:fl0 + N_HEADS], pad,
                            wt[fl0 + N_HEADS:]], axis=0)


def _from_z_layout(wt):
    return jnp.concatenate([_permute_slabs(wt, False, "from_z_layout"), wt[FL0:FL0 + N_HEADS], wt[MG0:]], axis=0)


def _col_gather(g):
    return g.transpose(1, 0, 2).reshape(g.shape[1], -1)


def _col_scatter(w):
    return w.reshape(w.shape[0], N_DEV, -1).transpose(1, 0, 2)


def kernel(x, c, w_ada, b_ada, g_norm1, w_in, w_rnn_conv, b_rnn_conv, w_lru_a, b_lru_a, w_lru_i, b_lru_i, lru_lambda, b_fgate, w_proj_rnn, w_proj_attn, w_out, g_norm2, w_ffn_up, w_ffn_conv, b_ffn_conv, w_ffn_down, w_ada_final, b_ada_final, g_final, loss_target, m_w_ada, m_b_ada, m_g_norm1, m_w_in, m_w_rnn_conv, m_b_rnn_conv, m_w_lru_a, m_b_lru_a, m_w_lru_i, m_b_lru_i, m_lru_lambda, m_b_fgate, m_w_proj_rnn, m_w_proj_attn, m_w_out, m_g_norm2, m_w_ffn_up, m_w_ffn_conv, m_b_ffn_conv, m_w_ffn_down, m_w_ada_final, m_b_ada_final, m_g_final, v_w_ada, v_b_ada, v_g_norm1, v_w_in, v_w_rnn_conv, v_b_rnn_conv, v_w_lru_a, v_b_lru_a, v_w_lru_i, v_b_lru_i, v_lru_lambda, v_b_fgate, v_w_proj_rnn, v_w_proj_attn, v_w_out, v_g_norm2, v_w_ffn_up, v_w_ffn_conv, v_b_ffn_conv, v_w_ffn_down, v_w_ada_final, v_b_ada_final, v_g_final):
    args = dict(locals())
    bsz, s, _ = x.shape
    t = bsz * s
    nb = N_DEV * bsz
    me = 4 * lax.axis_index("x") + 2 * lax.axis_index("y") + lax.axis_index("c")
    tm = min(2048, t)

    ex_a = _gather_two_level([c, w_in[0].T.astype(BF16), w_rnn_conv[0], w_ffn_conv[0]], "gather_first")
    c_all = ex_a[0].reshape(nb, D_MODEL)
    win_zt = _to_z_layout(ex_a[1].reshape(-1, D_MODEL))
    wrc = _col_gather(ex_a[2])
    wfc = _col_gather(ex_a[3])

    nmod = w_ada.shape[2]
    nmodf = w_ada_final.shape[1]
    mod_cols = _mm(c_all, w_ada[0], name="mod_cols", tm=nb, tn=nmod, tk=D_MODEL, silu_a=True,
                   bias=lax.dynamic_slice(b_ada, (0, me * nmod), (1, nmod)))
    modf_cols = _mm(c_all, w_ada_final, name="modf_cols", tm=nb, tn=nmodf, tk=D_MODEL, silu_a=True,
                    bias=lax.dynamic_slice(b_ada_final.reshape(1, -1), (0, me * nmodf), (1, nmodf)))
    ex_b = _exchange([mod_cols, modf_cols], [True, True], "gather_mod")
    mod = lax.dynamic_slice(ex_b[0], (0, me * bsz, 0), (N_DEV, bsz, nmod)).transpose(1, 0, 2).reshape(bsz, 6, 1, D_MODEL)
    modf = lax.dynamic_slice(ex_b[1], (0, me * bsz, 0), (N_DEV, bsz, nmodf)).transpose(1, 0, 2).reshape(bsz, 2, 1, D_MODEL)
    shift1, scale1, gate1, shift2, scale2, gate2 = [mod[:, i] for i in range(6)]
    shift_f, scale_f = modf[:, 0], modf[:, 1]

    h1 = _norm_fwd(x, g_norm1, shift1, scale1, name="norm1_fwd")[0]
    h1f = h1.reshape(t, D_MODEL)
    zq = _mm(h1f, win_zt, name="mm_in_qkv", tb=True, out_dtype=BF16, tm=tm, tn=1024, tk=D_MODEL,
             b_cols=(QKV0, XG0)).reshape(bsz, s, XG0)
    zr = _mm(h1f, win_zt, name="mm_in_rest", tb=True, tm=tm, tn=1024, tk=D_MODEL,
             b_cols=(XG0, NZR)).reshape(bsz, s, NZR)
    h_rnn, y_rnn, got = _rnn_fwd(zr, wrc, b_rnn_conv, w_lru_a[0], b_lru_a, w_lru_i[0], b_lru_i, lru_lambda,
                                 exchange=([w_ffn_up[0].astype(BF16)], ["gather_zf"]))
    wup_z = got[0]
    flt = zr[:, :, FL0R:FL0R + N_HEADS].transpose(0, 2, 1)
    bfg = b_fgate.reshape(N_HEADS, 1)
    fcum = _fgate_fwd(flt, bfg)
    fcol = fcum.reshape(bsz, N_HEADS, s, 1)
    frow = fcum.reshape(bsz, N_HEADS, 1, s)
    o_attn, lse, got = _attn_fwd(
        zq, fcol, frow, exchange=([w_proj_rnn[0].astype(BF16), w_proj_attn[0].astype(BF16), w_out[0].astype(BF16),
                                  w_ffn_down[0].astype(BF16)], [True] * 4))
    wpr = got[0].reshape(D_RNN, D_MODEL)
    wpa = got[1].reshape(D_MODEL, D_MODEL)
    wout = got[2].reshape(D_MODEL, D_MODEL)
    wdown = got[3].reshape(D_FF, D_MODEL)
    pr = _mm(y_rnn.reshape(t, D_RNN), wpr, name="mm_proj_rnn", out_dtype=BF16, tm=tm, tn=D_MODEL, tk=D_RNN)
    pa = _mm(o_attn.reshape(t, D_MODEL), wpa, name="mm_proj_attn", out_dtype=BF16, tm=tm, tn=D_MODEL, tk=D_MODEL)
    z2 = zr.reshape(t, NZR)
    merged = _merge_fwd(z2, pr, pa)
    mo = _mm(merged, wout, name="mm_out", out_dtype=BF16, tm=tm, tn=D_MODEL, tk=D_MODEL).reshape(bsz, s, D_MODEL)
    h2, x1 = _norm_fwd(x, g_norm2, shift2, scale2, name="norm2_fwd", yprev=mo, gate=gate1)
    zf = _mm(h2.reshape(t, D_MODEL), wup_z, name="mm_up", out_dtype=BF16, tm=tm, tn=1024,
             tk=D_MODEL).reshape(bsz, s, 2 * D_FF)
    act = _ffn_fwd(zf, wfc, b_ffn_conv)
    yf = _mm(act.reshape(t, D_FF), wdown, name="mm_down", out_dtype=BF16, tm=tm, tn=D_MODEL,
             tk=D_FF // 2).reshape(bsz, s, D_MODEL)
    lp, dx2, dyf, dshift_f, dscale_f, dgate2, dg_final = _final(x1, yf, gate2, g_final.reshape(1, -1), shift_f, scale_f, loss_target)
    loss = lax.psum(jnp.sum(lp[:, :, 0, 0]) * (0.5 / D_MODEL), ("x", "y", "c"))

    dyf2 = dyf.reshape(t, D_MODEL)
    act2 = act.reshape(t, D_FF)
    dact = _mm(dyf2, wdown, name="mm_down_dx", tb=True, out_dtype=BF16, tm=tm, tn=1024, tk=D_MODEL).reshape(bsz, s, D_FF)
    g_wdown = _mm(act2, dyf2, name="mm_down_dw", ta=True, out_dtype=BF16, tm=1024, tn=D_MODEL, tk=tm)
    dzf, dsm_ffn, got = _ffn_bwd(dact, zf, wfc, b_ffn_conv, exchange=([g_wdown.reshape(N_DEV, -1, D_MODEL)], [False]))
    p_wdown = got[0]
    dzf2 = dzf.reshape(t, 2 * D_FF)
    dh2 = _mm(dzf2, wup_z, name="mm_up_dx", tb=True, tm=tm, tn=D_MODEL, tk=2048).reshape(bsz, s, D_MODEL)
    g_wup_z = _mm(h2.reshape(t, D_MODEL), dzf2, name="mm_up_dw", ta=True, out_dtype=BF16, tm=D_MODEL, tn=1024, tk=tm)
    dx1, dg_norm2, dshift2, dscale2, dmo, dgate1 = _norm_bwd(x1, dh2, g_norm2, scale2, dx2, name="norm2_bwd", yprev=mo, gate=gate1)
    dmo2 = dmo.reshape(t, D_MODEL)
    dmerged = _mm(dmo2, wout, name="mm_out_dx", tb=True, out_dtype=BF16, tm=tm, tn=D_MODEL, tk=D_MODEL)
    g_wout = _mm(merged, dmo2, name="mm_out_dw", ta=True, out_dtype=BF16, tm=D_MODEL, tn=D_MODEL, tk=tm)
    dz2, dpr, dpa = _merge_bwd(dmerged, z2, pr, pa)
    dy_rnn = _mm(dpr, wpr, name="mm_proj_rnn_dx", tb=True, out_dtype=BF16, tm=tm, tn=D_RNN,
                 tk=D_MODEL).reshape(bsz, s, D_RNN)
    g_wpr = _mm(y_rnn.reshape(t, D_RNN), dpr, name="mm_proj_rnn_dw", ta=True, out_dtype=BF16, tm=D_RNN, tn=D_MODEL, tk=tm)
    do = _mm(dpa, wpa, name="mm_proj_attn_dx", tb=True, out_dtype=BF16, tm=tm, tn=D_MODEL, tk=D_MODEL).reshape(bsz, s, D_MODEL)
    g_wpa = _mm(o_attn.reshape(t, D_MODEL), dpa, name="mm_proj_attn_dw", ta=True, out_dtype=BF16, tm=D_MODEL, tn=D_MODEL, tk=tm)
    dz, dfrow, got = _attn_bwd(zq, do, lse, fcol, frow, dz2.reshape(bsz, s, NZ),
                               exchange=([g_wup_z, g_wout.reshape(N_DEV, -1, D_MODEL)], ["scatter_zf", False]))
    p_wup, p_wout = got
    dflt, db_fgate = _fgate_bwd(dfrow.reshape(bsz, N_HEADS, s), flt, bfg)
    dfl_blk = jnp.pad(dflt.transpose(0, 2, 1).astype(BF16), ((0, 0), (0, 0), (0, MG0 - FL0 - N_HEADS)))
    dz = _fl_write(dfl_blk, dz)
    dz, g_wa, g_wi, dsm_rnn, got = _rnn_bwd(
        zr, h_rnn, dy_rnn, dz, wrc, b_rnn_conv, w_lru_a[0], b_lru_a, w_lru_i[0], b_lru_i, lru_lambda,
        exchange=([g_wpr.reshape(N_DEV, -1, D_MODEL), g_wpa.reshape(N_DEV, -1, D_MODEL)], [False, False]))
    p_wpr, p_wpa = got
    dz2 = dz.reshape(t, NZ)
    small = [dg_norm2, dg_final, dsm_rnn, g_wa, g_wi, db_fgate, dsm_ffn]
    small_sizes = [a.size for a in small]
    n_pad = -sum(small_sizes) % 1024
    packed = jnp.concatenate([a.reshape(-1) for a in small] + [jnp.zeros((n_pad,), F32)]).reshape(-1, 128)
    g_win_zt, got = _mm(dz2, h1f, name="mm_in_dw", ta=True, out_dtype=BF16, tm=1024, tn=D_MODEL, tk=tm,
                        exchange=([packed], [True]))
    flat = got[0].reshape(N_DEV, -1)
    dh1, got = _mm(dz2, win_zt, name="mm_in_dx", tm=tm, tn=D_MODEL, tk=2048,
                   exchange=([_from_z_layout(g_win_zt).reshape(N_DEV, -1, D_MODEL)], [False]))
    p_win = got[0]
    grad_x, dg_norm1, dshift1, dscale1 = _norm_bwd(x, dh1.reshape(bsz, s, D_MODEL), g_norm1, scale1, dx1, name="norm1_bwd")

    dmod = jnp.concatenate([dshift1, dscale1, dgate1, dshift2, dscale2, dgate2], axis=1).reshape(bsz, 6 * D_MODEL)
    dmodf = jnp.concatenate([dshift_f, dscale_f], axis=1).reshape(bsz, 2 * D_MODEL)
    ex_c = _exchange([dmod, dmodf, dg_norm1], [True, True, True], "gather_last_grads")
    dmod_all = ex_c[0].reshape(nb, 6 * D_MODEL)
    dmodf_all = ex_c[1].reshape(nb, 2 * D_MODEL)
    p_g1 = ex_c[2]

    res = {}

    def upd(name, parts, shape2):
        w, m, v = args[name], args["m_" + name], args["v_" + name]
        outs = _adamw(w.reshape(shape2), m.reshape(shape2), v.reshape(shape2), parts, name="adamw_" + name)
        res[name] = [o.reshape(w.shape) for o in outs]

    g_wada = _mm(c_all, lax.dynamic_slice(dmod_all, (0, me * nmod), (nb, nmod)), name="mm_ada_dw", ta=True,
                 silu_a=True, tm=D_MODEL, tn=nmod, tk=nb)
    g_wadaf = _mm(c_all, lax.dynamic_slice(dmodf_all, (0, me * nmodf), (nb, nmodf)), name="mm_adaf_dw", ta=True,
                  silu_a=True, tm=D_MODEL, tn=nmodf, tk=nb)
    upd("w_ada", g_wada[None], (D_MODEL, nmod))
    upd("w_ada_final", g_wadaf[None], (D_MODEL, nmodf))
    upd("b_ada", dmod_all.reshape(nb, 1, -1), (1, 6 * D_MODEL))
    upd("b_ada_final", dmodf_all.reshape(nb, 1, -1), (1, 2 * D_MODEL))
    outs = _adamw(w_in[0].T, m_w_in[0].T, v_w_in[0].T, p_win, name="adamw_w_in")
    res["w_in"] = [o.T[None] for o in outs]
    upd("w_proj_rnn", p_wpr, w_proj_rnn.shape[1:])
    upd("w_proj_attn", p_wpa, w_proj_attn.shape[1:])
    upd("w_out", p_wout, w_out.shape[1:])
    upd("w_ffn_up", p_wup, w_ffn_up.shape[1:])
    upd("w_ffn_down", p_wdown, w_ffn_down.shape[1:])

    parts, off = [], 0
    for a, n in zip(small, small_sizes):
        parts.append(flat[:, off:off + n].reshape((N_DEV,) + a.shape))
        off += n
    p_g2, p_gf, p_rnn, p_wa, p_wi, p_bfg, p_ffn = parts
    upd("g_norm1", p_g1, (1, D_MODEL))
    upd("g_norm2", p_g2, (1, D_MODEL))
    upd("g_final", p_gf, (1, D_MODEL))
    nrc = w_rnn_conv.shape[2]
    upd("w_rnn_conv", lax.dynamic_slice(p_rnn, (0, 0, me * nrc), (N_DEV, 4, nrc)), (4, nrc))
    upd("b_rnn_conv", p_rnn[:, 4:5], (1, D_RNN))
    upd("b_lru_a", p_rnn[:, 5:6], (1, D_RNN))
    upd("b_lru_i", p_rnn[:, 6:7], (1, D_RNN))
    upd("lru_lambda", p_rnn[:, 7:8], (1, D_RNN))
    upd("w_lru_a", p_wa.reshape(N_DEV, D_RNN, RNN_BLK), (D_RNN, RNN_BLK))
    upd("w_lru_i", p_wi.reshape(N_DEV, D_RNN, RNN_BLK), (D_RNN, RNN_BLK))
    upd("b_fgate", p_bfg.reshape(N_DEV, 1, N_HEADS), (1, N_HEADS))
    nfc = w_ffn_conv.shape[2]
    upd("w_ffn_conv", lax.dynamic_slice(p_ffn, (0, 0, me * nfc), (N_DEV, 3, nfc)), (3, nfc))
    upd("b_ffn_conv", p_ffn[:, 3:4], (1, D_FF))

    names = ['w_ada', 'b_ada', 'g_norm1', 'w_in', 'w_rnn_conv', 'b_rnn_conv', 'w_lru_a', 'b_lru_a', 'w_lru_i',
             'b_lru_i', 'lru_lambda', 'b_fgate', 'w_proj_rnn', 'w_proj_attn', 'w_out', 'g_norm2', 'w_ffn_up',
             'w_ffn_conv', 'b_ffn_conv', 'w_ffn_down', 'w_ada_final', 'b_ada_final', 'g_final']
    return (loss, grad_x, *[res[n][0] for n in names], *[res[n][1] for n in names],
            *[res[n][2] for n in names], *[res[n][3] for n in names])
```

```python
import jax
import jax.numpy as jnp
from jax import lax
from jax.experimental import pallas as pl
from jax.experimental.pallas import tpu as pltpu

F32, BF16 = jnp.float32, jnp.bfloat16
D_MODEL, D_RNN, N_BLK, RNN_BLK = 1024, 1280, 10, 128
N_HEADS, HEAD_DIM, D_FF = 8, 128, 3072
N_DEV = 8
NZ = 8192
QKV0, XG0, FL0, MG0 = 0, 3072, 5632, 6144
NZR = NZ - XG0
XG0R, FL0R, MG0R = 0, FL0 - XG0, MG0 - XG0
RMS_EPS = 1e-6
LRU_C = 8.0
ADAM_LR, ADAM_B1, ADAM_B2, ADAM_EPS, ADAM_WD, ADAM_STEP = 0.001, 0.9, 0.999, 1e-08, 0.01, 10
VMEM_LIMIT = 56 << 20
SCAN_CHUNK = 256
ATTN_TQ = 256
CUM_BLK = 512
MESH = pl.DeviceIdType.MESH
ANY = pl.BlockSpec(memory_space=pl.ANY)


def _call(body, **kw):
    return pl.pallas_call(body, **kw)


def _params():
    return pltpu.CompilerParams(vmem_limit_bytes=VMEM_LIMIT)


ZF_PIECES = 2 * D_FF // 128 // N_DEV


def _zf_cols(owner, p):
    ob = owner * ZF_PIECES + p
    blk = 2 * (ob % (D_FF // 128)) + ob // (D_FF // 128)
    return pl.ds(pl.multiple_of(blk * 128, 128), 128)


def _n_pieces(mode):
    return ZF_PIECES if mode in ("gather_zf", "scatter_zf") else 1


def _exchanged_shape(a, mode):
    if mode is True:
        return (N_DEV,) + a.shape
    if mode is False:
        return a.shape
    if mode == "gather_zf":
        return (a.shape[0], N_DEV * a.shape[1])
    return (N_DEV, a.shape[0], a.shape[1] // N_DEV)


def _direct_copies(ins, outs, sems, modes):
    send_sems, recv_sems, loc_sems = sems
    x, y, c = lax.axis_index("x"), lax.axis_index("y"), lax.axis_index("c")
    me = 4 * x + 2 * y + c

    def ends(a, to, p):
        mode = modes[a]
        if mode is True:
            return ins[a], outs[a].at[me]
        if mode is False:
            return ins[a].at[to], outs[a].at[me]
        mine = slice(p * 128, (p + 1) * 128)
        if mode == "gather_zf":
            return ins[a].at[:, mine], outs[a].at[:, _zf_cols(me, p)]
        return ins[a].at[:, _zf_cols(to, p)], outs[a].at[me, :, mine]

    local, remote = [], []
    nloc = 0
    for a in range(len(ins)):
        for p in range(_n_pieces(modes[a])):
            src, dst = ends(a, me, p)
            local.append(pltpu.make_async_copy(src, dst, loc_sems.at[nloc]))
            nloc += 1
    for k in range(1, N_DEV):
        px = 1 - x if k & 4 else x
        py = 1 - y if k & 2 else y
        pc = 1 - c if k & 1 else c
        peer = 4 * px + 2 * py + pc
        nrem = (k - 1) * nloc
        for a in range(len(ins)):
            for p in range(_n_pieces(modes[a])):
                src, dst = ends(a, peer, p)
                remote.append(pltpu.make_async_remote_copy(
                    src_ref=src, dst_ref=dst, send_sem=send_sems.at[nrem], recv_sem=recv_sems.at[nrem],
                    device_id=(px, py, pc), device_id_type=MESH))
                nrem += 1
    return local, remote


def _with_exchange(body, args, arrs, bcast, *, name, grid, in_specs, out_specs, out_shape, scratch_shapes=(), **kw):
    n_in, n_out, n_sc, nc = len(args), len(out_shape), len(scratch_shapes), len(arrs)

    def wrapped(*refs):
        core_in, cin = refs[:n_in], refs[n_in:n_in + nc]
        core_out = refs[n_in + nc:n_in + nc + n_out]
        cout = refs[n_in + nc + n_out:n_in + 2 * nc + n_out]
        rest = refs[n_in + 2 * nc + n_out:]
        core_sc, sems = rest[:n_sc], rest[n_sc:]

        def start():
            local, remote = _direct_copies(cin, cout, sems, bcast)
            for cp in local + remote:
                cp.start()

        def wait():
            local, remote = _direct_copies(cin, cout, sems, bcast)
            for cp in remote + local:
                cp.wait()

        if not grid:
            start()
            body(*core_in, *core_out, *core_sc)
            wait()
            return
        first, last = None, None
        for d, g in enumerate(grid):
            i = pl.program_id(d)
            first = (i == 0) if first is None else jnp.logical_and(first, i == 0)
            last = (i == g - 1) if last is None else jnp.logical_and(last, i == g - 1)
        pl.when(first)(start)
        body(*core_in, *core_out, *core_sc)
        pl.when(last)(wait)

    ex_shape = [jax.ShapeDtypeStruct(_exchanged_shape(a, mode), a.dtype) for a, mode in zip(arrs, bcast)]
    ncopies = sum(_n_pieces(mode) for mode in bcast)
    sems = [pltpu.SemaphoreType.DMA((ncopies * (N_DEV - 1),)), pltpu.SemaphoreType.DMA((ncopies * (N_DEV - 1),)),
            pltpu.SemaphoreType.DMA((ncopies,))]
    res = _call(wrapped, name=name, grid=grid, in_specs=list(in_specs) + [ANY] * nc,
                out_specs=list(out_specs) + [ANY] * nc, out_shape=list(out_shape) + ex_shape,
                scratch_shapes=list(scratch_shapes) + sems, **kw)(*args, *arrs)
    return list(res[:n_out]), list(res[n_out:])


def _run(body, args, exchange, **kw):
    if exchange is None:
        return list(_call(body, **kw)(*args))
    outs, got = _with_exchange(body, args, *exchange, **kw)
    return outs + [got]


def _exchange(arrs, bcast, name):
    return _with_exchange(lambda: None, [], arrs, bcast, name=name, grid=(), in_specs=[], out_specs=[], out_shape=[],
                          compiler_params=pltpu.CompilerParams(has_side_effects=True))[1]


def _gather_two_level(arrs, name):
    n = len(arrs)

    def body(*refs):
        ins, outs = refs[:n], refs[n:2 * n]
        send_sems, recv_sems, loc_sems = refs[2 * n:]
        x, y, c = lax.axis_index("x"), lax.axis_index("y"), lax.axis_index("c")
        me, sibling = (x, y, c), (x, y, 1 - c)
        chips = [(1 - x, y), (x, 1 - y), (1 - x, 1 - y)]

        def slot(a, p):
            return outs[a].at[4 * p[0] + 2 * p[1] + p[2]]

        def copy(a, k, block, to, src=None):
            return pltpu.make_async_remote_copy(
                src_ref=slot(a, block) if src is None else src, dst_ref=slot(a, block),
                send_sem=send_sems.at[a * 7 + k], recv_sem=recv_sems.at[a * 7 + k],
                device_id=to, device_id_type=MESH)

        mine = [pltpu.make_async_copy(ins[a], slot(a, me), loc_sems.at[a]) for a in range(n)]
        first = []
        for a in range(n):
            first.append(copy(a, 0, me, sibling, src=ins[a]))
            first += [copy(a, 1 + j, me, (*chip, c), src=ins[a]) for j, chip in enumerate(chips)]
        for cp in mine + first:
            cp.start()
        passed = []
        for j, chip in enumerate(chips):
            for a in range(n):
                copy(a, 1 + j, (*chip, c), me).wait_recv()
                fwd = copy(a, 4 + j, (*chip, c), sibling)
                fwd.start()
                passed.append(fwd)
        for a in range(n):
            copy(a, 0, sibling, me).wait_recv()
        for j, chip in enumerate(chips):
            for a in range(n):
                copy(a, 4 + j, (*chip, 1 - c), me).wait_recv()
        for cp in first + passed:
            cp.wait_send()
        for cp in mine:
            cp.wait()

    return _call(
        body, name=name, in_specs=[ANY] * n, out_specs=[ANY] * n,
        out_shape=[jax.ShapeDtypeStruct((N_DEV,) + a.shape, a.dtype) for a in arrs],
        scratch_shapes=[pltpu.SemaphoreType.DMA((n * 7,)), pltpu.SemaphoreType.DMA((n * 7,)),
                        pltpu.SemaphoreType.DMA((n,))],
        compiler_params=pltpu.CompilerParams(has_side_effects=True),
    )(*arrs)


def _sigmoid(x):
    return 1.0 / (1.0 + jnp.exp(-x))


def _log1p(e):
    u = 1.0 + e
    d = u - 1.0
    return jnp.where(d == 0.0, e, jnp.log(u) * (e / jnp.where(d == 0.0, 1.0, d)))


def _softplus(x):
    return jnp.maximum(x, 0.0) + _log1p(jnp.exp(-jnp.abs(x)))


_GELU_C = 0.7978845608028654


def _gelu(x):
    t = jnp.tanh(_GELU_C * (x + 0.044715 * x * x * x))
    return 0.5 * x * (1.0 + t)


def _gelu_and_grad(x):
    t = jnp.tanh(_GELU_C * (x + 0.044715 * x * x * x))
    g = 0.5 * x * (1.0 + t)
    dg = 0.5 * (1.0 + t) + 0.5 * x * (1.0 - t * t) * _GELU_C * (1.0 + 3.0 * 0.044715 * x * x)
    return g, dg


def _acc(ref, idx, val, first):
    @pl.when(first)
    def _():
        ref[idx] = val

    @pl.when(jnp.logical_not(first))
    def _():
        ref[idx] = ref[idx] + val


def _split3(x):
    hi = x.astype(BF16)
    r1 = x - hi.astype(F32)
    mid = r1.astype(BF16)
    lo = (r1 - mid.astype(F32)).astype(BF16)
    return hi, mid, lo


def _mm(a, b, *, name, ta=False, tb=False, out_dtype=F32, tm, tn, tk, bias=None, silu_a=False, exchange=None,
        b_cols=None):
    m, k = (a.shape[1], a.shape[0]) if ta else a.shape
    col0, n = b_cols if b_cols is not None else (0, b.shape[0] if tb else b.shape[1])
    assert m % tm == 0 and n % tn == 0 and k % tk == 0 and col0 % tn == 0, (name, m, n, k)
    jo = col0 // tn
    nk = k // tk
    dn = (((0 if ta else 1,), (1 if tb else 0,)), ((), ()))
    use_acc = nk > 1 and out_dtype != F32

    def body(*refs):
        a_ref, b_ref = refs[0], refs[1]
        bias_ref = refs[2] if bias is not None else None
        o_ref = refs[3 if bias is not None else 2]
        av = a_ref[...]
        if silu_a:
            av = av * _sigmoid(av)
        p = lax.dot_general(av.astype(BF16), b_ref[...].astype(BF16), dn, preferred_element_type=F32)
        if bias is not None:
            p = p + bias_ref[...]
        if nk == 1:
            o_ref[...] = p.astype(out_dtype)
            return
        kk = pl.program_id(2)
        acc = refs[-1] if use_acc else o_ref

        @pl.when(kk == 0)
        def _():
            acc[...] = p

        @pl.when(kk > 0)
        def _():
            acc[...] = acc[...] + p

        if use_acc:
            @pl.when(kk == nk - 1)
            def _():
                o_ref[...] = acc[...].astype(out_dtype)

    a_spec = pl.BlockSpec((tk, tm), lambda i, j, q: (q, i)) if ta else pl.BlockSpec((tm, tk), lambda i, j, q: (i, q))
    b_spec = (pl.BlockSpec((tn, tk), lambda i, j, q: (j + jo, q)) if tb
              else pl.BlockSpec((tk, tn), lambda i, j, q: (q, j + jo)))
    in_specs, args = [a_spec, b_spec], [a, b]
    if bias is not None:
        in_specs.append(pl.BlockSpec((1, tn), lambda i, j, q: (0, j)))
        args.append(bias)
    kw = dict(name=name, grid=(m // tm, n // tn, nk), in_specs=in_specs,
              out_specs=[pl.BlockSpec((tm, tn), lambda i, j, q: (i, j))],
              out_shape=[jax.ShapeDtypeStruct((m, n), out_dtype)],
              scratch_shapes=[pltpu.VMEM((tm, tn), F32)] if use_acc else [],
              compiler_params=_params())
    if exchange is None:
        return _call(body, **kw)(*args)[0]
    outs, got = _with_exchange(body, args, *exchange, **kw)
    return outs[0], got


def _tok_spec(ts):
    return pl.BlockSpec((1, ts, D_MODEL), lambda b, s: (b, s, 0))


def _seq_spec():
    return pl.BlockSpec((1, 1, D_MODEL), lambda b, s: (b, 0, 0))


def _vec_spec():
    return pl.BlockSpec((1, D_MODEL), lambda b, s: (0, 0))


def _norm_fwd(x, g, shift, scale, *, name, yprev=None, gate=None, ts=512):
    bsz, s, _ = x.shape
    res = yprev is not None

    def body(*refs):
        if res:
            x_ref, g_ref, sh_ref, sc_ref, y_ref, gt_ref, h_ref, xn_ref = refs
            xv = x_ref[0] + gt_ref[0] * y_ref[0]
            xn_ref[0] = xv
        else:
            x_ref, g_ref, sh_ref, sc_ref, h_ref = refs
            xv = x_ref[0]
        rstd = lax.rsqrt(jnp.mean(xv * xv, axis=-1, keepdims=True) + RMS_EPS)
        nv = xv * rstd * g_ref[...]
        h_ref[0] = (nv * (1.0 + sc_ref[0]) + sh_ref[0]).astype(BF16)

    in_specs = [_tok_spec(ts), _vec_spec(), _seq_spec(), _seq_spec()]
    args = [x, g, shift, scale]
    out_specs = [_tok_spec(ts)]
    out_shape = [jax.ShapeDtypeStruct(x.shape, BF16)]
    if res:
        in_specs += [_tok_spec(ts), _seq_spec()]
        args += [yprev, gate]
        out_specs.append(_tok_spec(ts))
        out_shape.append(jax.ShapeDtypeStruct(x.shape, F32))
    return _call(body, name=name, grid=(bsz, s // ts), in_specs=in_specs, out_specs=out_specs,
                 out_shape=out_shape, compiler_params=_params())(*args)


def _final(x1, yf, gate2, g, shift, scale, target, *, ts=512):
    bsz, s, _ = x1.shape
    ns = s // ts

    def body(x_ref, y_ref, gt_ref, g_ref, sh_ref, sc_ref, t_ref,
             lp_ref, dx_ref, dyf_ref, dsh_ref, dsc_ref, dgt_ref, dg_ref):
        b, si = pl.program_id(0), pl.program_id(1)
        yv, gt, gv = y_ref[0], gt_ref[0], g_ref[...]
        xv = x_ref[0] + gt * yv
        rstd = lax.rsqrt(jnp.mean(xv * xv, axis=-1, keepdims=True) + RMS_EPS)
        xhat = xv * rstd
        nv = xhat * gv
        sc1 = 1.0 + sc_ref[0]
        err = nv * sc1 + sh_ref[0] - t_ref[0]
        lp_ref[...] = jnp.zeros((1, 1, 8, 128), F32) + jnp.sum(err * err)
        dy = err * (1.0 / D_MODEL)
        dn = dy * sc1
        dxh = dn * gv
        dx = rstd * (dxh - xhat * jnp.mean(dxh * xhat, axis=-1, keepdims=True))
        dx_ref[0] = dx
        dyf_ref[0] = (dx * gt).astype(BF16)
        _acc(dsh_ref, 0, jnp.sum(dy, axis=0, keepdims=True), si == 0)
        _acc(dsc_ref, 0, jnp.sum(dy * nv, axis=0, keepdims=True), si == 0)
        _acc(dgt_ref, 0, jnp.sum(dx * yv, axis=0, keepdims=True), si == 0)
        _acc(dg_ref, slice(None), jnp.sum(dn * xhat, axis=0, keepdims=True), jnp.logical_and(b == 0, si == 0))

    seq_shape = jax.ShapeDtypeStruct((bsz, 1, D_MODEL), F32)
    return _call(
        body, name="final_loss", grid=(bsz, ns),
        in_specs=[_tok_spec(ts), _tok_spec(ts), _seq_spec(), _vec_spec(), _seq_spec(), _seq_spec(), _tok_spec(ts)],
        out_specs=[pl.BlockSpec((1, 1, 8, 128), lambda b, s: (b, s, 0, 0)), _tok_spec(ts), _tok_spec(ts),
                   _seq_spec(), _seq_spec(), _seq_spec(), _vec_spec()],
        out_shape=[jax.ShapeDtypeStruct((bsz, ns, 8, 128), F32), jax.ShapeDtypeStruct(x1.shape, F32),
                   jax.ShapeDtypeStruct(x1.shape, BF16), seq_shape, seq_shape, seq_shape,
                   jax.ShapeDtypeStruct((1, D_MODEL), F32)],
        compiler_params=_params(),
    )(x1, yf, gate2, g, shift, scale, target)


def _norm_bwd(x, dh, g, scale, dres, *, name, yprev=None, gate=None, ts=512):
    bsz, s, _ = x.shape
    res = yprev is not None

    def body(*refs):
        if res:
            (x_ref, dh_ref, g_ref, sc_ref, dr_ref, y_ref, gt_ref,
             dx_ref, dg_ref, dsh_ref, dsc_ref, dy_ref, dgt_ref) = refs
        else:
            x_ref, dh_ref, g_ref, sc_ref, dr_ref, dx_ref, dg_ref, dsh_ref, dsc_ref = refs
        b, si = pl.program_id(0), pl.program_id(1)
        xv, dhv, gv = x_ref[0], dh_ref[0], g_ref[...]
        rstd = lax.rsqrt(jnp.mean(xv * xv, axis=-1, keepdims=True) + RMS_EPS)
        xhat = xv * rstd
        dn = dhv * (1.0 + sc_ref[0])
        dxh = dn * gv
        dx = dr_ref[0] + rstd * (dxh - xhat * jnp.mean(dxh * xhat, axis=-1, keepdims=True))
        dx_ref[0] = dx
        _acc(dsh_ref, 0, jnp.sum(dhv, axis=0, keepdims=True), si == 0)
        _acc(dsc_ref, 0, jnp.sum(dhv * (xhat * gv), axis=0, keepdims=True), si == 0)
        _acc(dg_ref, slice(None), jnp.sum(dn * xhat, axis=0, keepdims=True), jnp.logical_and(b == 0, si == 0))
        if res:
            dy_ref[0] = (dx * gt_ref[0]).astype(BF16)
            _acc(dgt_ref, 0, jnp.sum(dx * y_ref[0], axis=0, keepdims=True), si == 0)

    seq_shape = jax.ShapeDtypeStruct((bsz, 1, D_MODEL), F32)
    in_specs = [_tok_spec(ts), _tok_spec(ts), _vec_spec(), _seq_spec(), _tok_spec(ts)]
    args = [x, dh, g, scale, dres]
    out_specs = [_tok_spec(ts), _vec_spec(), _seq_spec(), _seq_spec()]
    out_shape = [jax.ShapeDtypeStruct(x.shape, F32), jax.ShapeDtypeStruct((1, D_MODEL), F32), seq_shape, seq_shape]
    if res:
        in_specs += [_tok_spec(ts), _seq_spec()]
        args += [yprev, gate]
        out_specs += [_tok_spec(ts), _seq_spec()]
        out_shape += [jax.ShapeDtypeStruct(x.shape, BF16), seq_shape]
    return _call(body, name=name, grid=(bsz, s // ts), in_specs=in_specs, out_specs=out_specs,
                 out_shape=out_shape, compiler_params=_params())(*args)


def _lru_gates(xr, wa, ba, wi, bi, sp):
    xb = xr.astype(BF16)
    r = _sigmoid(jnp.dot(xb, wa, preferred_element_type=F32) + ba)
    i = _sigmoid(jnp.dot(xb, wi, preferred_element_type=F32) + bi)
    la = -LRU_C * r * sp
    a = jnp.exp(la)
    mult = jnp.sqrt(-jnp.tanh(la) * (a * a + 1.0))
    return r, i, a, mult


def _scan_forward(a, u, h_in, row):
    n = a.shape[0]
    r8 = row & 7
    for k in (1, 2, 4):
        m = r8 >= k
        u = jnp.where(m, a * pltpu.roll(u, k, 0) + u, u)
        a = jnp.where(m, a * pltpu.roll(a, k, 0), a)
    carry = jnp.broadcast_to(h_in, (8, a.shape[1]))
    out = []
    for j in range(n // 8):
        hj = u[8 * j:8 * j + 8] + a[8 * j:8 * j + 8] * carry
        out.append(hj)
        carry = jnp.broadcast_to(hj[7:8], hj.shape)
    return jnp.concatenate(out, axis=0)


def _scan_backward(a_up, g, g_in, row):
    n = g.shape[0]
    r8 = row & 7
    for k in (1, 2, 4):
        m = r8 < 8 - k
        g = jnp.where(m, a_up * pltpu.roll(g, n - k, 0) + g, g)
        a_up = jnp.where(m, a_up * pltpu.roll(a_up, n - k, 0), a_up)
    carry = jnp.broadcast_to(g_in, (8, g.shape[1]))
    out = [None] * (n // 8)
    for j in reversed(range(n // 8)):
        gj = g[8 * j:8 * j + 8] + a_up[8 * j:8 * j + 8] * carry
        out[j] = gj
        carry = jnp.broadcast_to(gj[0:1], gj.shape)
    return jnp.concatenate(out, axis=0)


def _rnn_fwd(z, wc, bc, wa, ba, wi, bi, lam, exchange=None):
    bsz, s, _ = z.shape
    ch = min(SCAN_CHUNK, s)
    nch = s // ch

    def body(z_ref, wc_ref, bc_ref, wa_ref, ba_ref, wi_ref, bi_ref, lam_ref, h_ref, y_ref, g_ref):
        wcv, bcv = wc_ref[...], bc_ref[...]
        wav, wiv = wa_ref[0].astype(BF16), wi_ref[0].astype(BF16)
        bav, biv = ba_ref[...], bi_ref[...]
        sp = _softplus(-lam_ref[...])
        row = lax.broadcasted_iota(jnp.int32, (ch, RNN_BLK), 0)

        def chunk(ci, carry):
            prev8, hc = carry
            st = pl.multiple_of(ci * ch, ch)
            blk = z_ref[0, pl.ds(st, ch), :]
            xp, gr = blk[:, :RNN_BLK], blk[:, RNN_BLK:]
            xe = jnp.concatenate([prev8, xp], axis=0)
            xr = bcv + wcv[3:4] * xp
            for j in range(1, 4):
                xr = xr + wcv[3 - j:4 - j] * pltpu.roll(xe, j, 0)[8:]
            r, i, a, mult = _lru_gates(xr, wav, bav, wiv, biv, sp)
            g_ref[0, pl.ds(st, ch), :] = jnp.concatenate([r, i, a, mult], axis=1)
            hh = _scan_forward(a, mult * (i * xr), hc, row)
            h_ref[0, pl.ds(st, ch), :] = hh
            y_ref[0, pl.ds(st, ch), :] = (_gelu(gr) * hh).astype(BF16)
            return xp[ch - 8:], hh[ch - 1:]

        lax.fori_loop(0, nch, chunk, (jnp.zeros((8, RNN_BLK), F32), jnp.zeros((1, RNN_BLK), F32)))

    vec = pl.BlockSpec((1, RNN_BLK), lambda b, n: (0, n))
    mat = pl.BlockSpec((1, RNN_BLK, RNN_BLK), lambda b, n: (n, 0, 0))
    out = pl.BlockSpec((1, s, RNN_BLK), lambda b, n: (b, 0, n))
    return _run(
        body, [z, wc, bc, wa, ba, wi, bi, lam], exchange, name="rnn_fwd", grid=(bsz, N_BLK),
        in_specs=[pl.BlockSpec((1, s, 2 * RNN_BLK), lambda b, n: (b, 0, XG0R // (2 * RNN_BLK) + n)),
                  pl.BlockSpec((4, RNN_BLK), lambda b, n: (0, n)), vec, mat, vec, mat, vec, vec],
        out_specs=[out, out, pl.BlockSpec((1, s, 4 * RNN_BLK), lambda b, n: (b, 0, n))],
        out_shape=[jax.ShapeDtypeStruct((bsz, s, D_RNN), F32), jax.ShapeDtypeStruct((bsz, s, D_RNN), BF16),
                   jax.ShapeDtypeStruct((bsz, s, 4 * D_RNN), F32)],
        compiler_params=_params())


def _rnn_bwd(z, h, gates, dy, dz, wc, bc, wa, wi, lam, exchange=None):
    bsz, s, _ = z.shape
    ch = min(SCAN_CHUNK, s)
    nch = s // ch

    def body(z_ref, h_ref, g_ref, dy_ref, dz_in, wc_ref, bc_ref, wa_ref, wi_ref, lam_ref,
             dz_ref, dwa_ref, dwi_ref, dsm_ref):
        del dz_in
        b = pl.program_id(1)
        wcv, bcv = wc_ref[...], bc_ref[...]
        wav, wiv = wa_ref[0].astype(BF16), wi_ref[0].astype(BF16)
        lamv = lam_ref[...]
        sp = _softplus(-lamv)
        row = lax.broadcasted_iota(jnp.int32, (ch, RNN_BLK), 0)

        @pl.when(b == 0)
        def _():
            dwa_ref[...] = jnp.zeros_like(dwa_ref)
            dwi_ref[...] = jnp.zeros_like(dwi_ref)
            dsm_ref[...] = jnp.zeros_like(dsm_ref)

        def chunk(k, carry):
            a_next, g_next, dxr_next8 = carry
            ci = nch - 1 - k
            st = pl.multiple_of(ci * ch, ch)
            pst = pl.multiple_of(jnp.maximum(st - 8, 0), 8)
            has_prev = (ci > 0).astype(F32)
            blk = z_ref[0, pl.ds(st, ch), :]
            xp, gr = blk[:, :RNN_BLK], blk[:, RNN_BLK:]
            xprev8 = z_ref[0, pl.ds(pst, 8), :][:, :RNN_BLK] * has_prev
            hprev8 = h_ref[0, pl.ds(pst, 8), :] * has_prev
            xe = jnp.concatenate([xprev8, xp], axis=0)
            xs = [xp] + [pltpu.roll(xe, j, 0)[8:] for j in range(1, 4)]
            xr = bcv
            for kk in range(4):
                xr = xr + wcv[kk:kk + 1] * xs[3 - kk]
            gs = g_ref[0, pl.ds(st, ch), :]
            r, i, a, mult = [gs[:, q * RNN_BLK:(q + 1) * RNN_BLK] for q in range(4)]
            hh = h_ref[0, pl.ds(st, ch), :]
            hm1 = pltpu.roll(jnp.concatenate([hprev8, hh], axis=0), 1, 0)[8:]
            dyv = dy_ref[0, pl.ds(st, ch), :]
            ge, dge = _gelu_and_grad(gr)
            dgr = dyv * hh * dge
            gg = _scan_backward(jnp.where(row == ch - 1, a_next, pltpu.roll(a, ch - 1, 0)), dyv * ge, g_next, row)
            da = gg * hm1
            ix = i * xr
            dmult = gg * ix
            di = gg * mult * xr
            dxr = gg * mult * i
            dla = da * a - dmult * (a * a) / mult
            dr = dla * (-LRU_C * sp)
            dsp = jnp.sum(dla * (-LRU_C * r), axis=0, keepdims=True)
            dpa = dr * r * (1.0 - r)
            dpi = di * i * (1.0 - i)
            dpab, dpib, xrb = dpa.astype(BF16), dpi.astype(BF16), xr.astype(BF16)
            nt = (((1,), (1,)), ((), ()))
            tn = (((0,), (0,)), ((), ()))
            dxr = dxr + lax.dot_general(dpab, wav, nt, preferred_element_type=F32)
            dxr = dxr + lax.dot_general(dpib, wiv, nt, preferred_element_type=F32)
            dwa_ref[0] = dwa_ref[0] + lax.dot_general(xrb, dpab, tn, preferred_element_type=F32)
            dwi_ref[0] = dwi_ref[0] + lax.dot_general(xrb, dpib, tn, preferred_element_type=F32)
            rows = [jnp.sum(dxr * xs[3 - q], axis=0, keepdims=True) for q in range(4)]
            rows += [jnp.sum(dxr, axis=0, keepdims=True), jnp.sum(dpa, axis=0, keepdims=True),
                     jnp.sum(dpi, axis=0, keepdims=True), dsp * (-_sigmoid(-lamv))]
            dsm_ref[...] = dsm_ref[...] + jnp.concatenate(rows, axis=0)
            dxe = jnp.concatenate([dxr, dxr_next8], axis=0)
            dxp = wcv[3:4] * dxr
            for j in range(1, 4):
                dxp = dxp + wcv[3 - j:4 - j] * pltpu.roll(dxe, ch + 8 - j, 0)[:ch]
            dz_ref[0, pl.ds(st, ch), :] = jnp.concatenate([dxp, dgr], axis=1).astype(BF16)
            return a[0:1], gg[0:1], dxr[0:8]

        lax.fori_loop(0, nch, chunk, (jnp.zeros((1, RNN_BLK), F32), jnp.zeros((1, RNN_BLK), F32),
                                      jnp.zeros((8, RNN_BLK), F32)))

    vec = pl.BlockSpec((1, RNN_BLK), lambda n, b: (0, n))
    mat = pl.BlockSpec((1, RNN_BLK, RNN_BLK), lambda n, b: (n, 0, 0))
    tok = pl.BlockSpec((1, s, RNN_BLK), lambda n, b: (b, 0, n))
    zblk = pl.BlockSpec((1, s, 2 * RNN_BLK), lambda n, b: (b, 0, XG0R // (2 * RNN_BLK) + n))
    dzblk = pl.BlockSpec((1, s, 2 * RNN_BLK), lambda n, b: (b, 0, XG0 // (2 * RNN_BLK) + n))
    return _run(
        body, [z, h, gates, dy, dz, wc, bc, wa, wi, lam], exchange, name="rnn_bwd", grid=(N_BLK, bsz),
        in_specs=[zblk, tok, pl.BlockSpec((1, s, 4 * RNN_BLK), lambda n, b: (b, 0, n)), tok, ANY,
                  pl.BlockSpec((4, RNN_BLK), lambda n, b: (0, n)), vec, mat, mat, vec],
        out_specs=[dzblk, mat, mat, pl.BlockSpec((8, RNN_BLK), lambda n, b: (0, n))],
        out_shape=[jax.ShapeDtypeStruct(dz.shape, dz.dtype), jax.ShapeDtypeStruct((N_BLK, RNN_BLK, RNN_BLK), F32),
                   jax.ShapeDtypeStruct((N_BLK, RNN_BLK, RNN_BLK), F32), jax.ShapeDtypeStruct((8, D_RNN), F32)],
        input_output_aliases={4: 0},
        compiler_params=_params())


def _tri(n, upper):
    r = lax.broadcasted_iota(jnp.int32, (n, n), 0)
    c = lax.broadcasted_iota(jnp.int32, (n, n), 1)
    return (r <= c if upper else r >= c).astype(BF16)


def _exact_dot(x, t):
    hi, mid, lo = _split3(x)
    return (jnp.dot(hi, t, preferred_element_type=F32) + jnp.dot(mid, t, preferred_element_type=F32)
            + jnp.dot(lo, t, preferred_element_type=F32))


def _fgate_fwd(flt, bfg):
    bsz, nh, s = flt.shape
    blk = min(CUM_BLK, s)

    def body(fl_ref, b_ref, f_ref):
        tri = _tri(blk, True)
        carry = jnp.zeros((nh, 1), F32)
        for q in range(s // blk):
            xq = fl_ref[0, :, q * blk:(q + 1) * blk] + b_ref[...]
            fq = _exact_dot(-_softplus(-xq), tri) + carry
            f_ref[0, :, q * blk:(q + 1) * blk] = fq
            carry = fq[:, blk - 1:blk]

    spec = pl.BlockSpec((1, nh, s), lambda b: (b, 0, 0))
    return _call(body, name="fgate_fwd", grid=(bsz,), in_specs=[spec, pl.BlockSpec((nh, 1), lambda b: (0, 0))],
                 out_specs=spec, out_shape=jax.ShapeDtypeStruct(flt.shape, F32), compiler_params=_params())(flt, bfg)


def _fgate_bwd(dft, flt, bfg):
    bsz, nh, s = flt.shape
    blk = min(CUM_BLK, s)

    def body(df_ref, fl_ref, b_ref, dfl_ref, db_ref):
        b = pl.program_id(0)
        tri = _tri(blk, False)
        carry = jnp.zeros((nh, 1), F32)
        tot = jnp.zeros((nh, 1), F32)
        for q in reversed(range(s // blk)):
            sl = slice(q * blk, (q + 1) * blk)
            dlq = _exact_dot(df_ref[0, :, sl], tri) + carry
            carry = dlq[:, 0:1]
            dflq = dlq * _sigmoid(-(fl_ref[0, :, sl] + b_ref[...]))
            dfl_ref[0, :, sl] = dflq
            tot = tot + jnp.sum(dflq, axis=1, keepdims=True)
        _acc(db_ref, slice(None), tot, b == 0)

    spec = pl.BlockSpec((1, nh, s), lambda b: (b, 0, 0))
    small = pl.BlockSpec((nh, 1), lambda b: (0, 0))
    return _call(body, name="fgate_bwd", grid=(bsz,), in_specs=[spec, spec, small], out_specs=[spec, small],
                 out_shape=[jax.ShapeDtypeStruct(flt.shape, F32), jax.ShapeDtypeStruct((nh, 1), F32)],
                 compiler_params=_params())(dft, flt, bfg)


_NT = (((1,), (1,)), ((), ()))
_TN = (((0,), (0,)), ((), ()))


def _attn_block_scores(q_b, z_ref, frow_ref, k0, tq):
    kk = z_ref[0, k0:k0 + tq, HEAD_DIM:2 * HEAD_DIM]
    return lax.dot_general(q_b, kk, _NT, preferred_element_type=F32) - frow_ref[0, 0, :, k0:k0 + tq]


def _causal(tq):
    return lax.broadcasted_iota(jnp.int32, (tq, tq), 0) >= lax.broadcasted_iota(jnp.int32, (tq, tq), 1)


def _attn_fwd(z, fcol, frow, exchange=None):
    bsz, s, _ = z.shape
    tq = min(ATTN_TQ, s)
    scale = HEAD_DIM ** -0.5

    def body(z_ref, fcol_ref, frow_ref, o_ref, lse_ref):
        for qb in range(s // tq):
            q0 = qb * tq
            q_b = (z_ref[0, q0:q0 + tq, :HEAD_DIM].astype(F32) * scale).astype(BF16)
            m = jnp.full((tq, 1), -jnp.inf, F32)
            l = jnp.zeros((tq, 1), F32)
            acc = jnp.zeros((tq, HEAD_DIM), F32)
            for kb in range(qb + 1):
                k0 = kb * tq
                sc = _attn_block_scores(q_b, z_ref, frow_ref, k0, tq)
                if kb == qb:
                    sc = jnp.where(_causal(tq), sc, -jnp.inf)
                m_new = jnp.maximum(m, jnp.max(sc, axis=-1, keepdims=True))
                alpha = jnp.exp(m - m_new)
                p = jnp.exp(sc - m_new)
                l = alpha * l + jnp.sum(p, axis=-1, keepdims=True)
                vv = z_ref[0, k0:k0 + tq, 2 * HEAD_DIM:]
                acc = alpha * acc + jnp.dot(p.astype(BF16), vv, preferred_element_type=F32)
                m = m_new
            o_ref[0, q0:q0 + tq, :] = (acc / l).astype(BF16)
            lse_ref[0, 0, q0:q0 + tq, :] = m + jnp.log(l) + fcol_ref[0, 0, q0:q0 + tq, :]

    return _run(
        body, [z, fcol, frow], exchange, name="attn_fwd", grid=(bsz, N_HEADS),
        in_specs=[pl.BlockSpec((1, s, 3 * HEAD_DIM), lambda b, h: (b, 0, h)),
                  pl.BlockSpec((1, 1, s, 1), lambda b, h: (b, h, 0, 0)),
                  pl.BlockSpec((1, 1, 1, s), lambda b, h: (b, h, 0, 0))],
        out_specs=[pl.BlockSpec((1, s, HEAD_DIM), lambda b, h: (b, 0, h)),
                   pl.BlockSpec((1, 1, s, 1), lambda b, h: (b, h, 0, 0))],
        out_shape=[jax.ShapeDtypeStruct((bsz, s, N_HEADS * HEAD_DIM), BF16),
                   jax.ShapeDtypeStruct((bsz, N_HEADS, s, 1), F32)],
        compiler_params=_params())


def _attn_bwd(z, do, lse, fcol, frow, dz, exchange=None):
    bsz, s, _ = z.shape
    tq = min(ATTN_TQ, s)
    scale = HEAD_DIM ** -0.5

    def body(z_ref, do_ref, lse_ref, fcol_ref, frow_ref, dz_in, dz_ref, df_ref, dk_acc, dv_acc):
        del dz_in
        dk_acc[...] = jnp.zeros_like(dk_acc)
        dv_acc[...] = jnp.zeros_like(dv_acc)
        df_ref[...] = jnp.zeros_like(df_ref)
        for qb in range(s // tq):
            q0, ln = qb * tq, (qb + 1) * tq
            q_b = (z_ref[0, q0:ln, :HEAD_DIM].astype(F32) * scale).astype(BF16)
            kk = z_ref[0, :ln, HEAD_DIM:2 * HEAD_DIM]
            vv = z_ref[0, :ln, 2 * HEAD_DIM:]
            do_b = do_ref[0, q0:ln, :]
            lse_q = lse_ref[0, 0, q0:ln, :] - fcol_ref[0, 0, q0:ln, :]
            sc = lax.dot_general(q_b, kk, _NT, preferred_element_type=F32) - frow_ref[0, 0, :, :ln] - lse_q
            mask = (q0 + lax.broadcasted_iota(jnp.int32, (tq, ln), 0)) >= lax.broadcasted_iota(jnp.int32, (tq, ln), 1)
            p = jnp.where(mask, jnp.exp(sc), 0.0)
            pb = p.astype(BF16)
            dv_acc[:ln, :] = dv_acc[:ln, :] + lax.dot_general(pb, do_b, _TN, preferred_element_type=F32)
            dp = lax.dot_general(do_b, vv, _NT, preferred_element_type=F32)
            ds = p * (dp - jnp.sum(p * dp, axis=-1, keepdims=True))
            dsb = ds.astype(BF16)
            dq = jnp.dot(dsb, kk, preferred_element_type=F32) * scale
            dz_ref[0, q0:ln, :HEAD_DIM] = dq.astype(BF16)
            dk_acc[:ln, :] = dk_acc[:ln, :] + lax.dot_general(dsb, q_b, _TN, preferred_element_type=F32)
            df_ref[0, 0, :, :ln] = df_ref[0, 0, :, :ln] - jnp.sum(ds, axis=0, keepdims=True)
        dz_ref[0, :, HEAD_DIM:2 * HEAD_DIM] = dk_acc[...].astype(BF16)
        dz_ref[0, :, 2 * HEAD_DIM:] = dv_acc[...].astype(BF16)

    zblk = pl.BlockSpec((1, s, 3 * HEAD_DIM), lambda b, h: (b, 0, h))
    hblk = pl.BlockSpec((1, s, HEAD_DIM), lambda b, h: (b, 0, h))
    col = pl.BlockSpec((1, 1, s, 1), lambda b, h: (b, h, 0, 0))
    rowspec = pl.BlockSpec((1, 1, 1, s), lambda b, h: (b, h, 0, 0))
    return _run(
        body, [z, do, lse, fcol, frow, dz], exchange, name="attn_bwd", grid=(bsz, N_HEADS),
        in_specs=[zblk, hblk, col, col, rowspec, ANY],
        out_specs=[zblk, rowspec],
        out_shape=[jax.ShapeDtypeStruct(dz.shape, dz.dtype), jax.ShapeDtypeStruct((bsz, N_HEADS, 1, s), F32)],
        scratch_shapes=[pltpu.VMEM((s, HEAD_DIM), F32), pltpu.VMEM((s, HEAD_DIM), F32)],
        input_output_aliases={5: 0},
        compiler_params=_params())


def _merge_fwd(z2, pr, pa, *, tr=512):
    t = z2.shape[0]

    def body(mgr_ref, mga_ref, pr_ref, pa_ref, o_ref):
        gr_ = _sigmoid(mgr_ref[...])
        ga = _sigmoid(mga_ref[...])
        o_ref[...] = (gr_ * pr_ref[...] + ga * pa_ref[...]).astype(BF16)

    tok = pl.BlockSpec((tr, D_MODEL), lambda i: (i, 0))
    return _call(body, name="merge_fwd", grid=(t // tr,),
                 in_specs=[pl.BlockSpec((tr, D_MODEL), lambda i: (i, MG0R // D_MODEL)),
                           pl.BlockSpec((tr, D_MODEL), lambda i: (i, MG0R // D_MODEL + 1)), tok, tok],
                 out_specs=tok, out_shape=jax.ShapeDtypeStruct((t, D_MODEL), BF16),
                 compiler_params=_params())(z2, z2, pr, pa)


def _merge_bwd(dm, z2, pr, pa, *, tr=512):
    t = z2.shape[0]

    def body(dm_ref, mgr_ref, mga_ref, pr_ref, pa_ref, dz_ref, dpr_ref, dpa_ref):
        dmv = dm_ref[...]
        gr_ = _sigmoid(mgr_ref[...])
        ga = _sigmoid(mga_ref[...])
        dpr_ref[...] = (dmv * gr_).astype(BF16)
        dpa_ref[...] = (dmv * ga).astype(BF16)
        dz_ref[:, :D_MODEL] = (dmv * pr_ref[...] * gr_ * (1.0 - gr_)).astype(BF16)
        dz_ref[:, D_MODEL:] = (dmv * pa_ref[...] * ga * (1.0 - ga)).astype(BF16)

    tok = pl.BlockSpec((tr, D_MODEL), lambda i: (i, 0))
    mg = pl.BlockSpec((tr, 2 * D_MODEL), lambda i: (i, MG0 // (2 * D_MODEL)))
    return _call(body, name="merge_bwd", grid=(t // tr,),
                 in_specs=[tok, pl.BlockSpec((tr, D_MODEL), lambda i: (i, MG0R // D_MODEL)),
                           pl.BlockSpec((tr, D_MODEL), lambda i: (i, MG0R // D_MODEL + 1)), tok, tok],
                 out_specs=[mg, tok, tok],
                 out_shape=[jax.ShapeDtypeStruct((t, NZ), BF16), jax.ShapeDtypeStruct((t, D_MODEL), BF16),
                            jax.ShapeDtypeStruct((t, D_MODEL), BF16)],
                 compiler_params=_params())(dm, z2, z2, pr, pa)


def _fl_write(dfl_blk, dz):
    bsz, s, w = dfl_blk.shape

    def body(src_ref, dz_in, dz_ref):
        del dz_in
        dz_ref[...] = src_ref[...]

    return _call(body, name="fl_write", grid=(bsz,),
                 in_specs=[pl.BlockSpec((1, s, w), lambda b: (b, 0, 0)), ANY],
                 out_specs=pl.BlockSpec((1, s, w), lambda b: (b, 0, FL0 // w)),
                 out_shape=jax.ShapeDtypeStruct(dz.shape, dz.dtype), input_output_aliases={1: 0},
                 compiler_params=_params())(dfl_blk, dz)


def _ffn_fwd(zf, wc, bc):
    bsz, s, _ = zf.shape
    cw = 128

    def body(zf_ref, wc_ref, bc_ref, o_ref):
        gfp, uf = zf_ref[0, :, :cw].astype(F32), zf_ref[0, :, cw:].astype(F32)
        wcv = wc_ref[...]
        row = lax.broadcasted_iota(jnp.int32, (s, cw), 0)
        gf = bc_ref[...] + wcv[2:3] * gfp
        for j in (1, 2):
            gf = gf + wcv[2 - j:3 - j] * jnp.where(row >= j, pltpu.roll(gfp, j, 0), 0.0)
        o_ref[0] = (_gelu(gf) * uf).astype(BF16)

    return _call(body, name="ffn_fwd", grid=(bsz, D_FF // cw),
                 in_specs=[pl.BlockSpec((1, s, 2 * cw), lambda b, j: (b, 0, j)),
                           pl.BlockSpec((3, cw), lambda b, j: (0, j)), pl.BlockSpec((1, cw), lambda b, j: (0, j))],
                 out_specs=pl.BlockSpec((1, s, cw), lambda b, j: (b, 0, j)),
                 out_shape=jax.ShapeDtypeStruct((bsz, s, D_FF), BF16), compiler_params=_params())(zf, wc, bc)


def _ffn_bwd(dact, zf, wc, bc, exchange=None):
    bsz, s, _ = zf.shape
    cw = 128

    def body(da_ref, zf_ref, wc_ref, bc_ref, dzf_ref, dsm_ref):
        b = pl.program_id(1)
        gfp, uf = zf_ref[0, :, :cw].astype(F32), zf_ref[0, :, cw:].astype(F32)
        wcv = wc_ref[...]
        row = lax.broadcasted_iota(jnp.int32, (s, cw), 0)
        sh = [gfp] + [jnp.where(row >= j, pltpu.roll(gfp, j, 0), 0.0) for j in (1, 2)]
        gf = bc_ref[...]
        for kk in range(3):
            gf = gf + wcv[kk:kk + 1] * sh[2 - kk]
        ge, dge = _gelu_and_grad(gf)
        dav = da_ref[0].astype(F32)
        dgf = dav * uf * dge
        dgfp = wcv[2:3] * dgf
        for j in (1, 2):
            dgfp = dgfp + wcv[2 - j:3 - j] * jnp.where(row < s - j, pltpu.roll(dgf, s - j, 0), 0.0)
        dzf_ref[0] = jnp.concatenate([dgfp, dav * ge], axis=1).astype(BF16)
        rows = [jnp.sum(dgf * sh[2 - kk], axis=0, keepdims=True) for kk in range(3)]
        rows += [jnp.sum(dgf, axis=0, keepdims=True), jnp.zeros((4, cw), F32)]
        _acc(dsm_ref, slice(None), jnp.concatenate(rows, axis=0), b == 0)

    return _run(body, [dact, zf, wc, bc], exchange, name="ffn_bwd", grid=(D_FF // cw, bsz),
                in_specs=[pl.BlockSpec((1, s, cw), lambda j, b: (b, 0, j)),
                          pl.BlockSpec((1, s, 2 * cw), lambda j, b: (b, 0, j)),
                          pl.BlockSpec((3, cw), lambda j, b: (0, j)), pl.BlockSpec((1, cw), lambda j, b: (0, j))],
                out_specs=[pl.BlockSpec((1, s, 2 * cw), lambda j, b: (b, 0, j)),
                           pl.BlockSpec((8, cw), lambda j, b: (0, j))],
                out_shape=[jax.ShapeDtypeStruct(zf.shape, BF16), jax.ShapeDtypeStruct((8, D_FF), F32)],
                compiler_params=_params())


def _adamw(w, m, v, parts, *, name):
    r, c = w.shape
    p = parts.shape[0]
    by_cols = r > 512 and r % 256 != 0
    tr, tc = (r, 256) if by_cols else (r if r <= 512 else 256, c)
    assert r % tr == 0 and c % tc == 0 and parts.shape[1:] == (r, c), (name, w.shape, parts.shape)
    c1 = 1.0 / (1.0 - ADAM_B1 ** ADAM_STEP)
    c2 = 1.0 / (1.0 - ADAM_B2 ** ADAM_STEP)

    def body(w_ref, m_ref, v_ref, p_ref, g_ref, d_ref, mo_ref, vo_ref):
        g = p_ref[0].astype(F32)
        for q in range(1, p):
            g = g + p_ref[q].astype(F32)
        mn = ADAM_B1 * m_ref[...] + (1.0 - ADAM_B1) * g
        vn = ADAM_B2 * v_ref[...] + (1.0 - ADAM_B2) * (g * g)
        g_ref[...] = g
        mo_ref[...] = mn
        vo_ref[...] = vn
        d_ref[...] = -ADAM_LR * ((mn * c1) / (jnp.sqrt(vn * c2) + ADAM_EPS) + ADAM_WD * w_ref[...])

    spec = pl.BlockSpec((tr, tc), (lambda i: (0, i)) if by_cols else (lambda i: (i, 0)))
    shp = jax.ShapeDtypeStruct((r, c), F32)
    return _call(body, name=name, grid=(c // tc if by_cols else r // tr,),
                 in_specs=[spec, spec, spec,
                           pl.BlockSpec((p, tr, tc), (lambda i: (0, 0, i)) if by_cols else (lambda i: (0, i, 0)))],
                 out_specs=[spec] * 4, out_shape=[shp] * 4, compiler_params=_params())(w, m, v, parts)


def _ref_slab(i):
    nq = 3 * N_HEADS
    j = i - nq
    return jnp.where(i < nq, 2 * D_RNN // 128 + (i % 3) * N_HEADS + i // 3, (j % 2) * N_BLK + j // 2)


def _permute_slabs(x, to_z, name):
    n = FL0 // 128
    cols = x.shape[1]

    def body(x_ref, o_ref):
        o_ref[...] = x_ref[...]

    there = pl.BlockSpec((128, cols), lambda i: (_ref_slab(i), 0))
    here = pl.BlockSpec((128, cols), lambda i: (i, 0))
    return _call(body, name=name, grid=(n,), in_specs=[there if to_z else here],
                 out_specs=here if to_z else there, out_shape=jax.ShapeDtypeStruct((FL0, cols), x.dtype),
                 compiler_params=_params())(x)


def _to_z_layout(wt):
    fl0 = 2 * D_RNN + 3 * D_MODEL
    pad = jnp.zeros((MG0 - FL0 - N_HEADS, wt.shape[1]), wt.dtype)
    return jnp.concatenate([_permute_slabs(wt, True, "to_z_layout"), wt[fl0:fl0 + N_HEADS], pad,
                            wt[fl0 + N_HEADS:]], axis=0)


def _from_z_layout(wt):
    return jnp.concatenate([_permute_slabs(wt, False, "from_z_layout"), wt[FL0:FL0 + N_HEADS], wt[MG0:]], axis=0)


def _col_gather(g):
    return g.transpose(1, 0, 2).reshape(g.shape[1], -1)


def _col_scatter(w):
    return w.reshape(w.shape[0], N_DEV, -1).transpose(1, 0, 2)


def kernel(x, c, w_ada, b_ada, g_norm1, w_in, w_rnn_conv, b_rnn_conv, w_lru_a, b_lru_a, w_lru_i, b_lru_i, lru_lambda, b_fgate, w_proj_rnn, w_proj_attn, w_out, g_norm2, w_ffn_up, w_ffn_conv, b_ffn_conv, w_ffn_down, w_ada_final, b_ada_final, g_final, loss_target, m_w_ada, m_b_ada, m_g_norm1, m_w_in, m_w_rnn_conv, m_b_rnn_conv, m_w_lru_a, m_b_lru_a, m_w_lru_i, m_b_lru_i, m_lru_lambda, m_b_fgate, m_w_proj_rnn, m_w_proj_attn, m_w_out, m_g_norm2, m_w_ffn_up, m_w_ffn_conv, m_b_ffn_conv, m_w_ffn_down, m_w_ada_final, m_b_ada_final, m_g_final, v_w_ada, v_b_ada, v_g_norm1, v_w_in, v_w_rnn_conv, v_b_rnn_conv, v_w_lru_a, v_b_lru_a, v_w_lru_i, v_b_lru_i, v_lru_lambda, v_b_fgate, v_w_proj_rnn, v_w_proj_attn, v_w_out, v_g_norm2, v_w_ffn_up, v_w_ffn_conv, v_b_ffn_conv, v_w_ffn_down, v_w_ada_final, v_b_ada_final, v_g_final):
    args = dict(locals())
    bsz, s, _ = x.shape
    t = bsz * s
    nb = N_DEV * bsz
    me = 4 * lax.axis_index("x") + 2 * lax.axis_index("y") + lax.axis_index("c")
    tm = min(2048, t)

    ex_a = _gather_two_level([c, w_in[0].T.astype(BF16), w_rnn_conv[0], w_ffn_conv[0]], "gather_first")
    c_all = ex_a[0].reshape(nb, D_MODEL)
    win_zt = _to_z_layout(ex_a[1].reshape(-1, D_MODEL))
    wrc = _col_gather(ex_a[2])
    wfc = _col_gather(ex_a[3])

    nmod = w_ada.shape[2]
    nmodf = w_ada_final.shape[1]
    mod_cols = _mm(c_all, w_ada[0], name="mod_cols", tm=nb, tn=nmod, tk=D_MODEL, silu_a=True,
                   bias=lax.dynamic_slice(b_ada, (0, me * nmod), (1, nmod)))
    modf_cols = _mm(c_all, w_ada_final, name="modf_cols", tm=nb, tn=nmodf, tk=D_MODEL, silu_a=True,
                    bias=lax.dynamic_slice(b_ada_final.reshape(1, -1), (0, me * nmodf), (1, nmodf)))
    ex_b = _exchange([mod_cols, modf_cols], [True, True], "gather_mod")
    mod = lax.dynamic_slice(ex_b[0], (0, me * bsz, 0), (N_DEV, bsz, nmod)).transpose(1, 0, 2).reshape(bsz, 6, 1, D_MODEL)
    modf = lax.dynamic_slice(ex_b[1], (0, me * bsz, 0), (N_DEV, bsz, nmodf)).transpose(1, 0, 2).reshape(bsz, 2, 1, D_MODEL)
    shift1, scale1, gate1, shift2, scale2, gate2 = [mod[:, i] for i in range(6)]
    shift_f, scale_f = modf[:, 0], modf[:, 1]

    h1 = _norm_fwd(x, g_norm1, shift1, scale1, name="norm1_fwd")[0]
    h1f = h1.reshape(t, D_MODEL)
    zq = _mm(h1f, win_zt, name="mm_in_qkv", tb=True, out_dtype=BF16, tm=tm, tn=1024, tk=D_MODEL,
             b_cols=(QKV0, XG0)).reshape(bsz, s, XG0)
    zr = _mm(h1f, win_zt, name="mm_in_rest", tb=True, tm=tm, tn=1024, tk=D_MODEL,
             b_cols=(XG0, NZR)).reshape(bsz, s, NZR)
    h_rnn, y_rnn, lru_gates, got = _rnn_fwd(zr, wrc, b_rnn_conv, w_lru_a[0], b_lru_a, w_lru_i[0], b_lru_i, lru_lambda,
                                 exchange=([w_ffn_up[0].astype(BF16)], ["gather_zf"]))
    wup_z = got[0]
    flt = zr[:, :, FL0R:FL0R + N_HEADS].transpose(0, 2, 1)
    bfg = b_fgate.reshape(N_HEADS, 1)
    fcum = _fgate_fwd(flt, bfg)
    fcol = fcum.reshape(bsz, N_HEADS, s, 1)
    frow = fcum.reshape(bsz, N_HEADS, 1, s)
    o_attn, lse, got = _attn_fwd(
        zq, fcol, frow, exchange=([w_proj_rnn[0].astype(BF16), w_proj_attn[0].astype(BF16), w_out[0].astype(BF16),
                                  w_ffn_down[0].astype(BF16)], [True] * 4))
    wpr = got[0].reshape(D_RNN, D_MODEL)
    wpa = got[1].reshape(D_MODEL, D_MODEL)
    wout = got[2].reshape(D_MODEL, D_MODEL)
    wdown = got[3].reshape(D_FF, D_MODEL)
    pr = _mm(y_rnn.reshape(t, D_RNN), wpr, name="mm_proj_rnn", out_dtype=BF16, tm=tm, tn=D_MODEL, tk=D_RNN)
    pa = _mm(o_attn.reshape(t, D_MODEL), wpa, name="mm_proj_attn", out_dtype=BF16, tm=tm, tn=D_MODEL, tk=D_MODEL)
    z2 = zr.reshape(t, NZR)
    merged = _merge_fwd(z2, pr, pa)
    mo = _mm(merged, wout, name="mm_out", out_dtype=BF16, tm=tm, tn=D_MODEL, tk=D_MODEL).reshape(bsz, s, D_MODEL)
    h2, x1 = _norm_fwd(x, g_norm2, shift2, scale2, name="norm2_fwd", yprev=mo, gate=gate1)
    zf = _mm(h2.reshape(t, D_MODEL), wup_z, name="mm_up", out_dtype=BF16, tm=tm, tn=1024,
             tk=D_MODEL).reshape(bsz, s, 2 * D_FF)
    act = _ffn_fwd(zf, wfc, b_ffn_conv)
    yf = _mm(act.reshape(t, D_FF), wdown, name="mm_down", out_dtype=BF16, tm=tm, tn=D_MODEL,
             tk=D_FF // 2).reshape(bsz, s, D_MODEL)
    lp, dx2, dyf, dshift_f, dscale_f, dgate2, dg_final = _final(x1, yf, gate2, g_final.reshape(1, -1), shift_f, scale_f, loss_target)
    loss = lax.psum(jnp.sum(lp[:, :, 0, 0]) * (0.5 / D_MODEL), ("x", "y", "c"))

    dyf2 = dyf.reshape(t, D_MODEL)
    act2 = act.reshape(t, D_FF)
    dact = _mm(dyf2, wdown, name="mm_down_dx", tb=True, out_dtype=BF16, tm=tm, tn=1024, tk=D_MODEL).reshape(bsz, s, D_FF)
    g_wdown = _mm(act2, dyf2, name="mm_down_dw", ta=True, out_dtype=BF16, tm=1024, tn=D_MODEL, tk=tm)
    dzf, dsm_ffn, got = _ffn_bwd(dact, zf, wfc, b_ffn_conv, exchange=([g_wdown.reshape(N_DEV, -1, D_MODEL)], [False]))
    p_wdown = got[0]
    dzf2 = dzf.reshape(t, 2 * D_FF)
    dh2 = _mm(dzf2, wup_z, name="mm_up_dx", tb=True, tm=tm, tn=D_MODEL, tk=2048).reshape(bsz, s, D_MODEL)
    g_wup_z = _mm(h2.reshape(t, D_MODEL), dzf2, name="mm_up_dw", ta=True, out_dtype=BF16, tm=D_MODEL, tn=1024, tk=tm)
    dx1, dg_norm2, dshift2, dscale2, dmo, dgate1 = _norm_bwd(x1, dh2, g_norm2, scale2, dx2, name="norm2_bwd", yprev=mo, gate=gate1)
    dmo2 = dmo.reshape(t, D_MODEL)
    dmerged = _mm(dmo2, wout, name="mm_out_dx", tb=True, out_dtype=BF16, tm=tm, tn=D_MODEL, tk=D_MODEL)
    g_wout = _mm(merged, dmo2, name="mm_out_dw", ta=True, out_dtype=BF16, tm=D_MODEL, tn=D_MODEL, tk=tm)
    dz2, dpr, dpa = _merge_bwd(dmerged, z2, pr, pa)
    dy_rnn = _mm(dpr, wpr, name="mm_proj_rnn_dx", tb=True, out_dtype=BF16, tm=tm, tn=D_RNN,
                 tk=D_MODEL).reshape(bsz, s, D_RNN)
    g_wpr = _mm(y_rnn.reshape(t, D_RNN), dpr, name="mm_proj_rnn_dw", ta=True, out_dtype=BF16, tm=D_RNN, tn=D_MODEL, tk=tm)
    do = _mm(dpa, wpa, name="mm_proj_attn_dx", tb=True, out_dtype=BF16, tm=tm, tn=D_MODEL, tk=D_MODEL).reshape(bsz, s, D_MODEL)
    g_wpa = _mm(o_attn.reshape(t, D_MODEL), dpa, name="mm_proj_attn_dw", ta=True, out_dtype=BF16, tm=D_MODEL, tn=D_MODEL, tk=tm)
    dz, dfrow, got = _attn_bwd(zq, do, lse, fcol, frow, dz2.reshape(bsz, s, NZ),
                               exchange=([g_wup_z, g_wout.reshape(N_DEV, -1, D_MODEL)], ["scatter_zf", False]))
    p_wup, p_wout = got
    dflt, db_fgate = _fgate_bwd(dfrow.reshape(bsz, N_HEADS, s), flt, bfg)
    dfl_blk = jnp.pad(dflt.transpose(0, 2, 1).astype(BF16), ((0, 0), (0, 0), (0, MG0 - FL0 - N_HEADS)))
    dz = _fl_write(dfl_blk, dz)
    dz, g_wa, g_wi, dsm_rnn, got = _rnn_bwd(
        zr, h_rnn, lru_gates, dy_rnn, dz, wrc, b_rnn_conv, w_lru_a[0], w_lru_i[0], lru_lambda,
        exchange=([g_wpr.reshape(N_DEV, -1, D_MODEL), g_wpa.reshape(N_DEV, -1, D_MODEL)], [False, False]))
    p_wpr, p_wpa = got
    dz2 = dz.reshape(t, NZ)
    small = [dg_norm2, dg_final, dsm_rnn, g_wa, g_wi, db_fgate, dsm_ffn]
    small_sizes = [a.size for a in small]
    n_pad = -sum(small_sizes) % 1024
    packed = jnp.concatenate([a.reshape(-1) for a in small] + [jnp.zeros((n_pad,), F32)]).reshape(-1, 128)
    g_win_zt, got = _mm(dz2, h1f, name="mm_in_dw", ta=True, out_dtype=BF16, tm=1024, tn=D_MODEL, tk=tm,
                        exchange=([packed], [True]))
    flat = got[0].reshape(N_DEV, -1)
    dh1, got = _mm(dz2, win_zt, name="mm_in_dx", tm=tm, tn=D_MODEL, tk=2048,
                   exchange=([_from_z_layout(g_win_zt).reshape(N_DEV, -1, D_MODEL)], [False]))
    p_win = got[0]
    grad_x, dg_norm1, dshift1, dscale1 = _norm_bwd(x, dh1.reshape(bsz, s, D_MODEL), g_norm1, scale1, dx1, name="norm1_bwd")

    dmod = jnp.concatenate([dshift1, dscale1, dgate1, dshift2, dscale2, dgate2], axis=1).reshape(bsz, 6 * D_MODEL)
    dmodf = jnp.concatenate([dshift_f, dscale_f], axis=1).reshape(bsz, 2 * D_MODEL)
    ex_c = _exchange([dmod, dmodf, dg_norm1], [True, True, True], "gather_last_grads")
    dmod_all = ex_c[0].reshape(nb, 6 * D_MODEL)
    dmodf_all = ex_c[1].reshape(nb, 2 * D_MODEL)
    p_g1 = ex_c[2]

    res = {}

    def upd(name, parts, shape2):
        w, m, v = args[name], args["m_" + name], args["v_" + name]
        outs = _adamw(w.reshape(shape2), m.reshape(shape2), v.reshape(shape2), parts, name="adamw_" + name)
        res[name] = [o.reshape(w.shape) for o in outs]

    g_wada = _mm(c_all, lax.dynamic_slice(dmod_all, (0, me * nmod), (nb, nmod)), name="mm_ada_dw", ta=True,
                 silu_a=True, tm=D_MODEL, tn=nmod, tk=nb)
    g_wadaf = _mm(c_all, lax.dynamic_slice(dmodf_all, (0, me * nmodf), (nb, nmodf)), name="mm_adaf_dw", ta=True,
                  silu_a=True, tm=D_MODEL, tn=nmodf, tk=nb)
    upd("w_ada", g_wada[None], (D_MODEL, nmod))
    upd("w_ada_final", g_wadaf[None], (D_MODEL, nmodf))
    upd("b_ada", dmod_all.reshape(nb, 1, -1), (1, 6 * D_MODEL))
    upd("b_ada_final", dmodf_all.reshape(nb, 1, -1), (1, 2 * D_MODEL))
    outs = _adamw(w_in[0].T, m_w_in[0].T, v_w_in[0].T, p_win, name="adamw_w_in")
    res["w_in"] = [o.T[None] for o in outs]
    upd("w_proj_rnn", p_wpr, w_proj_rnn.shape[1:])
    upd("w_proj_attn", p_wpa, w_proj_attn.shape[1:])
    upd("w_out", p_wout, w_out.shape[1:])
    upd("w_ffn_up", p_wup, w_ffn_up.shape[1:])
    upd("w_ffn_down", p_wdown, w_ffn_down.shape[1:])

    parts, off = [], 0
    for a, n in zip(small, small_sizes):
        parts.append(flat[:, off:off + n].reshape((N_DEV,) + a.shape))
        off += n
    p_g2, p_gf, p_rnn, p_wa, p_wi, p_bfg, p_ffn = parts
    upd("g_norm1", p_g1, (1, D_MODEL))
    upd("g_norm2", p_g2, (1, D_MODEL))
    upd("g_final", p_gf, (1, D_MODEL))
    nrc = w_rnn_conv.shape[2]
    upd("w_rnn_conv", lax.dynamic_slice(p_rnn, (0, 0, me * nrc), (N_DEV, 4, nrc)), (4, nrc))
    upd("b_rnn_conv", p_rnn[:, 4:5], (1, D_RNN))
    upd("b_lru_a", p_rnn[:, 5:6], (1, D_RNN))
    upd("b_lru_i", p_rnn[:, 6:7], (1, D_RNN))
    upd("lru_lambda", p_rnn[:, 7:8], (1, D_RNN))
    upd("w_lru_a", p_wa.reshape(N_DEV, D_RNN, RNN_BLK), (D_RNN, RNN_BLK))
    upd("w_lru_i", p_wi.reshape(N_DEV, D_RNN, RNN_BLK), (D_RNN, RNN_BLK))
    upd("b_fgate", p_bfg.reshape(N_DEV, 1, N_HEADS), (1, N_HEADS))
    nfc = w_ffn_conv.shape[2]
    upd("w_ffn_conv", lax.dynamic_slice(p_ffn, (0, 0, me * nfc), (N_DEV, 3, nfc)), (3, nfc))
    upd("b_ffn_conv", p_ffn[:, 3:4], (1, D_FF))

    names = ['w_ada', 'b_ada', 'g_norm1', 'w_in', 'w_rnn_conv', 'b_rnn_conv', 'w_lru_a', 'b_lru_a', 'w_lru_i',
             'b_lru_i', 'lru_lambda', 'b_fgate', 'w_proj_rnn', 'w_proj_attn', 'w_out', 'g_norm2', 'w_ffn_up',
             'w_ffn_conv', 'b_ffn_conv', 'w_ffn_down', 'w_ada_final', 'b_ada_final', 'g_final']
    return (loss, grad_x, *[res[n][0] for n in names], *[res[n][1] for n in names],
            *[res[n][2] for n in names], *[res[n][3] for n in names])
```

```python
import jax
import jax.numpy as jnp
from jax import lax
from jax.experimental import pallas as pl
from jax.experimental.pallas import tpu as pltpu

F32, BF16 = jnp.float32, jnp.bfloat16
D_MODEL, D_RNN, N_BLK, RNN_BLK = 1024, 1280, 10, 128
N_HEADS, HEAD_DIM, D_FF = 8, 128, 3072
N_DEV = 8
NZ = 8192
QKV0, XG0, FL0, MG0 = 0, 3072, 5632, 6144
NZR = NZ - XG0
XG0R, FL0R, MG0R = 0, FL0 - XG0, MG0 - XG0
RMS_EPS = 1e-6
LRU_C = 8.0
ADAM_LR, ADAM_B1, ADAM_B2, ADAM_EPS, ADAM_WD, ADAM_STEP = 0.001, 0.9, 0.999, 1e-08, 0.01, 10
VMEM_LIMIT = 56 << 20
SCAN_CHUNK = 256
ATTN_TQ = 256
CUM_BLK = 512
MESH = pl.DeviceIdType.MESH
ANY = pl.BlockSpec(memory_space=pl.ANY)


def _call(body, **kw):
    return pl.pallas_call(body, **kw)


def _params():
    return pltpu.CompilerParams(vmem_limit_bytes=VMEM_LIMIT)


ZF_PIECES = 2 * D_FF // 128 // N_DEV


def _zf_cols(owner, p):
    ob = owner * ZF_PIECES + p
    blk = 2 * (ob % (D_FF // 128)) + ob // (D_FF // 128)
    return pl.ds(pl.multiple_of(blk * 128, 128), 128)


def _n_pieces(mode):
    return ZF_PIECES if mode in ("gather_zf", "scatter_zf") else 1


def _exchanged_shape(a, mode):
    if mode is True:
        return (N_DEV,) + a.shape
    if mode is False:
        return a.shape
    if mode == "gather_zf":
        return (a.shape[0], N_DEV * a.shape[1])
    return (N_DEV, a.shape[0], a.shape[1] // N_DEV)


def _direct_copies(ins, outs, sems, modes):
    send_sems, recv_sems, loc_sems = sems
    x, y, c = lax.axis_index("x"), lax.axis_index("y"), lax.axis_index("c")
    me = 4 * x + 2 * y + c

    def ends(a, to, p):
        mode = modes[a]
        if mode is True:
            return ins[a], outs[a].at[me]
        if mode is False:
            return ins[a].at[to], outs[a].at[me]
        mine = slice(p * 128, (p + 1) * 128)
        if mode == "gather_zf":
            return ins[a].at[:, mine], outs[a].at[:, _zf_cols(me, p)]
        return ins[a].at[:, _zf_cols(to, p)], outs[a].at[me, :, mine]

    local, remote = [], []
    nloc = 0
    for a in range(len(ins)):
        for p in range(_n_pieces(modes[a])):
            src, dst = ends(a, me, p)
            local.append(pltpu.make_async_copy(src, dst, loc_sems.at[nloc]))
            nloc += 1
    for k in range(1, N_DEV):
        px = 1 - x if k & 4 else x
        py = 1 - y if k & 2 else y
        pc = 1 - c if k & 1 else c
        peer = 4 * px + 2 * py + pc
        nrem = (k - 1) * nloc
        for a in range(len(ins)):
            for p in range(_n_pieces(modes[a])):
                src, dst = ends(a, peer, p)
                remote.append(pltpu.make_async_remote_copy(
                    src_ref=src, dst_ref=dst, send_sem=send_sems.at[nrem], recv_sem=recv_sems.at[nrem],
                    device_id=(px, py, pc), device_id_type=MESH))
                nrem += 1
    return local, remote


def _with_exchange(body, args, arrs, bcast, *, name, grid, in_specs, out_specs, out_shape, scratch_shapes=(), **kw):
    n_in, n_out, n_sc, nc = len(args), len(out_shape), len(scratch_shapes), len(arrs)

    def wrapped(*refs):
        core_in, cin = refs[:n_in], refs[n_in:n_in + nc]
        core_out = refs[n_in + nc:n_in + nc + n_out]
        cout = refs[n_in + nc + n_out:n_in + 2 * nc + n_out]
        rest = refs[n_in + 2 * nc + n_out:]
        core_sc, sems = rest[:n_sc], rest[n_sc:]

        def start():
            local, remote = _direct_copies(cin, cout, sems, bcast)
            for cp in local + remote:
                cp.start()

        def wait():
            local, remote = _direct_copies(cin, cout, sems, bcast)
            for cp in remote + local:
                cp.wait()

        if not grid:
            start()
            body(*core_in, *core_out, *core_sc)
            wait()
            return
        first, last = None, None
        for d, g in enumerate(grid):
            i = pl.program_id(d)
            first = (i == 0) if first is None else jnp.logical_and(first, i == 0)
            last = (i == g - 1) if last is None else jnp.logical_and(last, i == g - 1)
        pl.when(first)(start)
        body(*core_in, *core_out, *core_sc)
        pl.when(last)(wait)

    ex_shape = [jax.ShapeDtypeStruct(_exchanged_shape(a, mode), a.dtype) for a, mode in zip(arrs, bcast)]
    ncopies = sum(_n_pieces(mode) for mode in bcast)
    sems = [pltpu.SemaphoreType.DMA((ncopies * (N_DEV - 1),)), pltpu.SemaphoreType.DMA((ncopies * (N_DEV - 1),)),
            pltpu.SemaphoreType.DMA((ncopies,))]
    res = _call(wrapped, name=name, grid=grid, in_specs=list(in_specs) + [ANY] * nc,
                out_specs=list(out_specs) + [ANY] * nc, out_shape=list(out_shape) + ex_shape,
                scratch_shapes=list(scratch_shapes) + sems, **kw)(*args, *arrs)
    return list(res[:n_out]), list(res[n_out:])


def _run(body, args, exchange, **kw):
    if exchange is None:
        return list(_call(body, **kw)(*args))
    outs, got = _with_exchange(body, args, *exchange, **kw)
    return outs + [got]


def _exchange(arrs, bcast, name):
    return _with_exchange(lambda: None, [], arrs, bcast, name=name, grid=(), in_specs=[], out_specs=[], out_shape=[],
                          compiler_params=pltpu.CompilerParams(has_side_effects=True))[1]


def _gather_two_level(arrs, name):
    n = len(arrs)

    def body(*refs):
        ins, outs = refs[:n], refs[n:2 * n]
        send_sems, recv_sems, loc_sems = refs[2 * n:]
        x, y, c = lax.axis_index("x"), lax.axis_index("y"), lax.axis_index("c")
        me, sibling = (x, y, c), (x, y, 1 - c)
        chips = [(1 - x, y), (x, 1 - y), (1 - x, 1 - y)]

        def slot(a, p):
            return outs[a].at[4 * p[0] + 2 * p[1] + p[2]]

        def copy(a, k, block, to, src=None):
            return pltpu.make_async_remote_copy(
                src_ref=slot(a, block) if src is None else src, dst_ref=slot(a, block),
                send_sem=send_sems.at[a * 7 + k], recv_sem=recv_sems.at[a * 7 + k],
                device_id=to, device_id_type=MESH)

        mine = [pltpu.make_async_copy(ins[a], slot(a, me), loc_sems.at[a]) for a in range(n)]
        first = []
        for a in range(n):
            first.append(copy(a, 0, me, sibling, src=ins[a]))
            first += [copy(a, 1 + j, me, (*chip, c), src=ins[a]) for j, chip in enumerate(chips)]
        for cp in mine + first:
            cp.start()
        passed = []
        for j, chip in enumerate(chips):
            for a in range(n):
                copy(a, 1 + j, (*chip, c), me).wait_recv()
                fwd = copy(a, 4 + j, (*chip, c), sibling)
                fwd.start()
                passed.append(fwd)
        for a in range(n):
            copy(a, 0, sibling, me).wait_recv()
        for j, chip in enumerate(chips):
            for a in range(n):
                copy(a, 4 + j, (*chip, 1 - c), me).wait_recv()
        for cp in first + passed:
            cp.wait_send()
        for cp in mine:
            cp.wait()

    return _call(
        body, name=name, in_specs=[ANY] * n, out_specs=[ANY] * n,
        out_shape=[jax.ShapeDtypeStruct((N_DEV,) + a.shape, a.dtype) for a in arrs],
        scratch_shapes=[pltpu.SemaphoreType.DMA((n * 7,)), pltpu.SemaphoreType.DMA((n * 7,)),
                        pltpu.SemaphoreType.DMA((n,))],
        compiler_params=pltpu.CompilerParams(has_side_effects=True),
    )(*arrs)


def _sigmoid(x):
    return 1.0 / (1.0 + jnp.exp(-x))


def _log1p(e):
    u = 1.0 + e
    d = u - 1.0
    return jnp.where(d == 0.0, e, jnp.log(u) * (e / jnp.where(d == 0.0, 1.0, d)))


def _softplus(x):
    return jnp.maximum(x, 0.0) + _log1p(jnp.exp(-jnp.abs(x)))


_GELU_C = 0.7978845608028654


def _gelu(x):
    t = jnp.tanh(_GELU_C * (x + 0.044715 * x * x * x))
    return 0.5 * x * (1.0 + t)


def _gelu_and_grad(x):
    t = jnp.tanh(_GELU_C * (x + 0.044715 * x * x * x))
    g = 0.5 * x * (1.0 + t)
    dg = 0.5 * (1.0 + t) + 0.5 * x * (1.0 - t * t) * _GELU_C * (1.0 + 3.0 * 0.044715 * x * x)
    return g, dg


def _acc(ref, idx, val, first):
    @pl.when(first)
    def _():
        ref[idx] = val

    @pl.when(jnp.logical_not(first))
    def _():
        ref[idx] = ref[idx] + val


def _split3(x):
    hi = x.astype(BF16)
    r1 = x - hi.astype(F32)
    mid = r1.astype(BF16)
    lo = (r1 - mid.astype(F32)).astype(BF16)
    return hi, mid, lo


def _mm(a, b, *, name, ta=False, tb=False, out_dtype=F32, tm, tn, tk, bias=None, silu_a=False, exchange=None,
        b_cols=None):
    m, k = (a.shape[1], a.shape[0]) if ta else a.shape
    col0, n = b_cols if b_cols is not None else (0, b.shape[0] if tb else b.shape[1])
    assert m % tm == 0 and n % tn == 0 and k % tk == 0 and col0 % tn == 0, (name, m, n, k)
    jo = col0 // tn
    nk = k // tk
    dn = (((0 if ta else 1,), (1 if tb else 0,)), ((), ()))
    use_acc = nk > 1 and out_dtype != F32

    def body(*refs):
        a_ref, b_ref = refs[0], refs[1]
        bias_ref = refs[2] if bias is not None else None
        o_ref = refs[3 if bias is not None else 2]
        av = a_ref[...]
        if silu_a:
            av = av * _sigmoid(av)
        p = lax.dot_general(av.astype(BF16), b_ref[...].astype(BF16), dn, preferred_element_type=F32)
        if bias is not None:
            p = p + bias_ref[...]
        if nk == 1:
            o_ref[...] = p.astype(out_dtype)
            return
        kk = pl.program_id(2)
        acc = refs[-1] if use_acc else o_ref

        @pl.when(kk == 0)
        def _():
            acc[...] = p

        @pl.when(kk > 0)
        def _():
            acc[...] = acc[...] + p

        if use_acc:
            @pl.when(kk == nk - 1)
            def _():
                o_ref[...] = acc[...].astype(out_dtype)

    a_spec = pl.BlockSpec((tk, tm), lambda i, j, q: (q, i)) if ta else pl.BlockSpec((tm, tk), lambda i, j, q: (i, q))
    b_spec = (pl.BlockSpec((tn, tk), lambda i, j, q: (j + jo, q)) if tb
              else pl.BlockSpec((tk, tn), lambda i, j, q: (q, j + jo)))
    in_specs, args = [a_spec, b_spec], [a, b]
    if bias is not None:
        in_specs.append(pl.BlockSpec((1, tn), lambda i, j, q: (0, j)))
        args.append(bias)
    kw = dict(name=name, grid=(m // tm, n // tn, nk), in_specs=in_specs,
              out_specs=[pl.BlockSpec((tm, tn), lambda i, j, q: (i, j))],
              out_shape=[jax.ShapeDtypeStruct((m, n), out_dtype)],
              scratch_shapes=[pltpu.VMEM((tm, tn), F32)] if use_acc else [],
              compiler_params=_params())
    if exchange is None:
        return _call(body, **kw)(*args)[0]
    outs, got = _with_exchange(body, args, *exchange, **kw)
    return outs[0], got


def _tok_spec(ts):
    return pl.BlockSpec((1, ts, D_MODEL), lambda b, s: (b, s, 0))


def _seq_spec():
    return pl.BlockSpec((1, 1, D_MODEL), lambda b, s: (b, 0, 0))


def _vec_spec():
    return pl.BlockSpec((1, D_MODEL), lambda b, s: (0, 0))


def _norm_fwd(x, g, shift, scale, *, name, yprev=None, gate=None, ts=512):
    bsz, s, _ = x.shape
    res = yprev is not None

    def body(*refs):
        if res:
            x_ref, g_ref, sh_ref, sc_ref, y_ref, gt_ref, h_ref, xn_ref = refs
            xv = x_ref[0] + gt_ref[0] * y_ref[0]
            xn_ref[0] = xv
        else:
            x_ref, g_ref, sh_ref, sc_ref, h_ref = refs
            xv = x_ref[0]
        rstd = lax.rsqrt(jnp.mean(xv * xv, axis=-1, keepdims=True) + RMS_EPS)
        nv = xv * rstd * g_ref[...]
        h_ref[0] = (nv * (1.0 + sc_ref[0]) + sh_ref[0]).astype(BF16)

    in_specs = [_tok_spec(ts), _vec_spec(), _seq_spec(), _seq_spec()]
    args = [x, g, shift, scale]
    out_specs = [_tok_spec(ts)]
    out_shape = [jax.ShapeDtypeStruct(x.shape, BF16)]
    if res:
        in_specs += [_tok_spec(ts), _seq_spec()]
        args += [yprev, gate]
        out_specs.append(_tok_spec(ts))
        out_shape.append(jax.ShapeDtypeStruct(x.shape, F32))
    return _call(body, name=name, grid=(bsz, s // ts), in_specs=in_specs, out_specs=out_specs,
                 out_shape=out_shape, compiler_params=_params())(*args)


def _final(x1, yf, gate2, g, shift, scale, target, *, ts=512):
    bsz, s, _ = x1.shape
    ns = s // ts

    def body(x_ref, y_ref, gt_ref, g_ref, sh_ref, sc_ref, t_ref,
             lp_ref, dx_ref, dyf_ref, dsh_ref, dsc_ref, dgt_ref, dg_ref):
        b, si = pl.program_id(0), pl.program_id(1)
        yv, gt, gv = y_ref[0], gt_ref[0], g_ref[...]
        xv = x_ref[0] + gt * yv
        rstd = lax.rsqrt(jnp.mean(xv * xv, axis=-1, keepdims=True) + RMS_EPS)
        xhat = xv * rstd
        nv = xhat * gv
        sc1 = 1.0 + sc_ref[0]
        err = nv * sc1 + sh_ref[0] - t_ref[0]
        lp_ref[...] = jnp.zeros((1, 1, 8, 128), F32) + jnp.sum(err * err)
        dy = err * (1.0 / D_MODEL)
        dn = dy * sc1
        dxh = dn * gv
        dx = rstd * (dxh - xhat * jnp.mean(dxh * xhat, axis=-1, keepdims=True))
        dx_ref[0] = dx
        dyf_ref[0] = (dx * gt).astype(BF16)
        _acc(dsh_ref, 0, jnp.sum(dy, axis=0, keepdims=True), si == 0)
        _acc(dsc_ref, 0, jnp.sum(dy * nv, axis=0, keepdims=True), si == 0)
        _acc(dgt_ref, 0, jnp.sum(dx * yv, axis=0, keepdims=True), si == 0)
        _acc(dg_ref, slice(None), jnp.sum(dn * xhat, axis=0, keepdims=True), jnp.logical_and(b == 0, si == 0))

    seq_shape = jax.ShapeDtypeStruct((bsz, 1, D_MODEL), F32)
    return _call(
        body, name="final_loss", grid=(bsz, ns),
        in_specs=[_tok_spec(ts), _tok_spec(ts), _seq_spec(), _vec_spec(), _seq_spec(), _seq_spec(), _tok_spec(ts)],
        out_specs=[pl.BlockSpec((1, 1, 8, 128), lambda b, s: (b, s, 0, 0)), _tok_spec(ts), _tok_spec(ts),
                   _seq_spec(), _seq_spec(), _seq_spec(), _vec_spec()],
        out_shape=[jax.ShapeDtypeStruct((bsz, ns, 8, 128), F32), jax.ShapeDtypeStruct(x1.shape, F32),
                   jax.ShapeDtypeStruct(x1.shape, BF16), seq_shape, seq_shape, seq_shape,
                   jax.ShapeDtypeStruct((1, D_MODEL), F32)],
        compiler_params=_params(),
    )(x1, yf, gate2, g, shift, scale, target)


def _norm_bwd(x, dh, g, scale, dres, *, name, yprev=None, gate=None, ts=512):
    bsz, s, _ = x.shape
    res = yprev is not None

    def body(*refs):
        if res:
            (x_ref, dh_ref, g_ref, sc_ref, dr_ref, y_ref, gt_ref,
             dx_ref, dg_ref, dsh_ref, dsc_ref, dy_ref, dgt_ref) = refs
        else:
            x_ref, dh_ref, g_ref, sc_ref, dr_ref, dx_ref, dg_ref, dsh_ref, dsc_ref = refs
        b, si = pl.program_id(0), pl.program_id(1)
        xv, dhv, gv = x_ref[0], dh_ref[0], g_ref[...]
        rstd = lax.rsqrt(jnp.mean(xv * xv, axis=-1, keepdims=True) + RMS_EPS)
        xhat = xv * rstd
        dn = dhv * (1.0 + sc_ref[0])
        dxh = dn * gv
        dx = dr_ref[0] + rstd * (dxh - xhat * jnp.mean(dxh * xhat, axis=-1, keepdims=True))
        dx_ref[0] = dx
        _acc(dsh_ref, 0, jnp.sum(dhv, axis=0, keepdims=True), si == 0)
        _acc(dsc_ref, 0, jnp.sum(dhv * (xhat * gv), axis=0, keepdims=True), si == 0)
        _acc(dg_ref, slice(None), jnp.sum(dn * xhat, axis=0, keepdims=True), jnp.logical_and(b == 0, si == 0))
        if res:
            dy_ref[0] = (dx * gt_ref[0]).astype(BF16)
            _acc(dgt_ref, 0, jnp.sum(dx * y_ref[0], axis=0, keepdims=True), si == 0)

    seq_shape = jax.ShapeDtypeStruct((bsz, 1, D_MODEL), F32)
    in_specs = [_tok_spec(ts), _tok_spec(ts), _vec_spec(), _seq_spec(), _tok_spec(ts)]
    args = [x, dh, g, scale, dres]
    out_specs = [_tok_spec(ts), _vec_spec(), _seq_spec(), _seq_spec()]
    out_shape = [jax.ShapeDtypeStruct(x.shape, F32), jax.ShapeDtypeStruct((1, D_MODEL), F32), seq_shape, seq_shape]
    if res:
        in_specs += [_tok_spec(ts), _seq_spec()]
        args += [yprev, gate]
        out_specs += [_tok_spec(ts), _seq_spec()]
        out_shape += [jax.ShapeDtypeStruct(x.shape, BF16), seq_shape]
    return _call(body, name=name, grid=(bsz, s // ts), in_specs=in_specs, out_specs=out_specs,
                 out_shape=out_shape, compiler_params=_params())(*args)


def _lru_gates(xr, wa, ba, wi, bi, sp):
    xb = xr.astype(BF16)
    r = _sigmoid(jnp.dot(xb, wa, preferred_element_type=F32) + ba)
    i = _sigmoid(jnp.dot(xb, wi, preferred_element_type=F32) + bi)
    la = -LRU_C * r * sp
    a = jnp.exp(la)
    mult = jnp.sqrt(-jnp.tanh(la) * (a * a + 1.0))
    return r, i, a, mult


def _scan_forward(a, u, h_in, row):
    n = a.shape[0]
    r8 = row & 7
    for k in (1, 2, 4):
        m = r8 >= k
        u = jnp.where(m, a * pltpu.roll(u, k, 0) + u, u)
        a = jnp.where(m, a * pltpu.roll(a, k, 0), a)
    carry = jnp.broadcast_to(h_in, (8, a.shape[1]))
    out = []
    for j in range(n // 8):
        hj = u[8 * j:8 * j + 8] + a[8 * j:8 * j + 8] * carry
        out.append(hj)
        carry = jnp.broadcast_to(hj[7:8], hj.shape)
    return jnp.concatenate(out, axis=0)


def _scan_backward(a_up, g, g_in, row):
    n = g.shape[0]
    r8 = row & 7
    for k in (1, 2, 4):
        m = r8 < 8 - k
        g = jnp.where(m, a_up * pltpu.roll(g, n - k, 0) + g, g)
        a_up = jnp.where(m, a_up * pltpu.roll(a_up, n - k, 0), a_up)
    carry = jnp.broadcast_to(g_in, (8, g.shape[1]))
    out = [None] * (n // 8)
    for j in reversed(range(n // 8)):
        gj = g[8 * j:8 * j + 8] + a_up[8 * j:8 * j + 8] * carry
        out[j] = gj
        carry = jnp.broadcast_to(gj[0:1], gj.shape)
    return jnp.concatenate(out, axis=0)


def _rnn_fwd(z, wc, bc, wa, ba, wi, bi, lam, exchange=None):
    bsz, s, _ = z.shape
    ch = min(SCAN_CHUNK, s)
    nch = s // ch

    def body(z_ref, wc_ref, bc_ref, wa_ref, ba_ref, wi_ref, bi_ref, lam_ref, h_ref, y_ref, g_ref, gd_ref):
        wcv, bcv = wc_ref[...], bc_ref[...]
        wav, wiv = wa_ref[0].astype(BF16), wi_ref[0].astype(BF16)
        bav, biv = ba_ref[...], bi_ref[...]
        sp = _softplus(-lam_ref[...])
        row = lax.broadcasted_iota(jnp.int32, (ch, RNN_BLK), 0)

        def chunk(ci, carry):
            prev8, hc = carry
            st = pl.multiple_of(ci * ch, ch)
            blk = z_ref[0, pl.ds(st, ch), :]
            xp, gr = blk[:, :RNN_BLK], blk[:, RNN_BLK:]
            xe = jnp.concatenate([prev8, xp], axis=0)
            xr = bcv + wcv[3:4] * xp
            for j in range(1, 4):
                xr = xr + wcv[3 - j:4 - j] * pltpu.roll(xe, j, 0)[8:]
            r, i, a, mult = _lru_gates(xr, wav, bav, wiv, biv, sp)
            g_ref[0, pl.ds(st, ch), :] = jnp.concatenate([r, i, a, mult], axis=1)
            hh = _scan_forward(a, mult * (i * xr), hc, row)
            h_ref[0, pl.ds(st, ch), :] = hh
            ge, dge = _gelu_and_grad(gr)
            gd_ref[0, pl.ds(st, ch), :] = jnp.concatenate([ge, dge], axis=1).astype(BF16)
            y_ref[0, pl.ds(st, ch), :] = (ge * hh).astype(BF16)
            return xp[ch - 8:], hh[ch - 1:]

        lax.fori_loop(0, nch, chunk, (jnp.zeros((8, RNN_BLK), F32), jnp.zeros((1, RNN_BLK), F32)))

    vec = pl.BlockSpec((1, RNN_BLK), lambda b, n: (0, n))
    mat = pl.BlockSpec((1, RNN_BLK, RNN_BLK), lambda b, n: (n, 0, 0))
    out = pl.BlockSpec((1, s, RNN_BLK), lambda b, n: (b, 0, n))
    return _run(
        body, [z, wc, bc, wa, ba, wi, bi, lam], exchange, name="rnn_fwd", grid=(bsz, N_BLK),
        in_specs=[pl.BlockSpec((1, s, 2 * RNN_BLK), lambda b, n: (b, 0, XG0R // (2 * RNN_BLK) + n)),
                  pl.BlockSpec((4, RNN_BLK), lambda b, n: (0, n)), vec, mat, vec, mat, vec, vec],
        out_specs=[out, out, pl.BlockSpec((1, s, 4 * RNN_BLK), lambda b, n: (b, 0, n)),
                   pl.BlockSpec((1, s, 2 * RNN_BLK), lambda b, n: (b, 0, n))],
        out_shape=[jax.ShapeDtypeStruct((bsz, s, D_RNN), F32), jax.ShapeDtypeStruct((bsz, s, D_RNN), BF16),
                   jax.ShapeDtypeStruct((bsz, s, 4 * D_RNN), F32), jax.ShapeDtypeStruct((bsz, s, 2 * D_RNN), BF16)],
        compiler_params=_params())


def _rnn_bwd(z, h, gates, gelus, dy, dz, wc, bc, wa, wi, lam, exchange=None):
    bsz, s, _ = z.shape
    ch = min(SCAN_CHUNK, s)
    nch = s // ch

    def body(z_ref, h_ref, g_ref, gd_ref, dy_ref, dz_in, wc_ref, bc_ref, wa_ref, wi_ref, lam_ref,
             dz_ref, dwa_ref, dwi_ref, dsm_ref):
        del dz_in
        b = pl.program_id(1)
        wcv, bcv = wc_ref[...], bc_ref[...]
        wav, wiv = wa_ref[0].astype(BF16), wi_ref[0].astype(BF16)
        lamv = lam_ref[...]
        sp = _softplus(-lamv)
        row = lax.broadcasted_iota(jnp.int32, (ch, RNN_BLK), 0)

        @pl.when(b == 0)
        def _():
            dwa_ref[...] = jnp.zeros_like(dwa_ref)
            dwi_ref[...] = jnp.zeros_like(dwi_ref)
            dsm_ref[...] = jnp.zeros_like(dsm_ref)

        def chunk(k, carry):
            a_next, g_next, dxr_next8 = carry
            ci = nch - 1 - k
            st = pl.multiple_of(ci * ch, ch)
            pst = pl.multiple_of(jnp.maximum(st - 8, 0), 8)
            has_prev = (ci > 0).astype(F32)
            xp = z_ref[0, pl.ds(st, ch), :][:, :RNN_BLK]
            xprev8 =z_ref[0, pl.ds(pst, 8), :][:, :RNN_BLK] * has_prev
            hprev8 = h_ref[0, pl.ds(pst, 8), :] * has_prev
            xe = jnp.concatenate([xprev8, xp], axis=0)
            xs = [xp] + [pltpu.roll(xe, j, 0)[8:] for j in range(1, 4)]
            xr = bcv
            for kk in range(4):
                xr = xr + wcv[kk:kk + 1] * xs[3 - kk]
            gs = g_ref[0, pl.ds(st, ch), :]
            r, i, a, mult = [gs[:, q * RNN_BLK:(q + 1) * RNN_BLK] for q in range(4)]
            hh = h_ref[0, pl.ds(st, ch), :]
            hm1 = pltpu.roll(jnp.concatenate([hprev8, hh], axis=0), 1, 0)[8:]
            dyv = dy_ref[0, pl.ds(st, ch), :]
            gd = gd_ref[0, pl.ds(st, ch), :].astype(F32)
            ge, dge = gd[:, :RNN_BLK], gd[:, RNN_BLK:]
            dgr = dyv * hh * dge
            gg = _scan_backward(jnp.where(row == ch - 1, a_next, pltpu.roll(a, ch - 1, 0)), dyv * ge, g_next, row)
            da = gg * hm1
            ix = i * xr
            dmult = gg * ix
            di = gg * mult * xr
            dxr = gg * mult * i
            dla = da * a - dmult * (a * a) / mult
            dr = dla * (-LRU_C * sp)
            dsp = jnp.sum(dla * (-LRU_C * r), axis=0, keepdims=True)
            dpa = dr * r * (1.0 - r)
            dpi = di * i * (1.0 - i)
            dpab, dpib, xrb = dpa.astype(BF16), dpi.astype(BF16), xr.astype(BF16)
            nt = (((1,), (1,)), ((), ()))
            tn = (((0,), (0,)), ((), ()))
            dxr = dxr + lax.dot_general(dpab, wav, nt, preferred_element_type=F32)
            dxr = dxr + lax.dot_general(dpib, wiv, nt, preferred_element_type=F32)
            dwa_ref[0] = dwa_ref[0] + lax.dot_general(xrb, dpab, tn, preferred_element_type=F32)
            dwi_ref[0] = dwi_ref[0] + lax.dot_general(xrb, dpib, tn, preferred_element_type=F32)
            rows = [jnp.sum(dxr * xs[3 - q], axis=0, keepdims=True) for q in range(4)]
            rows += [jnp.sum(dxr, axis=0, keepdims=True), jnp.sum(dpa, axis=0, keepdims=True),
                     jnp.sum(dpi, axis=0, keepdims=True), dsp * (-_sigmoid(-lamv))]
            dsm_ref[...] = dsm_ref[...] + jnp.concatenate(rows, axis=0)
            dxe = jnp.concatenate([dxr, dxr_next8], axis=0)
            dxp = wcv[3:4] * dxr
            for j in range(1, 4):
                dxp = dxp + wcv[3 - j:4 - j] * pltpu.roll(dxe, ch + 8 - j, 0)[:ch]
            dz_ref[0, pl.ds(st, ch), :] = jnp.concatenate([dxp, dgr], axis=1).astype(BF16)
            return a[0:1], gg[0:1], dxr[0:8]

        lax.fori_loop(0, nch, chunk, (jnp.zeros((1, RNN_BLK), F32), jnp.zeros((1, RNN_BLK), F32),
                                      jnp.zeros((8, RNN_BLK), F32)))

    vec = pl.BlockSpec((1, RNN_BLK), lambda n, b: (0, n))
    mat = pl.BlockSpec((1, RNN_BLK, RNN_BLK), lambda n, b: (n, 0, 0))
    tok = pl.BlockSpec((1, s, RNN_BLK), lambda n, b: (b, 0, n))
    zblk = pl.BlockSpec((1, s, 2 * RNN_BLK), lambda n, b: (b, 0, XG0R // (2 * RNN_BLK) + n))
    dzblk = pl.BlockSpec((1, s, 2 * RNN_BLK), lambda n, b: (b, 0, XG0 // (2 * RNN_BLK) + n))
    return _run(
        body, [z, h, gates, gelus, dy, dz, wc, bc, wa, wi, lam], exchange, name="rnn_bwd", grid=(N_BLK, bsz),
        in_specs=[zblk, tok, pl.BlockSpec((1, s, 4 * RNN_BLK), lambda n, b: (b, 0, n)),
                  pl.BlockSpec((1, s, 2 * RNN_BLK), lambda n, b: (b, 0, n)), tok, ANY,
                  pl.BlockSpec((4, RNN_BLK), lambda n, b: (0, n)), vec, mat, mat, vec],
        out_specs=[dzblk, mat, mat, pl.BlockSpec((8, RNN_BLK), lambda n, b: (0, n))],
        out_shape=[jax.ShapeDtypeStruct(dz.shape, dz.dtype), jax.ShapeDtypeStruct((N_BLK, RNN_BLK, RNN_BLK), F32),
                   jax.ShapeDtypeStruct((N_BLK, RNN_BLK, RNN_BLK), F32), jax.ShapeDtypeStruct((8, D_RNN), F32)],
        input_output_aliases={5: 0},
        compiler_params=_params())


def _tri(n, upper):
    r = lax.broadcasted_iota(jnp.int32, (n, n), 0)
    c = lax.broadcasted_iota(jnp.int32, (n, n), 1)
    return (r <= c if upper else r >= c).astype(BF16)


def _exact_dot(x, t):
    hi, mid, lo = _split3(x)
    return (jnp.dot(hi, t, preferred_element_type=F32) + jnp.dot(mid, t, preferred_element_type=F32)
            + jnp.dot(lo, t, preferred_element_type=F32))


def _fgate_fwd(flt, bfg):
    bsz, nh, s = flt.shape
    blk = min(CUM_BLK, s)

    def body(fl_ref, b_ref, f_ref):
        tri = _tri(blk, True)
        carry = jnp.zeros((nh, 1), F32)
        for q in range(s // blk):
            xq = fl_ref[0, :, q * blk:(q + 1) * blk] + b_ref[...]
            fq = _exact_dot(-_softplus(-xq), tri) + carry
            f_ref[0, :, q * blk:(q + 1) * blk] = fq
            carry = fq[:, blk - 1:blk]

    spec = pl.BlockSpec((1, nh, s), lambda b: (b, 0, 0))
    return _call(body, name="fgate_fwd", grid=(bsz,), in_specs=[spec, pl.BlockSpec((nh, 1), lambda b: (0, 0))],
                 out_specs=spec, out_shape=jax.ShapeDtypeStruct(flt.shape, F32), compiler_params=_params())(flt, bfg)


def _fgate_bwd(dft, flt, bfg):
    bsz, nh, s = flt.shape
    blk = min(CUM_BLK, s)

    def body(df_ref, fl_ref, b_ref, dfl_ref, db_ref):
        b = pl.program_id(0)
        tri = _tri(blk, False)
        carry = jnp.zeros((nh, 1), F32)
        tot = jnp.zeros((nh, 1), F32)
        for q in reversed(range(s // blk)):
            sl = slice(q * blk, (q + 1) * blk)
            dlq = _exact_dot(df_ref[0, :, sl], tri) + carry
            carry = dlq[:, 0:1]
            dflq = dlq * _sigmoid(-(fl_ref[0, :, sl] + b_ref[...]))
            dfl_ref[0, :, sl] = dflq
            tot = tot + jnp.sum(dflq, axis=1, keepdims=True)
        _acc(db_ref, slice(None), tot, b == 0)

    spec = pl.BlockSpec((1, nh, s), lambda b: (b, 0, 0))
    small = pl.BlockSpec((nh, 1), lambda b: (0, 0))
    return _call(body, name="fgate_bwd", grid=(bsz,), in_specs=[spec, spec, small], out_specs=[spec, small],
                 out_shape=[jax.ShapeDtypeStruct(flt.shape, F32), jax.ShapeDtypeStruct((nh, 1), F32)],
                 compiler_params=_params())(dft, flt, bfg)


_NT = (((1,), (1,)), ((), ()))
_TN = (((0,), (0,)), ((), ()))


def _attn_block_scores(q_b, z_ref, frow_ref, k0, tq):
    kk = z_ref[0, k0:k0 + tq, HEAD_DIM:2 * HEAD_DIM]
    return lax.dot_general(q_b, kk, _NT, preferred_element_type=F32) - frow_ref[0, 0, :, k0:k0 + tq]


def _causal(tq):
    return lax.broadcasted_iota(jnp.int32, (tq, tq), 0) >= lax.broadcasted_iota(jnp.int32, (tq, tq), 1)


def _attn_fwd(z, fcol, frow, exchange=None):
    bsz, s, _ = z.shape
    tq = min(ATTN_TQ, s)
    scale = HEAD_DIM ** -0.5

    def body(z_ref, fcol_ref, frow_ref, o_ref, lse_ref):
        for qb in range(s // tq):
            q0 = qb * tq
            q_b = (z_ref[0, q0:q0 + tq, :HEAD_DIM].astype(F32) * scale).astype(BF16)
            m = jnp.full((tq, 1), -jnp.inf, F32)
            l = jnp.zeros((tq, 1), F32)
            acc = jnp.zeros((tq, HEAD_DIM), F32)
            for kb in range(qb + 1):
                k0 = kb * tq
                sc = _attn_block_scores(q_b, z_ref, frow_ref, k0, tq)
                if kb == qb:
                    sc = jnp.where(_causal(tq), sc, -jnp.inf)
                m_new = jnp.maximum(m, jnp.max(sc, axis=-1, keepdims=True))
                alpha = jnp.exp(m - m_new)
                p = jnp.exp(sc - m_new)
                l = alpha * l + jnp.sum(p, axis=-1, keepdims=True)
                vv = z_ref[0, k0:k0 + tq, 2 * HEAD_DIM:]
                acc = alpha * acc + jnp.dot(p.astype(BF16), vv, preferred_element_type=F32)
                m = m_new
            o_ref[0, q0:q0 + tq, :] = (acc / l).astype(BF16)
            lse_ref[0, 0, q0:q0 + tq, :] = m + jnp.log(l) + fcol_ref[0, 0, q0:q0 + tq, :]

    return _run(
        body, [z, fcol, frow], exchange, name="attn_fwd", grid=(bsz, N_HEADS),
        in_specs=[pl.BlockSpec((1, s, 3 * HEAD_DIM), lambda b, h: (b, 0, h)),
                  pl.BlockSpec((1, 1, s, 1), lambda b, h: (b, h, 0, 0)),
                  pl.BlockSpec((1, 1, 1, s), lambda b, h: (b, h, 0, 0))],
        out_specs=[pl.BlockSpec((1, s, HEAD_DIM), lambda b, h: (b, 0, h)),
                   pl.BlockSpec((1, 1, s, 1), lambda b, h: (b, h, 0, 0))],
        out_shape=[jax.ShapeDtypeStruct((bsz, s, N_HEADS * HEAD_DIM), BF16),
                   jax.ShapeDtypeStruct((bsz, N_HEADS, s, 1), F32)],
        compiler_params=_params())


def _attn_bwd(z, do, lse, fcol, frow, dz, exchange=None):
    bsz, s, _ = z.shape
    tq = min(ATTN_TQ, s)
    scale = HEAD_DIM ** -0.5

    def body(z_ref, do_ref, lse_ref, fcol_ref, frow_ref, dz_in, dz_ref, df_ref, dk_acc, dv_acc):
        del dz_in
        dk_acc[...] = jnp.zeros_like(dk_acc)
        dv_acc[...] = jnp.zeros_like(dv_acc)
        df_ref[...] = jnp.zeros_like(df_ref)
        for qb in range(s // tq):
            q0, ln = qb * tq, (qb + 1) * tq
            q_b = (z_ref[0, q0:ln, :HEAD_DIM].astype(F32) * scale).astype(BF16)
            kk = z_ref[0, :ln, HEAD_DIM:2 * HEAD_DIM]
            vv = z_ref[0, :ln, 2 * HEAD_DIM:]
            do_b = do_ref[0, q0:ln, :]
            lse_q = lse_ref[0, 0, q0:ln, :] - fcol_ref[0, 0, q0:ln, :]
            sc = lax.dot_general(q_b, kk, _NT, preferred_element_type=F32) - frow_ref[0, 0, :, :ln] - lse_q
            mask = (q0 + lax.broadcasted_iota(jnp.int32, (tq, ln), 0)) >= lax.broadcasted_iota(jnp.int32, (tq, ln), 1)
            p = jnp.where(mask, jnp.exp(sc), 0.0)
            pb = p.astype(BF16)
            dv_acc[:ln, :] = dv_acc[:ln, :] + lax.dot_general(pb, do_b, _TN, preferred_element_type=F32)
            dp = lax.dot_general(do_b, vv, _NT, preferred_element_type=F32)
            ds = p * (dp - jnp.sum(p * dp, axis=-1, keepdims=True))
            dsb = ds.astype(BF16)
            dq = jnp.dot(dsb, kk, preferred_element_type=F32) * scale
            dz_ref[0, q0:ln, :HEAD_DIM] = dq.astype(BF16)
            dk_acc[:ln, :] = dk_acc[:ln, :] + lax.dot_general(dsb, q_b, _TN, preferred_element_type=F32)
            df_ref[0, 0, :, :ln] = df_ref[0, 0, :, :ln] - jnp.sum(ds, axis=0, keepdims=True)
        dz_ref[0, :, HEAD_DIM:2 * HEAD_DIM] = dk_acc[...].astype(BF16)
        dz_ref[0, :, 2 * HEAD_DIM:] = dv_acc[...].astype(BF16)

    zblk = pl.BlockSpec((1, s, 3 * HEAD_DIM), lambda b, h: (b, 0, h))
    hblk = pl.BlockSpec((1, s, HEAD_DIM), lambda b, h: (b, 0, h))
    col = pl.BlockSpec((1, 1, s, 1), lambda b, h: (b, h, 0, 0))
    rowspec = pl.BlockSpec((1, 1, 1, s), lambda b, h: (b, h, 0, 0))
    return _run(
        body, [z, do, lse, fcol, frow, dz], exchange, name="attn_bwd", grid=(bsz, N_HEADS),
        in_specs=[zblk, hblk, col, col, rowspec, ANY],
        out_specs=[zblk, rowspec],
        out_shape=[jax.ShapeDtypeStruct(dz.shape, dz.dtype), jax.ShapeDtypeStruct((bsz, N_HEADS, 1, s), F32)],
        scratch_shapes=[pltpu.VMEM((s, HEAD_DIM), F32), pltpu.VMEM((s, HEAD_DIM), F32)],
        input_output_aliases={5: 0},
        compiler_params=_params())


def _merge_fwd(z2, pr, pa, *, tr=512):
    t = z2.shape[0]

    def body(mgr_ref, mga_ref, pr_ref, pa_ref, o_ref):
        gr_ = _sigmoid(mgr_ref[...])
        ga = _sigmoid(mga_ref[...])
        o_ref[...] = (gr_ * pr_ref[...] + ga * pa_ref[...]).astype(BF16)

    tok = pl.BlockSpec((tr, D_MODEL), lambda i: (i, 0))
    return _call(body, name="merge_fwd", grid=(t // tr,),
                 in_specs=[pl.BlockSpec((tr, D_MODEL), lambda i: (i, MG0R // D_MODEL)),
                           pl.BlockSpec((tr, D_MODEL), lambda i: (i, MG0R // D_MODEL + 1)), tok, tok],
                 out_specs=tok, out_shape=jax.ShapeDtypeStruct((t, D_MODEL), BF16),
                 compiler_params=_params())(z2, z2, pr, pa)


def _merge_bwd(dm, z2, pr, pa, *, tr=512):
    t = z2.shape[0]

    def body(dm_ref, mgr_ref, mga_ref, pr_ref, pa_ref, dz_ref, dpr_ref, dpa_ref):
        dmv = dm_ref[...]
        gr_ = _sigmoid(mgr_ref[...])
        ga = _sigmoid(mga_ref[...])
        dpr_ref[...] = (dmv * gr_).astype(BF16)
        dpa_ref[...] = (dmv * ga).astype(BF16)
        dz_ref[:, :D_MODEL] = (dmv * pr_ref[...] * gr_ * (1.0 - gr_)).astype(BF16)
        dz_ref[:, D_MODEL:] = (dmv * pa_ref[...] * ga * (1.0 - ga)).astype(BF16)

    tok = pl.BlockSpec((tr, D_MODEL), lambda i: (i, 0))
    mg = pl.BlockSpec((tr, 2 * D_MODEL), lambda i: (i, MG0 // (2 * D_MODEL)))
    return _call(body, name="merge_bwd", grid=(t // tr,),
                 in_specs=[tok, pl.BlockSpec((tr, D_MODEL), lambda i: (i, MG0R // D_MODEL)),
                           pl.BlockSpec((tr, D_MODEL), lambda i: (i, MG0R // D_MODEL + 1)), tok, tok],
                 out_specs=[mg, tok, tok],
                 out_shape=[jax.ShapeDtypeStruct((t, NZ), BF16), jax.ShapeDtypeStruct((t, D_MODEL), BF16),
                            jax.ShapeDtypeStruct((t, D_MODEL), BF16)],
                 compiler_params=_params())(dm, z2, z2, pr, pa)


def _fl_write(dfl_blk, dz):
    bsz, s, w = dfl_blk.shape

    def body(src_ref, dz_in, dz_ref):
        del dz_in
        dz_ref[...] = src_ref[...]

    return _call(body, name="fl_write", grid=(bsz,),
                 in_specs=[pl.BlockSpec((1, s, w), lambda b: (b, 0, 0)), ANY],
                 out_specs=pl.BlockSpec((1, s, w), lambda b: (b, 0, FL0 // w)),
                 out_shape=jax.ShapeDtypeStruct(dz.shape, dz.dtype), input_output_aliases={1: 0},
                 compiler_params=_params())(dfl_blk, dz)


def _ffn_fwd(zf, wc, bc):
    bsz, s, _ = zf.shape
    cw = 128

    def body(zf_ref, wc_ref, bc_ref, o_ref):
        gfp, uf = zf_ref[0, :, :cw].astype(F32), zf_ref[0, :, cw:].astype(F32)
        wcv = wc_ref[...]
        row = lax.broadcasted_iota(jnp.int32, (s, cw), 0)
        gf = bc_ref[...] + wcv[2:3] * gfp
        for j in (1, 2):
            gf = gf + wcv[2 - j:3 - j] * jnp.where(row >= j, pltpu.roll(gfp, j, 0), 0.0)
        o_ref[0] = (_gelu(gf) * uf).astype(BF16)

    return _call(body, name="ffn_fwd", grid=(bsz, D_FF // cw),
                 in_specs=[pl.BlockSpec((1, s, 2 * cw), lambda b, j: (b, 0, j)),
                           pl.BlockSpec((3, cw), lambda b, j: (0, j)), pl.BlockSpec((1, cw), lambda b, j: (0, j))],
                 out_specs=pl.BlockSpec((1, s, cw), lambda b, j: (b, 0, j)),
                 out_shape=jax.ShapeDtypeStruct((bsz, s, D_FF), BF16), compiler_params=_params())(zf, wc, bc)


def _ffn_bwd(dact, zf, wc, bc, exchange=None):
    bsz, s, _ = zf.shape
    cw = 128

    def body(da_ref, zf_ref, wc_ref, bc_ref, dzf_ref, dsm_ref):
        b = pl.program_id(1)
        gfp, uf = zf_ref[0, :, :cw].astype(F32), zf_ref[0, :, cw:].astype(F32)
        wcv = wc_ref[...]
        row = lax.broadcasted_iota(jnp.int32, (s, cw), 0)
        sh = [gfp] + [jnp.where(row >= j, pltpu.roll(gfp, j, 0), 0.0) for j in (1, 2)]
        gf = bc_ref[...]
        for kk in range(3):
            gf = gf + wcv[kk:kk + 1] * sh[2 - kk]
        ge, dge = _gelu_and_grad(gf)
        dav = da_ref[0].astype(F32)
        dgf = dav * uf * dge
        dgfp = wcv[2:3] * dgf
        for j in (1, 2):
            dgfp = dgfp + wcv[2 - j:3 - j] * jnp.where(row < s - j, pltpu.roll(dgf, s - j, 0), 0.0)
        dzf_ref[0] = jnp.concatenate([dgfp, dav * ge], axis=1).astype(BF16)
        rows = [jnp.sum(dgf * sh[2 - kk], axis=0, keepdims=True) for kk in range(3)]
        rows += [jnp.sum(dgf, axis=0, keepdims=True), jnp.zeros((4, cw), F32)]
        _acc(dsm_ref, slice(None), jnp.concatenate(rows, axis=0), b == 0)

    return _run(body, [dact, zf, wc, bc], exchange, name="ffn_bwd", grid=(D_FF // cw, bsz),
                in_specs=[pl.BlockSpec((1, s, cw), lambda j, b: (b, 0, j)),
                          pl.BlockSpec((1, s, 2 * cw), lambda j, b: (b, 0, j)),
                          pl.BlockSpec((3, cw), lambda j, b: (0, j)), pl.BlockSpec((1, cw), lambda j, b: (0, j))],
                out_specs=[pl.BlockSpec((1, s, 2 * cw), lambda j, b: (b, 0, j)),
                           pl.BlockSpec((8, cw), lambda j, b: (0, j))],
                out_shape=[jax.ShapeDtypeStruct(zf.shape, BF16), jax.ShapeDtypeStruct((8, D_FF), F32)],
                compiler_params=_params())


def _adamw(w, m, v, parts, *, name):
    r, c = w.shape
    p = parts.shape[0]
    by_cols = r > 512 and r % 256 != 0
    tr, tc = (r, 256) if by_cols else (r if r <= 512 else 256, c)
    assert r % tr == 0 and c % tc == 0 and parts.shape[1:] == (r, c), (name, w.shape, parts.shape)
    c1 = 1.0 / (1.0 - ADAM_B1 ** ADAM_STEP)
    c2 = 1.0 / (1.0 - ADAM_B2 ** ADAM_STEP)

    def body(w_ref, m_ref, v_ref, p_ref, g_ref, d_ref, mo_ref, vo_ref):
        g = p_ref[0].astype(F32)
        for q in range(1, p):
            g = g + p_ref[q].astype(F32)
        mn = ADAM_B1 * m_ref[...] + (1.0 - ADAM_B1) * g
        vn = ADAM_B2 * v_ref[...] + (1.0 - ADAM_B2) * (g * g)
        g_ref[...] = g
        mo_ref[...] = mn
        vo_ref[...] = vn
        d_ref[...] = -ADAM_LR * ((mn * c1) / (jnp.sqrt(vn * c2) + ADAM_EPS) + ADAM_WD * w_ref[...])

    spec = pl.BlockSpec((tr, tc), (lambda i: (0, i)) if by_cols else (lambda i: (i, 0)))
    shp = jax.ShapeDtypeStruct((r, c), F32)
    return _call(body, name=name, grid=(c // tc if by_cols else r // tr,),
                 in_specs=[spec, spec, spec,
                           pl.BlockSpec((p, tr, tc), (lambda i: (0, 0, i)) if by_cols else (lambda i: (0, i, 0)))],
                 out_specs=[spec] * 4, out_shape=[shp] * 4, compiler_params=_params())(w, m, v, parts)


def _ref_slab(i):
    nq = 3 * N_HEADS
    j = i - nq
    return jnp.where(i < nq, 2 * D_RNN // 128 + (i % 3) * N_HEADS + i // 3, (j % 2) * N_BLK + j // 2)


def _permute_slabs(x, to_z, name):
    n = FL0 // 128
    cols = x.shape[1]

    def body(x_ref, o_ref):
        o_ref[...] = x_ref[...]

    there = pl.BlockSpec((128, cols), lambda i: (_ref_slab(i), 0))
    here = pl.BlockSpec((128, cols), lambda i: (i, 0))
    return _call(body, name=name, grid=(n,), in_specs=[there if to_z else here],
                 out_specs=here if to_z else there, out_shape=jax.ShapeDtypeStruct((FL0, cols), x.dtype),
                 compiler_params=_params())(x)


def _to_z_layout(wt):
    fl0 = 2 * D_RNN + 3 * D_MODEL
    pad = jnp.zeros((MG0 - FL0 - N_HEADS, wt.shape[1]), wt.dtype)
    return jnp.concatenate([_permute_slabs(wt, True, "to_z_layout"), wt[fl0:fl0 + N_HEADS], pad,
                            wt[fl0 + N_HEADS:]], axis=0)


def _from_z_layout(wt):
    return jnp.concatenate([_permute_slabs(wt, False, "from_z_layout"), wt[FL0:FL0 + N_HEADS], wt[MG0:]], axis=0)


def _col_gather(g):
    return g.transpose(1, 0, 2).reshape(g.shape[1], -1)


def _col_scatter(w):
    return w.reshape(w.shape[0], N_DEV, -1).transpose(1, 0, 2)


def kernel(x, c, w_ada, b_ada, g_norm1, w_in, w_rnn_conv, b_rnn_conv, w_lru_a, b_lru_a, w_lru_i, b_lru_i, lru_lambda, b_fgate, w_proj_rnn, w_proj_attn, w_out, g_norm2, w_ffn_up, w_ffn_conv, b_ffn_conv, w_ffn_down, w_ada_final, b_ada_final, g_final, loss_target, m_w_ada, m_b_ada, m_g_norm1, m_w_in, m_w_rnn_conv, m_b_rnn_conv, m_w_lru_a, m_b_lru_a, m_w_lru_i, m_b_lru_i, m_lru_lambda, m_b_fgate, m_w_proj_rnn, m_w_proj_attn, m_w_out, m_g_norm2, m_w_ffn_up, m_w_ffn_conv, m_b_ffn_conv, m_w_ffn_down, m_w_ada_final, m_b_ada_final, m_g_final, v_w_ada, v_b_ada, v_g_norm1, v_w_in, v_w_rnn_conv, v_b_rnn_conv, v_w_lru_a, v_b_lru_a, v_w_lru_i, v_b_lru_i, v_lru_lambda, v_b_fgate, v_w_proj_rnn, v_w_proj_attn, v_w_out, v_g_norm2, v_w_ffn_up, v_w_ffn_conv, v_b_ffn_conv, v_w_ffn_down, v_w_ada_final, v_b_ada_final, v_g_final):
    args = dict(locals())
    bsz, s, _ = x.shape
    t = bsz * s
    nb = N_DEV * bsz
    me = 4 * lax.axis_index("x") + 2 * lax.axis_index("y") + lax.axis_index("c")
    tm = min(2048, t)

    ex_a = _gather_two_level([c, w_in[0].T.astype(BF16), w_rnn_conv[0], w_ffn_conv[0]], "gather_first")
    c_all = ex_a[0].reshape(nb, D_MODEL)
    win_zt = _to_z_layout(ex_a[1].reshape(-1, D_MODEL))
    wrc = _col_gather(ex_a[2])
    wfc = _col_gather(ex_a[3])

    nmod = w_ada.shape[2]
    nmodf = w_ada_final.shape[1]
    mod_cols = _mm(c_all, w_ada[0], name="mod_cols", tm=nb, tn=nmod, tk=D_MODEL, silu_a=True,
                   bias=lax.dynamic_slice(b_ada, (0, me * nmod), (1, nmod)))
    modf_cols = _mm(c_all, w_ada_final, name="modf_cols", tm=nb, tn=nmodf, tk=D_MODEL, silu_a=True,
                    bias=lax.dynamic_slice(b_ada_final.reshape(1, -1), (0, me * nmodf), (1, nmodf)))
    ex_b = _exchange([mod_cols, modf_cols], [True, True], "gather_mod")
    mod = lax.dynamic_slice(ex_b[0], (0, me * bsz, 0), (N_DEV, bsz, nmod)).transpose(1, 0, 2).reshape(bsz, 6, 1, D_MODEL)
    modf = lax.dynamic_slice(ex_b[1], (0, me * bsz, 0), (N_DEV, bsz, nmodf)).transpose(1, 0, 2).reshape(bsz, 2, 1, D_MODEL)
    shift1, scale1, gate1, shift2, scale2, gate2 = [mod[:, i] for i in range(6)]
    shift_f, scale_f = modf[:, 0], modf[:, 1]

    h1 = _norm_fwd(x, g_norm1, shift1, scale1, name="norm1_fwd")[0]
    h1f = h1.reshape(t, D_MODEL)
    zq = _mm(h1f, win_zt, name="mm_in_qkv", tb=True, out_dtype=BF16, tm=tm, tn=1024, tk=D_MODEL,
             b_cols=(QKV0, XG0)).reshape(bsz, s, XG0)
    zr = _mm(h1f, win_zt, name="mm_in_rest", tb=True, tm=tm, tn=1024, tk=D_MODEL,
             b_cols=(XG0, NZR)).reshape(bsz, s, NZR)
    h_rnn, y_rnn, lru_gates, lru_gelus, got = _rnn_fwd(zr, wrc, b_rnn_conv, w_lru_a[0], b_lru_a, w_lru_i[0], b_lru_i, lru_lambda,
                                 exchange=([w_ffn_up[0].astype(BF16)], ["gather_zf"]))
    wup_z = got[0]
    flt = zr[:, :, FL0R:FL0R + N_HEADS].transpose(0, 2, 1)
    bfg = b_fgate.reshape(N_HEADS, 1)
    fcum = _fgate_fwd(flt, bfg)
    fcol = fcum.reshape(bsz, N_HEADS, s, 1)
    frow = fcum.reshape(bsz, N_HEADS, 1, s)
    o_attn, lse, got = _attn_fwd(
        zq, fcol, frow, exchange=([w_proj_rnn[0].astype(BF16), w_proj_attn[0].astype(BF16), w_out[0].astype(BF16),
                                  w_ffn_down[0].astype(BF16)], [True] * 4))
    wpr = got[0].reshape(D_RNN, D_MODEL)
    wpa = got[1].reshape(D_MODEL, D_MODEL)
    wout = got[2].reshape(D_MODEL, D_MODEL)
    wdown = got[3].reshape(D_FF, D_MODEL)
    pr = _mm(y_rnn.reshape(t, D_RNN), wpr, name="mm_proj_rnn", out_dtype=BF16, tm=tm, tn=D_MODEL, tk=D_RNN)
    pa = _mm(o_attn.reshape(t, D_MODEL), wpa, name="mm_proj_attn", out_dtype=BF16, tm=tm, tn=D_MODEL, tk=D_MODEL)
    z2 = zr.reshape(t, NZR)
    merged = _merge_fwd(z2, pr, pa)
    mo = _mm(merged, wout, name="mm_out", out_dtype=BF16, tm=tm, tn=D_MODEL, tk=D_MODEL).reshape(bsz, s, D_MODEL)
    h2, x1 = _norm_fwd(x, g_norm2, shift2, scale2, name="norm2_fwd", yprev=mo, gate=gate1)
    zf = _mm(h2.reshape(t, D_MODEL), wup_z, name="mm_up", out_dtype=BF16, tm=tm, tn=1024,
             tk=D_MODEL).reshape(bsz, s, 2 * D_FF)
    act = _ffn_fwd(zf, wfc, b_ffn_conv)
    yf = _mm(act.reshape(t, D_FF), wdown, name="mm_down", out_dtype=BF16, tm=tm, tn=D_MODEL,
             tk=D_FF // 2).reshape(bsz, s, D_MODEL)
    lp, dx2, dyf, dshift_f, dscale_f, dgate2, dg_final = _final(x1, yf, gate2, g_final.reshape(1, -1), shift_f, scale_f, loss_target)
    loss = lax.psum(jnp.sum(lp[:, :, 0, 0]) * (0.5 / D_MODEL), ("x", "y", "c"))

    dyf2 = dyf.reshape(t, D_MODEL)
    act2 = act.reshape(t, D_FF)
    dact = _mm(dyf2, wdown, name="mm_down_dx", tb=True, out_dtype=BF16, tm=tm, tn=1024, tk=D_MODEL).reshape(bsz, s, D_FF)
    g_wdown = _mm(act2, dyf2, name="mm_down_dw", ta=True, out_dtype=BF16, tm=1024, tn=D_MODEL, tk=tm)
    dzf, dsm_ffn, got = _ffn_bwd(dact, zf, wfc, b_ffn_conv, exchange=([g_wdown.reshape(N_DEV, -1, D_MODEL)], [False]))
    p_wdown = got[0]
    dzf2 = dzf.reshape(t, 2 * D_FF)
    dh2 = _mm(dzf2, wup_z, name="mm_up_dx", tb=True, tm=tm, tn=D_MODEL, tk=2048).reshape(bsz, s, D_MODEL)
    g_wup_z = _mm(h2.reshape(t, D_MODEL), dzf2, name="mm_up_dw", ta=True, out_dtype=BF16, tm=D_MODEL, tn=1024, tk=tm)
    dx1, dg_norm2, dshift2, dscale2, dmo, dgate1 = _norm_bwd(x1, dh2, g_norm2, scale2, dx2, name="norm2_bwd", yprev=mo, gate=gate1)
    dmo2 = dmo.reshape(t, D_MODEL)
    dmerged = _mm(dmo2, wout, name="mm_out_dx", tb=True, out_dtype=BF16, tm=tm, tn=D_MODEL, tk=D_MODEL)
    g_wout = _mm(merged, dmo2, name="mm_out_dw", ta=True, out_dtype=BF16, tm=D_MODEL, tn=D_MODEL, tk=tm)
    dz2, dpr, dpa = _merge_bwd(dmerged, z2, pr, pa)
    dy_rnn = _mm(dpr, wpr, name="mm_proj_rnn_dx", tb=True, out_dtype=BF16, tm=tm, tn=D_RNN,
                 tk=D_MODEL).reshape(bsz, s, D_RNN)
    g_wpr = _mm(y_rnn.reshape(t, D_RNN), dpr, name="mm_proj_rnn_dw", ta=True, out_dtype=BF16, tm=D_RNN, tn=D_MODEL, tk=tm)
    do = _mm(dpa, wpa, name="mm_proj_attn_dx", tb=True, out_dtype=BF16, tm=tm, tn=D_MODEL, tk=D_MODEL).reshape(bsz, s, D_MODEL)
    g_wpa = _mm(o_attn.reshape(t, D_MODEL), dpa, name="mm_proj_attn_dw", ta=True, out_dtype=BF16, tm=D_MODEL, tn=D_MODEL, tk=tm)
    dz, dfrow, got = _attn_bwd(zq, do, lse, fcol, frow, dz2.reshape(bsz, s, NZ),
                               exchange=([g_wup_z, g_wout.reshape(N_DEV, -1, D_MODEL)], ["scatter_zf", False]))
    p_wup, p_wout = got
    dflt, db_fgate = _fgate_bwd(dfrow.reshape(bsz, N_HEADS, s), flt, bfg)
    dfl_blk = jnp.pad(dflt.transpose(0, 2, 1).astype(BF16), ((0, 0), (0, 0), (0, MG0 - FL0 - N_HEADS)))
    dz = _fl_write(dfl_blk, dz)
    dz, g_wa, g_wi, dsm_rnn, got = _rnn_bwd(
        zr, h_rnn, lru_gates, lru_gelus, dy_rnn, dz, wrc, b_rnn_conv, w_lru_a[0], w_lru_i[0], lru_lambda,
        exchange=([g_wpr.reshape(N_DEV, -1, D_MODEL), g_wpa.reshape(N_DEV, -1, D_MODEL)], [False, False]))
    p_wpr, p_wpa = got
    dz2 = dz.reshape(t, NZ)
    small = [dg_norm2, dg_final, dsm_rnn, g_wa, g_wi, db_fgate, dsm_ffn]
    small_sizes = [a.size for a in small]
    n_pad = -sum(small_sizes) % 1024
    packed = jnp.concatenate([a.reshape(-1) for a in small] + [jnp.zeros((n_pad,), F32)]).reshape(-1, 128)
    g_win_zt, got = _mm(dz2, h1f, name="mm_in_dw", ta=True, out_dtype=BF16, tm=1024, tn=D_MODEL, tk=tm,
                        exchange=([packed], [True]))
    flat = got[0].reshape(N_DEV, -1)
    dh1, got = _mm(dz2, win_zt, name="mm_in_dx", tm=tm, tn=D_MODEL, tk=2048,
                   exchange=([_from_z_layout(g_win_zt).reshape(N_DEV, -1, D_MODEL)], [False]))
    p_win = got[0]
    grad_x, dg_norm1, dshift1, dscale1 = _norm_bwd(x, dh1.reshape(bsz, s, D_MODEL), g_norm1, scale1, dx1, name="norm1_bwd")

    dmod = jnp.concatenate([dshift1, dscale1, dgate1, dshift2, dscale2, dgate2], axis=1).reshape(bsz, 6 * D_MODEL)
    dmodf = jnp.concatenate([dshift_f, dscale_f], axis=1).reshape(bsz, 2 * D_MODEL)
    ex_c = _exchange([dmod, dmodf, dg_norm1], [True, True, True], "gather_last_grads")
    dmod_all = ex_c[0].reshape(nb, 6 * D_MODEL)
    dmodf_all = ex_c[1].reshape(nb, 2 * D_MODEL)
    p_g1 = ex_c[2]

    res = {}

    def upd(name, parts, shape2):
        w, m, v = args[name], args["m_" + name], args["v_" + name]
        outs = _adamw(w.reshape(shape2), m.reshape(shape2), v.reshape(shape2), parts, name="adamw_" + name)
        res[name] = [o.reshape(w.shape) for o in outs]

    g_wada = _mm(c_all, lax.dynamic_slice(dmod_all, (0, me * nmod), (nb, nmod)), name="mm_ada_dw", ta=True,
                 silu_a=True, tm=D_MODEL, tn=nmod, tk=nb)
    g_wadaf = _mm(c_all, lax.dynamic_slice(dmodf_all, (0, me * nmodf), (nb, nmodf)), name="mm_adaf_dw", ta=True,
                  silu_a=True, tm=D_MODEL, tn=nmodf, tk=nb)
    upd("w_ada", g_wada[None], (D_MODEL, nmod))
    upd("w_ada_final", g_wadaf[None], (D_MODEL, nmodf))
    upd("b_ada", dmod_all.reshape(nb, 1, -1), (1, 6 * D_MODEL))
    upd("b_ada_final", dmodf_all.reshape(nb, 1, -1), (1, 2 * D_MODEL))
    outs = _adamw(w_in[0].T, m_w_in[0].T, v_w_in[0].T, p_win, name="adamw_w_in")
    res["w_in"] = [o.T[None] for o in outs]
    upd("w_proj_rnn", p_wpr, w_proj_rnn.shape[1:])
    upd("w_proj_attn", p_wpa, w_proj_attn.shape[1:])
    upd("w_out", p_wout, w_out.shape[1:])
    upd("w_ffn_up", p_wup, w_ffn_up.shape[1:])
    upd("w_ffn_down", p_wdown, w_ffn_down.shape[1:])

    parts, off = [], 0
    for a, n in zip(small, small_sizes):
        parts.append(flat[:, off:off + n].reshape((N_DEV,) + a.shape))
        off += n
    p_g2, p_gf, p_rnn, p_wa, p_wi, p_bfg, p_ffn = parts
    upd("g_norm1", p_g1, (1, D_MODEL))
    upd("g_norm2", p_g2, (1, D_MODEL))
    upd("g_final", p_gf, (1, D_MODEL))
    nrc = w_rnn_conv.shape[2]
    upd("w_rnn_conv", lax.dynamic_slice(p_rnn, (0, 0, me * nrc), (N_DEV, 4, nrc)), (4, nrc))
    upd("b_rnn_conv", p_rnn[:, 4:5], (1, D_RNN))
    upd("b_lru_a", p_rnn[:, 5:6], (1, D_RNN))
    upd("b_lru_i", p_rnn[:, 6:7], (1, D_RNN))
    upd("lru_lambda", p_rnn[:, 7:8], (1, D_RNN))
    upd("w_lru_a", p_wa.reshape(N_DEV, D_RNN, RNN_BLK), (D_RNN, RNN_BLK))
    upd("w_lru_i", p_wi.reshape(N_DEV, D_RNN, RNN_BLK), (D_RNN, RNN_BLK))
    upd("b_fgate", p_bfg.reshape(N_DEV, 1, N_HEADS), (1, N_HEADS))
    nfc = w_ffn_conv.shape[2]
    upd("w_ffn_conv", lax.dynamic_slice(p_ffn, (0, 0, me * nfc), (N_DEV, 3, nfc)), (3, nfc))
    upd("b_ffn_conv", p_ffn[:, 3:4], (1, D_FF))

    names = ['w_ada', 'b_ada', 'g_norm1', 'w_in', 'w_rnn_conv', 'b_rnn_conv', 'w_lru_a', 'b_lru_a', 'w_lru_i',
             'b_lru_i', 'lru_lambda', 'b_fgate', 'w_proj_rnn', 'w_proj_attn', 'w_out', 'g_norm2', 'w_ffn_up',
             'w_ffn_conv', 'b_ffn_conv', 'w_ffn_down', 'w_ada_final', 'b_ada_final', 'g_final']
    return (loss, grad_x, *[res[n][0] for n in names], *[res[n][1] for n in names],
            *[res[n][2] for n in names], *[res[n][3] for n in names])
```

```python
import jax
import jax.numpy as jnp
from jax import lax
from jax.experimental import pallas as pl
from jax.experimental.pallas import tpu as pltpu

F32, BF16 = jnp.float32, jnp.bfloat16
D_MODEL, D_RNN, N_BLK, RNN_BLK = 1024, 1280, 10, 128
N_HEADS, HEAD_DIM, D_FF = 8, 128, 3072
N_DEV = 8
NZ = 8192
QKV0, XG0, FL0, MG0 = 0, 3072, 5632, 6144
NZR = NZ - XG0
XG0R, FL0R, MG0R = 0, FL0 - XG0, MG0 - XG0
RMS_EPS = 1e-6
LRU_C = 8.0
ADAM_LR, ADAM_B1, ADAM_B2, ADAM_EPS, ADAM_WD, ADAM_STEP = 0.001, 0.9, 0.999, 1e-08, 0.01, 10
VMEM_LIMIT = 56 << 20
SCAN_CHUNK = 256
ATTN_TQ = 256
CUM_BLK = 512
MESH = pl.DeviceIdType.MESH
ANY = pl.BlockSpec(memory_space=pl.ANY)


def _call(body, **kw):
    return pl.pallas_call(body, **kw)


def _params():
    return pltpu.CompilerParams(vmem_limit_bytes=VMEM_LIMIT)


ZF_PIECES = 2 * D_FF // 128 // N_DEV


def _zf_cols(owner, p):
    ob = owner * ZF_PIECES + p
    blk = 2 * (ob % (D_FF // 128)) + ob // (D_FF // 128)
    return pl.ds(pl.multiple_of(blk * 128, 128), 128)


def _n_pieces(mode):
    return ZF_PIECES if mode in ("gather_zf", "scatter_zf") else 1


def _exchanged_shape(a, mode):
    if mode is True:
        return (N_DEV,) + a.shape
    if mode is False:
        return a.shape
    if mode == "gather_zf":
        return (a.shape[0], N_DEV * a.shape[1])
    return (N_DEV, a.shape[0], a.shape[1] // N_DEV)


def _direct_copies(ins, outs, sems, modes):
    send_sems, recv_sems, loc_sems = sems
    x, y, c = lax.axis_index("x"), lax.axis_index("y"), lax.axis_index("c")
    me = 4 * x + 2 * y + c

    def ends(a, to, p):
        mode = modes[a]
        if mode is True:
            return ins[a], outs[a].at[me]
        if mode is False:
            return ins[a].at[to], outs[a].at[me]
        mine = slice(p * 128, (p + 1) * 128)
        if mode == "gather_zf":
            return ins[a].at[:, mine], outs[a].at[:, _zf_cols(me, p)]
        return ins[a].at[:, _zf_cols(to, p)], outs[a].at[me, :, mine]

    local, remote = [], []
    nloc = 0
    for a in range(len(ins)):
        for p in range(_n_pieces(modes[a])):
            src, dst = ends(a, me, p)
            local.append(pltpu.make_async_copy(src, dst, loc_sems.at[nloc]))
            nloc += 1
    for k in range(1, N_DEV):
        px = 1 - x if k & 4 else x
        py = 1 - y if k & 2 else y
        pc = 1 - c if k & 1 else c
        peer = 4 * px + 2 * py + pc
        nrem = (k - 1) * nloc
        for a in range(len(ins)):
            for p in range(_n_pieces(modes[a])):
                src, dst = ends(a, peer, p)
                remote.append(pltpu.make_async_remote_copy(
                    src_ref=src, dst_ref=dst, send_sem=send_sems.at[nrem], recv_sem=recv_sems.at[nrem],
                    device_id=(px, py, pc), device_id_type=MESH))
                nrem += 1
    return local, remote


def _with_exchange(body, args, arrs, bcast, *, name, grid, in_specs, out_specs, out_shape, scratch_shapes=(), **kw):
    n_in, n_out, n_sc, nc = len(args), len(out_shape), len(scratch_shapes), len(arrs)

    def wrapped(*refs):
        core_in, cin = refs[:n_in], refs[n_in:n_in + nc]
        core_out = refs[n_in + nc:n_in + nc + n_out]
        cout = refs[n_in + nc + n_out:n_in + 2 * nc + n_out]
        rest = refs[n_in + 2 * nc + n_out:]
        core_sc, sems = rest[:n_sc], rest[n_sc:]

        def start():
            local, remote = _direct_copies(cin, cout, sems, bcast)
            for cp in local + remote:
                cp.start()

        def wait():
            local, remote = _direct_copies(cin, cout, sems, bcast)
            for cp in remote + local:
                cp.wait()

        if not grid:
            start()
            body(*core_in, *core_out, *core_sc)
            wait()
            return
        first, last = None, None
        for d, g in enumerate(grid):
            i = pl.program_id(d)
            first = (i == 0) if first is None else jnp.logical_and(first, i == 0)
            last = (i == g - 1) if last is None else jnp.logical_and(last, i == g - 1)
        pl.when(first)(start)
        body(*core_in, *core_out, *core_sc)
        pl.when(last)(wait)

    ex_shape = [jax.ShapeDtypeStruct(_exchanged_shape(a, mode), a.dtype) for a, mode in zip(arrs, bcast)]
    ncopies = sum(_n_pieces(mode) for mode in bcast)
    sems = [pltpu.SemaphoreType.DMA((ncopies * (N_DEV - 1),)), pltpu.SemaphoreType.DMA((ncopies * (N_DEV - 1),)),
            pltpu.SemaphoreType.DMA((ncopies,))]
    res = _call(wrapped, name=name, grid=grid, in_specs=list(in_specs) + [ANY] * nc,
                out_specs=list(out_specs) + [ANY] * nc, out_shape=list(out_shape) + ex_shape,
                scratch_shapes=list(scratch_shapes) + sems, **kw)(*args, *arrs)
    return list(res[:n_out]), list(res[n_out:])


def _run(body, args, exchange, **kw):
    if exchange is None:
        return list(_call(body, **kw)(*args))
    outs, got = _with_exchange(body, args, *exchange, **kw)
    return outs + [got]


def _exchange(arrs, bcast, name):
    return _with_exchange(lambda: None, [], arrs, bcast, name=name, grid=(), in_specs=[], out_specs=[], out_shape=[],
                          compiler_params=pltpu.CompilerParams(has_side_effects=True))[1]


def _gather_two_level(arrs, name):
    n = len(arrs)

    def body(*refs):
        ins, outs = refs[:n], refs[n:2 * n]
        send_sems, recv_sems, loc_sems = refs[2 * n:]
        x, y, c = lax.axis_index("x"), lax.axis_index("y"), lax.axis_index("c")
        me, sibling = (x, y, c), (x, y, 1 - c)
        chips = [(1 - x, y), (x, 1 - y), (1 - x, 1 - y)]

        def slot(a, p):
            return outs[a].at[4 * p[0] + 2 * p[1] + p[2]]

        def copy(a, k, block, to, src=None):
            return pltpu.make_async_remote_copy(
                src_ref=slot(a, block) if src is None else src, dst_ref=slot(a, block),
                send_sem=send_sems.at[a * 7 + k], recv_sem=recv_sems.at[a * 7 + k],
                device_id=to, device_id_type=MESH)

        mine = [pltpu.make_async_copy(ins[a], slot(a, me), loc_sems.at[a]) for a in range(n)]
        first = []
        for a in range(n):
            first.append(copy(a, 0, me, sibling, src=ins[a]))
            first += [copy(a, 1 + j, me, (*chip, c), src=ins[a]) for j, chip in enumerate(chips)]
        for cp in mine + first:
            cp.start()
        passed = []
        for j, chip in enumerate(chips):
            for a in range(n):
                copy(a, 1 + j, (*chip, c), me).wait_recv()
                fwd = copy(a, 4 + j, (*chip, c), sibling)
                fwd.start()
                passed.append(fwd)
        for a in range(n):
            copy(a, 0, sibling, me).wait_recv()
        for j, chip in enumerate(chips):
            for a in range(n):
                copy(a, 4 + j, (*chip, 1 - c), me).wait_recv()
        for cp in first + passed:
            cp.wait_send()
        for cp in mine:
            cp.wait()

    return _call(
        body, name=name, in_specs=[ANY] * n, out_specs=[ANY] * n,
        out_shape=[jax.ShapeDtypeStruct((N_DEV,) + a.shape, a.dtype) for a in arrs],
        scratch_shapes=[pltpu.SemaphoreType.DMA((n * 7,)), pltpu.SemaphoreType.DMA((n * 7,)),
                        pltpu.SemaphoreType.DMA((n,))],
        compiler_params=pltpu.CompilerParams(has_side_effects=True),
    )(*arrs)


def _sigmoid(x):
    return 1.0 / (1.0 + jnp.exp(-x))


def _log1p(e):
    u = 1.0 + e
    d = u - 1.0
    return jnp.where(d == 0.0, e, jnp.log(u) * (e / jnp.where(d == 0.0, 1.0, d)))


def _softplus(x):
    return jnp.maximum(x, 0.0) + _log1p(jnp.exp(-jnp.abs(x)))


_GELU_C = 0.7978845608028654


def _gelu(x):
    t = jnp.tanh(_GELU_C * (x + 0.044715 * x * x * x))
    return 0.5 * x * (1.0 + t)


def _gelu_and_grad(x):
    t = jnp.tanh(_GELU_C * (x + 0.044715 * x * x * x))
    g = 0.5 * x * (1.0 + t)
    dg = 0.5 * (1.0 + t) + 0.5 * x * (1.0 - t * t) * _GELU_C * (1.0 + 3.0 * 0.044715 * x * x)
    return g, dg


def _acc(ref, idx, val, first):
    @pl.when(first)
    def _():
        ref[idx] = val

    @pl.when(jnp.logical_not(first))
    def _():
        ref[idx] = ref[idx] + val


def _split3(x):
    hi = x.astype(BF16)
    r1 = x - hi.astype(F32)
    mid = r1.astype(BF16)
    lo = (r1 - mid.astype(F32)).astype(BF16)
    return hi, mid, lo


def _mm(a, b, *, name, ta=False, tb=False, out_dtype=F32, tm, tn, tk, bias=None, silu_a=False, exchange=None,
        b_cols=None):
    m, k = (a.shape[1], a.shape[0]) if ta else a.shape
    col0, n = b_cols if b_cols is not None else (0, b.shape[0] if tb else b.shape[1])
    assert m % tm == 0 and n % tn == 0 and k % tk == 0 and col0 % tn == 0, (name, m, n, k)
    jo = col0 // tn
    nk = k // tk
    dn = (((0 if ta else 1,), (1 if tb else 0,)), ((), ()))
    use_acc = nk > 1 and out_dtype != F32

    def body(*refs):
        a_ref, b_ref = refs[0], refs[1]
        bias_ref = refs[2] if bias is not None else None
        o_ref = refs[3 if bias is not None else 2]
        av = a_ref[...]
        if silu_a:
            av = av * _sigmoid(av)
        p = lax.dot_general(av.astype(BF16), b_ref[...].astype(BF16), dn, preferred_element_type=F32)
        if bias is not None:
            p = p + bias_ref[...]
        if nk == 1:
            o_ref[...] = p.astype(out_dtype)
            return
        kk = pl.program_id(2)
        acc = refs[-1] if use_acc else o_ref

        @pl.when(kk == 0)
        def _():
            acc[...] = p

        @pl.when(kk > 0)
        def _():
            acc[...] = acc[...] + p

        if use_acc:
            @pl.when(kk == nk - 1)
            def _():
                o_ref[...] = acc[...].astype(out_dtype)

    a_spec = pl.BlockSpec((tk, tm), lambda i, j, q: (q, i)) if ta else pl.BlockSpec((tm, tk), lambda i, j, q: (i, q))
    b_spec = (pl.BlockSpec((tn, tk), lambda i, j, q: (j + jo, q)) if tb
              else pl.BlockSpec((tk, tn), lambda i, j, q: (q, j + jo)))
    in_specs, args = [a_spec, b_spec], [a, b]
    if bias is not None:
        in_specs.append(pl.BlockSpec((1, tn), lambda i, j, q: (0, j)))
        args.append(bias)
    kw = dict(name=name, grid=(m // tm, n // tn, nk), in_specs=in_specs,
              out_specs=[pl.BlockSpec((tm, tn), lambda i, j, q: (i, j))],
              out_shape=[jax.ShapeDtypeStruct((m, n), out_dtype)],
              scratch_shapes=[pltpu.VMEM((tm, tn), F32)] if use_acc else [],
              compiler_params=_params())
    if exchange is None:
        return _call(body, **kw)(*args)[0]
    outs, got = _with_exchange(body, args, *exchange, **kw)
    return outs[0], got


def _tok_spec(ts):
    return pl.BlockSpec((1, ts, D_MODEL), lambda b, s: (b, s, 0))


def _seq_spec():
    return pl.BlockSpec((1, 1, D_MODEL), lambda b, s: (b, 0, 0))


def _vec_spec():
    return pl.BlockSpec((1, D_MODEL), lambda b, s: (0, 0))


def _norm_fwd(x, g, shift, scale, *, name, yprev=None, gate=None, ts=512):
    bsz, s, _ = x.shape
    res = yprev is not None

    def body(*refs):
        if res:
            x_ref, g_ref, sh_ref, sc_ref, y_ref, gt_ref, h_ref, xn_ref = refs
            xv = x_ref[0] + gt_ref[0] * y_ref[0]
            xn_ref[0] = xv
        else:
            x_ref, g_ref, sh_ref, sc_ref, h_ref = refs
            xv = x_ref[0]
        rstd = lax.rsqrt(jnp.mean(xv * xv, axis=-1, keepdims=True) + RMS_EPS)
        nv = xv * rstd * g_ref[...]
        h_ref[0] = (nv * (1.0 + sc_ref[0]) + sh_ref[0]).astype(BF16)

    in_specs = [_tok_spec(ts), _vec_spec(), _seq_spec(), _seq_spec()]
    args = [x, g, shift, scale]
    out_specs = [_tok_spec(ts)]
    out_shape = [jax.ShapeDtypeStruct(x.shape, BF16)]
    if res:
        in_specs += [_tok_spec(ts), _seq_spec()]
        args += [yprev, gate]
        out_specs.append(_tok_spec(ts))
        out_shape.append(jax.ShapeDtypeStruct(x.shape, F32))
    return _call(body, name=name, grid=(bsz, s // ts), in_specs=in_specs, out_specs=out_specs,
                 out_shape=out_shape, compiler_params=_params())(*args)


def _final(x1, yf, gate2, g, shift, scale, target, *, ts=512):
    bsz, s, _ = x1.shape
    ns = s // ts

    def body(x_ref, y_ref, gt_ref, g_ref, sh_ref, sc_ref, t_ref,
             lp_ref, dx_ref, dyf_ref, dsh_ref, dsc_ref, dgt_ref, dg_ref):
        b, si = pl.program_id(0), pl.program_id(1)
        yv, gt, gv = y_ref[0], gt_ref[0], g_ref[...]
        xv = x_ref[0] + gt * yv
        rstd = lax.rsqrt(jnp.mean(xv * xv, axis=-1, keepdims=True) + RMS_EPS)
        xhat = xv * rstd
        nv = xhat * gv
        sc1 = 1.0 + sc_ref[0]
        err = nv * sc1 + sh_ref[0] - t_ref[0]
        lp_ref[...] = jnp.zeros((1, 1, 8, 128), F32) + jnp.sum(err * err)
        dy = err * (1.0 / D_MODEL)
        dn = dy * sc1
        dxh = dn * gv
        dx = rstd * (dxh - xhat * jnp.mean(dxh * xhat, axis=-1, keepdims=True))
        dx_ref[0] = dx
        dyf_ref[0] = (dx * gt).astype(BF16)
        _acc(dsh_ref, 0, jnp.sum(dy, axis=0, keepdims=True), si == 0)
        _acc(dsc_ref, 0, jnp.sum(dy * nv, axis=0, keepdims=True), si == 0)
        _acc(dgt_ref, 0, jnp.sum(dx * yv, axis=0, keepdims=True), si == 0)
        _acc(dg_ref, slice(None), jnp.sum(dn * xhat, axis=0, keepdims=True), jnp.logical_and(b == 0, si == 0))

    seq_shape = jax.ShapeDtypeStruct((bsz, 1, D_MODEL), F32)
    return _call(
        body, name="final_loss", grid=(bsz, ns),
        in_specs=[_tok_spec(ts), _tok_spec(ts), _seq_spec(), _vec_spec(), _seq_spec(), _seq_spec(), _tok_spec(ts)],
        out_specs=[pl.BlockSpec((1, 1, 8, 128), lambda b, s: (b, s, 0, 0)), _tok_spec(ts), _tok_spec(ts),
                   _seq_spec(), _seq_spec(), _seq_spec(), _vec_spec()],
        out_shape=[jax.ShapeDtypeStruct((bsz, ns, 8, 128), F32), jax.ShapeDtypeStruct(x1.shape, F32),
                   jax.ShapeDtypeStruct(x1.shape, BF16), seq_shape, seq_shape, seq_shape,
                   jax.ShapeDtypeStruct((1, D_MODEL), F32)],
        compiler_params=_params(),
    )(x1, yf, gate2, g, shift, scale, target)


def _norm_bwd(x, dh, g, scale, dres, *, name, yprev=None, gate=None, ts=512):
    bsz, s, _ = x.shape
    res = yprev is not None

    def body(*refs):
        if res:
            (x_ref, dh_ref, g_ref, sc_ref, dr_ref, y_ref, gt_ref,
             dx_ref, dg_ref, dsh_ref, dsc_ref, dy_ref, dgt_ref) = refs
        else:
            x_ref, dh_ref, g_ref, sc_ref, dr_ref, dx_ref, dg_ref, dsh_ref, dsc_ref = refs
        b, si = pl.program_id(0), pl.program_id(1)
        xv, dhv, gv = x_ref[0], dh_ref[0], g_ref[...]
        rstd = lax.rsqrt(jnp.mean(xv * xv, axis=-1, keepdims=True) + RMS_EPS)
        xhat = xv * rstd
        dn = dhv * (1.0 + sc_ref[0])
        dxh = dn * gv
        dx = dr_ref[0] + rstd * (dxh - xhat * jnp.mean(dxh * xhat, axis=-1, keepdims=True))
        dx_ref[0] = dx
        _acc(dsh_ref, 0, jnp.sum(dhv, axis=0, keepdims=True), si == 0)
        _acc(dsc_ref, 0, jnp.sum(dhv * (xhat * gv), axis=0, keepdims=True), si == 0)
        _acc(dg_ref, slice(None), jnp.sum(dn * xhat, axis=0, keepdims=True), jnp.logical_and(b == 0, si == 0))
        if res:
            dy_ref[0] = (dx * gt_ref[0]).astype(BF16)
            _acc(dgt_ref, 0, jnp.sum(dx * y_ref[0], axis=0, keepdims=True), si == 0)

    seq_shape = jax.ShapeDtypeStruct((bsz, 1, D_MODEL), F32)
    in_specs = [_tok_spec(ts), _tok_spec(ts), _vec_spec(), _seq_spec(), _tok_spec(ts)]
    args = [x, dh, g, scale, dres]
    out_specs = [_tok_spec(ts), _vec_spec(), _seq_spec(), _seq_spec()]
    out_shape = [jax.ShapeDtypeStruct(x.shape, F32), jax.ShapeDtypeStruct((1, D_MODEL), F32), seq_shape, seq_shape]
    if res:
        in_specs += [_tok_spec(ts), _seq_spec()]
        args += [yprev, gate]
        out_specs += [_tok_spec(ts), _seq_spec()]
        out_shape += [jax.ShapeDtypeStruct(x.shape, BF16), seq_shape]
    return _call(body, name=name, grid=(bsz, s // ts), in_specs=in_specs, out_specs=out_specs,
                 out_shape=out_shape, compiler_params=_params())(*args)


def _lru_gates(xr, wa, ba, wi, bi, sp):
    xb = xr.astype(BF16)
    r = _sigmoid(jnp.dot(xb, wa, preferred_element_type=F32) + ba)
    i = _sigmoid(jnp.dot(xb, wi, preferred_element_type=F32) + bi)
    la = -LRU_C * r * sp
    a = jnp.exp(la)
    mult = jnp.sqrt(-jnp.tanh(la) * (a * a + 1.0))
    return r, i, a, mult


def _scan_forward(a, u, h_in, row):
    n = a.shape[0]
    r8 = row & 7
    for k in (1, 2, 4):
        m = r8 >= k
        u = jnp.where(m, a * pltpu.roll(u, k, 0) + u, u)
        a = jnp.where(m, a * pltpu.roll(a, k, 0), a)
    carry = jnp.broadcast_to(h_in, (8, a.shape[1]))
    out = []
    for j in range(n // 8):
        hj = u[8 * j:8 * j + 8] + a[8 * j:8 * j + 8] * carry
        out.append(hj)
        carry = jnp.broadcast_to(hj[7:8], hj.shape)
    return jnp.concatenate(out, axis=0)


def _scan_backward(a_up, g, g_in, row):
    n = g.shape[0]
    r8 = row & 7
    for k in (1, 2, 4):
        m = r8 < 8 - k
        g = jnp.where(m, a_up * pltpu.roll(g, n - k, 0) + g, g)
        a_up = jnp.where(m, a_up * pltpu.roll(a_up, n - k, 0), a_up)
    carry = jnp.broadcast_to(g_in, (8, g.shape[1]))
    out = [None] * (n // 8)
    for j in reversed(range(n // 8)):
        gj = g[8 * j:8 * j + 8] + a_up[8 * j:8 * j + 8] * carry
        out[j] = gj
        carry = jnp.broadcast_to(gj[0:1], gj.shape)
    return jnp.concatenate(out, axis=0)


def _rnn_fwd(z, wc, bc, wa, ba, wi, bi, lam, exchange=None):
    bsz, s, _ = z.shape
    ch = min(SCAN_CHUNK, s)
    nch = s // ch

    def body(z_ref, wc_ref, bc_ref, wa_ref, ba_ref, wi_ref, bi_ref, lam_ref, h_ref, y_ref, g_ref, gd_ref):
        wcv, bcv = wc_ref[...], bc_ref[...]
        wav, wiv = wa_ref[0].astype(BF16), wi_ref[0].astype(BF16)
        bav, biv = ba_ref[...], bi_ref[...]
        sp = _softplus(-lam_ref[...])
        row = lax.broadcasted_iota(jnp.int32, (ch, RNN_BLK), 0)

        def chunk(ci, carry):
            prev8, hc = carry
            st = pl.multiple_of(ci * ch, ch)
            blk = z_ref[0, pl.ds(st, ch), :]
            xp, gr = blk[:, :RNN_BLK], blk[:, RNN_BLK:]
            xe = jnp.concatenate([prev8, xp], axis=0)
            xr = bcv + wcv[3:4] * xp
            for j in range(1, 4):
                xr = xr + wcv[3 - j:4 - j] * pltpu.roll(xe, j, 0)[8:]
            r, i, a, mult = _lru_gates(xr, wav, bav, wiv, biv, sp)
            g_ref[0, pl.ds(st, ch), :] = jnp.concatenate([r, i, a, mult], axis=1)
            hh = _scan_forward(a, mult * (i * xr), hc, row)
            h_ref[0, pl.ds(st, ch), :] = hh
            ge, dge = _gelu_and_grad(gr)
            gd_ref[0, pl.ds(st, ch), :] = jnp.concatenate([ge, dge], axis=1).astype(BF16)
            y_ref[0, pl.ds(st, ch), :] = (ge * hh).astype(BF16)
            return xp[ch - 8:], hh[ch - 1:]

        lax.fori_loop(0, nch, chunk, (jnp.zeros((8, RNN_BLK), F32), jnp.zeros((1, RNN_BLK), F32)))

    vec = pl.BlockSpec((1, RNN_BLK), lambda b, n: (0, n))
    mat = pl.BlockSpec((1, RNN_BLK, RNN_BLK), lambda b, n: (n, 0, 0))
    out = pl.BlockSpec((1, s, RNN_BLK), lambda b, n: (b, 0, n))
    return _run(
        body, [z, wc, bc, wa, ba, wi, bi, lam], exchange, name="rnn_fwd", grid=(bsz, N_BLK),
        in_specs=[pl.BlockSpec((1, s, 2 * RNN_BLK), lambda b, n: (b, 0, XG0R // (2 * RNN_BLK) + n)),
                  pl.BlockSpec((4, RNN_BLK), lambda b, n: (0, n)), vec, mat, vec, mat, vec, vec],
        out_specs=[out, out, pl.BlockSpec((1, s, 4 * RNN_BLK), lambda b, n: (b, 0, n)),
                   pl.BlockSpec((1, s, 2 * RNN_BLK), lambda b, n: (b, 0, n))],
        out_shape=[jax.ShapeDtypeStruct((bsz, s, D_RNN), F32), jax.ShapeDtypeStruct((bsz, s, D_RNN), BF16),
                   jax.ShapeDtypeStruct((bsz, s, 4 * D_RNN), F32), jax.ShapeDtypeStruct((bsz, s, 2 * D_RNN), BF16)],
        compiler_params=_params())


def _rnn_bwd(z, h, gates, gelus, dy, dz, wc, bc, wa, wi, lam, exchange=None):
    bsz, s, _ = z.shape
    ch = min(SCAN_CHUNK, s)
    nch = s // ch

    def body(z_ref, h_ref, g_ref, gd_ref, dy_ref, dz_in, wc_ref, bc_ref, wa_ref, wi_ref, lam_ref,
             dz_ref, dwa_ref, dwi_ref, dsm_ref):
        del dz_in
        b = pl.program_id(1)
        wcv, bcv = wc_ref[...], bc_ref[...]
        wav, wiv = wa_ref[0].astype(BF16), wi_ref[0].astype(BF16)
        lamv = lam_ref[...]
        sp = _softplus(-lamv)
        row = lax.broadcasted_iota(jnp.int32, (ch, RNN_BLK), 0)

        @pl.when(b == 0)
        def _():
            dwa_ref[...] = jnp.zeros_like(dwa_ref)
            dwi_ref[...] = jnp.zeros_like(dwi_ref)
            dsm_ref[...] = jnp.zeros_like(dsm_ref)

        def chunk(k, carry):
            a_next, g_next, dxr_next8 = carry
            ci = nch - 1 - k
            st = pl.multiple_of(ci * ch, ch)
            pst = pl.multiple_of(jnp.maximum(st - 8, 0), 8)
            has_prev = (ci > 0).astype(F32)
            xp = z_ref[0, pl.ds(st, ch), :][:, :RNN_BLK]
            xprev8 =z_ref[0, pl.ds(pst, 8), :][:, :RNN_BLK] * has_prev
            hprev8 = h_ref[0, pl.ds(pst, 8), :] * has_prev
            xe = jnp.concatenate([xprev8, xp], axis=0)
            xs = [xp] + [pltpu.roll(xe, j, 0)[8:] for j in range(1, 4)]
            xr = bcv
            for kk in range(4):
                xr = xr + wcv[kk:kk + 1] * xs[3 - kk]
            gs = g_ref[0, pl.ds(st, ch), :]
            r, i, a, mult = [gs[:, q * RNN_BLK:(q + 1) * RNN_BLK] for q in range(4)]
            hh = h_ref[0, pl.ds(st, ch), :]
            hm1 = pltpu.roll(jnp.concatenate([hprev8, hh], axis=0), 1, 0)[8:]
            dyv = dy_ref[0, pl.ds(st, ch), :]
            gd = gd_ref[0, pl.ds(st, ch), :].astype(F32)
            ge, dge = gd[:, :RNN_BLK], gd[:, RNN_BLK:]
            dgr = dyv * hh * dge
            gg = _scan_backward(jnp.where(row == ch - 1, a_next, pltpu.roll(a, ch - 1, 0)), dyv * ge, g_next, row)
            da = gg * hm1
            ix = i * xr
            dmult = gg * ix
            di = gg * mult * xr
            dxr = gg * mult * i
            dla = da * a - dmult * (a * a) / mult
            dr = dla * (-LRU_C * sp)
            dsp = jnp.sum(dla * (-LRU_C * r), axis=0, keepdims=True)
            dpa = dr * r * (1.0 - r)
            dpi = di * i * (1.0 - i)
            dpab, dpib, xrb = dpa.astype(BF16), dpi.astype(BF16), xr.astype(BF16)
            nt = (((1,), (1,)), ((), ()))
            tn = (((0,), (0,)), ((), ()))
            dxr = dxr + lax.dot_general(dpab, wav, nt, preferred_element_type=F32)
            dxr = dxr + lax.dot_general(dpib, wiv, nt, preferred_element_type=F32)
            dwa_ref[0] = dwa_ref[0] + lax.dot_general(xrb, dpab, tn, preferred_element_type=F32)
            dwi_ref[0] = dwi_ref[0] + lax.dot_general(xrb, dpib, tn, preferred_element_type=F32)
            rows = [jnp.sum(dxr * xs[3 - q], axis=0, keepdims=True) for q in range(4)]
            rows += [jnp.sum(dxr, axis=0, keepdims=True), jnp.sum(dpa, axis=0, keepdims=True),
                     jnp.sum(dpi, axis=0, keepdims=True), dsp * (-_sigmoid(-lamv))]
            dsm_ref[...] = dsm_ref[...] + jnp.concatenate(rows, axis=0)
            dxe = jnp.concatenate([dxr, dxr_next8], axis=0)
            dxp = wcv[3:4] * dxr
            for j in range(1, 4):
                dxp = dxp + wcv[3 - j:4 - j] * pltpu.roll(dxe, ch + 8 - j, 0)[:ch]
            dz_ref[0, pl.ds(st, ch), :] = jnp.concatenate([dxp, dgr], axis=1).astype(BF16)
            return a[0:1], gg[0:1], dxr[0:8]

        lax.fori_loop(0, nch, chunk, (jnp.zeros((1, RNN_BLK), F32), jnp.zeros((1, RNN_BLK), F32),
                                      jnp.zeros((8, RNN_BLK), F32)))

    vec = pl.BlockSpec((1, RNN_BLK), lambda n, b: (0, n))
    mat = pl.BlockSpec((1, RNN_BLK, RNN_BLK), lambda n, b: (n, 0, 0))
    tok = pl.BlockSpec((1, s, RNN_BLK), lambda n, b: (b, 0, n))
    zblk = pl.BlockSpec((1, s, 2 * RNN_BLK), lambda n, b: (b, 0, XG0R // (2 * RNN_BLK) + n))
    dzblk = pl.BlockSpec((1, s, 2 * RNN_BLK), lambda n, b: (b, 0, XG0 // (2 * RNN_BLK) + n))
    return _run(
        body, [z, h, gates, gelus, dy, dz, wc, bc, wa, wi, lam], exchange, name="rnn_bwd", grid=(N_BLK, bsz),
        in_specs=[zblk, tok, pl.BlockSpec((1, s, 4 * RNN_BLK), lambda n, b: (b, 0, n)),
                  pl.BlockSpec((1, s, 2 * RNN_BLK), lambda n, b: (b, 0, n)), tok, ANY,
                  pl.BlockSpec((4, RNN_BLK), lambda n, b: (0, n)), vec, mat, mat, vec],
        out_specs=[dzblk, mat, mat, pl.BlockSpec((8, RNN_BLK), lambda n, b: (0, n))],
        out_shape=[jax.ShapeDtypeStruct(dz.shape, dz.dtype), jax.ShapeDtypeStruct((N_BLK, RNN_BLK, RNN_BLK), F32),
                   jax.ShapeDtypeStruct((N_BLK, RNN_BLK, RNN_BLK), F32), jax.ShapeDtypeStruct((8, D_RNN), F32)],
        input_output_aliases={5: 0},
        compiler_params=_params())


def _tri(n, upper):
    r = lax.broadcasted_iota(jnp.int32, (n, n), 0)
    c = lax.broadcasted_iota(jnp.int32, (n, n), 1)
    return (r <= c if upper else r >= c).astype(BF16)


def _exact_dot(x, t):
    hi, mid, lo = _split3(x)
    return (jnp.dot(hi, t, preferred_element_type=F32) + jnp.dot(mid, t, preferred_element_type=F32)
            + jnp.dot(lo, t, preferred_element_type=F32))


def _fgate_fwd(flt, bfg):
    bsz, nh, s = flt.shape
    blk = min(CUM_BLK, s)

    def body(fl_ref, b_ref, f_ref):
        tri = _tri(blk, True)
        carry = jnp.zeros((nh, 1), F32)
        for q in range(s // blk):
            xq = fl_ref[0, :, q * blk:(q + 1) * blk] + b_ref[...]
            fq = _exact_dot(-_softplus(-xq), tri) + carry
            f_ref[0, :, q * blk:(q + 1) * blk] = fq
            carry = fq[:, blk - 1:blk]

    spec = pl.BlockSpec((1, nh, s), lambda b: (b, 0, 0))
    return _call(body, name="fgate_fwd", grid=(bsz,), in_specs=[spec, pl.BlockSpec((nh, 1), lambda b: (0, 0))],
                 out_specs=spec, out_shape=jax.ShapeDtypeStruct(flt.shape, F32), compiler_params=_params())(flt, bfg)


def _fgate_bwd(dft, flt, bfg):
    bsz, nh, s = flt.shape
    blk = min(CUM_BLK, s)

    def body(df_ref, fl_ref, b_ref, dfl_ref, db_ref):
        b = pl.program_id(0)
        tri = _tri(blk, False)
        carry = jnp.zeros((nh, 1), F32)
        tot = jnp.zeros((nh, 1), F32)
        for q in reversed(range(s // blk)):
            sl = slice(q * blk, (q + 1) * blk)
            dlq = _exact_dot(df_ref[0, :, sl], tri) + carry
            carry = dlq[:, 0:1]
            dflq = dlq * _sigmoid(-(fl_ref[0, :, sl] + b_ref[...]))
            dfl_ref[0, :, sl] = dflq
            tot = tot + jnp.sum(dflq, axis=1, keepdims=True)
        _acc(db_ref, slice(None), tot, b == 0)

    spec = pl.BlockSpec((1, nh, s), lambda b: (b, 0, 0))
    small = pl.BlockSpec((nh, 1), lambda b: (0, 0))
    return _call(body, name="fgate_bwd", grid=(bsz,), in_specs=[spec, spec, small], out_specs=[spec, small],
                 out_shape=[jax.ShapeDtypeStruct(flt.shape, F32), jax.ShapeDtypeStruct((nh, 1), F32)],
                 compiler_params=_params())(dft, flt, bfg)


_NT = (((1,), (1,)), ((), ()))
_TN = (((0,), (0,)), ((), ()))


def _attn_block_scores(q_b, z_ref, frow_ref, k0, tq):
    kk = z_ref[0, k0:k0 + tq, HEAD_DIM:2 * HEAD_DIM]
    return lax.dot_general(q_b, kk, _NT, preferred_element_type=F32) - frow_ref[0, 0, :, k0:k0 + tq]


def _causal(tq):
    return lax.broadcasted_iota(jnp.int32, (tq, tq), 0) >= lax.broadcasted_iota(jnp.int32, (tq, tq), 1)


def _attn_fwd(z, fcol, frow, exchange=None):
    bsz, s, _ = z.shape
    tq = min(ATTN_TQ, s)
    scale = HEAD_DIM ** -0.5

    def body(z_ref, fcol_ref, frow_ref, o_ref, lse_ref):
        for qb in range(s // tq):
            q0 = qb * tq
            q_b = (z_ref[0, q0:q0 + tq, :HEAD_DIM].astype(F32) * scale).astype(BF16)
            m = jnp.full((tq, 1), -jnp.inf, F32)
            l = jnp.zeros((tq, 1), F32)
            acc = jnp.zeros((tq, HEAD_DIM), F32)
            for kb in range(qb + 1):
                k0 = kb * tq
                sc = _attn_block_scores(q_b, z_ref, frow_ref, k0, tq)
                if kb == qb:
                    sc = jnp.where(_causal(tq), sc, -jnp.inf)
                m_new = jnp.maximum(m, jnp.max(sc, axis=-1, keepdims=True))
                alpha = jnp.exp(m - m_new)
                p = jnp.exp(sc - m_new)
                l = alpha * l + jnp.sum(p, axis=-1, keepdims=True)
                vv = z_ref[0, k0:k0 + tq, 2 * HEAD_DIM:]
                acc = alpha * acc + jnp.dot(p.astype(BF16), vv, preferred_element_type=F32)
                m = m_new
            o_ref[0, q0:q0 + tq, :] = (acc / l).astype(BF16)
            lse_ref[0, 0, q0:q0 + tq, :] = m + jnp.log(l) + fcol_ref[0, 0, q0:q0 + tq, :]

    return _run(
        body, [z, fcol, frow], exchange, name="attn_fwd", grid=(bsz, N_HEADS),
        in_specs=[pl.BlockSpec((1, s, 3 * HEAD_DIM), lambda b, h: (b, 0, h)),
                  pl.BlockSpec((1, 1, s, 1), lambda b, h: (b, h, 0, 0)),
                  pl.BlockSpec((1, 1, 1, s), lambda b, h: (b, h, 0, 0))],
        out_specs=[pl.BlockSpec((1, s, HEAD_DIM), lambda b, h: (b, 0, h)),
                   pl.BlockSpec((1, 1, s, 1), lambda b, h: (b, h, 0, 0))],
        out_shape=[jax.ShapeDtypeStruct((bsz, s, N_HEADS * HEAD_DIM), BF16),
                   jax.ShapeDtypeStruct((bsz, N_HEADS, s, 1), F32)],
        compiler_params=_params())


def _attn_bwd(z, do, lse, fcol, frow, dz, exchange=None):
    bsz, s, _ = z.shape
    tq = min(ATTN_TQ, s)
    scale = HEAD_DIM ** -0.5

    def body(z_ref, do_ref, lse_ref, fcol_ref, frow_ref, dz_in, dz_ref, df_ref, dk_acc, dv_acc):
        del dz_in
        dk_acc[...] = jnp.zeros_like(dk_acc)
        dv_acc[...] = jnp.zeros_like(dv_acc)
        df_ref[...] = jnp.zeros_like(df_ref)
        for qb in range(s // tq):
            q0, ln = qb * tq, (qb + 1) * tq
            q_b = (z_ref[0, q0:ln, :HEAD_DIM].astype(F32) * scale).astype(BF16)
            kk = z_ref[0, :ln, HEAD_DIM:2 * HEAD_DIM]
            vv = z_ref[0, :ln, 2 * HEAD_DIM:]
            do_b = do_ref[0, q0:ln, :]
            lse_q = lse_ref[0, 0, q0:ln, :] - fcol_ref[0, 0, q0:ln, :]
            sc = lax.dot_general(q_b, kk, _NT, preferred_element_type=F32) - frow_ref[0, 0, :, :ln] - lse_q
            mask = (q0 + lax.broadcasted_iota(jnp.int32, (tq, ln), 0)) >= lax.broadcasted_iota(jnp.int32, (tq, ln), 1)
            p = jnp.where(mask, jnp.exp(sc), 0.0)
            pb = p.astype(BF16)
            dv_acc[:ln, :] = dv_acc[:ln, :] + lax.dot_general(pb, do_b, _TN, preferred_element_type=F32)
            dp = lax.dot_general(do_b, vv, _NT, preferred_element_type=F32)
            ds = p * (dp - jnp.sum(p * dp, axis=-1, keepdims=True))
            dsb = ds.astype(BF16)
            dq = jnp.dot(dsb, kk, preferred_element_type=F32) * scale
            dz_ref[0, q0:ln, :HEAD_DIM] = dq.astype(BF16)
            dk_acc[:ln, :] = dk_acc[:ln, :] + lax.dot_general(dsb, q_b, _TN, preferred_element_type=F32)
            df_ref[0, 0, :, :ln] = df_ref[0, 0, :, :ln] - jnp.sum(ds, axis=0, keepdims=True)
        dz_ref[0, :, HEAD_DIM:2 * HEAD_DIM] = dk_acc[...].astype(BF16)
        dz_ref[0, :, 2 * HEAD_DIM:] = dv_acc[...].astype(BF16)

    zblk = pl.BlockSpec((1, s, 3 * HEAD_DIM), lambda b, h: (b, 0, h))
    hblk = pl.BlockSpec((1, s, HEAD_DIM), lambda b, h: (b, 0, h))
    col = pl.BlockSpec((1, 1, s, 1), lambda b, h: (b, h, 0, 0))
    rowspec = pl.BlockSpec((1, 1, 1, s), lambda b, h: (b, h, 0, 0))
    return _run(
        body, [z, do, lse, fcol, frow, dz], exchange, name="attn_bwd", grid=(bsz, N_HEADS),
        in_specs=[zblk, hblk, col, col, rowspec, ANY],
        out_specs=[zblk, rowspec],
        out_shape=[jax.ShapeDtypeStruct(dz.shape, dz.dtype), jax.ShapeDtypeStruct((bsz, N_HEADS, 1, s), F32)],
        scratch_shapes=[pltpu.VMEM((s, HEAD_DIM), F32), pltpu.VMEM((s, HEAD_DIM), F32)],
        input_output_aliases={5: 0},
        compiler_params=_params())


def _merge_fwd(z2, pr, pa, *, tr=512):
    t = z2.shape[0]

    def body(mgr_ref, mga_ref, pr_ref, pa_ref, o_ref):
        gr_ = _sigmoid(mgr_ref[...])
        ga = _sigmoid(mga_ref[...])
        o_ref[...] = (gr_ * pr_ref[...] + ga * pa_ref[...]).astype(BF16)

    tok = pl.BlockSpec((tr, D_MODEL), lambda i: (i, 0))
    return _call(body, name="merge_fwd", grid=(t // tr,),
                 in_specs=[pl.BlockSpec((tr, D_MODEL), lambda i: (i, MG0R // D_MODEL)),
                           pl.BlockSpec((tr, D_MODEL), lambda i: (i, MG0R // D_MODEL + 1)), tok, tok],
                 out_specs=tok, out_shape=jax.ShapeDtypeStruct((t, D_MODEL), BF16),
                 compiler_params=_params())(z2, z2, pr, pa)


def _merge_bwd(dm, z2, pr, pa, *, tr=512):
    t = z2.shape[0]

    def body(dm_ref, mgr_ref, mga_ref, pr_ref, pa_ref, dz_ref, dpr_ref, dpa_ref):
        dmv = dm_ref[...]
        gr_ = _sigmoid(mgr_ref[...])
        ga = _sigmoid(mga_ref[...])
        dpr_ref[...] = (dmv * gr_).astype(BF16)
        dpa_ref[...] = (dmv * ga).astype(BF16)
        dz_ref[:, :D_MODEL] = (dmv * pr_ref[...] * gr_ * (1.0 - gr_)).astype(BF16)
        dz_ref[:, D_MODEL:] = (dmv * pa_ref[...] * ga * (1.0 - ga)).astype(BF16)

    tok = pl.BlockSpec((tr, D_MODEL), lambda i: (i, 0))
    mg = pl.BlockSpec((tr, 2 * D_MODEL), lambda i: (i, MG0 // (2 * D_MODEL)))
    return _call(body, name="merge_bwd", grid=(t // tr,),
                 in_specs=[tok, pl.BlockSpec((tr, D_MODEL), lambda i: (i, MG0R // D_MODEL)),
                           pl.BlockSpec((tr, D_MODEL), lambda i: (i, MG0R // D_MODEL + 1)), tok, tok],
                 out_specs=[mg, tok, tok],
                 out_shape=[jax.ShapeDtypeStruct((t, NZ), BF16), jax.ShapeDtypeStruct((t, D_MODEL), BF16),
                            jax.ShapeDtypeStruct((t, D_MODEL), BF16)],
                 compiler_params=_params())(dm, z2, z2, pr, pa)


def _fl_write(dfl_blk, dz):
    bsz, s, w = dfl_blk.shape

    def body(src_ref, dz_in, dz_ref):
        del dz_in
        dz_ref[...] = src_ref[...]

    return _call(body, name="fl_write", grid=(bsz,),
                 in_specs=[pl.BlockSpec((1, s, w), lambda b: (b, 0, 0)), ANY],
                 out_specs=pl.BlockSpec((1, s, w), lambda b: (b, 0, FL0 // w)),
                 out_shape=jax.ShapeDtypeStruct(dz.shape, dz.dtype), input_output_aliases={1: 0},
                 compiler_params=_params())(dfl_blk, dz)


def _ffn_fwd(zf, wc, bc):
    bsz, s, _ = zf.shape
    cw = 128

    ch = min(SCAN_CHUNK, s)

    def body(zf_ref, wc_ref, bc_ref, o_ref):
        wcv, bcv = wc_ref[...], bc_ref[...]

        def chunk(ci, prev8):
            st = pl.multiple_of(ci * ch, ch)
            blk = zf_ref[0, pl.ds(st, ch), :].astype(F32)
            gfp, uf = blk[:, :cw], blk[:, cw:]
            xe = jnp.concatenate([prev8, gfp], axis=0)
            gf = bcv + wcv[2:3] * gfp
            for j in (1, 2):
                gf = gf + wcv[2 - j:3 - j] * pltpu.roll(xe, j, 0)[8:]
            o_ref[0, pl.ds(st, ch), :] = (_gelu(gf) * uf).astype(BF16)
            return gfp[ch - 8:]

        lax.fori_loop(0, s // ch, chunk, jnp.zeros((8, cw), F32))

    return _call(body, name="ffn_fwd", grid=(bsz, D_FF // cw),
                 in_specs=[pl.BlockSpec((1, s, 2 * cw), lambda b, j: (b, 0, j)),
                           pl.BlockSpec((3, cw), lambda b, j: (0, j)), pl.BlockSpec((1, cw), lambda b, j: (0, j))],
                 out_specs=pl.BlockSpec((1, s, cw), lambda b, j: (b, 0, j)),
                 out_shape=jax.ShapeDtypeStruct((bsz, s, D_FF), BF16), compiler_params=_params())(zf, wc, bc)


def _ffn_bwd(dact, zf, wc, bc, exchange=None):
    bsz, s, _ = zf.shape
    cw = 128

    def body(da_ref, zf_ref, wc_ref, bc_ref, dzf_ref, dsm_ref):
        b = pl.program_id(1)
        gfp, uf = zf_ref[0, :, :cw].astype(F32), zf_ref[0, :, cw:].astype(F32)
        wcv = wc_ref[...]
        row = lax.broadcasted_iota(jnp.int32, (s, cw), 0)
        sh = [gfp] + [jnp.where(row >= j, pltpu.roll(gfp, j, 0), 0.0) for j in (1, 2)]
        gf = bc_ref[...]
        for kk in range(3):
            gf = gf + wcv[kk:kk + 1] * sh[2 - kk]
        ge, dge = _gelu_and_grad(gf)
        dav = da_ref[0].astype(F32)
        dgf = dav * uf * dge
        dgfp = wcv[2:3] * dgf
        for j in (1, 2):
            dgfp = dgfp + wcv[2 - j:3 - j] * jnp.where(row < s - j, pltpu.roll(dgf, s - j, 0), 0.0)
        dzf_ref[0] = jnp.concatenate([dgfp, dav * ge], axis=1).astype(BF16)
        rows = [jnp.sum(dgf * sh[2 - kk], axis=0, keepdims=True) for kk in range(3)]
        rows += [jnp.sum(dgf, axis=0, keepdims=True), jnp.zeros((4, cw), F32)]
        _acc(dsm_ref, slice(None), jnp.concatenate(rows, axis=0), b == 0)

    return _run(body, [dact, zf, wc, bc], exchange, name="ffn_bwd", grid=(D_FF // cw, bsz),
                in_specs=[pl.BlockSpec((1, s, cw), lambda j, b: (b, 0, j)),
                          pl.BlockSpec((1, s, 2 * cw), lambda j, b: (b, 0, j)),
                          pl.BlockSpec((3, cw), lambda j, b: (0, j)), pl.BlockSpec((1, cw), lambda j, b: (0, j))],
                out_specs=[pl.BlockSpec((1, s, 2 * cw), lambda j, b: (b, 0, j)),
                           pl.BlockSpec((8, cw), lambda j, b: (0, j))],
                out_shape=[jax.ShapeDtypeStruct(zf.shape, BF16), jax.ShapeDtypeStruct((8, D_FF), F32)],
                compiler_params=_params())


def _adamw(w, m, v, parts, *, name):
    r, c = w.shape
    p = parts.shape[0]
    by_cols = r > 512 and r % 256 != 0
    tr, tc = (r, 256) if by_cols else (r if r <= 512 else 256, c)
    assert r % tr == 0 and c % tc == 0 and parts.shape[1:] == (r, c), (name, w.shape, parts.shape)
    c1 = 1.0 / (1.0 - ADAM_B1 ** ADAM_STEP)
    c2 = 1.0 / (1.0 - ADAM_B2 ** ADAM_STEP)

    def body(w_ref, m_ref, v_ref, p_ref, g_ref, d_ref, mo_ref, vo_ref):
        g = p_ref[0].astype(F32)
        for q in range(1, p):
            g = g + p_ref[q].astype(F32)
        mn = ADAM_B1 * m_ref[...] + (1.0 - ADAM_B1) * g
        vn = ADAM_B2 * v_ref[...] + (1.0 - ADAM_B2) * (g * g)
        g_ref[...] = g
        mo_ref[...] = mn
        vo_ref[...] = vn
        d_ref[...] = -ADAM_LR * ((mn * c1) / (jnp.sqrt(vn * c2) + ADAM_EPS) + ADAM_WD * w_ref[...])

    spec = pl.BlockSpec((tr, tc), (lambda i: (0, i)) if by_cols else (lambda i: (i, 0)))
    shp = jax.ShapeDtypeStruct((r, c), F32)
    return _call(body, name=name, grid=(c // tc if by_cols else r // tr,),
                 in_specs=[spec, spec, spec,
                           pl.BlockSpec((p, tr, tc), (lambda i: (0, 0, i)) if by_cols else (lambda i: (0, i, 0)))],
                 out_specs=[spec] * 4, out_shape=[shp] * 4, compiler_params=_params())(w, m, v, parts)


def _ref_slab(i):
    nq = 3 * N_HEADS
    j = i - nq
    return jnp.where(i < nq, 2 * D_RNN // 128 + (i % 3) * N_HEADS + i // 3, (j % 2) * N_BLK + j // 2)


def _permute_slabs(x, to_z, name):
    n = FL0 // 128
    cols = x.shape[1]

    def body(x_ref, o_ref):
        o_ref[...] = x_ref[...]

    there = pl.BlockSpec((128, cols), lambda i: (_ref_slab(i), 0))
    here = pl.BlockSpec((128, cols), lambda i: (i, 0))
    return _call(body, name=name, grid=(n,), in_specs=[there if to_z else here],
                 out_specs=here if to_z else there, out_shape=jax.ShapeDtypeStruct((FL0, cols), x.dtype),
                 compiler_params=_params())(x)


def _to_z_layout(wt):
    fl0 = 2 * D_RNN + 3 * D_MODEL
    pad = jnp.zeros((MG0 - FL0 - N_HEADS, wt.shape[1]), wt.dtype)
    return jnp.concatenate([_permute_slabs(wt, True, "to_z_layout"), wt[fl0:fl0 + N_HEADS], pad,
                            wt[fl0 + N_HEADS:]], axis=0)


def _from_z_layout(wt):
    return jnp.concatenate([_permute_slabs(wt, False, "from_z_layout"), wt[FL0:FL0 + N_HEADS], wt[MG0:]], axis=0)


def _col_gather(g):
    return g.transpose(1, 0, 2).reshape(g.shape[1], -1)


def _col_scatter(w):
    return w.reshape(w.shape[0], N_DEV, -1).transpose(1, 0, 2)


def kernel(x, c, w_ada, b_ada, g_norm1, w_in, w_rnn_conv, b_rnn_conv, w_lru_a, b_lru_a, w_lru_i, b_lru_i, lru_lambda, b_fgate, w_proj_rnn, w_proj_attn, w_out, g_norm2, w_ffn_up, w_ffn_conv, b_ffn_conv, w_ffn_down, w_ada_final, b_ada_final, g_final, loss_target, m_w_ada, m_b_ada, m_g_norm1, m_w_in, m_w_rnn_conv, m_b_rnn_conv, m_w_lru_a, m_b_lru_a, m_w_lru_i, m_b_lru_i, m_lru_lambda, m_b_fgate, m_w_proj_rnn, m_w_proj_attn, m_w_out, m_g_norm2, m_w_ffn_up, m_w_ffn_conv, m_b_ffn_conv, m_w_ffn_down, m_w_ada_final, m_b_ada_final, m_g_final, v_w_ada, v_b_ada, v_g_norm1, v_w_in, v_w_rnn_conv, v_b_rnn_conv, v_w_lru_a, v_b_lru_a, v_w_lru_i, v_b_lru_i, v_lru_lambda, v_b_fgate, v_w_proj_rnn, v_w_proj_attn, v_w_out, v_g_norm2, v_w_ffn_up, v_w_ffn_conv, v_b_ffn_conv, v_w_ffn_down, v_w_ada_final, v_b_ada_final, v_g_final):
    args = dict(locals())
    bsz, s, _ = x.shape
    t = bsz * s
    nb = N_DEV * bsz
    me = 4 * lax.axis_index("x") + 2 * lax.axis_index("y") + lax.axis_index("c")
    tm = min(2048, t)

    ex_a = _gather_two_level([c, w_in[0].T.astype(BF16), w_rnn_conv[0], w_ffn_conv[0]], "gather_first")
    c_all = ex_a[0].reshape(nb, D_MODEL)
    win_zt = _to_z_layout(ex_a[1].reshape(-1, D_MODEL))
    wrc = _col_gather(ex_a[2])
    wfc = _col_gather(ex_a[3])

    nmod = w_ada.shape[2]
    nmodf = w_ada_final.shape[1]
    mod_cols = _mm(c_all, w_ada[0], name="mod_cols", tm=nb, tn=nmod, tk=D_MODEL, silu_a=True,
                   bias=lax.dynamic_slice(b_ada, (0, me * nmod), (1, nmod)))
    modf_cols = _mm(c_all, w_ada_final, name="modf_cols", tm=nb, tn=nmodf, tk=D_MODEL, silu_a=True,
                    bias=lax.dynamic_slice(b_ada_final.reshape(1, -1), (0, me * nmodf), (1, nmodf)))
    ex_b = _exchange([mod_cols, modf_cols], [True, True], "gather_mod")
    mod = lax.dynamic_slice(ex_b[0], (0, me * bsz, 0), (N_DEV, bsz, nmod)).transpose(1, 0, 2).reshape(bsz, 6, 1, D_MODEL)
    modf = lax.dynamic_slice(ex_b[1], (0, me * bsz, 0), (N_DEV, bsz, nmodf)).transpose(1, 0, 2).reshape(bsz, 2, 1, D_MODEL)
    shift1, scale1, gate1, shift2, scale2, gate2 = [mod[:, i] for i in range(6)]
    shift_f, scale_f = modf[:, 0], modf[:, 1]

    h1 = _norm_fwd(x, g_norm1, shift1, scale1, name="norm1_fwd")[0]
    h1f = h1.reshape(t, D_MODEL)
    zq = _mm(h1f, win_zt, name="mm_in_qkv", tb=True, out_dtype=BF16, tm=tm, tn=1024, tk=D_MODEL,
             b_cols=(QKV0, XG0)).reshape(bsz, s, XG0)
    zr = _mm(h1f, win_zt, name="mm_in_rest", tb=True, tm=tm, tn=1024, tk=D_MODEL,
             b_cols=(XG0, NZR)).reshape(bsz, s, NZR)
    h_rnn, y_rnn, lru_gates, lru_gelus, got = _rnn_fwd(zr, wrc, b_rnn_conv, w_lru_a[0], b_lru_a, w_lru_i[0], b_lru_i, lru_lambda,
                                 exchange=([w_ffn_up[0].astype(BF16)], ["gather_zf"]))
    wup_z = got[0]
    flt = zr[:, :, FL0R:FL0R + N_HEADS].transpose(0, 2, 1)
    bfg = b_fgate.reshape(N_HEADS, 1)
    fcum = _fgate_fwd(flt, bfg)
    fcol = fcum.reshape(bsz, N_HEADS, s, 1)
    frow = fcum.reshape(bsz, N_HEADS, 1, s)
    o_attn, lse, got = _attn_fwd(
        zq, fcol, frow, exchange=([w_proj_rnn[0].astype(BF16), w_proj_attn[0].astype(BF16), w_out[0].astype(BF16),
                                  w_ffn_down[0].astype(BF16)], [True] * 4))
    wpr = got[0].reshape(D_RNN, D_MODEL)
    wpa = got[1].reshape(D_MODEL, D_MODEL)
    wout = got[2].reshape(D_MODEL, D_MODEL)
    wdown = got[3].reshape(D_FF, D_MODEL)
    pr = _mm(y_rnn.reshape(t, D_RNN), wpr, name="mm_proj_rnn", out_dtype=BF16, tm=tm, tn=D_MODEL, tk=D_RNN)
    pa = _mm(o_attn.reshape(t, D_MODEL), wpa, name="mm_proj_attn", out_dtype=BF16, tm=tm, tn=D_MODEL, tk=D_MODEL)
    z2 = zr.reshape(t, NZR)
    merged = _merge_fwd(z2, pr, pa)
    mo = _mm(merged, wout, name="mm_out", out_dtype=BF16, tm=tm, tn=D_MODEL, tk=D_MODEL).reshape(bsz, s, D_MODEL)
    h2, x1 = _norm_fwd(x, g_norm2, shift2, scale2, name="norm2_fwd", yprev=mo, gate=gate1)
    zf = _mm(h2.reshape(t, D_MODEL), wup_z, name="mm_up", out_dtype=BF16, tm=tm, tn=1024,
             tk=D_MODEL).reshape(bsz, s, 2 * D_FF)
    act = _ffn_fwd(zf, wfc, b_ffn_conv)
    yf = _mm(act.reshape(t, D_FF), wdown, name="mm_down", out_dtype=BF16, tm=tm, tn=D_MODEL,
             tk=D_FF // 2).reshape(bsz, s, D_MODEL)
    lp, dx2, dyf, dshift_f, dscale_f, dgate2, dg_final = _final(x1, yf, gate2, g_final.reshape(1, -1), shift_f, scale_f, loss_target)
    loss = lax.psum(jnp.sum(lp[:, :, 0, 0]) * (0.5 / D_MODEL), ("x", "y", "c"))

    dyf2 = dyf.reshape(t, D_MODEL)
    act2 = act.reshape(t, D_FF)
    dact = _mm(dyf2, wdown, name="mm_down_dx", tb=True, out_dtype=BF16, tm=tm, tn=1024, tk=D_MODEL).reshape(bsz, s, D_FF)
    g_wdown = _mm(act2, dyf2, name="mm_down_dw", ta=True, out_dtype=BF16, tm=1024, tn=D_MODEL, tk=tm)
    dzf, dsm_ffn, got = _ffn_bwd(dact, zf, wfc, b_ffn_conv, exchange=([g_wdown.reshape(N_DEV, -1, D_MODEL)], [False]))
    p_wdown = got[0]
    dzf2 = dzf.reshape(t, 2 * D_FF)
    dh2 = _mm(dzf2, wup_z, name="mm_up_dx", tb=True, tm=tm, tn=D_MODEL, tk=2048).reshape(bsz, s, D_MODEL)
    g_wup_z = _mm(h2.reshape(t, D_MODEL), dzf2, name="mm_up_dw", ta=True, out_dtype=BF16, tm=D_MODEL, tn=1024, tk=tm)
    dx1, dg_norm2, dshift2, dscale2, dmo, dgate1 = _norm_bwd(x1, dh2, g_norm2, scale2, dx2, name="norm2_bwd", yprev=mo, gate=gate1)
    dmo2 = dmo.reshape(t, D_MODEL)
    dmerged = _mm(dmo2, wout, name="mm_out_dx", tb=True, out_dtype=BF16, tm=tm, tn=D_MODEL, tk=D_MODEL)
    g_wout = _mm(merged, dmo2, name="mm_out_dw", ta=True, out_dtype=BF16, tm=D_MODEL, tn=D_MODEL, tk=tm)
    dz2, dpr, dpa = _merge_bwd(dmerged, z2, pr, pa)
    dy_rnn = _mm(dpr, wpr, name="mm_proj_rnn_dx", tb=True, out_dtype=BF16, tm=tm, tn=D_RNN,
                 tk=D_MODEL).reshape(bsz, s, D_RNN)
    g_wpr = _mm(y_rnn.reshape(t, D_RNN), dpr, name="mm_proj_rnn_dw", ta=True, out_dtype=BF16, tm=D_RNN, tn=D_MODEL, tk=tm)
    do = _mm(dpa, wpa, name="mm_proj_attn_dx", tb=True, out_dtype=BF16, tm=tm, tn=D_MODEL, tk=D_MODEL).reshape(bsz, s, D_MODEL)
    g_wpa = _mm(o_attn.reshape(t, D_MODEL), dpa, name="mm_proj_attn_dw", ta=True, out_dtype=BF16, tm=D_MODEL, tn=D_MODEL, tk=tm)
    dz, dfrow, got = _attn_bwd(zq, do, lse, fcol, frow, dz2.reshape(bsz, s, NZ),
                               exchange=([g_wup_z, g_wout.reshape(N_DEV, -1, D_MODEL)], ["scatter_zf", False]))
    p_wup, p_wout = got
    dflt, db_fgate = _fgate_bwd(dfrow.reshape(bsz, N_HEADS, s), flt, bfg)
    dfl_blk = jnp.pad(dflt.transpose(0, 2, 1).astype(BF16), ((0, 0), (0, 0), (0, MG0 - FL0 - N_HEADS)))
    dz = _fl_write(dfl_blk, dz)
    dz, g_wa, g_wi, dsm_rnn, got = _rnn_bwd(
        zr, h_rnn, lru_gates, lru_gelus, dy_rnn, dz, wrc, b_rnn_conv, w_lru_a[0], w_lru_i[0], lru_lambda,
        exchange=([g_wpr.reshape(N_DEV, -1, D_MODEL), g_wpa.reshape(N_DEV, -1, D_MODEL)], [False, False]))
    p_wpr, p_wpa = got
    dz2 = dz.reshape(t, NZ)
    small = [dg_norm2, dg_final, dsm_rnn, g_wa, g_wi, db_fgate, dsm_ffn]
    small_sizes = [a.size for a in small]
    n_pad = -sum(small_sizes) % 1024
    packed = jnp.concatenate([a.reshape(-1) for a in small] + [jnp.zeros((n_pad,), F32)]).reshape(-1, 128)
    g_win_zt, got = _mm(dz2, h1f, name="mm_in_dw", ta=True, out_dtype=BF16, tm=1024, tn=D_MODEL, tk=tm,
                        exchange=([packed], [True]))
    flat = got[0].reshape(N_DEV, -1)
    dh1, got = _mm(dz2, win_zt, name="mm_in_dx", tm=tm, tn=D_MODEL, tk=2048,
                   exchange=([_from_z_layout(g_win_zt).reshape(N_DEV, -1, D_MODEL)], [False]))
    p_win = got[0]
    grad_x, dg_norm1, dshift1, dscale1 = _norm_bwd(x, dh1.reshape(bsz, s, D_MODEL), g_norm1, scale1, dx1, name="norm1_bwd")

    dmod = jnp.concatenate([dshift1, dscale1, dgate1, dshift2, dscale2, dgate2], axis=1).reshape(bsz, 6 * D_MODEL)
    dmodf = jnp.concatenate([dshift_f, dscale_f], axis=1).reshape(bsz, 2 * D_MODEL)
    ex_c = _exchange([dmod, dmodf, dg_norm1], [True, True, True], "gather_last_grads")
    dmod_all = ex_c[0].reshape(nb, 6 * D_MODEL)
    dmodf_all = ex_c[1].reshape(nb, 2 * D_MODEL)
    p_g1 = ex_c[2]

    res = {}

    def upd(name, parts, shape2):
        w, m, v = args[name], args["m_" + name], args["v_" + name]
        outs = _adamw(w.reshape(shape2), m.reshape(shape2), v.reshape(shape2), parts, name="adamw_" + name)
        res[name] = [o.reshape(w.shape) for o in outs]

    g_wada = _mm(c_all, lax.dynamic_slice(dmod_all, (0, me * nmod), (nb, nmod)), name="mm_ada_dw", ta=True,
                 silu_a=True, tm=D_MODEL, tn=nmod, tk=nb)
    g_wadaf = _mm(c_all, lax.dynamic_slice(dmodf_all, (0, me * nmodf), (nb, nmodf)), name="mm_adaf_dw", ta=True,
                  silu_a=True, tm=D_MODEL, tn=nmodf, tk=nb)
    upd("w_ada", g_wada[None], (D_MODEL, nmod))
    upd("w_ada_final", g_wadaf[None], (D_MODEL, nmodf))
    upd("b_ada", dmod_all.reshape(nb, 1, -1), (1, 6 * D_MODEL))
    upd("b_ada_final", dmodf_all.reshape(nb, 1, -1), (1, 2 * D_MODEL))
    outs = _adamw(w_in[0].T, m_w_in[0].T, v_w_in[0].T, p_win, name="adamw_w_in")
    res["w_in"] = [o.T[None] for o in outs]
    upd("w_proj_rnn", p_wpr, w_proj_rnn.shape[1:])
    upd("w_proj_attn", p_wpa, w_proj_attn.shape[1:])
    upd("w_out", p_wout, w_out.shape[1:])
    upd("w_ffn_up", p_wup, w_ffn_up.shape[1:])
    upd("w_ffn_down", p_wdown, w_ffn_down.shape[1:])

    parts, off = [], 0
    for a, n in zip(small, small_sizes):
        parts.append(flat[:, off:off + n].reshape((N_DEV,) + a.shape))
        off += n
    p_g2, p_gf, p_rnn, p_wa, p_wi, p_bfg, p_ffn = parts
    upd("g_norm1", p_g1, (1, D_MODEL))
    upd("g_norm2", p_g2, (1, D_MODEL))
    upd("g_final", p_gf, (1, D_MODEL))
    nrc = w_rnn_conv.shape[2]
    upd("w_rnn_conv", lax.dynamic_slice(p_rnn, (0, 0, me * nrc), (N_DEV, 4, nrc)), (4, nrc))
    upd("b_rnn_conv", p_rnn[:, 4:5], (1, D_RNN))
    upd("b_lru_a", p_rnn[:, 5:6], (1, D_RNN))
    upd("b_lru_i", p_rnn[:, 6:7], (1, D_RNN))
    upd("lru_lambda", p_rnn[:, 7:8], (1, D_RNN))
    upd("w_lru_a", p_wa.reshape(N_DEV, D_RNN, RNN_BLK), (D_RNN, RNN_BLK))
    upd("w_lru_i", p_wi.reshape(N_DEV, D_RNN, RNN_BLK), (D_RNN, RNN_BLK))
    upd("b_fgate", p_bfg.reshape(N_DEV, 1, N_HEADS), (1, N_HEADS))
    nfc = w_ffn_conv.shape[2]
    upd("w_ffn_conv", lax.dynamic_slice(p_ffn, (0, 0, me * nfc), (N_DEV, 3, nfc)), (3, nfc))
    upd("b_ffn_conv", p_ffn[:, 3:4], (1, D_FF))

    names = ['w_ada', 'b_ada', 'g_norm1', 'w_in', 'w_rnn_conv', 'b_rnn_conv', 'w_lru_a', 'b_lru_a', 'w_lru_i',
             'b_lru_i', 'lru_lambda', 'b_fgate', 'w_proj_rnn', 'w_proj_attn', 'w_out', 'g_norm2', 'w_ffn_up',
             'w_ffn_conv', 'b_ffn_conv', 'w_ffn_down', 'w_ada_final', 'b_ada_final', 'g_final']
    return (loss, grad_x, *[res[n][0] for n in names], *[res[n][1] for n in names],
            *[res[n][2] for n in names], *[res[n][3] for n in names])
```
